```python
import math
import jax, jax.numpy as jnp
from jax import lax
import numpy as np

D_MODEL = 1024
BATCH = 2
SEQ = 8192
DEPTH = 2
DEC_BATCH = 16
DEC_SEQ = 64
PAST_LEN = 2048

CHUNK = 64
LEFT_CHUNKS = 8
LEFT_CTX = LEFT_CHUNKS * CHUNK
BAND = (LEFT_CHUNKS + 1) * CHUNK
D_A = D_MODEL // 2
N_HEADS_A = 8
HEAD_DIM_A = D_A // N_HEADS_A
REL_CLIP = 128
ATTN_SCALE = HEAD_DIM_A ** -0.5
D_B = D_MODEL - D_A
N_GROUPS_B = 8
GROUP_DIM_B = D_B // N_GROUPS_B
MLP_CHUNK = 128
D_IN = 3 * D_A + 2 * D_B
SPLITS = (D_A, 2 * D_A, 3 * D_A, 3 * D_A + D_B)
FFN_DIM = ((8 * D_MODEL // 3 + 255) // 256) * 256
N_EXPERTS = 8
TOP_K = 2
N_DENSE = (DEPTH + 1) // 2
N_MOE = DEPTH // 2
ALPHA = (2 * DEPTH) ** 0.25
BETA = (8 * DEPTH) ** -0.25
LN_EPS = 1e-5

kernel_name = 'hybrid_chunk_attn_gmlp_streaming_step'


def layer_norm(x, g, b):
    xf = x.astype(jnp.float32)
    mu = jnp.mean(xf, -1, keepdims=True)
    var = jnp.mean(jnp.square(xf - mu), -1, keepdims=True)
    return ((xf - mu) * lax.rsqrt(var + LN_EPS) * g.astype(jnp.float32) + b.astype(jnp.float32)).astype(x.dtype)


def rms_norm(x, g):
    xf = x.astype(jnp.float32)
    return (xf * lax.rsqrt(jnp.mean(xf * xf, -1, keepdims=True) + LN_EPS) * g.astype(jnp.float32)).astype(x.dtype)


def adaln_mod(c, w, b):
    m = jax.nn.silu(c) @ w + b
    return jnp.split(m[:, None, :], 6, axis=-1)


def rel_bias(table, dist):
    return table[:, jnp.clip(dist, -REL_CLIP, REL_CLIP) + REL_CLIP]


def chunk_band_attention(q, k, v, table):
    B, S, H, Dh = q.shape
    nc = S // CHUNK
    qc = q.reshape(B, nc, CHUNK, H, Dh)
    pad = jnp.zeros((B, LEFT_CTX, H, Dh), k.dtype)
    kp = jnp.concatenate([pad, k], 1).reshape(B, nc + LEFT_CHUNKS, CHUNK, H, Dh)
    vp = jnp.concatenate([pad, v], 1).reshape(B, nc + LEFT_CHUNKS, CHUNK, H, Dh)
    kb = jnp.concatenate([kp[:, o:o + nc] for o in range(LEFT_CHUNKS + 1)], axis=2)
    vb = jnp.concatenate([vp[:, o:o + nc] for o in range(LEFT_CHUNKS + 1)], axis=2)
    dist = LEFT_CTX + jnp.arange(CHUNK)[:, None] - jnp.arange(BAND)[None, :]
    bias = rel_bias(table, dist).astype(jnp.float32)
    valid = (jnp.arange(nc)[:, None] - LEFT_CHUNKS + jnp.arange(BAND)[None, :] // CHUNK) >= 0
    s = jnp.einsum('bnqhd,bnkhd->bnhqk', qc, kb).astype(jnp.float32) * ATTN_SCALE + bias
    s = jnp.where(valid[None, :, None, None, :], s, -1e30)
    p = jax.nn.softmax(s, axis=-1).astype(v.dtype)
    o = jnp.einsum('bnhqk,bnkhd->bnqhd', p, vb)
    return o.reshape(B, S, H * Dh)


def cached_band_attention(q, k, v, ck, cv, table):
    B, T, H, Dh = q.shape
    W = ck.shape[1]
    kk = jnp.concatenate([ck, k], 1)
    vv = jnp.concatenate([cv, v], 1)
    dist = W + jnp.arange(T)[:, None] - jnp.arange(W + T)[None, :]
    bias = rel_bias(table, dist).astype(jnp.float32)
    s = jnp.einsum('bqhd,bkhd->bhqk', q, kk).astype(jnp.float32) * ATTN_SCALE + bias
    p = jax.nn.softmax(s, axis=-1).astype(v.dtype)
    o = jnp.einsum('bhqk,bkhd->bqhd', p, vv)
    return o.reshape(B, T, H * Dh)


def chunk_spatial_gating(u, vn, w_s, b_s):
    B, S, _ = u.shape
    L = min(S, MLP_CHUNK)
    nm = S // L
    tril = jnp.tril(jnp.ones((L, L), dtype=bool))
    wm = jnp.where(tril[None], w_s[:, :L, :L], 0)
    vg = vn.reshape(B, nm, L, N_GROUPS_B, GROUP_DIM_B)
    mixed = jnp.einsum('gij,bnjgc->bnigc', wm, vg) + jnp.transpose(b_s[:, :L])[:, :, None]
    return u * mixed.reshape(B, S, D_B)


def swiglu(h, wg, wu, wd):
    return (jax.nn.silu(h @ wg) * (h @ wu)) @ wd


def moe_swiglu(h, w_router, wg, wu, wd):
    shp = h.shape
    t = h.reshape(-1, shp[-1])
    logits = (t @ w_router).astype(jnp.float32)
    top_v, top_i = lax.top_k(logits, TOP_K)
    top_w = jax.nn.softmax(top_v, axis=-1)
    gates = jnp.sum(jax.nn.one_hot(top_i, N_EXPERTS, dtype=jnp.float32) * top_w[..., None], axis=1).astype(h.dtype)
    out = jnp.zeros_like(t)
    for e in range(N_EXPERTS):
        out = out + gates[:, e:e + 1] * swiglu(t, wg[e], wu[e], wd[e])
    return out.reshape(shp)


def setup_inputs(seed: int = 0) -> dict:
    key = jax.random.key(seed)
    ks = iter(jax.random.split(key, 40))
    f32 = jnp.float32
    nrm = lambda shape, s: jax.random.normal(next(ks), shape, f32) * s
    kv_win = min(LEFT_CTX, PAST_LEN)
    return {
        'x_prompt': nrm((BATCH, SEQ, D_MODEL), 1.0),
        'x_sample': nrm((DEC_BATCH, DEC_SEQ, D_MODEL), 1.0),
        'cache_k': nrm((DEPTH, DEC_BATCH, kv_win, N_HEADS_A, HEAD_DIM_A), 1.0),
        'cache_v': nrm((DEPTH, DEC_BATCH, kv_win, N_HEADS_A, HEAD_DIM_A), 1.0),
        'c_prompt': nrm((BATCH, D_MODEL), 1.0),
        'c_sample': nrm((DEC_BATCH, D_MODEL), 1.0),
        'ln_in_g': 1.0 + nrm((D_MODEL,), 0.02),
        'ln_in_b': nrm((D_MODEL,), 0.02),
        'w_in': nrm((DEPTH, D_MODEL, D_IN), D_MODEL ** -0.5),
        'w_out': nrm((DEPTH, D_MODEL, D_MODEL), BETA * D_MODEL ** -0.5),
        'rel_bias_table': nrm((DEPTH, N_HEADS_A, 2 * REL_CLIP + 1), 0.2),
        'gmlp_ln_g': 1.0 + nrm((DEPTH, D_B), 0.02),
        'gmlp_ln_b': nrm((DEPTH, D_B), 0.02),
        'gmlp_w_s': nrm((DEPTH, N_GROUPS_B, MLP_CHUNK, MLP_CHUNK), MLP_CHUNK ** -0.5),
        'gmlp_b_s': 1.0 + nrm((DEPTH, N_GROUPS_B, MLP_CHUNK), 0.02),
        'out_norm_a': 1.0 + nrm((DEPTH, D_A), 0.02),
        'out_norm_b': 1.0 + nrm((DEPTH, D_B), 0.02),
        'ada_w': nrm((DEPTH, D_MODEL, 6 * D_MODEL), 0.1 * D_MODEL ** -0.5),
        'ada_b': nrm((DEPTH, 6 * D_MODEL), 0.02),
        'ln1_g': 1.0 + nrm((DEPTH, D_MODEL), 0.02),
        'ln1_b': nrm((DEPTH, D_MODEL), 0.02),
        'ln2_g': 1.0 + nrm((DEPTH, D_MODEL), 0.02),
        'ln2_b': nrm((DEPTH, D_MODEL), 0.02),
        'ffn_w_gate': nrm((N_DENSE, D_MODEL, FFN_DIM), D_MODEL ** -0.5),
        'ffn_w_up': nrm((N_DENSE, D_MODEL, FFN_DIM), D_MODEL ** -0.5),
        'ffn_w_down': nrm((N_DENSE, FFN_DIM, D_MODEL), BETA * FFN_DIM ** -0.5),
        'moe_router': nrm((N_MOE, D_MODEL, N_EXPERTS), D_MODEL ** -0.5),
        'moe_w_gate': nrm((N_MOE, N_EXPERTS, D_MODEL, FFN_DIM), D_MODEL ** -0.5),
        'moe_w_up': nrm((N_MOE, N_EXPERTS, D_MODEL, FFN_DIM), D_MODEL ** -0.5),
        'moe_w_down': nrm((N_MOE, N_EXPERTS, FFN_DIM, D_MODEL), BETA * FFN_DIM ** -0.5),
    }


def reference(x_prompt, x_sample, cache_k, cache_v, c_prompt, c_sample, ln_in_g, ln_in_b, w_in, w_out,
              rel_bias_table, gmlp_ln_g, gmlp_ln_b, gmlp_w_s, gmlp_b_s, out_norm_a, out_norm_b, ada_w, ada_b,
              ln1_g, ln1_b, ln2_g, ln2_b, ffn_w_gate, ffn_w_up, ffn_w_down, moe_router, moe_w_gate, moe_w_up,
              moe_w_down):

    def run_group(x, c, ck, cv):
        x = layer_norm(x, ln_in_g, ln_in_b)
        B, S, _ = x.shape
        k_rows, v_rows, gv_rows = [], [], []
        for l in range(DEPTH):
            sh1, sc1, g1, sh2, sc2, g2 = adaln_mod(c, ada_w[l], ada_b[l])
            h = x * (1 + sc1) + sh1
            q, k, v, u, vb = jnp.split(h @ w_in[l], SPLITS, axis=-1)
            q = q.reshape(B, S, N_HEADS_A, HEAD_DIM_A)
            k = k.reshape(B, S, N_HEADS_A, HEAD_DIM_A)
            v = v.reshape(B, S, N_HEADS_A, HEAD_DIM_A)
            u = jax.nn.gelu(u)
            vn = layer_norm(jax.nn.gelu(vb), gmlp_ln_g[l], gmlp_ln_b[l])
            if ck is None:
                a = chunk_band_attention(q, k, v, rel_bias_table[l])
                k_rows.append(k[:, -LEFT_CTX:])
                v_rows.append(v[:, -LEFT_CTX:])
            else:
                a = cached_band_attention(q, k, v, ck[l], cv[l], rel_bias_table[l])
                k_rows.append(k)
                v_rows.append(v)
                gv_rows.append(vn)
            bo = chunk_spatial_gating(u, vn, gmlp_w_s[l], gmlp_b_s[l])
            mix = jnp.concatenate([rms_norm(a, out_norm_a[l]), rms_norm(bo, out_norm_b[l])], axis=-1) @ w_out[l]
            x = layer_norm(ALPHA * x + (1 + g1) * mix, ln1_g[l], ln1_b[l])
            h = x * (1 + sc2) + sh2
            if l % 2 == 0:
                f = swiglu(h, ffn_w_gate[l // 2], ffn_w_up[l // 2], ffn_w_down[l // 2])
            else:
                f = moe_swiglu(h, moe_router[l // 2], moe_w_gate[l // 2], moe_w_up[l // 2], moe_w_down[l // 2])
            x = layer_norm(ALPHA * x + (1 + g2) * f, ln2_g[l], ln2_b[l])
        return x, k_rows, v_rows, gv_rows

    y_prompt, kp, vp, _ = run_group(x_prompt, c_prompt, None, None)
    y_sample, ksn, vsn, gvs = run_group(x_sample, c_sample, cache_k, cache_v)
    k_prompt_new = jnp.stack(kp)
    v_prompt_new = jnp.stack(vp)
    k_sample_new = jnp.stack(ksn)
    v_sample_new = jnp.stack(vsn)
    gmlp_v_sample_new = jnp.stack(gvs)
    return (y_prompt, y_sample, k_prompt_new, v_prompt_new, k_sample_new, v_sample_new, gmlp_v_sample_new)
```

```python
import functools

import jax
import jax.numpy as jnp
from jax import lax
from jax.experimental import pallas as pl
from jax.experimental.pallas import tpu as pltpu

CHUNK = 64
LEFT_CHUNKS = 8
LEFT_CTX = LEFT_CHUNKS * CHUNK
N_HEADS = 8
HEAD_DIM = 64
D_A = N_HEADS * HEAD_DIM
N_GROUPS = 8
GROUP_DIM = 64
D_B = N_GROUPS * GROUP_DIM
MLP_CHUNK = 128
REL_CLIP = 128
N_EXPERTS = 8
DEPTH = 2
ALPHA = (2 * DEPTH) ** 0.25
LN_EPS = 1e-5
ATTN_SCALE = HEAD_DIM ** -0.5
NEG = -1e30

LANES = 128
ROW_TILE = 512
CHUNKS_PER_TILE = ROW_TILE // CHUNK
Q_TILE = 256
FFN_TILE = 1024
FFN_SUB = 256
FFN_COLS = 256
VMEM_LIMIT = 48 * 1024 * 1024

BF16 = jnp.bfloat16
F32 = jnp.float32


def _cparams(*sem):
    return pltpu.CompilerParams(dimension_semantics=sem, vmem_limit_bytes=VMEM_LIMIT)


def _ln(x, g, b):
    mu = jnp.mean(x, axis=-1, keepdims=True)
    xc = x - mu
    var = jnp.mean(xc * xc, axis=-1, keepdims=True)
    return xc * lax.rsqrt(var + LN_EPS) * g + b


def _rms(x, g):
    return x * lax.rsqrt(jnp.mean(x * x, axis=-1, keepdims=True) + LN_EPS) * g


def _modulate(x, sc_ref, sh_ref):
    parts = []
    for c in range(CHUNKS_PER_TILE):
        xc = x[c * CHUNK:(c + 1) * CHUNK]
        parts.append(xc * (1.0 + sc_ref[c]) + sh_ref[c])
    return jnp.concatenate(parts, axis=0)


def _gate_rows(x, g_ref):
    parts = []
    for c in range(CHUNKS_PER_TILE):
        parts.append(x[c * CHUNK:(c + 1) * CHUNK] * (1.0 + g_ref[c]))
    return jnp.concatenate(parts, axis=0)


def _split_bf16(x):
    hi = x.astype(BF16)
    lo = (x - hi.astype(F32)).astype(BF16)
    return hi, lo


def _dot(a, b):
    return jnp.dot(a, b, preferred_element_type=F32)


def _ada_kernel(c_ref, w_ref, b_ref, o_ref):
    c = c_ref[...]
    s = c * jax.nn.sigmoid(c)
    s_hi, s_lo = _split_bf16(s)
    w_hi, w_lo = _split_bf16(w_ref[0])
    o_ref[0] = _dot(s_hi, w_hi) + _dot(s_lo, w_hi) + _dot(s_hi, w_lo) + b_ref[0]


def _ada_call(c_pad, ada_w, ada_b):
    depth, d, n = ada_w.shape
    rows = c_pad.shape[0]
    tn = n // 4
    return pl.pallas_call(
        _ada_kernel,
        grid=(depth, n // tn),
        in_specs=[
            pl.BlockSpec((rows, d), lambda l, j: (0, 0)),
            pl.BlockSpec((1, d, tn), lambda l, j: (l, 0, j)),
            pl.BlockSpec((1, 1, tn), lambda l, j: (l, 0, j)),
        ],
        out_specs=pl.BlockSpec((1, rows, tn), lambda l, j: (l, 0, j)),
        out_shape=jax.ShapeDtypeStruct((depth, rows, n), F32),
        compiler_params=_cparams("arbitrary", "arbitrary"),
        name="ada_mod",
    )(c_pad, ada_w, ada_b.reshape(depth, 1, n))


def _inproj_kernel(first, x_ref, lng_ref, lnb_ref, sc_ref, sh_ref, w_ref, gg_ref, gb_ref, wcat_ref, bs_ref,
                   nb_ref, *outs):
    if first:
        xn_ref, q_ref, k_ref, v_ref, kf_ref, vf_ref, gv_ref, bn_ref = outs
    else:
        q_ref, k_ref, v_ref, kf_ref, vf_ref, gv_ref, bn_ref = outs
    x = x_ref[...]
    if first:
        x = _ln(x, lng_ref[...], lnb_ref[...])
        xn_ref[...] = x
    h = _modulate(x, sc_ref, sh_ref).astype(BF16)

    q = _dot(h, w_ref[:, 0:D_A])
    q_ref[...] = (q * ATTN_SCALE).astype(BF16)
    k = _dot(h, w_ref[:, D_A:2 * D_A])
    k_ref[...] = k.astype(BF16)
    kf_ref[...] = k
    v = _dot(h, w_ref[:, 2 * D_A:3 * D_A])
    v_ref[...] = v.astype(BF16)
    vf_ref[...] = v

    u = jax.nn.gelu(_dot(h, w_ref[:, 3 * D_A:3 * D_A + D_B]), approximate=True)
    vb = jax.nn.gelu(_dot(h, w_ref[:, 3 * D_A + D_B:3 * D_A + 2 * D_B]), approximate=True)
    vn = _ln(vb, gg_ref[...], gb_ref[...])
    gv_ref[...] = vn

    lane = lax.broadcasted_iota(jnp.int32, (MLP_CHUNK, LANES), 1)
    lo = lane < GROUP_DIM
    vnb = vn.astype(BF16)
    zero = jnp.zeros((MLP_CHUNK, LANES), BF16)
    rows = []
    for c in range(ROW_TILE // MLP_CHUNK):
        r0 = c * MLP_CHUNK
        cols = []
        for p in range(D_B // LANES):
            slab = vnb[r0:r0 + MLP_CHUNK, p * LANES:(p + 1) * LANES]
            rhs = jnp.concatenate([jnp.where(lo, slab, zero), jnp.where(lo, zero, slab)], axis=0)
            mixed = _dot(wcat_ref[0, p], rhs) + bs_ref[0, :, p * LANES:(p + 1) * LANES]
            cols.append(u[r0:r0 + MLP_CHUNK, p * LANES:(p + 1) * LANES] * mixed)
        rows.append(jnp.concatenate(cols, axis=1))
    bo = jnp.concatenate(rows, axis=0)
    bn_ref[...] = _rms(bo, nb_ref[...]).astype(BF16)


def _inproj_call(first, x, ln_g, ln_b, sc, sh, w_bf, gg, gb, wcat, bs, nb, *, n_prompt_tiles, tiles_per_batch,
                 n_batch, n_sample_rows):
    t, d = x.shape
    nt = t // ROW_TILE
    ntp = n_prompt_tiles
    n_tail = n_batch + (nt - ntp)

    def tile(i):
        return (i, 0)

    def const2(i):
        return (0, 0)

    def modmap(i):
        return (i, 0, 0)

    def selmap4(i):
        return (jnp.where(i >= ntp, 1, 0), 0, 0, 0)

    def selmap3(i):
        return (jnp.where(i >= ntp, 1, 0), 0, 0)

    def tailmap(i):
        return (jnp.where(i < ntp, i // tiles_per_batch, n_batch + i - ntp), 0)

    def gvmap(i):
        return (jnp.maximum(i - ntp, 0), 0)

    in_specs = [
        pl.BlockSpec((ROW_TILE, d), tile),
        pl.BlockSpec((1, d), const2),
        pl.BlockSpec((1, d), const2),
        pl.BlockSpec((CHUNKS_PER_TILE, 1, d), modmap),
        pl.BlockSpec((CHUNKS_PER_TILE, 1, d), modmap),
        pl.BlockSpec(w_bf.shape, const2),
        pl.BlockSpec((1, D_B), const2),
        pl.BlockSpec((1, D_B), const2),
        pl.BlockSpec((1,) + wcat.shape[1:], selmap4),
        pl.BlockSpec((1, MLP_CHUNK, D_B), selmap3),
        pl.BlockSpec((1, D_B), const2),
    ]
    out_specs = [
        pl.BlockSpec((ROW_TILE, D_A), tile),
        pl.BlockSpec((ROW_TILE, D_A), tile),
        pl.BlockSpec((ROW_TILE, D_A), tile),
        pl.BlockSpec((ROW_TILE, D_A), tailmap),
        pl.BlockSpec((ROW_TILE, D_A), tailmap),
        pl.BlockSpec((ROW_TILE, D_B), gvmap),
        pl.BlockSpec((ROW_TILE, D_B), tile),
    ]
    out_shape = [
        jax.ShapeDtypeStruct((t, D_A), BF16),
        jax.ShapeDtypeStruct((t, D_A), BF16),
        jax.ShapeDtypeStruct((t, D_A), BF16),
        jax.ShapeDtypeStruct((n_tail * ROW_TILE, D_A), F32),
        jax.ShapeDtypeStruct((n_tail * ROW_TILE, D_A), F32),
        jax.ShapeDtypeStruct((n_sample_rows, D_B), F32),
        jax.ShapeDtypeStruct((t, D_B), BF16),
    ]
    if first:
        out_specs = [pl.BlockSpec((ROW_TILE, d), tile)] + out_specs
        out_shape = [jax.ShapeDtypeStruct((t, d), F32)] + out_shape
    return pl.pallas_call(
        functools.partial(_inproj_kernel, first),
        grid=(nt,),
        in_specs=in_specs,
        out_specs=out_specs,
        out_shape=out_shape,
        compiler_params=_cparams("arbitrary"),
        name="in_proj_first" if first else "in_proj",
    )(x, ln_g, ln_b, sc, sh, w_bf, gg, gb, wcat, bs, nb)


def _attend(q, parts, bias_of, col_floor, na):
    r = q.shape[0]
    lane = lax.broadcasted_iota(jnp.int32, (1, LANES), 1)
    lo = lane < HEAD_DIM
    outs = []
    for p in range(D_A // LANES):
        sl = slice(p * LANES, (p + 1) * LANES)
        qp = q[:, sl]
        kps = [k[:, sl] for k, _ in parts]
        vps = [v[:, sl] for _, v in parts]
        o_pair = jnp.zeros((r, LANES), F32)
        for half in range(2):
            h = 2 * p + half
            keep = lo if half == 0 else jnp.logical_not(lo)
            qh = jnp.where(keep, qp, jnp.zeros_like(qp))
            ss = []
            for j, kp in enumerate(kps):
                s = lax.dot_general(qh, kp, (((1,), (1,)), ((), ())), preferred_element_type=F32)
                s = s + bias_of(h, j)
                if col_floor is not None:
                    s = jnp.where(col_floor[j], s, NEG)
                ss.append(s)
            m = ss[0].max(axis=1, keepdims=True)
            for s in ss[1:]:
                m = jnp.maximum(m, s.max(axis=1, keepdims=True))
            acc = jnp.zeros((r, LANES), F32)
            l = jnp.zeros((r, 1), F32)
            for s, vp in zip(ss, vps):
                e = jnp.exp(s - m)
                l = l + e.sum(axis=1, keepdims=True)
                vh = jnp.where(keep, vp, jnp.zeros_like(vp))
                acc = acc + _dot(e.astype(BF16), vh)
            o_pair = o_pair + acc / l
        outs.append(o_pair)
    a = jnp.concatenate(outs, axis=1)
    return _rms(a, na)


def _attn_prompt_kernel(q_ref, k0_ref, k1_ref, k2_ref, v0_ref, v1_ref, v2_ref, bias_ref, na_ref, o_ref):
    j = pl.program_id(1)
    col = lax.broadcasted_iota(jnp.int32, (1, Q_TILE), 1)
    floors = [(j - 2 + b) * Q_TILE + col >= 0 for b in range(3)]
    parts = [(k0_ref[...], v0_ref[...]), (k1_ref[...], v1_ref[...]), (k2_ref[...], v2_ref[...])]

    def bias_of(h, b):
        return bias_ref[h, :, b * Q_TILE:(b + 1) * Q_TILE]

    o_ref[...] = _attend(q_ref[...], parts, bias_of, floors, na_ref[...]).astype(BF16)


def _attn_prompt_call(q, k, v, bias, na, *, n_batch, q_tiles_per_batch):
    nq = q_tiles_per_batch

    def qmap(b, j):
        return (b * nq + j, 0)

    def kmap(off):
        return lambda b, j: (b * nq + jnp.maximum(j - off, 0), 0)

    blk = pl.BlockSpec((Q_TILE, D_A), qmap)
    kv_specs = [pl.BlockSpec((Q_TILE, D_A), kmap(off)) for off in (2, 1, 0)]
    return pl.pallas_call(
        _attn_prompt_kernel,
        grid=(n_batch, nq),
        in_specs=[blk] + kv_specs + kv_specs + [
            pl.BlockSpec(bias.shape, lambda b, j: (0, 0, 0)),
            pl.BlockSpec((1, D_A), lambda b, j: (0, 0)),
        ],
        out_specs=blk,
        out_shape=jax.ShapeDtypeStruct((n_batch * nq * Q_TILE, D_A), BF16),
        compiler_params=_cparams("arbitrary", "arbitrary"),
        name="attn_prompt",
    )(q, k, k, k, v, v, v, bias, na)


def _attn_sample_kernel(q_ref, kn_ref, vn_ref, ck_ref, cv_ref, bias_c_ref, bias_n_ref, na_ref, o_ref):
    parts = [(ck_ref[0].astype(BF16), cv_ref[0].astype(BF16)), (kn_ref[...], vn_ref[...])]

    def bias_of(h, b):
        return bias_c_ref[h] if b == 0 else bias_n_ref[h]

    o_ref[...] = _attend(q_ref[...], parts, bias_of, None, na_ref[...]).astype(BF16)


def _attn_sample_call(q, k, v, ck, cv, bias_c, bias_n, na, *, first_chunk):
    nb, win, _ = ck.shape
    blk = pl.BlockSpec((CHUNK, D_A), lambda b: (first_chunk + b, 0))
    cache = pl.BlockSpec((1, win, D_A), lambda b: (b, 0, 0))
    return pl.pallas_call(
        _attn_sample_kernel,
        grid=(nb,),
        in_specs=[blk, blk, blk, cache, cache,
                  pl.BlockSpec(bias_c.shape, lambda b: (0, 0, 0)),
                  pl.BlockSpec(bias_n.shape, lambda b: (0, 0, 0)),
                  pl.BlockSpec((1, D_A), lambda b: (0, 0))],
        out_specs=pl.BlockSpec((CHUNK, D_A), lambda b: (b, 0)),
        out_shape=jax.ShapeDtypeStruct((nb * CHUNK, D_A), BF16),
        compiler_params=_cparams("arbitrary"),
        name="attn_sample",
    )(q, k, v, ck, cv, bias_c, bias_n, na)


def _pack_bf16_pair(x):
    n = x.shape[1] // 2
    lo = pltpu.bitcast(x[:, :n].astype(BF16).astype(F32), jnp.uint32)
    hi = pltpu.bitcast(x[:, n:].astype(BF16).astype(F32), jnp.uint32)
    return hi | (lo >> 16)


def _unpack_bf16_pair(u):
    lo = pltpu.bitcast(u << 16, F32).astype(BF16)
    hi = pltpu.bitcast(u & jnp.uint32(0xFFFF0000), F32).astype(BF16)
    return lo, hi


def _outproj_kernel(route, ntp, x_ref, anp_ref, ans_ref, bn_ref, w_ref, g1_ref, sc_ref, sh_ref, lg_ref, lb_ref, *rest):
    if route:
        rh_ref, rl_ref, x1_ref, hp_ref, rt_ref = rest
    else:
        x1_ref, hp_ref = rest
    an = jnp.where(pl.program_id(0) >= ntp, ans_ref[...], anp_ref[...])
    mix = _dot(an, w_ref[0:D_A, :]) + _dot(bn_ref[...], w_ref[D_A:D_A + D_B, :])
    x1 = _ln(ALPHA * x_ref[...] + _gate_rows(mix, g1_ref), lg_ref[...], lb_ref[...])
    x1_ref[...] = x1
    h2 = _modulate(x1, sc_ref, sh_ref)
    hp_ref[...] = _pack_bf16_pair(h2)
    if route:
        h_hi, h_lo = _split_bf16(h2)
        logits = _dot(h_hi, rh_ref[...]) + _dot(h_lo, rh_ref[...]) + _dot(h_hi, rl_ref[...])
        lane = lax.broadcasted_iota(jnp.int32, logits.shape, 1)
        logits = jnp.where(lane < N_EXPERTS, logits, -jnp.inf)
        m1 = logits.max(axis=1, keepdims=True)
        i1 = jnp.where(logits == m1, lane, LANES).min(axis=1, keepdims=True)
        rest_l = jnp.where(lane == i1, -jnp.inf, logits)
        m2 = rest_l.max(axis=1, keepdims=True)
        i2 = jnp.where(rest_l == m2, lane, LANES).min(axis=1, keepdims=True)
        e2 = jnp.exp(m2 - m1)
        w1 = 1.0 / (1.0 + e2)
        w2 = e2 / (1.0 + e2)
        rt = jnp.where(lane == 0, i1.astype(F32),
                       jnp.where(lane == 1, i2.astype(F32),
                                 jnp.where(lane == 2, w1, jnp.where(lane == 3, w2, 0.0))))
        rt_ref[...] = rt


def _outproj_call(route, x, an_p, an_s, bn, w_bf, g1, sc, sh, lg, lb, r_hi=None, r_lo=None):
    t, d = x.shape
    nt = t // ROW_TILE
    ntp = an_p.shape[0] // ROW_TILE

    def tile(i):
        return (i, 0)

    def const2(i):
        return (0, 0)

    def modmap(i):
        return (i, 0, 0)

    mod = pl.BlockSpec((CHUNKS_PER_TILE, 1, d), modmap)
    in_specs = [
        pl.BlockSpec((ROW_TILE, d), tile),
        pl.BlockSpec((ROW_TILE, D_A), lambda i: (jnp.minimum(i, ntp - 1), 0)),
        pl.BlockSpec((ROW_TILE, D_A), lambda i: (jnp.maximum(i - ntp, 0), 0)),
        pl.BlockSpec((ROW_TILE, D_B), tile),
        pl.BlockSpec(w_bf.shape, const2),
        mod, mod, mod,
        pl.BlockSpec((1, d), const2),
        pl.BlockSpec((1, d), const2),
    ]
    out_specs = [pl.BlockSpec((ROW_TILE, d), tile), pl.BlockSpec((ROW_TILE, d // 2), tile)]
    out_shape = [jax.ShapeDtypeStruct((t, d), F32), jax.ShapeDtypeStruct((t, d // 2), jnp.uint32)]
    args = [x, an_p, an_s, bn, w_bf, g1, sc, sh, lg, lb]
    if route:
        in_specs += [pl.BlockSpec(r_hi.shape, const2), pl.BlockSpec(r_lo.shape, const2)]
        out_specs.append(pl.BlockSpec((ROW_TILE, LANES), tile))
        out_shape.append(jax.ShapeDtypeStruct((t, LANES), F32))
        args += [r_hi, r_lo]
    return pl.pallas_call(
        functools.partial(_outproj_kernel, route, ntp),
        grid=(nt,),
        in_specs=in_specs,
        out_specs=out_specs,
        out_shape=out_shape,
        compiler_params=_cparams("arbitrary"),
        name="out_proj_route" if route else "out_proj",
    )(*args)


def _ffn_kernel(te_ref, tv_ref, x_ref, wg_ref, wu_ref, wd_ref, o_ref, xb_ref):
    del te_ref
    i = pl.program_id(0)
    j = pl.program_id(1)
    valid = tv_ref[i]
    nsub = (valid + FFN_SUB - 1) // FFN_SUB
    half = x_ref.shape[1]

    @pl.when(j == 0)
    def _():
        def unpack(s, carry):
            rows = pl.ds(pl.multiple_of(s * FFN_SUB, FFN_SUB), FFN_SUB)
            lo, hi = _unpack_bf16_pair(x_ref[rows, :])
            rid = s * FFN_SUB + lax.broadcasted_iota(jnp.int32, (FFN_SUB, 1), 0)
            ok = rid < valid
            xb_ref[rows, 0:half] = jnp.where(ok, lo, jnp.zeros_like(lo))
            xb_ref[rows, half:2 * half] = jnp.where(ok, hi, jnp.zeros_like(hi))
            return carry
        lax.fori_loop(0, nsub, unpack, 0)

    @pl.when(valid > 0)
    def _():
        wg = wg_ref[0].astype(BF16)
        wu = wu_ref[0].astype(BF16)
        wd = wd_ref[0].astype(BF16)

        def step(s, carry):
            rows = pl.ds(pl.multiple_of(s * FFN_SUB, FFN_SUB), FFN_SUB)
            xs = xb_ref[rows, :]
            g = _dot(xs, wg)
            u = _dot(xs, wu)
            a = (g * jax.nn.sigmoid(g) * u).astype(BF16)
            dlt = _dot(a, wd)

            @pl.when(j == 0)
            def _():
                o_ref[rows, :] = dlt

            @pl.when(j > 0)
            def _():
                o_ref[rows, :] += dlt
            return carry
        lax.fori_loop(0, nsub, step, 0)

    @pl.when(j == pl.num_programs(1) - 1)
    def _():
        def clear(s, carry):
            rows = pl.ds(pl.multiple_of(s * FFN_SUB, FFN_SUB), FFN_SUB)
            o_ref[rows, :] = jnp.zeros((FFN_SUB, o_ref.shape[1]), F32)
            return carry
        lax.fori_loop(nsub, FFN_TILE // FFN_SUB, clear, 0)


def _ffn_call(tile_expert, tile_valid, xs, wg, wu, wd):
    p, half = xs.shape
    d = 2 * half
    f = wg.shape[2]
    nt = p // FFN_TILE
    nc = f // FFN_COLS

    def active_col(i, j, tv):
        return jnp.where(tv[i] > 0, j, nc - 1)

    return pl.pallas_call(
        _ffn_kernel,
        grid_spec=pltpu.PrefetchScalarGridSpec(
            num_scalar_prefetch=2,
            grid=(nt, nc),
            in_specs=[
                pl.BlockSpec((FFN_TILE, half), lambda i, j, te, tv: (i, 0)),
                pl.BlockSpec((1, d, FFN_COLS), lambda i, j, te, tv: (te[i], 0, active_col(i, j, tv))),
                pl.BlockSpec((1, d, FFN_COLS), lambda i, j, te, tv: (te[i], 0, active_col(i, j, tv))),
                pl.BlockSpec((1, FFN_COLS, d), lambda i, j, te, tv: (te[i], active_col(i, j, tv), 0)),
            ],
            out_specs=pl.BlockSpec((FFN_TILE, d), lambda i, j, te, tv: (i, 0)),
            scratch_shapes=[pltpu.VMEM((FFN_TILE, d), BF16)],
        ),
        out_shape=jax.ShapeDtypeStruct((p, d), F32),
        compiler_params=_cparams("arbitrary", "arbitrary"),
        name="swiglu_grouped",
    )(tile_expert, tile_valid, xs, wg, wu, wd)


def _dispatch_kernel(pos_ref, src_ref, init_ref, dst_ref, sem):
    del init_ref
    i = pl.program_id(0)

    def issue(r, carry):
        row = src_ref.at[pl.ds(i * ROW_TILE + r, 1), :]
        pltpu.make_async_copy(row, dst_ref.at[pl.ds(pos_ref[0, 2 * r], 1), :], sem).start()
        pltpu.make_async_copy(row, dst_ref.at[pl.ds(pos_ref[0, 2 * r + 1], 1), :], sem).start()
        return carry
    lax.fori_loop(0, ROW_TILE, issue, 0, unroll=8)
    for _ in range(2):
        pltpu.make_async_copy(src_ref.at[pl.ds(0, ROW_TILE), :], dst_ref.at[pl.ds(0, ROW_TILE), :], sem).wait()


def _dispatch_call(pos_tiles, hp, n_rows_sorted):
    t, half = hp.shape
    nt = t // ROW_TILE
    return pl.pallas_call(
        _dispatch_kernel,
        grid=(nt,),
        in_specs=[
            pl.BlockSpec((None, 1, 2 * ROW_TILE), lambda i: (i, 0, 0), memory_space=pltpu.SMEM),
            pl.BlockSpec(memory_space=pl.ANY),
            pl.BlockSpec(memory_space=pl.ANY),
        ],
        out_specs=pl.BlockSpec(memory_space=pl.ANY),
        out_shape=jax.ShapeDtypeStruct((n_rows_sorted, half), hp.dtype),
        scratch_shapes=[pltpu.SemaphoreType.DMA(())],
        input_output_aliases={2: 0},
        compiler_params=_cparams("arbitrary"),
        name="dispatch_rows",
    )(pos_tiles, hp, jnp.zeros((n_rows_sorted, half), hp.dtype))


def _final_dense_kernel(x_ref, f_ref, g2_ref, lg_ref, lb_ref, o_ref):
    o_ref[...] = _ln(ALPHA * x_ref[...] + _gate_rows(f_ref[...], g2_ref), lg_ref[...], lb_ref[...])


def _final_dense_call(x1, f, g2, lg, lb):
    t, d = x1.shape
    tile = pl.BlockSpec((ROW_TILE, d), lambda i: (i, 0))
    vec = pl.BlockSpec((1, d), lambda i: (0, 0))
    return pl.pallas_call(
        _final_dense_kernel,
        grid=(t // ROW_TILE,),
        in_specs=[tile, tile, pl.BlockSpec((CHUNKS_PER_TILE, 1, d), lambda i: (i, 0, 0)), vec, vec],
        out_specs=tile,
        out_shape=jax.ShapeDtypeStruct((t, d), F32),
        compiler_params=_cparams("arbitrary"),
        name="final_dense",
    )(x1, f, g2, lg, lb)


def _final_moe_kernel(pos_ref, x_ref, rt_ref, g2_ref, lg_ref, lb_ref, ys_ref, o_ref, ybuf, sem):
    def issue(r, carry):
        pltpu.make_async_copy(ys_ref.at[pl.ds(pos_ref[0, 2 * r], 1), :], ybuf.at[0, pl.ds(r, 1), :], sem).start()
        pltpu.make_async_copy(ys_ref.at[pl.ds(pos_ref[0, 2 * r + 1], 1), :], ybuf.at[1, pl.ds(r, 1), :], sem).start()
        return carry
    lax.fori_loop(0, ROW_TILE, issue, 0, unroll=8)
    for k in range(2):
        pltpu.make_async_copy(ys_ref.at[pl.ds(0, ROW_TILE), :], ybuf.at[k], sem).wait()
    rt = rt_ref[...]
    f = rt[:, 2:3] * ybuf[0] + rt[:, 3:4] * ybuf[1]
    o_ref[...] = _ln(ALPHA * x_ref[...] + _gate_rows(f, g2_ref), lg_ref[...], lb_ref[...])


def _final_moe_call(pos_tiles, x1, rt, g2, lg, lb, ys):
    t, d = x1.shape
    tile = pl.BlockSpec((ROW_TILE, d), lambda i: (i, 0))
    vec = pl.BlockSpec((1, d), lambda i: (0, 0))
    return pl.pallas_call(
        _final_moe_kernel,
        grid=(t // ROW_TILE,),
        in_specs=[
            pl.BlockSpec((None, 1, 2 * ROW_TILE), lambda i: (i, 0, 0), memory_space=pltpu.SMEM),
            tile,
            pl.BlockSpec((ROW_TILE, LANES), lambda i: (i, 0)),
            pl.BlockSpec((CHUNKS_PER_TILE, 1, d), lambda i: (i, 0, 0)),
            vec, vec,
            pl.BlockSpec(memory_space=pl.ANY),
        ],
        out_specs=tile,
        out_shape=jax.ShapeDtypeStruct((t, d), F32),
        scratch_shapes=[pltpu.VMEM((2, ROW_TILE, d), F32), pltpu.SemaphoreType.DMA(())],
        compiler_params=_cparams("arbitrary"),
        name="final_moe",
    )(pos_tiles, x1, rt, g2, lg, lb, ys)


def _rel_bias(table, n_q, n_k, banded):
    r = jnp.arange(n_q)[:, None]
    w = jnp.arange(n_k)[None, :]
    dist = LEFT_CTX + r - w
    bias = table[:, jnp.clip(dist, -REL_CLIP, REL_CLIP) + REL_CLIP].astype(F32)
    if banded:
        qa = r // CHUNK
        kc = w // CHUNK
        vis = (kc >= qa) & (kc <= qa + LEFT_CHUNKS)
        bias = jnp.where(vis[None], bias, NEG)
    return bias


def _gating_weights(w_s, b_s):
    n = MLP_CHUNK
    tril = jnp.tril(jnp.ones((n, n), bool))
    wm = jnp.where(tril[None], w_s, 0.0)
    h = CHUNK
    top = wm[:, :h, :h]
    z = jnp.zeros_like(top)
    wm_s = jnp.concatenate([jnp.concatenate([top, z], 2), jnp.concatenate([z, top], 2)], 1)
    both = jnp.stack([wm, wm_s])
    wcat = jnp.concatenate([both[:, 0::2], both[:, 1::2]], axis=-1)
    bias_p = jnp.repeat(jnp.transpose(b_s), GROUP_DIM, axis=1)
    bias_s = jnp.concatenate([bias_p[:h], bias_p[:h]], 0)
    return wcat.astype(BF16), jnp.stack([bias_p, bias_s]).astype(F32)


def _route_plan(rt, n_tiles):
    t = rt.shape[0]
    e_flat = rt[:, 0:2].astype(jnp.int32).reshape(-1)
    oh = (e_flat[:, None] == jnp.arange(N_EXPERTS)[None, :]).astype(jnp.int32)
    csum = jnp.cumsum(oh, axis=0)
    rank = jnp.sum(oh * (csum - 1), axis=1)
    counts = csum[-1]
    tiles_e = (counts + FFN_TILE - 1) // FFN_TILE
    tile_end = jnp.cumsum(tiles_e)
    tile_start = tile_end - tiles_e
    pos = (tile_start * FFN_TILE)[e_flat] + rank
    ti = jnp.arange(n_tiles)
    te = jnp.minimum(jnp.sum((ti[:, None] >= tile_end[None, :]).astype(jnp.int32), axis=1), N_EXPERTS - 1)
    tv = jnp.clip(counts[te] - (ti - tile_start[te]) * FFN_TILE, 0, FFN_TILE)
    tv = jnp.where(ti < tile_end[-1], tv, 0)
    last_e = te[jnp.maximum(tile_end[-1] - 1, 0)]
    te = jnp.where(ti < tile_end[-1], te, last_e)
    return pos.reshape(t // ROW_TILE, 1, 2 * ROW_TILE).astype(jnp.int32), te.astype(jnp.int32), tv.astype(jnp.int32)


def kernel(x_prompt, x_sample, cache_k, cache_v, c_prompt, c_sample, ln_in_g, ln_in_b, w_in, w_out, rel_bias_table, gmlp_ln_g, gmlp_ln_b, gmlp_w_s, gmlp_b_s, out_norm_a, out_norm_b, ada_w, ada_b, ln1_g, ln1_b, ln2_g, ln2_b, ffn_w_gate, ffn_w_up, ffn_w_down, moe_router, moe_w_gate, moe_w_up, moe_w_down):
    nb, seq, d = x_prompt.shape
    ns, dseq, _ = x_sample.shape
    depth = w_in.shape[0]
    tp = nb * seq
    ts = ns * dseq
    t = tp + ts
    assert depth == DEPTH and dseq == CHUNK and seq % ROW_TILE == 0 and ts % ROW_TILE == 0
    assert cache_k.shape[2] == LEFT_CTX
    ntp = tp // ROW_TILE
    n_chunks = t // CHUNK

    x = jnp.concatenate([x_prompt.reshape(tp, d), x_sample.reshape(ts, d)], axis=0)

    n_cond = nb + ns
    c_all = jnp.concatenate([c_prompt, c_sample], axis=0)
    c_pad = jnp.pad(c_all, ((0, (-n_cond) % 8), (0, 0)))
    mods = _ada_call(c_pad, ada_w, ada_b)
    cpb = seq // CHUNK
    mods_p = jnp.broadcast_to(mods[:, :nb, None, :], (depth, nb, cpb, 6 * d)).reshape(depth, nb * cpb, 6 * d)
    mods = jnp.concatenate([mods_p, mods[:, nb:n_cond]], axis=1).reshape(depth, n_chunks, 6, 1, d)

    def row(v):
        return v.reshape(1, -1)

    k_tail, v_tail, gv_rows = [], [], []
    for l in range(depth):
        sh1, sc1, g1, sh2, sc2, g2 = (mods[l, :, m] for m in range(6))
        w_in_bf = w_in[l].astype(BF16)
        w_out_bf = w_out[l].astype(BF16)
        wcat, bs = _gating_weights(gmlp_w_s[l], gmlp_b_s[l])
        res = _inproj_call(l == 0, x, row(ln_in_g), row(ln_in_b), sc1, sh1, w_in_bf, row(gmlp_ln_g[l]),
                           row(gmlp_ln_b[l]), wcat, bs, row(out_norm_b[l]),
                           n_prompt_tiles=ntp, tiles_per_batch=seq // ROW_TILE, n_batch=nb, n_sample_rows=ts)
        if l == 0:
            x, q, k, v, kf, vf, gv, bn = res
        else:
            q, k, v, kf, vf, gv, bn = res
        k_tail.append(kf)
        v_tail.append(vf)
        gv_rows.append(gv)

        table = rel_bias_table[l]
        bias_p = _rel_bias(table, Q_TILE, 3 * Q_TILE, True)
        bias_s = _rel_bias(table, CHUNK, LEFT_CTX + CHUNK, False)
        na = row(out_norm_a[l])
        an_p = _attn_prompt_call(q, k, v, bias_p, na, n_batch=nb, q_tiles_per_batch=seq // Q_TILE)
        an_s = _attn_sample_call(q, k, v, cache_k[l].reshape(ns, LEFT_CTX, D_A), cache_v[l].reshape(ns, LEFT_CTX, D_A),
                               bias_s[:, :, :LEFT_CTX], bias_s[:, :, LEFT_CTX:], na, first_chunk=tp // CHUNK)

        lg1, lb1, lg2, lb2 = row(ln1_g[l]), row(ln1_b[l]), row(ln2_g[l]), row(ln2_b[l])
        if l % 2 == 0:
            x1, hp = _outproj_call(False, x, an_p, an_s, bn, w_out_bf, g1, sc2, sh2, lg1, lb1)
            nt = t // FFN_TILE
            te = jnp.zeros((nt,), jnp.int32)
            tv = jnp.full((nt,), FFN_TILE, jnp.int32)
            i = l // 2
            f = _ffn_call(te, tv, hp, ffn_w_gate[i:i + 1], ffn_w_up[i:i + 1], ffn_w_down[i:i + 1])
            x = _final_dense_call(x1, f, g2, lg2, lb2)
        else:
            i = l // 2
            r_pad = jnp.pad(moe_router[i], ((0, 0), (0, LANES - N_EXPERTS)))
            r_hi = r_pad.astype(BF16)
            r_lo = (r_pad - r_hi.astype(F32)).astype(BF16)
            x1, hp, rt = _outproj_call(True, x, an_p, an_s, bn, w_out_bf, g1, sc2, sh2, lg1, lb1, r_hi, r_lo)
            n_tiles = (2 * t) // FFN_TILE + N_EXPERTS
            pos, te, tv = _route_plan(rt, n_tiles)
            xs = _dispatch_call(pos, hp, n_tiles * FFN_TILE)
            ys = _ffn_call(te, tv, xs, moe_w_gate[i], moe_w_up[i], moe_w_down[i])
            x = _final_moe_call(pos, x1, rt, g2, lg2, lb2, ys)

    y_prompt = x[:tp].reshape(nb, seq, d)
    y_sample = x[tp:].reshape(ns, dseq, d)

    def tails(rows):
        kp = jnp.stack([r[:nb * ROW_TILE].reshape(nb, ROW_TILE, N_HEADS, HEAD_DIM) for r in rows])
        ksn = jnp.stack([r[nb * ROW_TILE:].reshape(ns, dseq, N_HEADS, HEAD_DIM) for r in rows])
        return kp, ksn

    k_prompt_new, k_sample_new = tails(k_tail)
    v_prompt_new, v_sample_new = tails(v_tail)
    gmlp_v_sample_new = jnp.stack([g.reshape(ns, dseq, D_B) for g in gv_rows])
    return (y_prompt, y_sample, k_prompt_new, v_prompt_new, k_sample_new, v_sample_new, gmlp_v_sample_new)
```

```python
import functools

import jax
import jax.numpy as jnp
from jax import lax
from jax.experimental import pallas as pl
from jax.experimental.pallas import tpu as pltpu

CHUNK = 64
LEFT_CHUNKS = 8
LEFT_CTX = LEFT_CHUNKS * CHUNK
N_HEADS = 8
HEAD_DIM = 64
D_A = N_HEADS * HEAD_DIM
N_GROUPS = 8
GROUP_DIM = 64
D_B = N_GROUPS * GROUP_DIM
MLP_CHUNK = 128
REL_CLIP = 128
N_EXPERTS = 8
DEPTH = 2
ALPHA = (2 * DEPTH) ** 0.25
LN_EPS = 1e-5
ATTN_SCALE = HEAD_DIM ** -0.5
NEG = -1e30

LANES = 128
ROW_TILE = 512
CHUNKS_PER_TILE = ROW_TILE // CHUNK
Q_TILE = 256
FFN_TILE = 1024
FFN_SUB = 512
FFN_COLS = 256
VMEM_LIMIT = 48 * 1024 * 1024

BF16 = jnp.bfloat16
F32 = jnp.float32


def _cparams(*sem):
    return pltpu.CompilerParams(dimension_semantics=sem, vmem_limit_bytes=VMEM_LIMIT)


def _ln(x, g, b):
    mu = jnp.mean(x, axis=-1, keepdims=True)
    xc = x - mu
    var = jnp.mean(xc * xc, axis=-1, keepdims=True)
    return xc * lax.rsqrt(var + LN_EPS) * g + b


def _rms(x, g):
    return x * lax.rsqrt(jnp.mean(x * x, axis=-1, keepdims=True) + LN_EPS) * g


def _modulate(x, sc_ref, sh_ref):
    parts = []
    for c in range(CHUNKS_PER_TILE):
        xc = x[c * CHUNK:(c + 1) * CHUNK]
        parts.append(xc * (1.0 + sc_ref[c]) + sh_ref[c])
    return jnp.concatenate(parts, axis=0)


def _gate_rows(x, g_ref):
    parts = []
    for c in range(CHUNKS_PER_TILE):
        parts.append(x[c * CHUNK:(c + 1) * CHUNK] * (1.0 + g_ref[c]))
    return jnp.concatenate(parts, axis=0)


def _split_bf16(x):
    hi = x.astype(BF16)
    lo = (x - hi.astype(F32)).astype(BF16)
    return hi, lo


def _dot(a, b):
    return jnp.dot(a, b, preferred_element_type=F32)


def _ada_kernel(c_ref, w_ref, b_ref, o_ref):
    c = c_ref[...]
    s = c * jax.nn.sigmoid(c)
    s_hi, s_lo = _split_bf16(s)
    w_hi, w_lo = _split_bf16(w_ref[0])
    o_ref[0] = _dot(s_hi, w_hi) + _dot(s_lo, w_hi) + _dot(s_hi, w_lo) + b_ref[0]


def _ada_call(c_pad, ada_w, ada_b):
    depth, d, n = ada_w.shape
    rows = c_pad.shape[0]
    tn = n // 4
    return pl.pallas_call(
        _ada_kernel,
        grid=(depth, n // tn),
        in_specs=[
            pl.BlockSpec((rows, d), lambda l, j: (0, 0)),
            pl.BlockSpec((1, d, tn), lambda l, j: (l, 0, j)),
            pl.BlockSpec((1, 1, tn), lambda l, j: (l, 0, j)),
        ],
        out_specs=pl.BlockSpec((1, rows, tn), lambda l, j: (l, 0, j)),
        out_shape=jax.ShapeDtypeStruct((depth, rows, n), F32),
        compiler_params=_cparams("arbitrary", "arbitrary"),
        name="ada_mod",
    )(c_pad, ada_w, ada_b.reshape(depth, 1, n))


def _inproj_kernel(first, x_ref, lng_ref, lnb_ref, sc_ref, sh_ref, w_ref, gg_ref, gb_ref, wcat_ref, bs_ref,
                   nb_ref, *outs):
    if first:
        xn_ref, q_ref, k_ref, v_ref, kf_ref, vf_ref, gv_ref, bn_ref = outs
    else:
        q_ref, k_ref, v_ref, kf_ref, vf_ref, gv_ref, bn_ref = outs
    x = x_ref[...]
    if first:
        x = _ln(x, lng_ref[...], lnb_ref[...])
        xn_ref[...] = x
    h = _modulate(x, sc_ref, sh_ref).astype(BF16)

    q = _dot(h, w_ref[:, 0:D_A])
    q_ref[...] = (q * ATTN_SCALE).astype(BF16)
    k = _dot(h, w_ref[:, D_A:2 * D_A])
    k_ref[...] = k.astype(BF16)
    kf_ref[...] = k
    v = _dot(h, w_ref[:, 2 * D_A:3 * D_A])
    v_ref[...] = v.astype(BF16)
    vf_ref[...] = v

    u = jax.nn.gelu(_dot(h, w_ref[:, 3 * D_A:3 * D_A + D_B]), approximate=True)
    vb = jax.nn.gelu(_dot(h, w_ref[:, 3 * D_A + D_B:3 * D_A + 2 * D_B]), approximate=True)
    vn = _ln(vb, gg_ref[...], gb_ref[...])
    gv_ref[...] = vn

    lane = lax.broadcasted_iota(jnp.int32, (MLP_CHUNK, LANES), 1)
    lo = lane < GROUP_DIM
    vnb = vn.astype(BF16)
    zero = jnp.zeros((MLP_CHUNK, LANES), BF16)
    rows = []
    for c in range(ROW_TILE // MLP_CHUNK):
        r0 = c * MLP_CHUNK
        cols = []
        for p in range(D_B // LANES):
            slab = vnb[r0:r0 + MLP_CHUNK, p * LANES:(p + 1) * LANES]
            rhs = jnp.concatenate([jnp.where(lo, slab, zero), jnp.where(lo, zero, slab)], axis=0)
            mixed = _dot(wcat_ref[0, p], rhs) + bs_ref[0, :, p * LANES:(p + 1) * LANES]
            cols.append(u[r0:r0 + MLP_CHUNK, p * LANES:(p + 1) * LANES] * mixed)
        rows.append(jnp.concatenate(cols, axis=1))
    bo = jnp.concatenate(rows, axis=0)
    bn_ref[...] = _rms(bo, nb_ref[...]).astype(BF16)


def _inproj_call(first, x, ln_g, ln_b, sc, sh, w_bf, gg, gb, wcat, bs, nb, *, n_prompt_tiles, tiles_per_batch,
                 n_batch, n_sample_rows):
    t, d = x.shape
    nt = t // ROW_TILE
    ntp = n_prompt_tiles
    n_tail = n_batch + (nt - ntp)

    def tile(i):
        return (i, 0)

    def const2(i):
        return (0, 0)

    def modmap(i):
        return (i, 0, 0)

    def selmap4(i):
        return (jnp.where(i >= ntp, 1, 0), 0, 0, 0)

    def selmap3(i):
        return (jnp.where(i >= ntp, 1, 0), 0, 0)

    def tailmap(i):
        return (jnp.where(i < ntp, i // tiles_per_batch, n_batch + i - ntp), 0)

    def gvmap(i):
        return (jnp.maximum(i - ntp, 0), 0)

    in_specs = [
        pl.BlockSpec((ROW_TILE, d), tile),
        pl.BlockSpec((1, d), const2),
        pl.BlockSpec((1, d), const2),
        pl.BlockSpec((CHUNKS_PER_TILE, 1, d), modmap),
        pl.BlockSpec((CHUNKS_PER_TILE, 1, d), modmap),
        pl.BlockSpec(w_bf.shape, const2),
        pl.BlockSpec((1, D_B), const2),
        pl.BlockSpec((1, D_B), const2),
        pl.BlockSpec((1,) + wcat.shape[1:], selmap4),
        pl.BlockSpec((1, MLP_CHUNK, D_B), selmap3),
        pl.BlockSpec((1, D_B), const2),
    ]
    out_specs = [
        pl.BlockSpec((ROW_TILE, D_A), tile),
        pl.BlockSpec((ROW_TILE, D_A), tile),
        pl.BlockSpec((ROW_TILE, D_A), tile),
        pl.BlockSpec((ROW_TILE, D_A), tailmap),
        pl.BlockSpec((ROW_TILE, D_A), tailmap),
        pl.BlockSpec((ROW_TILE, D_B), gvmap),
        pl.BlockSpec((ROW_TILE, D_B), tile),
    ]
    out_shape = [
        jax.ShapeDtypeStruct((t, D_A), BF16),
        jax.ShapeDtypeStruct((t, D_A), BF16),
        jax.ShapeDtypeStruct((t, D_A), BF16),
        jax.ShapeDtypeStruct((n_tail * ROW_TILE, D_A), F32),
        jax.ShapeDtypeStruct((n_tail * ROW_TILE, D_A), F32),
        jax.ShapeDtypeStruct((n_sample_rows, D_B), F32),
        jax.ShapeDtypeStruct((t, D_B), BF16),
    ]
    if first:
        out_specs = [pl.BlockSpec((ROW_TILE, d), tile)] + out_specs
        out_shape = [jax.ShapeDtypeStruct((t, d), F32)] + out_shape
    return pl.pallas_call(
        functools.partial(_inproj_kernel, first),
        grid=(nt,),
        in_specs=in_specs,
        out_specs=out_specs,
        out_shape=out_shape,
        compiler_params=_cparams("arbitrary"),
        name="in_proj_first" if first else "in_proj",
    )(x, ln_g, ln_b, sc, sh, w_bf, gg, gb, wcat, bs, nb)


def _attend(q, parts, bias_of, col_floor, na):
    r = q.shape[0]
    lane = lax.broadcasted_iota(jnp.int32, (1, LANES), 1)
    lo = lane < HEAD_DIM
    outs = []
    for p in range(D_A // LANES):
        sl = slice(p * LANES, (p + 1) * LANES)
        qp = q[:, sl]
        kps = [k[:, sl] for k, _ in parts]
        vps = [v[:, sl] for _, v in parts]
        o_pair = jnp.zeros((r, LANES), F32)
        for half in range(2):
            h = 2 * p + half
            keep = lo if half == 0 else jnp.logical_not(lo)
            qh = jnp.where(keep, qp, jnp.zeros_like(qp))
            ss = []
            for j, kp in enumerate(kps):
                s = lax.dot_general(qh, kp, (((1,), (1,)), ((), ())), preferred_element_type=F32)
                s = s + bias_of(h, j)
                if col_floor is not None:
                    s = jnp.where(col_floor[j], s, NEG)
                ss.append(s)
            m = ss[0].max(axis=1, keepdims=True)
            for s in ss[1:]:
                m = jnp.maximum(m, s.max(axis=1, keepdims=True))
            acc = jnp.zeros((r, LANES), F32)
            l = jnp.zeros((r, 1), F32)
            for s, vp in zip(ss, vps):
                e = jnp.exp(s - m)
                l = l + e.sum(axis=1, keepdims=True)
                vh = jnp.where(keep, vp, jnp.zeros_like(vp))
                acc = acc + _dot(e.astype(BF16), vh)
            o_pair = o_pair + acc / l
        outs.append(o_pair)
    a = jnp.concatenate(outs, axis=1)
    return _rms(a, na)


def _attn_prompt_kernel(q_ref, k0_ref, k1_ref, k2_ref, v0_ref, v1_ref, v2_ref, bias_ref, na_ref, o_ref):
    j = pl.program_id(1)
    col = lax.broadcasted_iota(jnp.int32, (1, Q_TILE), 1)
    floors = [(j - 2 + b) * Q_TILE + col >= 0 for b in range(3)]
    parts = [(k0_ref[...], v0_ref[...]), (k1_ref[...], v1_ref[...]), (k2_ref[...], v2_ref[...])]

    def bias_of(h, b):
        return bias_ref[h, :, b * Q_TILE:(b + 1) * Q_TILE]

    o_ref[...] = _attend(q_ref[...], parts, bias_of, floors, na_ref[...]).astype(BF16)


def _attn_prompt_call(q, k, v, bias, na, *, n_batch, q_tiles_per_batch):
    nq = q_tiles_per_batch

    def qmap(b, j):
        return (b * nq + j, 0)

    def kmap(off):
        return lambda b, j: (b * nq + jnp.maximum(j - off, 0), 0)

    blk = pl.BlockSpec((Q_TILE, D_A), qmap)
    kv_specs = [pl.BlockSpec((Q_TILE, D_A), kmap(off)) for off in (2, 1, 0)]
    return pl.pallas_call(
        _attn_prompt_kernel,
        grid=(n_batch, nq),
        in_specs=[blk] + kv_specs + kv_specs + [
            pl.BlockSpec(bias.shape, lambda b, j: (0, 0, 0)),
            pl.BlockSpec((1, D_A), lambda b, j: (0, 0)),
        ],
        out_specs=blk,
        out_shape=jax.ShapeDtypeStruct((n_batch * nq * Q_TILE, D_A), BF16),
        compiler_params=_cparams("arbitrary", "arbitrary"),
        name="attn_prompt",
    )(q, k, k, k, v, v, v, bias, na)


def _attn_sample_kernel(q_ref, kn_ref, vn_ref, ck_ref, cv_ref, bias_c_ref, bias_n_ref, na_ref, o_ref):
    parts = [(ck_ref[0].astype(BF16), cv_ref[0].astype(BF16)), (kn_ref[...], vn_ref[...])]

    def bias_of(h, b):
        return bias_c_ref[h] if b == 0 else bias_n_ref[h]

    o_ref[...] = _attend(q_ref[...], parts, bias_of, None, na_ref[...]).astype(BF16)


def _attn_sample_call(q, k, v, ck, cv, bias_c, bias_n, na, *, first_chunk):
    nb, win, _ = ck.shape
    blk = pl.BlockSpec((CHUNK, D_A), lambda b: (first_chunk + b, 0))
    cache = pl.BlockSpec((1, win, D_A), lambda b: (b, 0, 0))
    return pl.pallas_call(
        _attn_sample_kernel,
        grid=(nb,),
        in_specs=[blk, blk, blk, cache, cache,
                  pl.BlockSpec(bias_c.shape, lambda b: (0, 0, 0)),
                  pl.BlockSpec(bias_n.shape, lambda b: (0, 0, 0)),
                  pl.BlockSpec((1, D_A), lambda b: (0, 0))],
        out_specs=pl.BlockSpec((CHUNK, D_A), lambda b: (b, 0)),
        out_shape=jax.ShapeDtypeStruct((nb * CHUNK, D_A), BF16),
        compiler_params=_cparams("arbitrary"),
        name="attn_sample",
    )(q, k, v, ck, cv, bias_c, bias_n, na)


def _outproj_kernel(route, ntp, x_ref, anp_ref, ans_ref, bn_ref, w_ref, g1_ref, sc_ref, sh_ref, lg_ref, lb_ref, *rest):
    if route:
        rh_ref, rl_ref, x1_ref, hp_ref, rt_ref = rest
    else:
        x1_ref, hp_ref = rest
    an = jnp.where(pl.program_id(0) >= ntp, ans_ref[...], anp_ref[...])
    mix = _dot(an, w_ref[0:D_A, :]) + _dot(bn_ref[...], w_ref[D_A:D_A + D_B, :])
    x1 = _ln(ALPHA * x_ref[...] + _gate_rows(mix, g1_ref), lg_ref[...], lb_ref[...])
    x1_ref[...] = x1
    h2 = _modulate(x1, sc_ref, sh_ref)
    hp_ref[...] = h2
    if route:
        h_hi, h_lo = _split_bf16(h2)
        logits = _dot(h_hi, rh_ref[...]) + _dot(h_lo, rh_ref[...]) + _dot(h_hi, rl_ref[...])
        lane = lax.broadcasted_iota(jnp.int32, logits.shape, 1)
        logits = jnp.where(lane < N_EXPERTS, logits, -jnp.inf)
        m1 = logits.max(axis=1, keepdims=True)
        i1 = jnp.where(logits == m1, lane, LANES).min(axis=1, keepdims=True)
        rest_l = jnp.where(lane == i1, -jnp.inf, logits)
        m2 = rest_l.max(axis=1, keepdims=True)
        i2 = jnp.where(rest_l == m2, lane, LANES).min(axis=1, keepdims=True)
        e2 = jnp.exp(m2 - m1)
        w1 = 1.0 / (1.0 + e2)
        w2 = e2 / (1.0 + e2)
        rt = jnp.where(lane == 0, i1.astype(F32),
                       jnp.where(lane == 1, i2.astype(F32),
                                 jnp.where(lane == 2, w1, jnp.where(lane == 3, w2, 0.0))))
        rt_ref[...] = rt


def _outproj_call(route, x, an_p, an_s, bn, w_bf, g1, sc, sh, lg, lb, r_hi=None, r_lo=None):
    t, d = x.shape
    nt = t // ROW_TILE
    ntp = an_p.shape[0] // ROW_TILE

    def tile(i):
        return (i, 0)

    def const2(i):
        return (0, 0)

    def modmap(i):
        return (i, 0, 0)

    mod = pl.BlockSpec((CHUNKS_PER_TILE, 1, d), modmap)
    in_specs = [
        pl.BlockSpec((ROW_TILE, d), tile),
        pl.BlockSpec((ROW_TILE, D_A), lambda i: (jnp.minimum(i, ntp - 1), 0)),
        pl.BlockSpec((ROW_TILE, D_A), lambda i: (jnp.maximum(i - ntp, 0), 0)),
        pl.BlockSpec((ROW_TILE, D_B), tile),
        pl.BlockSpec(w_bf.shape, const2),
        mod, mod, mod,
        pl.BlockSpec((1, d), const2),
        pl.BlockSpec((1, d), const2),
    ]
    out_specs = [pl.BlockSpec((ROW_TILE, d), tile), pl.BlockSpec((ROW_TILE, d), tile)]
    out_shape = [jax.ShapeDtypeStruct((t, d), F32), jax.ShapeDtypeStruct((t, d), F32)]
    args = [x, an_p, an_s, bn, w_bf, g1, sc, sh, lg, lb]
    if route:
        in_specs += [pl.BlockSpec(r_hi.shape, const2), pl.BlockSpec(r_lo.shape, const2)]
        out_specs.append(pl.BlockSpec((ROW_TILE, LANES), tile))
        out_shape.append(jax.ShapeDtypeStruct((t, LANES), F32))
        args += [r_hi, r_lo]
    return pl.pallas_call(
        functools.partial(_outproj_kernel, route, ntp),
        grid=(nt,),
        in_specs=in_specs,
        out_specs=out_specs,
        out_shape=out_shape,
        compiler_params=_cparams("arbitrary"),
        name="out_proj_route" if route else "out_proj",
    )(*args)


def _ffn_kernel(te_ref, tv_ref, x_ref, wg_ref, wu_ref, wd_ref, o_ref, xb_ref):
    del te_ref
    i = pl.program_id(0)
    j = pl.program_id(1)
    valid = tv_ref[i]
    nsub = (valid + FFN_SUB - 1) // FFN_SUB

    @pl.when(j == 0)
    def _():
        def to_bf16(s, carry):
            rows = pl.ds(pl.multiple_of(s * FFN_SUB, FFN_SUB), FFN_SUB)
            xb_ref[rows, :] = x_ref[rows, :].astype(BF16)
            return carry
        lax.fori_loop(0, nsub, to_bf16, 0)

    @pl.when(valid > 0)
    def _():
        wg = wg_ref[0].astype(BF16)
        wu = wu_ref[0].astype(BF16)
        wd = wd_ref[0].astype(BF16)

        def step(s, carry):
            rows = pl.ds(pl.multiple_of(s * FFN_SUB, FFN_SUB), FFN_SUB)
            xs = xb_ref[rows, :]
            g = _dot(xs, wg)
            u = _dot(xs, wu)
            a = (g * jax.nn.sigmoid(g) * u).astype(BF16)
            dlt = _dot(a, wd)

            @pl.when(j == 0)
            def _():
                o_ref[rows, :] = dlt

            @pl.when(j > 0)
            def _():
                o_ref[rows, :] += dlt
            return carry
        lax.fori_loop(0, nsub, step, 0)

    @pl.when(j == pl.num_programs(1) - 1)
    def _():
        def clear(s, carry):
            rows = pl.ds(pl.multiple_of(s * FFN_SUB, FFN_SUB), FFN_SUB)
            o_ref[rows, :] = jnp.zeros((FFN_SUB, o_ref.shape[1]), F32)
            return carry
        lax.fori_loop(nsub, FFN_TILE // FFN_SUB, clear, 0)


def _ffn_call(tile_expert, tile_valid, xs, wg, wu, wd):
    p, d = xs.shape
    f = wg.shape[2]
    nt = p // FFN_TILE
    nc = f // FFN_COLS

    def active_col(i, j, tv):
        return jnp.where(tv[i] > 0, j, nc - 1)

    return pl.pallas_call(
        _ffn_kernel,
        grid_spec=pltpu.PrefetchScalarGridSpec(
            num_scalar_prefetch=2,
            grid=(nt, nc),
            in_specs=[
                pl.BlockSpec((FFN_TILE, d), lambda i, j, te, tv: (i, 0)),
                pl.BlockSpec((1, d, FFN_COLS), lambda i, j, te, tv: (te[i], 0, active_col(i, j, tv))),
                pl.BlockSpec((1, d, FFN_COLS), lambda i, j, te, tv: (te[i], 0, active_col(i, j, tv))),
                pl.BlockSpec((1, FFN_COLS, d), lambda i, j, te, tv: (te[i], active_col(i, j, tv), 0)),
            ],
            out_specs=pl.BlockSpec((FFN_TILE, d), lambda i, j, te, tv: (i, 0)),
            scratch_shapes=[pltpu.VMEM((FFN_TILE, d), BF16)],
        ),
        out_shape=jax.ShapeDtypeStruct((p, d), F32),
        compiler_params=_cparams("arbitrary", "arbitrary"),
        name="swiglu_grouped",
    )(tile_expert, tile_valid, xs, wg, wu, wd)


def _dispatch_kernel(pos_ref, src_ref, init_ref, dst_ref, sem):
    del init_ref
    def issue(r, carry):
        row = src_ref.at[pl.ds(r, 1), :]
        pltpu.make_async_copy(row, dst_ref.at[pl.ds(pos_ref[0, 2 * r], 1), :], sem).start()
        pltpu.make_async_copy(row, dst_ref.at[pl.ds(pos_ref[0, 2 * r + 1], 1), :], sem).start()
        return carry
    lax.fori_loop(0, ROW_TILE, issue, 0, unroll=8)
    for _ in range(2):
        pltpu.make_async_copy(src_ref, dst_ref.at[pl.ds(0, ROW_TILE), :], sem).wait()


def _dispatch_call(pos_tiles, hp, n_rows_sorted):
    t, half = hp.shape
    nt = t // ROW_TILE
    return pl.pallas_call(
        _dispatch_kernel,
        grid=(nt,),
        in_specs=[
            pl.BlockSpec((None, 1, 2 * ROW_TILE), lambda i: (i, 0, 0), memory_space=pltpu.SMEM),
            pl.BlockSpec((ROW_TILE, half), lambda i: (i, 0)),
            pl.BlockSpec(memory_space=pl.ANY),
        ],
        out_specs=pl.BlockSpec(memory_space=pl.ANY),
        out_shape=jax.ShapeDtypeStruct((n_rows_sorted, half), hp.dtype),
        scratch_shapes=[pltpu.SemaphoreType.DMA(())],
        input_output_aliases={2: 0},
        compiler_params=_cparams("arbitrary"),
        name="dispatch_rows",
    )(pos_tiles, hp, jnp.zeros((n_rows_sorted, half), hp.dtype))


def _final_dense_kernel(x_ref, f_ref, g2_ref, lg_ref, lb_ref, o_ref):
    o_ref[...] = _ln(ALPHA * x_ref[...] + _gate_rows(f_ref[...], g2_ref), lg_ref[...], lb_ref[...])


def _final_dense_call(x1, f, g2, lg, lb):
    t, d = x1.shape
    tile = pl.BlockSpec((ROW_TILE, d), lambda i: (i, 0))
    vec = pl.BlockSpec((1, d), lambda i: (0, 0))
    return pl.pallas_call(
        _final_dense_kernel,
        grid=(t // ROW_TILE,),
        in_specs=[tile, tile, pl.BlockSpec((CHUNKS_PER_TILE, 1, d), lambda i: (i, 0, 0)), vec, vec],
        out_specs=tile,
        out_shape=jax.ShapeDtypeStruct((t, d), F32),
        compiler_params=_cparams("arbitrary"),
        name="final_dense",
    )(x1, f, g2, lg, lb)


def _final_moe_kernel(pos_ref, x_ref, rt_ref, g2_ref, lg_ref, lb_ref, ys_ref, o_ref, ybuf, sem):
    def issue(r, carry):
        pltpu.make_async_copy(ys_ref.at[pl.ds(pos_ref[0, 2 * r], 1), :], ybuf.at[0, pl.ds(r, 1), :], sem).start()
        pltpu.make_async_copy(ys_ref.at[pl.ds(pos_ref[0, 2 * r + 1], 1), :], ybuf.at[1, pl.ds(r, 1), :], sem).start()
        return carry
    lax.fori_loop(0, ROW_TILE, issue, 0, unroll=8)
    for k in range(2):
        pltpu.make_async_copy(ys_ref.at[pl.ds(0, ROW_TILE), :], ybuf.at[k], sem).wait()
    rt = rt_ref[...]
    f = rt[:, 2:3] * ybuf[0] + rt[:, 3:4] * ybuf[1]
    o_ref[...] = _ln(ALPHA * x_ref[...] + _gate_rows(f, g2_ref), lg_ref[...], lb_ref[...])


def _final_moe_call(pos_tiles, x1, rt, g2, lg, lb, ys):
    t, d = x1.shape
    tile = pl.BlockSpec((ROW_TILE, d), lambda i: (i, 0))
    vec = pl.BlockSpec((1, d), lambda i: (0, 0))
    return pl.pallas_call(
        _final_moe_kernel,
        grid=(t // ROW_TILE,),
        in_specs=[
            pl.BlockSpec((None, 1, 2 * ROW_TILE), lambda i: (i, 0, 0), memory_space=pltpu.SMEM),
            tile,
            pl.BlockSpec((ROW_TILE, LANES), lambda i: (i, 0)),
            pl.BlockSpec((CHUNKS_PER_TILE, 1, d), lambda i: (i, 0, 0)),
            vec, vec,
            pl.BlockSpec(memory_space=pl.ANY),
        ],
        out_specs=tile,
        out_shape=jax.ShapeDtypeStruct((t, d), F32),
        scratch_shapes=[pltpu.VMEM((2, ROW_TILE, d), F32), pltpu.SemaphoreType.DMA(())],
        compiler_params=_cparams("arbitrary"),
        name="final_moe",
    )(pos_tiles, x1, rt, g2, lg, lb, ys)


def _rel_bias(table, n_q, n_k, banded):
    r = jnp.arange(n_q)[:, None]
    w = jnp.arange(n_k)[None, :]
    n_diag = n_q + n_k - 1
    k = jnp.arange(n_diag)
    diag = table[:, jnp.clip(LEFT_CTX + (n_q - 1) - k, -REL_CLIP, REL_CLIP) + REL_CLIP].astype(F32)
    padded = jnp.concatenate([diag, jnp.zeros((diag.shape[0], 1), F32)], axis=1)
    skew = jnp.tile(padded, (1, n_q))[:, :n_q * n_diag].reshape(-1, n_q, n_diag)
    bias = skew[:, :, n_q - 1:n_q - 1 + n_k]
    if banded:
        qa = r // CHUNK
        kc = w // CHUNK
        vis = (kc >= qa) & (kc <= qa + LEFT_CHUNKS)
        bias = jnp.where(vis[None], bias, NEG)
    return bias


def _gating_weights(w_s, b_s):
    n = MLP_CHUNK
    tril = jnp.tril(jnp.ones((n, n), bool))
    wm = jnp.where(tril[None], w_s, 0.0)
    h = CHUNK
    top = wm[:, :h, :h]
    z = jnp.zeros_like(top)
    wm_s = jnp.concatenate([jnp.concatenate([top, z], 2), jnp.concatenate([z, top], 2)], 1)
    both = jnp.stack([wm, wm_s])
    wcat = jnp.concatenate([both[:, 0::2], both[:, 1::2]], axis=-1)
    bias_p = jnp.repeat(jnp.transpose(b_s), GROUP_DIM, axis=1)
    bias_s = jnp.concatenate([bias_p[:h], bias_p[:h]], 0)
    return wcat.astype(BF16), jnp.stack([bias_p, bias_s]).astype(F32)


def _route_plan(rt, n_tiles):
    t = rt.shape[0]
    e_flat = rt[:, 0:2].astype(jnp.int32).reshape(-1)
    oh = (e_flat[:, None] == jnp.arange(N_EXPERTS)[None, :]).astype(jnp.int32)
    csum = jnp.cumsum(oh, axis=0)
    rank = jnp.sum(oh * (csum - 1), axis=1)
    counts = csum[-1]
    tiles_e = (counts + FFN_TILE - 1) // FFN_TILE
    tile_end = jnp.cumsum(tiles_e)
    tile_start = tile_end - tiles_e
    pos = (tile_start * FFN_TILE)[e_flat] + rank
    ti = jnp.arange(n_tiles)
    te = jnp.minimum(jnp.sum((ti[:, None] >= tile_end[None, :]).astype(jnp.int32), axis=1), N_EXPERTS - 1)
    tv = jnp.clip(counts[te] - (ti - tile_start[te]) * FFN_TILE, 0, FFN_TILE)
    tv = jnp.where(ti < tile_end[-1], tv, 0)
    last_e = te[jnp.maximum(tile_end[-1] - 1, 0)]
    te = jnp.where(ti < tile_end[-1], te, last_e)
    return pos.reshape(t // ROW_TILE, 1, 2 * ROW_TILE).astype(jnp.int32), te.astype(jnp.int32), tv.astype(jnp.int32)


def kernel(x_prompt, x_sample, cache_k, cache_v, c_prompt, c_sample, ln_in_g, ln_in_b, w_in, w_out, rel_bias_table, gmlp_ln_g, gmlp_ln_b, gmlp_w_s, gmlp_b_s, out_norm_a, out_norm_b, ada_w, ada_b, ln1_g, ln1_b, ln2_g, ln2_b, ffn_w_gate, ffn_w_up, ffn_w_down, moe_router, moe_w_gate, moe_w_up, moe_w_down):
    nb, seq, d = x_prompt.shape
    ns, dseq, _ = x_sample.shape
    depth = w_in.shape[0]
    tp = nb * seq
    ts = ns * dseq
    t = tp + ts
    assert depth == DEPTH and dseq == CHUNK and seq % ROW_TILE == 0 and ts % ROW_TILE == 0
    assert cache_k.shape[2] == LEFT_CTX
    ntp = tp // ROW_TILE
    n_chunks = t // CHUNK

    x = jnp.concatenate([x_prompt.reshape(tp, d), x_sample.reshape(ts, d)], axis=0)

    n_cond = nb + ns
    c_all = jnp.concatenate([c_prompt, c_sample], axis=0)
    c_pad = jnp.pad(c_all, ((0, (-n_cond) % 8), (0, 0)))
    mods = _ada_call(c_pad, ada_w, ada_b)
    cpb = seq // CHUNK
    mods_p = jnp.broadcast_to(mods[:, :nb, None, :], (depth, nb, cpb, 6 * d)).reshape(depth, nb * cpb, 6 * d)
    mods = jnp.concatenate([mods_p, mods[:, nb:n_cond]], axis=1).reshape(depth, n_chunks, 6, 1, d)

    def row(v):
        return v.reshape(1, -1)

    k_tail, v_tail, gv_rows = [], [], []
    for l in range(depth):
        sh1, sc1, g1, sh2, sc2, g2 = (mods[l, :, m] for m in range(6))
        w_in_bf = w_in[l].astype(BF16)
        w_out_bf = w_out[l].astype(BF16)
        wcat, bs = _gating_weights(gmlp_w_s[l], gmlp_b_s[l])
        res = _inproj_call(l == 0, x, row(ln_in_g), row(ln_in_b), sc1, sh1, w_in_bf, row(gmlp_ln_g[l]),
                           row(gmlp_ln_b[l]), wcat, bs, row(out_norm_b[l]),
                           n_prompt_tiles=ntp, tiles_per_batch=seq // ROW_TILE, n_batch=nb, n_sample_rows=ts)
        if l == 0:
            x, q, k, v, kf, vf, gv, bn = res
        else:
            q, k, v, kf, vf, gv, bn = res
        k_tail.append(kf)
        v_tail.append(vf)
        gv_rows.append(gv)

        table = rel_bias_table[l]
        bias_p = _rel_bias(table, Q_TILE, 3 * Q_TILE, True)
        bias_s = _rel_bias(table, CHUNK, LEFT_CTX + CHUNK, False)
        na = row(out_norm_a[l])
        an_p = _attn_prompt_call(q, k, v, bias_p, na, n_batch=nb, q_tiles_per_batch=seq // Q_TILE)
        an_s = _attn_sample_call(q, k, v, cache_k[l].reshape(ns, LEFT_CTX, D_A), cache_v[l].reshape(ns, LEFT_CTX, D_A),
                               bias_s[:, :, :LEFT_CTX], bias_s[:, :, LEFT_CTX:], na, first_chunk=tp // CHUNK)

        lg1, lb1, lg2, lb2 = row(ln1_g[l]), row(ln1_b[l]), row(ln2_g[l]), row(ln2_b[l])
        if l % 2 == 0:
            x1, hp = _outproj_call(False, x, an_p, an_s, bn, w_out_bf, g1, sc2, sh2, lg1, lb1)
            nt = t // FFN_TILE
            te = jnp.zeros((nt,), jnp.int32)
            tv = jnp.full((nt,), FFN_TILE, jnp.int32)
            i = l // 2
            f = _ffn_call(te, tv, hp, ffn_w_gate[i:i + 1], ffn_w_up[i:i + 1], ffn_w_down[i:i + 1])
            x = _final_dense_call(x1, f, g2, lg2, lb2)
        else:
            i = l // 2
            r_pad = jnp.pad(moe_router[i], ((0, 0), (0, LANES - N_EXPERTS)))
            r_hi = r_pad.astype(BF16)
            r_lo = (r_pad - r_hi.astype(F32)).astype(BF16)
            x1, hp, rt = _outproj_call(True, x, an_p, an_s, bn, w_out_bf, g1, sc2, sh2, lg1, lb1, r_hi, r_lo)
            n_tiles = (2 * t) // FFN_TILE + N_EXPERTS
            pos, te, tv = _route_plan(rt, n_tiles)
            xs = _dispatch_call(pos, hp, n_tiles * FFN_TILE)
            ys = _ffn_call(te, tv, xs, moe_w_gate[i], moe_w_up[i], moe_w_down[i])
            x = _final_moe_call(pos, x1, rt, g2, lg2, lb2, ys)

    y_prompt = x[:tp].reshape(nb, seq, d)
    y_sample = x[tp:].reshape(ns, dseq, d)

    def tails(rows):
        kp = jnp.stack([r[:nb * ROW_TILE].reshape(nb, ROW_TILE, N_HEADS, HEAD_DIM) for r in rows])
        ksn = jnp.stack([r[nb * ROW_TILE:].reshape(ns, dseq, N_HEADS, HEAD_DIM) for r in rows])
        return kp, ksn

    k_prompt_new, k_sample_new = tails(k_tail)
    v_prompt_new, v_sample_new = tails(v_tail)
    gmlp_v_sample_new = jnp.stack([g.reshape(ns, dseq, D_B) for g in gv_rows])
    return (y_prompt, y_sample, k_prompt_new, v_prompt_new, k_sample_new, v_sample_new, gmlp_v_sample_new)
```

```python
import functools

import jax
import jax.numpy as jnp
from jax import lax
from jax.experimental import pallas as pl
from jax.experimental.pallas import tpu as pltpu

CHUNK = 64
LEFT_CHUNKS = 8
LEFT_CTX = LEFT_CHUNKS * CHUNK
N_HEADS = 8
HEAD_DIM = 64
D_A = N_HEADS * HEAD_DIM
N_GROUPS = 8
GROUP_DIM = 64
D_B = N_GROUPS * GROUP_DIM
MLP_CHUNK = 128
REL_CLIP = 128
N_EXPERTS = 8
DEPTH = 2
ALPHA = (2 * DEPTH) ** 0.25
LN_EPS = 1e-5
ATTN_SCALE = HEAD_DIM ** -0.5
NEG = -1e30

LANES = 128
ROW_TILE = 512
CHUNKS_PER_TILE = ROW_TILE // CHUNK
Q_TILE = 256
FFN_TILE = 1024
FFN_SUB = 512
FFN_COLS = 256
VMEM_LIMIT = 48 * 1024 * 1024

BF16 = jnp.bfloat16
F32 = jnp.float32


def _cparams(*sem):
    return pltpu.CompilerParams(dimension_semantics=sem, vmem_limit_bytes=VMEM_LIMIT)


def _ln(x, g, b):
    mu = jnp.mean(x, axis=-1, keepdims=True)
    xc = x - mu
    var = jnp.mean(xc * xc, axis=-1, keepdims=True)
    return xc * lax.rsqrt(var + LN_EPS) * g + b


def _rms(x, g):
    return x * lax.rsqrt(jnp.mean(x * x, axis=-1, keepdims=True) + LN_EPS) * g


def _modulate(x, sc_ref, sh_ref):
    parts = []
    for c in range(CHUNKS_PER_TILE):
        xc = x[c * CHUNK:(c + 1) * CHUNK]
        parts.append(xc * (1.0 + sc_ref[c]) + sh_ref[c])
    return jnp.concatenate(parts, axis=0)


def _gate_rows(x, g_ref):
    parts = []
    for c in range(CHUNKS_PER_TILE):
        parts.append(x[c * CHUNK:(c + 1) * CHUNK] * (1.0 + g_ref[c]))
    return jnp.concatenate(parts, axis=0)


def _split_bf16(x):
    hi = x.astype(BF16)
    lo = (x - hi.astype(F32)).astype(BF16)
    return hi, lo


def _dot(a, b):
    return jnp.dot(a, b, preferred_element_type=F32)


def _ada_kernel(c_ref, w_ref, b_ref, o_ref):
    c = c_ref[...]
    s = c * jax.nn.sigmoid(c)
    s_hi, s_lo = _split_bf16(s)
    w_hi, w_lo = _split_bf16(w_ref[0])
    o_ref[0] = _dot(s_hi, w_hi) + _dot(s_lo, w_hi) + _dot(s_hi, w_lo) + b_ref[0]


def _ada_call(c_pad, ada_w, ada_b):
    depth, d, n = ada_w.shape
    rows = c_pad.shape[0]
    tn = n // 4
    return pl.pallas_call(
        _ada_kernel,
        grid=(depth, n // tn),
        in_specs=[
            pl.BlockSpec((rows, d), lambda l, j: (0, 0)),
            pl.BlockSpec((1, d, tn), lambda l, j: (l, 0, j)),
            pl.BlockSpec((1, 1, tn), lambda l, j: (l, 0, j)),
        ],
        out_specs=pl.BlockSpec((1, rows, tn), lambda l, j: (l, 0, j)),
        out_shape=jax.ShapeDtypeStruct((depth, rows, n), F32),
        compiler_params=_cparams("arbitrary", "arbitrary"),
        name="ada_mod",
    )(c_pad, ada_w, ada_b.reshape(depth, 1, n))


def _inproj_kernel(first, ntp, *refs):
    if first:
        xp_ref, xs_ref, lng_ref, lnb_ref = refs[:4]
        xn_ref = refs[-8]
        x = jnp.where(pl.program_id(0) >= ntp, xs_ref[...], xp_ref[...])
        x = _ln(x, lng_ref[...], lnb_ref[...])
        xn_ref[...] = x
        refs = refs[4:-8] + refs[-7:]
    else:
        x = refs[0][...]
        refs = refs[1:]
    (sc_ref, sh_ref, w_ref, gg_ref, gb_ref, wcat_ref, bs_ref, nb_ref,
     q_ref, k_ref, v_ref, kf_ref, vf_ref, gv_ref, bn_ref) = refs
    h = _modulate(x, sc_ref, sh_ref).astype(BF16)

    q = _dot(h, w_ref[:, 0:D_A])
    q_ref[...] = (q * ATTN_SCALE).astype(BF16)
    k = _dot(h, w_ref[:, D_A:2 * D_A])
    k_ref[...] = k.astype(BF16)
    kf_ref[...] = k
    v = _dot(h, w_ref[:, 2 * D_A:3 * D_A])
    v_ref[...] = v.astype(BF16)
    vf_ref[...] = v

    u = jax.nn.gelu(_dot(h, w_ref[:, 3 * D_A:3 * D_A + D_B]), approximate=True)
    vb = jax.nn.gelu(_dot(h, w_ref[:, 3 * D_A + D_B:3 * D_A + 2 * D_B]), approximate=True)
    vn = _ln(vb, gg_ref[...], gb_ref[...])
    gv_ref[...] = vn

    lane = lax.broadcasted_iota(jnp.int32, (MLP_CHUNK, LANES), 1)
    lo = lane < GROUP_DIM
    vnb = vn.astype(BF16)
    zero = jnp.zeros((MLP_CHUNK, LANES), BF16)
    rows = []
    for c in range(ROW_TILE // MLP_CHUNK):
        r0 = c * MLP_CHUNK
        cols = []
        for p in range(D_B // LANES):
            slab = vnb[r0:r0 + MLP_CHUNK, p * LANES:(p + 1) * LANES]
            rhs = jnp.concatenate([jnp.where(lo, slab, zero), jnp.where(lo, zero, slab)], axis=0)
            mixed = _dot(wcat_ref[0, p], rhs) + bs_ref[0, :, p * LANES:(p + 1) * LANES]
            cols.append(u[r0:r0 + MLP_CHUNK, p * LANES:(p + 1) * LANES] * mixed)
        rows.append(jnp.concatenate(cols, axis=1))
    bo = jnp.concatenate(rows, axis=0)
    bn_ref[...] = _rms(bo, nb_ref[...]).astype(BF16)


def _inproj_call(x_in, sc, sh, w_bf, gg, gb, wcat, bs, nb, *, n_prompt_tiles, tiles_per_batch, n_batch,
                 n_sample_rows):
    first = len(x_in) == 4
    d = x_in[0].shape[1]
    t = sc.shape[0] * CHUNK
    nt = t // ROW_TILE
    ntp = n_prompt_tiles
    n_tail = n_batch + (nt - ntp)

    def tile(i):
        return (i, 0)

    def const2(i):
        return (0, 0)

    def modmap(i):
        return (i, 0, 0)

    def selmap4(i):
        return (jnp.where(i >= ntp, 1, 0), 0, 0, 0)

    def selmap3(i):
        return (jnp.where(i >= ntp, 1, 0), 0, 0)

    def tailmap(i):
        return (jnp.where(i < ntp, i // tiles_per_batch, n_batch + i - ntp), 0)

    def gvmap(i):
        return (jnp.maximum(i - ntp, 0), 0)

    if first:
        x_specs = [
            pl.BlockSpec((ROW_TILE, d), lambda i: (jnp.minimum(i, ntp - 1), 0)),
            pl.BlockSpec((ROW_TILE, d), lambda i: (jnp.maximum(i - ntp, 0), 0)),
            pl.BlockSpec((1, d), const2),
            pl.BlockSpec((1, d), const2),
        ]
    else:
        x_specs = [pl.BlockSpec((ROW_TILE, d), tile)]
    in_specs = x_specs + [
        pl.BlockSpec((CHUNKS_PER_TILE, 1, d), modmap),
        pl.BlockSpec((CHUNKS_PER_TILE, 1, d), modmap),
        pl.BlockSpec(w_bf.shape, const2),
        pl.BlockSpec((1, D_B), const2),
        pl.BlockSpec((1, D_B), const2),
        pl.BlockSpec((1,) + wcat.shape[1:], selmap4),
        pl.BlockSpec((1, MLP_CHUNK, D_B), selmap3),
        pl.BlockSpec((1, D_B), const2),
    ]
    out_specs = [
        pl.BlockSpec((ROW_TILE, D_A), tile),
        pl.BlockSpec((ROW_TILE, D_A), tile),
        pl.BlockSpec((ROW_TILE, D_A), tile),
        pl.BlockSpec((ROW_TILE, D_A), tailmap),
        pl.BlockSpec((ROW_TILE, D_A), tailmap),
        pl.BlockSpec((ROW_TILE, D_B), gvmap),
        pl.BlockSpec((ROW_TILE, D_B), tile),
    ]
    out_shape = [
        jax.ShapeDtypeStruct((t, D_A), BF16),
        jax.ShapeDtypeStruct((t, D_A), BF16),
        jax.ShapeDtypeStruct((t, D_A), BF16),
        jax.ShapeDtypeStruct((n_tail * ROW_TILE, D_A), F32),
        jax.ShapeDtypeStruct((n_tail * ROW_TILE, D_A), F32),
        jax.ShapeDtypeStruct((n_sample_rows, D_B), F32),
        jax.ShapeDtypeStruct((t, D_B), BF16),
    ]
    if first:
        out_specs = [pl.BlockSpec((ROW_TILE, d), tile)] + out_specs
        out_shape = [jax.ShapeDtypeStruct((t, d), F32)] + out_shape
    return pl.pallas_call(
        functools.partial(_inproj_kernel, first, ntp),
        grid=(nt,),
        in_specs=in_specs,
        out_specs=out_specs,
        out_shape=out_shape,
        compiler_params=_cparams("arbitrary"),
        name="in_proj_first" if first else "in_proj",
    )(*x_in, sc, sh, w_bf, gg, gb, wcat, bs, nb)


def _attend(q, parts, bias_of, col_floor, na):
    r = q.shape[0]
    lane = lax.broadcasted_iota(jnp.int32, (1, LANES), 1)
    lo = lane < HEAD_DIM
    outs = []
    for p in range(D_A // LANES):
        sl = slice(p * LANES, (p + 1) * LANES)
        qp = q[:, sl]
        kps = [k[:, sl] for k, _ in parts]
        vps = [v[:, sl] for _, v in parts]
        o_pair = jnp.zeros((r, LANES), F32)
        for half in range(2):
            h = 2 * p + half
            keep = lo if half == 0 else jnp.logical_not(lo)
            qh = jnp.where(keep, qp, jnp.zeros_like(qp))
            ss = []
            for j, kp in enumerate(kps):
                s = lax.dot_general(qh, kp, (((1,), (1,)), ((), ())), preferred_element_type=F32)
                s = s + bias_of(h, j)
                if col_floor is not None:
                    s = jnp.where(col_floor[j], s, NEG)
                ss.append(s)
            m = ss[0].max(axis=1, keepdims=True)
            for s in ss[1:]:
                m = jnp.maximum(m, s.max(axis=1, keepdims=True))
            acc = jnp.zeros((r, LANES), F32)
            l = jnp.zeros((r, 1), F32)
            for s, vp in zip(ss, vps):
                e = jnp.exp(s - m)
                l = l + e.sum(axis=1, keepdims=True)
                vh = jnp.where(keep, vp, jnp.zeros_like(vp))
                acc = acc + _dot(e.astype(BF16), vh)
            o_pair = o_pair + acc / l
        outs.append(o_pair)
    a = jnp.concatenate(outs, axis=1)
    return _rms(a, na)


def _attn_prompt_kernel(q_ref, k0_ref, k1_ref, k2_ref, v0_ref, v1_ref, v2_ref, bias_ref, na_ref, o_ref):
    j = pl.program_id(1)
    col = lax.broadcasted_iota(jnp.int32, (1, Q_TILE), 1)
    floors = [(j - 2 + b) * Q_TILE + col >= 0 for b in range(3)]
    parts = [(k0_ref[...], v0_ref[...]), (k1_ref[...], v1_ref[...]), (k2_ref[...], v2_ref[...])]

    def bias_of(h, b):
        return bias_ref[h, :, b * Q_TILE:(b + 1) * Q_TILE]

    o_ref[...] = _attend(q_ref[...], parts, bias_of, floors, na_ref[...]).astype(BF16)


def _attn_prompt_call(q, k, v, bias, na, *, n_batch, q_tiles_per_batch):
    nq = q_tiles_per_batch

    def qmap(b, j):
        return (b * nq + j, 0)

    def kmap(off):
        return lambda b, j: (b * nq + jnp.maximum(j - off, 0), 0)

    blk = pl.BlockSpec((Q_TILE, D_A), qmap)
    kv_specs = [pl.BlockSpec((Q_TILE, D_A), kmap(off)) for off in (2, 1, 0)]
    return pl.pallas_call(
        _attn_prompt_kernel,
        grid=(n_batch, nq),
        in_specs=[blk] + kv_specs + kv_specs + [
            pl.BlockSpec(bias.shape, lambda b, j: (0, 0, 0)),
            pl.BlockSpec((1, D_A), lambda b, j: (0, 0)),
        ],
        out_specs=blk,
        out_shape=jax.ShapeDtypeStruct((n_batch * nq * Q_TILE, D_A), BF16),
        compiler_params=_cparams("arbitrary", "arbitrary"),
        name="attn_prompt",
    )(q, k, k, k, v, v, v, bias, na)


def _attn_sample_kernel(q_ref, kn_ref, vn_ref, ck_ref, cv_ref, bias_c_ref, bias_n_ref, na_ref, o_ref):
    parts = [(ck_ref[0].astype(BF16), cv_ref[0].astype(BF16)), (kn_ref[...], vn_ref[...])]

    def bias_of(h, b):
        return bias_c_ref[h] if b == 0 else bias_n_ref[h]

    o_ref[...] = _attend(q_ref[...], parts, bias_of, None, na_ref[...]).astype(BF16)


def _attn_sample_call(q, k, v, ck, cv, bias_c, bias_n, na, *, first_chunk):
    nb, win, _ = ck.shape
    blk = pl.BlockSpec((CHUNK, D_A), lambda b: (first_chunk + b, 0))
    cache = pl.BlockSpec((1, win, D_A), lambda b: (b, 0, 0))
    return pl.pallas_call(
        _attn_sample_kernel,
        grid=(nb,),
        in_specs=[blk, blk, blk, cache, cache,
                  pl.BlockSpec(bias_c.shape, lambda b: (0, 0, 0)),
                  pl.BlockSpec(bias_n.shape, lambda b: (0, 0, 0)),
                  pl.BlockSpec((1, D_A), lambda b: (0, 0))],
        out_specs=pl.BlockSpec((CHUNK, D_A), lambda b: (b, 0)),
        out_shape=jax.ShapeDtypeStruct((nb * CHUNK, D_A), BF16),
        compiler_params=_cparams("arbitrary"),
        name="attn_sample",
    )(q, k, v, ck, cv, bias_c, bias_n, na)


def _outproj_kernel(route, ntp, x_ref, anp_ref, ans_ref, bn_ref, w_ref, g1_ref, sc_ref, sh_ref, lg_ref, lb_ref, *rest):
    if route:
        rc_ref, x1_ref, hp_ref, rt_ref = rest
    else:
        x1_ref, hp_ref = rest
    an = jnp.where(pl.program_id(0) >= ntp, ans_ref[...], anp_ref[...])
    mix = _dot(an, w_ref[0:D_A, :]) + _dot(bn_ref[...], w_ref[D_A:D_A + D_B, :])
    x1 = _ln(ALPHA * x_ref[...] + _gate_rows(mix, g1_ref), lg_ref[...], lb_ref[...])
    x1_ref[...] = x1
    h2 = _modulate(x1, sc_ref, sh_ref)
    hp_ref[...] = h2
    if route:
        h_hi, h_lo = _split_bf16(h2)
        both = _dot(h_hi, rc_ref[...])
        logits = both[:, :LANES] + both[:, LANES:] + _dot(h_lo, rc_ref[:, :LANES])
        lane = lax.broadcasted_iota(jnp.int32, logits.shape, 1)
        logits = jnp.where(lane < N_EXPERTS, logits, -jnp.inf)
        m1 = logits.max(axis=1, keepdims=True)
        i1 = jnp.where(logits == m1, lane, LANES).min(axis=1, keepdims=True)
        rest_l = jnp.where(lane == i1, -jnp.inf, logits)
        m2 = rest_l.max(axis=1, keepdims=True)
        i2 = jnp.where(rest_l == m2, lane, LANES).min(axis=1, keepdims=True)
        e2 = jnp.exp(m2 - m1)
        w1 = 1.0 / (1.0 + e2)
        w2 = e2 / (1.0 + e2)
        rt = jnp.where(lane == 0, i1.astype(F32),
                       jnp.where(lane == 1, i2.astype(F32),
                                 jnp.where(lane == 2, w1, jnp.where(lane == 3, w2, 0.0))))
        rt_ref[...] = rt


def _outproj_call(route, x, an_p, an_s, bn, w_bf, g1, sc, sh, lg, lb, r_cat=None):
    t, d = x.shape
    nt = t // ROW_TILE
    ntp = an_p.shape[0] // ROW_TILE

    def tile(i):
        return (i, 0)

    def const2(i):
        return (0, 0)

    def modmap(i):
        return (i, 0, 0)

    mod = pl.BlockSpec((CHUNKS_PER_TILE, 1, d), modmap)
    in_specs = [
        pl.BlockSpec((ROW_TILE, d), tile),
        pl.BlockSpec((ROW_TILE, D_A), lambda i: (jnp.minimum(i, ntp - 1), 0)),
        pl.BlockSpec((ROW_TILE, D_A), lambda i: (jnp.maximum(i - ntp, 0), 0)),
        pl.BlockSpec((ROW_TILE, D_B), tile),
        pl.BlockSpec(w_bf.shape, const2),
        mod, mod, mod,
        pl.BlockSpec((1, d), const2),
        pl.BlockSpec((1, d), const2),
    ]
    out_specs = [pl.BlockSpec((ROW_TILE, d), tile), pl.BlockSpec((ROW_TILE, d), tile)]
    out_shape = [jax.ShapeDtypeStruct((t, d), F32), jax.ShapeDtypeStruct((t, d), F32)]
    args = [x, an_p, an_s, bn, w_bf, g1, sc, sh, lg, lb]
    if route:
        in_specs.append(pl.BlockSpec(r_cat.shape, const2))
        out_specs.append(pl.BlockSpec((ROW_TILE, LANES), tile))
        out_shape.append(jax.ShapeDtypeStruct((t, LANES), F32))
        args.append(r_cat)
    return pl.pallas_call(
        functools.partial(_outproj_kernel, route, ntp),
        grid=(nt,),
        in_specs=in_specs,
        out_specs=out_specs,
        out_shape=out_shape,
        compiler_params=_cparams("arbitrary"),
        name="out_proj_route" if route else "out_proj",
    )(*args)


def _ffn_kernel(te_ref, tv_ref, x_ref, wg_ref, wu_ref, wd_ref, o_ref, xb_ref):
    del te_ref
    i = pl.program_id(0)
    j = pl.program_id(1)
    valid = tv_ref[i]
    nsub = (valid + FFN_SUB - 1) // FFN_SUB

    @pl.when(j == 0)
    def _():
        xb_ref[...] = x_ref[...].astype(BF16)
        o_ref[...] = jnp.zeros(o_ref.shape, F32)

    def run(n_sub):
        wg = wg_ref[0].astype(BF16)
        wu = wu_ref[0].astype(BF16)
        wd = wd_ref[0].astype(BF16)
        for s in range(n_sub):
            rows = slice(s * FFN_SUB, (s + 1) * FFN_SUB)
            xs = xb_ref[rows, :]
            g = _dot(xs, wg)
            u = _dot(xs, wu)
            a = (g * jax.nn.sigmoid(g) * u).astype(BF16)
            o_ref[rows, :] += _dot(a, wd)

    for n_sub in range(1, FFN_TILE // FFN_SUB + 1):
        pl.when(nsub == n_sub)(functools.partial(run, n_sub))


def _ffn_call(tile_expert, tile_valid, xs, wg, wu, wd):
    p, d = xs.shape
    f = wg.shape[2]
    nt = p // FFN_TILE
    nc = f // FFN_COLS

    def active_col(i, j, tv):
        return jnp.where(tv[i] > 0, j, nc - 1)

    return pl.pallas_call(
        _ffn_kernel,
        grid_spec=pltpu.PrefetchScalarGridSpec(
            num_scalar_prefetch=2,
            grid=(nt, nc),
            in_specs=[
                pl.BlockSpec((FFN_TILE, d), lambda i, j, te, tv: (i, 0)),
                pl.BlockSpec((1, d, FFN_COLS), lambda i, j, te, tv: (te[i], 0, active_col(i, j, tv))),
                pl.BlockSpec((1, d, FFN_COLS), lambda i, j, te, tv: (te[i], 0, active_col(i, j, tv))),
                pl.BlockSpec((1, FFN_COLS, d), lambda i, j, te, tv: (te[i], active_col(i, j, tv), 0)),
            ],
            out_specs=pl.BlockSpec((FFN_TILE, d), lambda i, j, te, tv: (i, 0)),
            scratch_shapes=[pltpu.VMEM((FFN_TILE, d), BF16)],
        ),
        out_shape=jax.ShapeDtypeStruct((p, d), F32),
        compiler_params=_cparams("arbitrary", "arbitrary"),
        name="swiglu_grouped",
    )(tile_expert, tile_valid, xs, wg, wu, wd)


def _dispatch_kernel(pos_ref, src_ref, init_ref, dst_ref, sem):
    del init_ref
    def issue(r, carry):
        row = src_ref.at[pl.ds(r, 1), :]
        pltpu.make_async_copy(row, dst_ref.at[pl.ds(pos_ref[0, 2 * r], 1), :], sem).start()
        pltpu.make_async_copy(row, dst_ref.at[pl.ds(pos_ref[0, 2 * r + 1], 1), :], sem).start()
        return carry
    lax.fori_loop(0, ROW_TILE, issue, 0, unroll=8)
    for _ in range(2):
        pltpu.make_async_copy(src_ref, dst_ref.at[pl.ds(0, ROW_TILE), :], sem).wait()


def _dispatch_call(pos_tiles, hp, n_rows_sorted):
    t, half = hp.shape
    nt = t // ROW_TILE
    return pl.pallas_call(
        _dispatch_kernel,
        grid=(nt,),
        in_specs=[
            pl.BlockSpec((None, 1, 2 * ROW_TILE), lambda i: (i, 0, 0), memory_space=pltpu.SMEM),
            pl.BlockSpec((ROW_TILE, half), lambda i: (i, 0)),
            pl.BlockSpec(memory_space=pl.ANY),
        ],
        out_specs=pl.BlockSpec(memory_space=pl.ANY),
        out_shape=jax.ShapeDtypeStruct((n_rows_sorted, half), hp.dtype),
        scratch_shapes=[pltpu.SemaphoreType.DMA(())],
        input_output_aliases={2: 0},
        compiler_params=_cparams("arbitrary"),
        name="dispatch_rows",
    )(pos_tiles, hp, jnp.zeros((n_rows_sorted, half), hp.dtype))


def _final_dense_kernel(x_ref, f_ref, g2_ref, lg_ref, lb_ref, o_ref):
    o_ref[...] = _ln(ALPHA * x_ref[...] + _gate_rows(f_ref[...], g2_ref), lg_ref[...], lb_ref[...])


def _final_dense_call(x1, f, g2, lg, lb):
    t, d = x1.shape
    tile = pl.BlockSpec((ROW_TILE, d), lambda i: (i, 0))
    vec = pl.BlockSpec((1, d), lambda i: (0, 0))
    return pl.pallas_call(
        _final_dense_kernel,
        grid=(t // ROW_TILE,),
        in_specs=[tile, tile, pl.BlockSpec((CHUNKS_PER_TILE, 1, d), lambda i: (i, 0, 0)), vec, vec],
        out_specs=tile,
        out_shape=jax.ShapeDtypeStruct((t, d), F32),
        compiler_params=_cparams("arbitrary"),
        name="final_dense",
    )(x1, f, g2, lg, lb)


def _final_moe_kernel(ntp, pos_ref, x_ref, rt_ref, g2_ref, lg_ref, lb_ref, ys_ref, op_ref, os_ref, ybuf, sem):
    def issue(r, carry):
        pltpu.make_async_copy(ys_ref.at[pl.ds(pos_ref[0, 2 * r], 1), :], ybuf.at[0, pl.ds(r, 1), :], sem).start()
        pltpu.make_async_copy(ys_ref.at[pl.ds(pos_ref[0, 2 * r + 1], 1), :], ybuf.at[1, pl.ds(r, 1), :], sem).start()
        return carry
    lax.fori_loop(0, ROW_TILE, issue, 0, unroll=8)
    for k in range(2):
        pltpu.make_async_copy(ys_ref.at[pl.ds(0, ROW_TILE), :], ybuf.at[k], sem).wait()
    rt = rt_ref[...]
    f = rt[:, 2:3] * ybuf[0] + rt[:, 3:4] * ybuf[1]
    y = _ln(ALPHA * x_ref[...] + _gate_rows(f, g2_ref), lg_ref[...], lb_ref[...])
    i = pl.program_id(0)

    @pl.when(i < ntp)
    def _():
        op_ref[...] = y

    @pl.when(i >= ntp)
    def _():
        os_ref[...] = y


def _final_moe_call(pos_tiles, x1, rt, g2, lg, lb, ys, *, n_prompt_tiles):
    t, d = x1.shape
    ntp = n_prompt_tiles
    tile = pl.BlockSpec((ROW_TILE, d), lambda i: (i, 0))
    vec = pl.BlockSpec((1, d), lambda i: (0, 0))
    return pl.pallas_call(
        functools.partial(_final_moe_kernel, ntp),
        grid=(t // ROW_TILE,),
        in_specs=[
            pl.BlockSpec((None, 1, 2 * ROW_TILE), lambda i: (i, 0, 0), memory_space=pltpu.SMEM),
            tile,
            pl.BlockSpec((ROW_TILE, LANES), lambda i: (i, 0)),
            pl.BlockSpec((CHUNKS_PER_TILE, 1, d), lambda i: (i, 0, 0)),
            vec, vec,
            pl.BlockSpec(memory_space=pl.ANY),
        ],
        out_specs=[pl.BlockSpec((ROW_TILE, d), lambda i: (jnp.minimum(i, ntp - 1), 0)),
                   pl.BlockSpec((ROW_TILE, d), lambda i: (jnp.maximum(i - ntp, 0), 0))],
        out_shape=[jax.ShapeDtypeStruct((ntp * ROW_TILE, d), F32),
                   jax.ShapeDtypeStruct((t - ntp * ROW_TILE, d), F32)],
        scratch_shapes=[pltpu.VMEM((2, ROW_TILE, d), F32), pltpu.SemaphoreType.DMA(())],
        compiler_params=_cparams("arbitrary"),
        name="final_moe",
    )(pos_tiles, x1, rt, g2, lg, lb, ys)


def _rel_bias(table, n_q, n_k, banded):
    r = jnp.arange(n_q)[:, None]
    w = jnp.arange(n_k)[None, :]
    n_diag = n_q + n_k - 1
    k = jnp.arange(n_diag)
    diag = table[:, jnp.clip(LEFT_CTX + (n_q - 1) - k, -REL_CLIP, REL_CLIP) + REL_CLIP].astype(F32)
    padded = jnp.concatenate([diag, jnp.zeros((diag.shape[0], 1), F32)], axis=1)
    skew = jnp.tile(padded, (1, n_q))[:, :n_q * n_diag].reshape(-1, n_q, n_diag)
    bias = skew[:, :, n_q - 1:n_q - 1 + n_k]
    if banded:
        qa = r // CHUNK
        kc = w // CHUNK
        vis = (kc >= qa) & (kc <= qa + LEFT_CHUNKS)
        bias = jnp.where(vis[None], bias, NEG)
    return bias


def _gating_weights(w_s, b_s):
    n = MLP_CHUNK
    tril = jnp.tril(jnp.ones((n, n), bool))
    wm = jnp.where(tril[None], w_s, 0.0)
    h = CHUNK
    top = wm[:, :h, :h]
    z = jnp.zeros_like(top)
    wm_s = jnp.concatenate([jnp.concatenate([top, z], 2), jnp.concatenate([z, top], 2)], 1)
    both = jnp.stack([wm, wm_s])
    wcat = jnp.concatenate([both[:, 0::2], both[:, 1::2]], axis=-1)
    bias_p = jnp.repeat(jnp.transpose(b_s), GROUP_DIM, axis=1)
    bias_s = jnp.concatenate([bias_p[:h], bias_p[:h]], 0)
    return wcat.astype(BF16), jnp.stack([bias_p, bias_s]).astype(F32)


def _route_plan(rt, n_tiles):
    t = rt.shape[0]
    e_flat = rt[:, 0:2].astype(jnp.int32).reshape(-1)
    oh = (e_flat[:, None] == jnp.arange(N_EXPERTS)[None, :]).astype(jnp.int32)
    csum = jnp.cumsum(oh, axis=0)
    rank = jnp.sum(oh * (csum - 1), axis=1)
    counts = csum[-1]
    tiles_e = (counts + FFN_TILE - 1) // FFN_TILE
    tile_end = jnp.cumsum(tiles_e)
    tile_start = tile_end - tiles_e
    pos = (tile_start * FFN_TILE)[e_flat] + rank
    ti = jnp.arange(n_tiles)
    te = jnp.minimum(jnp.sum((ti[:, None] >= tile_end[None, :]).astype(jnp.int32), axis=1), N_EXPERTS - 1)
    tv = jnp.clip(counts[te] - (ti - tile_start[te]) * FFN_TILE, 0, FFN_TILE)
    tv = jnp.where(ti < tile_end[-1], tv, 0)
    last_e = te[jnp.maximum(tile_end[-1] - 1, 0)]
    te = jnp.where(ti < tile_end[-1], te, last_e)
    return pos.reshape(t // ROW_TILE, 1, 2 * ROW_TILE).astype(jnp.int32), te.astype(jnp.int32), tv.astype(jnp.int32)


def kernel(x_prompt, x_sample, cache_k, cache_v, c_prompt, c_sample, ln_in_g, ln_in_b, w_in, w_out, rel_bias_table, gmlp_ln_g, gmlp_ln_b, gmlp_w_s, gmlp_b_s, out_norm_a, out_norm_b, ada_w, ada_b, ln1_g, ln1_b, ln2_g, ln2_b, ffn_w_gate, ffn_w_up, ffn_w_down, moe_router, moe_w_gate, moe_w_up, moe_w_down):
    nb, seq, d = x_prompt.shape
    ns, dseq, _ = x_sample.shape
    depth = w_in.shape[0]
    tp = nb * seq
    ts = ns * dseq
    t = tp + ts
    assert depth == DEPTH and dseq == CHUNK and seq % ROW_TILE == 0 and ts % ROW_TILE == 0
    assert cache_k.shape[2] == LEFT_CTX
    ntp = tp // ROW_TILE
    n_chunks = t // CHUNK

    n_cond = nb + ns
    c_all = jnp.concatenate([c_prompt, c_sample], axis=0)
    c_pad = jnp.pad(c_all, ((0, (-n_cond) % 8), (0, 0)))
    mods = _ada_call(c_pad, ada_w, ada_b)
    cpb = seq // CHUNK
    mods_p = jnp.broadcast_to(mods[:, :nb, None, :], (depth, nb, cpb, 6 * d)).reshape(depth, nb * cpb, 6 * d)
    mods = jnp.concatenate([mods_p, mods[:, nb:n_cond]], axis=1).reshape(depth, n_chunks, 6, 1, d)

    def row(v):
        return v.reshape(1, -1)

    k_tail, v_tail, gv_rows = [], [], []
    for l in range(depth):
        sh1, sc1, g1, sh2, sc2, g2 = (mods[l, :, m] for m in range(6))
        w_in_bf = w_in[l].astype(BF16)
        w_out_bf = w_out[l].astype(BF16)
        wcat, bs = _gating_weights(gmlp_w_s[l], gmlp_b_s[l])
        if l == 0:
            x_in = (x_prompt.reshape(tp, d), x_sample.reshape(ts, d), row(ln_in_g), row(ln_in_b))
        else:
            x_in = (x,)
        res = _inproj_call(x_in, sc1, sh1, w_in_bf, row(gmlp_ln_g[l]), row(gmlp_ln_b[l]), wcat, bs,
                           row(out_norm_b[l]),
                           n_prompt_tiles=ntp, tiles_per_batch=seq // ROW_TILE, n_batch=nb, n_sample_rows=ts)
        if l == 0:
            x, q, k, v, kf, vf, gv, bn = res
        else:
            q, k, v, kf, vf, gv, bn = res
        k_tail.append(kf)
        v_tail.append(vf)
        gv_rows.append(gv)

        table = rel_bias_table[l]
        bias_p = _rel_bias(table, Q_TILE, 3 * Q_TILE, True)
        bias_s = _rel_bias(table, CHUNK, LEFT_CTX + CHUNK, False)
        na = row(out_norm_a[l])
        an_p = _attn_prompt_call(q, k, v, bias_p, na, n_batch=nb, q_tiles_per_batch=seq // Q_TILE)
        an_s = _attn_sample_call(q, k, v, cache_k[l].reshape(ns, LEFT_CTX, D_A), cache_v[l].reshape(ns, LEFT_CTX, D_A),
                               bias_s[:, :, :LEFT_CTX], bias_s[:, :, LEFT_CTX:], na, first_chunk=tp // CHUNK)

        lg1, lb1, lg2, lb2 = row(ln1_g[l]), row(ln1_b[l]), row(ln2_g[l]), row(ln2_b[l])
        if l % 2 == 0:
            x1, hp = _outproj_call(False, x, an_p, an_s, bn, w_out_bf, g1, sc2, sh2, lg1, lb1)
            nt = t // FFN_TILE
            te = jnp.zeros((nt,), jnp.int32)
            tv = jnp.full((nt,), FFN_TILE, jnp.int32)
            i = l // 2
            f = _ffn_call(te, tv, hp, ffn_w_gate[i:i + 1], ffn_w_up[i:i + 1], ffn_w_down[i:i + 1])
            x = _final_dense_call(x1, f, g2, lg2, lb2)
        else:
            i = l // 2
            r_pad = jnp.pad(moe_router[i], ((0, 0), (0, LANES - N_EXPERTS)))
            r_hi = r_pad.astype(BF16)
            r_lo = (r_pad - r_hi.astype(F32)).astype(BF16)
            r_cat = jnp.concatenate([r_hi, r_lo], axis=1)
            x1, hp, rt = _outproj_call(True, x, an_p, an_s, bn, w_out_bf, g1, sc2, sh2, lg1, lb1, r_cat)
            n_tiles = (2 * t) // FFN_TILE + N_EXPERTS
            pos, te, tv = _route_plan(rt, n_tiles)
            xs = _dispatch_call(pos, hp, n_tiles * FFN_TILE)
            ys = _ffn_call(te, tv, xs, moe_w_gate[i], moe_w_up[i], moe_w_down[i])
            y_p, y_s = _final_moe_call(pos, x1, rt, g2, lg2, lb2, ys, n_prompt_tiles=ntp)

    y_prompt = y_p.reshape(nb, seq, d)
    y_sample = y_s.reshape(ns, dseq, d)

    def tails(rows):
        kp = jnp.stack([r[:nb * ROW_TILE].reshape(nb, ROW_TILE, N_HEADS, HEAD_DIM) for r in rows])
        ksn = jnp.stack([r[nb * ROW_TILE:].reshape(ns, dseq, N_HEADS, HEAD_DIM) for r in rows])
        return kp, ksn

    k_prompt_new, k_sample_new = tails(k_tail)
    v_prompt_new, v_sample_new = tails(v_tail)
    gmlp_v_sample_new = jnp.stack([g.reshape(ns, dseq, D_B) for g in gv_rows])
    return (y_prompt, y_sample, k_prompt_new, v_prompt_new, k_sample_new, v_sample_new, gmlp_v_sample_new)
```

```python
import functools

import jax
import jax.numpy as jnp
from jax import lax
from jax.experimental import pallas as pl
from jax.experimental.pallas import tpu as pltpu

CHUNK = 64
LEFT_CHUNKS = 8
LEFT_CTX = LEFT_CHUNKS * CHUNK
N_HEADS = 8
HEAD_DIM = 64
D_A = N_HEADS * HEAD_DIM
N_GROUPS = 8
GROUP_DIM = 64
D_B = N_GROUPS * GROUP_DIM
MLP_CHUNK = 128
REL_CLIP = 128
N_EXPERTS = 8
DEPTH = 2
ALPHA = (2 * DEPTH) ** 0.25
LN_EPS = 1e-5
ATTN_SCALE = HEAD_DIM ** -0.5
LOG2E = 1.4426950408889634
NEG = -1e30
SH1, SC1, G1, SH2, SC2, G2 = range(6)

LANES = 128
ROW_TILE = 512
Q_TILE = 256
FFN_TILE = 1024
FFN_SUB = 512
FFN_COLS = 256
VMEM_LIMIT = 48 * 1024 * 1024

BF16 = jnp.bfloat16
F32 = jnp.float32


def _cparams(*sem):
    return pltpu.CompilerParams(dimension_semantics=sem, vmem_limit_bytes=VMEM_LIMIT)


def _ln(x, g, b):
    mu = jnp.mean(x, axis=-1, keepdims=True)
    xc = x - mu
    var = jnp.mean(xc * xc, axis=-1, keepdims=True)
    return xc * lax.rsqrt(var + LN_EPS) * g + b


def _rms(x, g):
    return x * lax.rsqrt(jnp.mean(x * x, axis=-1, keepdims=True) + LN_EPS) * g


def _modulate(x, sc_ref, sh_ref):
    parts = []
    for c in range(x.shape[0] // CHUNK):
        xc = x[c * CHUNK:(c + 1) * CHUNK]
        parts.append(xc * (1.0 + sc_ref[c:c + 1, :]) + sh_ref[c:c + 1, :])
    return jnp.concatenate(parts, axis=0)


def _gate_rows(x, g_ref):
    parts = []
    for c in range(x.shape[0] // CHUNK):
        parts.append(x[c * CHUNK:(c + 1) * CHUNK] * (1.0 + g_ref[c:c + 1, :]))
    return jnp.concatenate(parts, axis=0)


def _mod_spec(layer, comp, rows, d):
    return pl.BlockSpec((None, None, rows // CHUNK, d), lambda i, *_: (layer, comp, i, 0))


def _split_bf16(x):
    hi = x.astype(BF16)
    lo = (x - hi.astype(F32)).astype(BF16)
    return hi, lo


def _dot(a, b):
    return jnp.dot(a, b, preferred_element_type=F32)


def _ada_kernel(c_ref, w_ref, b_ref, o_ref):
    c = c_ref[...]
    s = c * jax.nn.sigmoid(c)
    s_hi, s_lo = _split_bf16(s)
    w_hi, w_lo = _split_bf16(w_ref[0])
    o_ref[0] = _dot(s_hi, w_hi) + _dot(s_lo, w_hi) + _dot(s_hi, w_lo) + b_ref[0]


def _ada_call(c_pad, ada_w, ada_b):
    depth, d, n = ada_w.shape
    rows = c_pad.shape[0]
    tn = n // 4
    return pl.pallas_call(
        _ada_kernel,
        grid=(depth, n // tn),
        in_specs=[
            pl.BlockSpec((rows, d), lambda l, j: (0, 0)),
            pl.BlockSpec((1, d, tn), lambda l, j: (l, 0, j)),
            pl.BlockSpec((1, 1, tn), lambda l, j: (l, 0, j)),
        ],
        out_specs=pl.BlockSpec((1, rows, tn), lambda l, j: (l, 0, j)),
        out_shape=jax.ShapeDtypeStruct((depth, rows, n), F32),
        compiler_params=_cparams("arbitrary", "arbitrary"),
        name="ada_mod",
    )(c_pad, ada_w, ada_b.reshape(depth, 1, n))


def _inproj_kernel(first, ntp, *refs):
    if first:
        xp_ref, xs_ref, lng_ref, lnb_ref = refs[:4]
        xn_ref = refs[-8]
        x = jnp.where(pl.program_id(0) >= ntp, xs_ref[...], xp_ref[...])
        x = _ln(x, lng_ref[...], lnb_ref[...])
        xn_ref[...] = x
        refs = refs[4:-8] + refs[-7:]
    else:
        x = refs[0][...]
        refs = refs[1:]
    (sc_ref, sh_ref, w_ref, gg_ref, gb_ref, wcat_ref, bs_ref, nb_ref,
     q_ref, k_ref, v_ref, kf_ref, vf_ref, gv_ref, bn_ref) = refs
    h = _modulate(x, sc_ref, sh_ref).astype(BF16)

    q = _dot(h, w_ref[:, 0:D_A])
    q_ref[...] = (q * (ATTN_SCALE * LOG2E)).astype(BF16)
    k = _dot(h, w_ref[:, D_A:2 * D_A])
    k_ref[...] = k.astype(BF16)
    kf_ref[...] = k
    v = _dot(h, w_ref[:, 2 * D_A:3 * D_A])
    v_ref[...] = v.astype(BF16)
    vf_ref[...] = v

    u = jax.nn.gelu(_dot(h, w_ref[:, 3 * D_A:3 * D_A + D_B]), approximate=True)
    vb = jax.nn.gelu(_dot(h, w_ref[:, 3 * D_A + D_B:3 * D_A + 2 * D_B]), approximate=True)
    vn = _ln(vb, gg_ref[...], gb_ref[...])
    gv_ref[...] = vn

    lane = lax.broadcasted_iota(jnp.int32, (MLP_CHUNK, LANES), 1)
    lo = lane < GROUP_DIM
    vnb = vn.astype(BF16)
    zero = jnp.zeros((MLP_CHUNK, LANES), BF16)
    rows = []
    for c in range(ROW_TILE // MLP_CHUNK):
        r0 = c * MLP_CHUNK
        cols = []
        for p in range(D_B // LANES):
            slab = vnb[r0:r0 + MLP_CHUNK, p * LANES:(p + 1) * LANES]
            rhs = jnp.concatenate([jnp.where(lo, slab, zero), jnp.where(lo, zero, slab)], axis=0)
            mixed = _dot(wcat_ref[0, p], rhs) + bs_ref[0, :, p * LANES:(p + 1) * LANES]
            cols.append(u[r0:r0 + MLP_CHUNK, p * LANES:(p + 1) * LANES] * mixed)
        rows.append(jnp.concatenate(cols, axis=1))
    bo = jnp.concatenate(rows, axis=0)
    bn_ref[...] = _rms(bo, nb_ref[...]).astype(BF16)


def _inproj_call(layer, x_in, mods, w_bf, gg, gb, wcat, bs, nb, *, n_prompt_tiles, tiles_per_batch, n_batch,
                 n_sample_rows):
    first = len(x_in) == 4
    d = x_in[0].shape[1]
    t = mods.shape[2] * CHUNK
    nt = t // ROW_TILE
    ntp = n_prompt_tiles
    n_tail = n_batch + (nt - ntp)

    def tile(i):
        return (i, 0)

    def const2(i):
        return (0, 0)

    def selmap4(i):
        return (jnp.where(i >= ntp, 1, 0), 0, 0, 0)

    def selmap3(i):
        return (jnp.where(i >= ntp, 1, 0), 0, 0)

    def tailmap(i):
        return (jnp.where(i < ntp, i // tiles_per_batch, n_batch + i - ntp), 0)

    def gvmap(i):
        return (jnp.maximum(i - ntp, 0), 0)

    if first:
        x_specs = [
            pl.BlockSpec((ROW_TILE, d), lambda i: (jnp.minimum(i, ntp - 1), 0)),
            pl.BlockSpec((ROW_TILE, d), lambda i: (jnp.maximum(i - ntp, 0), 0)),
            pl.BlockSpec((1, d), const2),
            pl.BlockSpec((1, d), const2),
        ]
    else:
        x_specs = [pl.BlockSpec((ROW_TILE, d), tile)]
    in_specs = x_specs + [
        _mod_spec(layer, SC1, ROW_TILE, d),
        _mod_spec(layer, SH1, ROW_TILE, d),
        pl.BlockSpec(w_bf.shape, const2),
        pl.BlockSpec((1, D_B), const2),
        pl.BlockSpec((1, D_B), const2),
        pl.BlockSpec((1,) + wcat.shape[1:], selmap4),
        pl.BlockSpec((1, MLP_CHUNK, D_B), selmap3),
        pl.BlockSpec((1, D_B), const2),
    ]
    out_specs = [
        pl.BlockSpec((ROW_TILE, D_A), tile),
        pl.BlockSpec((ROW_TILE, D_A), tile),
        pl.BlockSpec((ROW_TILE, D_A), tile),
        pl.BlockSpec((ROW_TILE, D_A), tailmap),
        pl.BlockSpec((ROW_TILE, D_A), tailmap),
        pl.BlockSpec((ROW_TILE, D_B), gvmap),
        pl.BlockSpec((ROW_TILE, D_B), tile),
    ]
    out_shape = [
        jax.ShapeDtypeStruct((t, D_A), BF16),
        jax.ShapeDtypeStruct((t, D_A), BF16),
        jax.ShapeDtypeStruct((t, D_A), BF16),
        jax.ShapeDtypeStruct((n_tail * ROW_TILE, D_A), F32),
        jax.ShapeDtypeStruct((n_tail * ROW_TILE, D_A), F32),
        jax.ShapeDtypeStruct((n_sample_rows, D_B), F32),
        jax.ShapeDtypeStruct((t, D_B), BF16),
    ]
    if first:
        out_specs = [pl.BlockSpec((ROW_TILE, d), tile)] + out_specs
        out_shape = [jax.ShapeDtypeStruct((t, d), F32)] + out_shape
    return pl.pallas_call(
        functools.partial(_inproj_kernel, first, ntp),
        grid=(nt,),
        in_specs=in_specs,
        out_specs=out_specs,
        out_shape=out_shape,
        compiler_params=_cparams("arbitrary"),
        name="in_proj_first" if first else "in_proj",
    )(*x_in, mods, mods, w_bf, gg, gb, wcat, bs, nb)


def _attend(q_ref, kv_slabs, bias_of, part_ok, na):
    r = q_ref.shape[0]
    lane = lax.broadcasted_iota(jnp.int32, (1, LANES), 1)
    lo = lane < HEAD_DIM
    outs = []
    for p in range(D_A // LANES):
        qp = q_ref[:, p * LANES:(p + 1) * LANES]
        slabs = kv_slabs(p)
        o_pair = jnp.zeros((r, LANES), F32)
        for half in range(2):
            h = 2 * p + half
            keep = lo if half == 0 else jnp.logical_not(lo)
            qh = jnp.where(keep, qp, jnp.zeros_like(qp))
            ss = []
            m = None
            for j, (kp, _) in enumerate(slabs):
                s = lax.dot_general(qh, kp, (((1,), (1,)), ((), ())), preferred_element_type=F32) + bias_of(h, j)
                if part_ok is not None and part_ok[j] is not None:
                    s = jnp.where(part_ok[j], s, NEG)
                mj = s.max(axis=1, keepdims=True)
                m = mj if m is None else jnp.maximum(m, mj)
                ss.append(s)
            acc = jnp.zeros((r, LANES), F32)
            for s, (_, vp) in zip(ss, slabs):
                e = jnp.exp2(s - m).astype(BF16)
                acc = acc + _dot(e, jnp.where(keep, vp, jnp.ones_like(vp)))
            denom = acc[:, HEAD_DIM:HEAD_DIM + 1] if half == 0 else acc[:, 0:1]
            o_pair = jnp.where(keep, acc / denom, o_pair)
        outs.append(o_pair)
    a = jnp.concatenate(outs, axis=1)
    return _rms(a, na)


def _attn_prompt_kernel(q_ref, k0_ref, k1_ref, k2_ref, v0_ref, v1_ref, v2_ref, bias_ref, na_ref, o_ref):
    j = pl.program_id(1)
    k_refs = (k0_ref, k1_ref, k2_ref)
    v_refs = (v0_ref, v1_ref, v2_ref)

    def kv_slabs(p):
        sl = slice(p * LANES, (p + 1) * LANES)
        return [(kr[:, sl], vr[:, sl]) for kr, vr in zip(k_refs, v_refs)]

    def bias_of(h, b):
        return bias_ref[h, :, b * Q_TILE:(b + 1) * Q_TILE]

    @pl.when(j >= 2)
    def _():
        o_ref[...] = _attend(q_ref, kv_slabs, bias_of, None, na_ref[...]).astype(BF16)

    @pl.when(j < 2)
    def _():
        part_ok = [j >= 2, j >= 1, None]
        o_ref[...] = _attend(q_ref, kv_slabs, bias_of, part_ok, na_ref[...]).astype(BF16)


def _attn_prompt_call(q, k, v, bias, na, *, n_batch, q_tiles_per_batch):
    nq = q_tiles_per_batch

    def qmap(b, j):
        return (b * nq + j, 0)

    def kmap(off):
        return lambda b, j: (b * nq + jnp.maximum(j - off, 0), 0)

    blk = pl.BlockSpec((Q_TILE, D_A), qmap)
    kv_specs = [pl.BlockSpec((Q_TILE, D_A), kmap(off)) for off in (2, 1, 0)]
    return pl.pallas_call(
        _attn_prompt_kernel,
        grid=(n_batch, nq),
        in_specs=[blk] + kv_specs + kv_specs + [
            pl.BlockSpec(bias.shape, lambda b, j: (0, 0, 0)),
            pl.BlockSpec((1, D_A), lambda b, j: (0, 0)),
        ],
        out_specs=blk,
        out_shape=jax.ShapeDtypeStruct((n_batch * nq * Q_TILE, D_A), BF16),
        compiler_params=_cparams("arbitrary", "arbitrary"),
        name="attn_prompt",
    )(q, k, k, k, v, v, v, bias, na)


def _attn_sample_kernel(q_ref, kn_ref, vn_ref, ck_ref, cv_ref, bias_c_ref, bias_n_ref, na_ref, o_ref):
    def kv_slabs(p):
        sl = slice(p * LANES, (p + 1) * LANES)

        def cached(c_ref):
            return jnp.concatenate([c_ref[:, 2 * p, :], c_ref[:, 2 * p + 1, :]], axis=1).astype(BF16)
        return [(cached(ck_ref), cached(cv_ref)), (kn_ref[:, sl], vn_ref[:, sl])]

    def bias_of(h, b):
        return bias_c_ref[h] if b == 0 else bias_n_ref[h]

    o_ref[...] = _attend(q_ref, kv_slabs, bias_of, None, na_ref[...]).astype(BF16)


def _attn_sample_call(layer, q, k, v, cache_k, cache_v, bias_c, bias_n, na, *, first_chunk):
    _, nb, win, nh, hd = cache_k.shape
    blk = pl.BlockSpec((CHUNK, D_A), lambda b: (first_chunk + b, 0))
    cache = pl.BlockSpec((None, None, win, nh, hd), lambda b: (layer, b, 0, 0, 0))
    return pl.pallas_call(
        _attn_sample_kernel,
        grid=(nb,),
        in_specs=[blk, blk, blk, cache, cache,
                  pl.BlockSpec(bias_c.shape, lambda b: (0, 0, 0)),
                  pl.BlockSpec(bias_n.shape, lambda b: (0, 0, 0)),
                  pl.BlockSpec((1, D_A), lambda b: (0, 0))],
        out_specs=pl.BlockSpec((CHUNK, D_A), lambda b: (b, 0)),
        out_shape=jax.ShapeDtypeStruct((nb * CHUNK, D_A), BF16),
        compiler_params=_cparams("arbitrary"),
        name="attn_sample",
    )(q, k, v, cache_k, cache_v, bias_c, bias_n, na)


def _outproj_kernel(route, ntp, x_ref, anp_ref, ans_ref, bn_ref, w_ref, g1_ref, sc_ref, sh_ref, lg_ref, lb_ref, *rest):
    if route:
        rc_ref, x1_ref, hp_ref, rt_ref = rest
    else:
        x1_ref, hp_ref = rest
    an = jnp.where(pl.program_id(0) >= ntp, ans_ref[...], anp_ref[...])
    mix = _dot(an, w_ref[0:D_A, :]) + _dot(bn_ref[...], w_ref[D_A:D_A + D_B, :])
    x1 = _ln(ALPHA * x_ref[...] + _gate_rows(mix, g1_ref), lg_ref[...], lb_ref[...])
    x1_ref[...] = x1
    h2 = _modulate(x1, sc_ref, sh_ref)
    hp_ref[...] = h2
    if route:
        h_hi, h_lo = _split_bf16(h2)
        both = _dot(h_hi, rc_ref[...])
        logits = both[:, :LANES] + both[:, LANES:] + _dot(h_lo, rc_ref[:, :LANES])
        lane = lax.broadcasted_iota(jnp.int32, logits.shape, 1)
        logits = jnp.where(lane < N_EXPERTS, logits, -jnp.inf)
        m1 = logits.max(axis=1, keepdims=True)
        i1 = jnp.where(logits == m1, lane, LANES).min(axis=1, keepdims=True)
        rest_l = jnp.where(lane == i1, -jnp.inf, logits)
        m2 = rest_l.max(axis=1, keepdims=True)
        i2 = jnp.where(rest_l == m2, lane, LANES).min(axis=1, keepdims=True)
        e2 = jnp.exp(m2 - m1)
        w1 = 1.0 / (1.0 + e2)
        w2 = e2 / (1.0 + e2)
        rt = jnp.where(lane == 0, i1.astype(F32),
                       jnp.where(lane == 1, i2.astype(F32),
                                 jnp.where(lane == 2, w1, jnp.where(lane == 3, w2, 0.0))))
        rt_ref[...] = rt


def _outproj_call(layer, route, x, an_p, an_s, bn, w_bf, mods, lg, lb, r_cat=None):
    t, d = x.shape
    nt = t // ROW_TILE
    ntp = an_p.shape[0] // ROW_TILE

    def tile(i):
        return (i, 0)

    def const2(i):
        return (0, 0)

    in_specs = [
        pl.BlockSpec((ROW_TILE, d), tile),
        pl.BlockSpec((ROW_TILE, D_A), lambda i: (jnp.minimum(i, ntp - 1), 0)),
        pl.BlockSpec((ROW_TILE, D_A), lambda i: (jnp.maximum(i - ntp, 0), 0)),
        pl.BlockSpec((ROW_TILE, D_B), tile),
        pl.BlockSpec(w_bf.shape, const2),
        _mod_spec(layer, G1, ROW_TILE, d),
        _mod_spec(layer, SC2, ROW_TILE, d),
        _mod_spec(layer, SH2, ROW_TILE, d),
        pl.BlockSpec((1, d), const2),
        pl.BlockSpec((1, d), const2),
    ]
    out_specs = [pl.BlockSpec((ROW_TILE, d), tile), pl.BlockSpec((ROW_TILE, d), tile)]
    out_shape = [jax.ShapeDtypeStruct((t, d), F32), jax.ShapeDtypeStruct((t, d), F32)]
    args = [x, an_p, an_s, bn, w_bf, mods, mods, mods, lg, lb]
    if route:
        in_specs.append(pl.BlockSpec(r_cat.shape, const2))
        out_specs.append(pl.BlockSpec((ROW_TILE, LANES), tile))
        out_shape.append(jax.ShapeDtypeStruct((t, LANES), F32))
        args.append(r_cat)
    return pl.pallas_call(
        functools.partial(_outproj_kernel, route, ntp),
        grid=(nt,),
        in_specs=in_specs,
        out_specs=out_specs,
        out_shape=out_shape,
        compiler_params=_cparams("arbitrary"),
        name="out_proj_route" if route else "out_proj",
    )(*args)


def _ffn_kernel(close, te_ref, tv_ref, x_ref, wg_ref, wu_ref, wd_ref, *rest):
    if close:
        x1_ref, g2_ref, lg_ref, lb_ref, o_ref, xb_ref = rest
    else:
        o_ref, xb_ref = rest
    del te_ref
    i = pl.program_id(0)
    j = pl.program_id(1)
    valid = tv_ref[i]
    nsub = (valid + FFN_SUB - 1) // FFN_SUB

    @pl.when(j == 0)
    def _():
        xb_ref[...] = x_ref[...].astype(BF16)
        o_ref[...] = jnp.zeros(o_ref.shape, F32)

    def run(n_sub):
        wg = wg_ref[0].astype(BF16)
        wu = wu_ref[0].astype(BF16)
        wd = wd_ref[0].astype(BF16)
        for s in range(n_sub):
            rows = slice(s * FFN_SUB, (s + 1) * FFN_SUB)
            xs = xb_ref[rows, :]
            g = _dot(xs, wg)
            u = _dot(xs, wu)
            a = (g * jax.nn.sigmoid(g) * u).astype(BF16)
            o_ref[rows, :] += _dot(a, wd)

    for n_sub in range(1, FFN_TILE // FFN_SUB + 1):
        pl.when(nsub == n_sub)(functools.partial(run, n_sub))

    if close:
        @pl.when(j == pl.num_programs(1) - 1)
        def _():
            o_ref[...] = _ln(ALPHA * x1_ref[...] + _gate_rows(o_ref[...], g2_ref), lg_ref[...], lb_ref[...])


def _ffn_call(tile_expert, tile_valid, xs, wg, wu, wd, closing=None):
    p, d = xs.shape
    f = wg.shape[2]
    nt = p // FFN_TILE
    nc = f // FFN_COLS

    def active_col(i, j, tv):
        return jnp.where(tv[i] > 0, j, nc - 1)

    in_specs = [
        pl.BlockSpec((FFN_TILE, d), lambda i, j, te, tv: (i, 0)),
        pl.BlockSpec((1, d, FFN_COLS), lambda i, j, te, tv: (te[i], 0, active_col(i, j, tv))),
        pl.BlockSpec((1, d, FFN_COLS), lambda i, j, te, tv: (te[i], 0, active_col(i, j, tv))),
        pl.BlockSpec((1, FFN_COLS, d), lambda i, j, te, tv: (te[i], active_col(i, j, tv), 0)),
    ]
    args = [tile_expert, tile_valid, xs, wg, wu, wd]
    if closing is not None:
        layer, x1, mods, lg, lb = closing
        in_specs += [
            pl.BlockSpec((FFN_TILE, d), lambda i, j, te, tv: (i, 0)),
            _mod_spec(layer, G2, FFN_TILE, d),
            pl.BlockSpec((1, d), lambda i, j, te, tv: (0, 0)),
            pl.BlockSpec((1, d), lambda i, j, te, tv: (0, 0)),
        ]
        args += [x1, mods, lg, lb]
    return pl.pallas_call(
        functools.partial(_ffn_kernel, closing is not None),
        grid_spec=pltpu.PrefetchScalarGridSpec(
            num_scalar_prefetch=2,
            grid=(nt, nc),
            in_specs=in_specs,
            out_specs=pl.BlockSpec((FFN_TILE, d), lambda i, j, te, tv: (i, 0)),
            scratch_shapes=[pltpu.VMEM((FFN_TILE, d), BF16)],
        ),
        out_shape=jax.ShapeDtypeStruct((p, d), F32),
        compiler_params=_cparams("arbitrary", "arbitrary"),
        name="swiglu_close" if closing is not None else "swiglu_grouped",
    )(*args)


def _dispatch_kernel(pos_ref, src_ref, init_ref, dst_ref, sem):
    del init_ref

    def issue(r, carry):
        row = src_ref.at[pl.ds(r, 1), :]
        pltpu.make_async_copy(row, dst_ref.at[pl.ds(pos_ref[0, 2 * r], 1), :], sem).start()
        pltpu.make_async_copy(row, dst_ref.at[pl.ds(pos_ref[0, 2 * r + 1], 1), :], sem).start()
        return carry
    lax.fori_loop(0, ROW_TILE, issue, 0, unroll=8)
    for _ in range(2):
        pltpu.make_async_copy(src_ref, dst_ref.at[pl.ds(0, ROW_TILE), :], sem).wait()


def _dispatch_call(pos_tiles, hp, n_rows_sorted):
    t, d = hp.shape
    nt = t // ROW_TILE
    return pl.pallas_call(
        _dispatch_kernel,
        grid=(nt,),
        in_specs=[
            pl.BlockSpec((None, 1, 2 * ROW_TILE), lambda i: (i, 0, 0), memory_space=pltpu.SMEM),
            pl.BlockSpec((ROW_TILE, d), lambda i: (i, 0)),
            pl.BlockSpec(memory_space=pl.ANY),
        ],
        out_specs=pl.BlockSpec(memory_space=pl.ANY),
        out_shape=jax.ShapeDtypeStruct((n_rows_sorted, d), hp.dtype),
        scratch_shapes=[pltpu.SemaphoreType.DMA(())],
        input_output_aliases={2: 0},
        compiler_params=_cparams("arbitrary"),
        name="dispatch_rows",
    )(pos_tiles, hp, jnp.zeros((n_rows_sorted, d), hp.dtype))


def _final_moe_kernel(ntp, pos_ref, x_ref, rt_ref, g2_ref, lg_ref, lb_ref, ys_ref, op_ref, os_ref, ybuf, sem):
    def issue(r, carry):
        pltpu.make_async_copy(ys_ref.at[pl.ds(pos_ref[0, 2 * r], 1), :], ybuf.at[0, pl.ds(r, 1), :], sem).start()
        pltpu.make_async_copy(ys_ref.at[pl.ds(pos_ref[0, 2 * r + 1], 1), :], ybuf.at[1, pl.ds(r, 1), :], sem).start()
        return carry
    lax.fori_loop(0, ROW_TILE, issue, 0, unroll=8)
    for k in range(2):
        pltpu.make_async_copy(ys_ref.at[pl.ds(0, ROW_TILE), :], ybuf.at[k], sem).wait()
    rt = rt_ref[...]
    f = rt[:, 2:3] * ybuf[0] + rt[:, 3:4] * ybuf[1]
    y = _ln(ALPHA * x_ref[...] + _gate_rows(f, g2_ref), lg_ref[...], lb_ref[...])
    i = pl.program_id(0)

    @pl.when(i < ntp)
    def _():
        op_ref[...] = y

    @pl.when(i >= ntp)
    def _():
        os_ref[...] = y


def _final_moe_call(layer, pos_tiles, x1, rt, mods, lg, lb, ys, *, n_prompt_tiles):
    t, d = x1.shape
    ntp = n_prompt_tiles
    tile = pl.BlockSpec((ROW_TILE, d), lambda i: (i, 0))
    vec = pl.BlockSpec((1, d), lambda i: (0, 0))
    return pl.pallas_call(
        functools.partial(_final_moe_kernel, ntp),
        grid=(t // ROW_TILE,),
        in_specs=[
            pl.BlockSpec((None, 1, 2 * ROW_TILE), lambda i: (i, 0, 0), memory_space=pltpu.SMEM),
            tile,
            pl.BlockSpec((ROW_TILE, LANES), lambda i: (i, 0)),
            _mod_spec(layer, G2, ROW_TILE, d),
            vec, vec,
            pl.BlockSpec(memory_space=pl.ANY),
        ],
        out_specs=[pl.BlockSpec((ROW_TILE, d), lambda i: (jnp.minimum(i, ntp - 1), 0)),
                   pl.BlockSpec((ROW_TILE, d), lambda i: (jnp.maximum(i - ntp, 0), 0))],
        out_shape=[jax.ShapeDtypeStruct((ntp * ROW_TILE, d), F32),
                   jax.ShapeDtypeStruct((t - ntp * ROW_TILE, d), F32)],
        scratch_shapes=[pltpu.VMEM((2, ROW_TILE, d), F32), pltpu.SemaphoreType.DMA(())],
        compiler_params=_cparams("arbitrary"),
        name="final_moe",
    )(pos_tiles, x1, rt, mods, lg, lb, ys)


def _rel_bias(table, n_q, n_k, banded):
    r = jnp.arange(n_q)[:, None]
    w = jnp.arange(n_k)[None, :]
    n_diag = n_q + n_k - 1
    k = jnp.arange(n_diag)
    diag = table[:, jnp.clip(LEFT_CTX + (n_q - 1) - k, -REL_CLIP, REL_CLIP) + REL_CLIP].astype(F32) * LOG2E
    padded = jnp.concatenate([diag, jnp.zeros((diag.shape[0], 1), F32)], axis=1)
    skew = jnp.tile(padded, (1, n_q))[:, :n_q * n_diag].reshape(-1, n_q, n_diag)
    bias = skew[:, :, n_q - 1:n_q - 1 + n_k]
    if banded:
        qa = r // CHUNK
        kc = w // CHUNK
        vis = (kc >= qa) & (kc <= qa + LEFT_CHUNKS)
        bias = jnp.where(vis[None], bias, NEG)
    return bias


def _gating_weights(w_s, b_s):
    n = MLP_CHUNK
    tril = jnp.tril(jnp.ones((n, n), bool))
    wm = jnp.where(tril[None], w_s, 0.0)
    h = CHUNK
    top = wm[:, :h, :h]
    z = jnp.zeros_like(top)
    wm_s = jnp.concatenate([jnp.concatenate([top, z], 2), jnp.concatenate([z, top], 2)], 1)
    both = jnp.stack([wm, wm_s])
    wcat = jnp.concatenate([both[:, 0::2], both[:, 1::2]], axis=-1)
    bias_p = jnp.repeat(jnp.transpose(b_s), GROUP_DIM, axis=1)
    bias_s = jnp.concatenate([bias_p[:h], bias_p[:h]], 0)
    return wcat.astype(BF16), jnp.stack([bias_p, bias_s]).astype(F32)


def _route_plan(rt, n_tiles):
    t = rt.shape[0]
    e_flat = rt[:, 0:2].astype(jnp.int32).reshape(-1)
    oh = (e_flat[:, None] == jnp.arange(N_EXPERTS)[None, :]).astype(jnp.int32)
    csum = jnp.cumsum(oh, axis=0)
    rank = jnp.sum(oh * (csum - 1), axis=1)
    counts = csum[-1]
    tiles_e = (counts + FFN_TILE - 1) // FFN_TILE
    tile_end = jnp.cumsum(tiles_e)
    tile_start = tile_end - tiles_e
    pos = (tile_start * FFN_TILE)[e_flat] + rank
    ti = jnp.arange(n_tiles)
    te = jnp.minimum(jnp.sum((ti[:, None] >= tile_end[None, :]).astype(jnp.int32), axis=1), N_EXPERTS - 1)
    tv = jnp.clip(counts[te] - (ti - tile_start[te]) * FFN_TILE, 0, FFN_TILE)
    tv = jnp.where(ti < tile_end[-1], tv, 0)
    last_e = te[jnp.maximum(tile_end[-1] - 1, 0)]
    te = jnp.where(ti < tile_end[-1], te, last_e)
    return pos.reshape(t // ROW_TILE, 1, 2 * ROW_TILE).astype(jnp.int32), te.astype(jnp.int32), tv.astype(jnp.int32)


def kernel(x_prompt, x_sample, cache_k, cache_v, c_prompt, c_sample, ln_in_g, ln_in_b, w_in, w_out, rel_bias_table, gmlp_ln_g, gmlp_ln_b, gmlp_w_s, gmlp_b_s, out_norm_a, out_norm_b, ada_w, ada_b, ln1_g, ln1_b, ln2_g, ln2_b, ffn_w_gate, ffn_w_up, ffn_w_down, moe_router, moe_w_gate, moe_w_up, moe_w_down):
    nb, seq, d = x_prompt.shape
    ns, dseq, _ = x_sample.shape
    depth = w_in.shape[0]
    tp = nb * seq
    ts = ns * dseq
    t = tp + ts
    assert depth == DEPTH and dseq == CHUNK and seq % ROW_TILE == 0 and ts % ROW_TILE == 0
    assert t % FFN_TILE == 0 and cache_k.shape[2] == LEFT_CTX
    ntp = tp // ROW_TILE

    n_cond = nb + ns
    c_all = jnp.concatenate([c_prompt, c_sample], axis=0)
    c_pad = jnp.pad(c_all, ((0, (-n_cond) % 8), (0, 0)))
    mods = _ada_call(c_pad, ada_w, ada_b)
    mods = jnp.transpose(mods[:, :n_cond].reshape(depth, n_cond, 6, d), (0, 2, 1, 3))
    cpb = seq // CHUNK
    mods_p = jnp.broadcast_to(mods[:, :, :nb, None, :], (depth, 6, nb, cpb, d)).reshape(depth, 6, nb * cpb, d)
    mods = jnp.concatenate([mods_p, mods[:, :, nb:]], axis=2)

    def row(v):
        return v.reshape(1, -1)

    w_in_bf = w_in.astype(BF16)
    w_out_bf = w_out.astype(BF16)
    k_tail, v_tail, gv_rows = [], [], []
    for l in range(depth):
        wcat, bs = _gating_weights(gmlp_w_s[l], gmlp_b_s[l])
        if l == 0:
            x_in = (x_prompt.reshape(tp, d), x_sample.reshape(ts, d), row(ln_in_g), row(ln_in_b))
        else:
            x_in = (x,)
        res = _inproj_call(l, x_in, mods, w_in_bf[l], row(gmlp_ln_g[l]), row(gmlp_ln_b[l]), wcat, bs,
                           row(out_norm_b[l]),
                           n_prompt_tiles=ntp, tiles_per_batch=seq // ROW_TILE, n_batch=nb, n_sample_rows=ts)
        if l == 0:
            x, q, k, v, kf, vf, gv, bn = res
        else:
            q, k, v, kf, vf, gv, bn = res
        k_tail.append(kf)
        v_tail.append(vf)
        gv_rows.append(gv)

        table = rel_bias_table[l]
        bias_p = _rel_bias(table, Q_TILE, 3 * Q_TILE, True)
        bias_s = _rel_bias(table, CHUNK, LEFT_CTX + CHUNK, False)
        na = row(out_norm_a[l])
        an_p = _attn_prompt_call(q, k, v, bias_p, na, n_batch=nb, q_tiles_per_batch=seq // Q_TILE)
        an_s = _attn_sample_call(l, q, k, v, cache_k, cache_v, bias_s[:, :, :LEFT_CTX], bias_s[:, :, LEFT_CTX:], na,
                                 first_chunk=tp // CHUNK)

        lg1, lb1, lg2, lb2 = row(ln1_g[l]), row(ln1_b[l]), row(ln2_g[l]), row(ln2_b[l])
        i = l // 2
        if l % 2 == 0:
            x1, hp = _outproj_call(l, False, x, an_p, an_s, bn, w_out_bf[l], mods, lg1, lb1)
            nt = t // FFN_TILE
            te = jnp.zeros((nt,), jnp.int32)
            tv = jnp.full((nt,), FFN_TILE, jnp.int32)
            x = _ffn_call(te, tv, hp, ffn_w_gate[i:i + 1], ffn_w_up[i:i + 1], ffn_w_down[i:i + 1],
                          closing=(l, x1, mods, lg2, lb2))
        else:
            r_pad = jnp.pad(moe_router[i], ((0, 0), (0, LANES - N_EXPERTS)))
            r_hi = r_pad.astype(BF16)
            r_lo = (r_pad - r_hi.astype(F32)).astype(BF16)
            r_cat = jnp.concatenate([r_hi, r_lo], axis=1)
            x1, hp, rt = _outproj_call(l, True, x, an_p, an_s, bn, w_out_bf[l], mods, lg1, lb1, r_cat)
            n_tiles = (2 * t) // FFN_TILE + N_EXPERTS
            pos, te, tv = _route_plan(rt, n_tiles)
            xs = _dispatch_call(pos, hp, n_tiles * FFN_TILE)
            ys = _ffn_call(te, tv, xs, moe_w_gate[i], moe_w_up[i], moe_w_down[i])
            y_p, y_s = _final_moe_call(l, pos, x1, rt, mods, lg2, lb2, ys, n_prompt_tiles=ntp)

    y_prompt = y_p.reshape(nb, seq, d)
    y_sample = y_s.reshape(ns, dseq, d)

    def tails(rows):
        kp = jnp.stack([r[:nb * ROW_TILE].reshape(nb, ROW_TILE, N_HEADS, HEAD_DIM) for r in rows])
        ksn = jnp.stack([r[nb * ROW_TILE:].reshape(ns, dseq, N_HEADS, HEAD_DIM) for r in rows])
        return kp, ksn

    k_prompt_new, k_sample_new = tails(k_tail)
    v_prompt_new, v_sample_new = tails(v_tail)
    gmlp_v_sample_new = jnp.stack([g.reshape(ns, dseq, D_B) for g in gv_rows])
    return (y_prompt, y_sample, k_prompt_new, v_prompt_new, k_sample_new, v_sample_new, gmlp_v_sample_new)
```

```python
import functools

import jax
import jax.numpy as jnp
from jax import lax
from jax.experimental import pallas as pl
from jax.experimental.pallas import tpu as pltpu

CHUNK = 64
LEFT_CHUNKS = 8
LEFT_CTX = LEFT_CHUNKS * CHUNK
N_HEADS = 8
HEAD_DIM = 64
D_A = N_HEADS * HEAD_DIM
N_GROUPS = 8
GROUP_DIM = 64
D_B = N_GROUPS * GROUP_DIM
MLP_CHUNK = 128
REL_CLIP = 128
N_EXPERTS = 8
DEPTH = 2
ALPHA = (2 * DEPTH) ** 0.25
LN_EPS = 1e-5
ATTN_SCALE = HEAD_DIM ** -0.5
LOG2E = 1.4426950408889634
NEG = -1e30
SH1, SC1, G1, SH2, SC2, G2 = range(6)

LANES = 128
ROW_TILE = 512
Q_TILE = 256
FFN_TILE = 1024
MOE_TILE = 2048
FFN_SUB = 512
FFN_COLS = 256
VMEM_LIMIT = 56 * 1024 * 1024

BF16 = jnp.bfloat16
F32 = jnp.float32


def _cparams(*sem):
    return pltpu.CompilerParams(dimension_semantics=sem, vmem_limit_bytes=VMEM_LIMIT)


def _ln(x, g, b):
    mu = jnp.mean(x, axis=-1, keepdims=True)
    xc = x - mu
    var = jnp.mean(xc * xc, axis=-1, keepdims=True)
    return xc * lax.rsqrt(var + LN_EPS) * g + b


def _rms(x, g):
    return x * lax.rsqrt(jnp.mean(x * x, axis=-1, keepdims=True) + LN_EPS) * g


def _modulate(x, sc_ref, sh_ref):
    parts = []
    for c in range(x.shape[0] // CHUNK):
        xc = x[c * CHUNK:(c + 1) * CHUNK]
        parts.append(xc * (1.0 + sc_ref[c:c + 1, :]) + sh_ref[c:c + 1, :])
    return jnp.concatenate(parts, axis=0)


def _gate_rows(x, g_ref):
    parts = []
    for c in range(x.shape[0] // CHUNK):
        parts.append(x[c * CHUNK:(c + 1) * CHUNK] * (1.0 + g_ref[c:c + 1, :]))
    return jnp.concatenate(parts, axis=0)


def _mod_spec(layer, comp, rows, d):
    return pl.BlockSpec((None, None, rows // CHUNK, d), lambda i, *_: (layer, comp, i, 0))


def _split_bf16(x):
    hi = x.astype(BF16)
    lo = (x - hi.astype(F32)).astype(BF16)
    return hi, lo


def _dot(a, b):
    return jnp.dot(a, b, preferred_element_type=F32)


def _ada_kernel(c_ref, w_ref, b_ref, o_ref):
    c = c_ref[...]
    s = c * jax.nn.sigmoid(c)
    s_hi, s_lo = _split_bf16(s)
    w_hi, w_lo = _split_bf16(w_ref[0])
    o_ref[0] = _dot(s_hi, w_hi) + _dot(s_lo, w_hi) + _dot(s_hi, w_lo) + b_ref[0]


def _ada_call(c_pad, ada_w, ada_b):
    depth, d, n = ada_w.shape
    rows = c_pad.shape[0]
    tn = n // 4
    return pl.pallas_call(
        _ada_kernel,
        grid=(depth, n // tn),
        in_specs=[
            pl.BlockSpec((rows, d), lambda l, j: (0, 0)),
            pl.BlockSpec((1, d, tn), lambda l, j: (l, 0, j)),
            pl.BlockSpec((1, 1, tn), lambda l, j: (l, 0, j)),
        ],
        out_specs=pl.BlockSpec((1, rows, tn), lambda l, j: (l, 0, j)),
        out_shape=jax.ShapeDtypeStruct((depth, rows, n), F32),
        compiler_params=_cparams("arbitrary", "arbitrary"),
        name="ada_mod",
    )(c_pad, ada_w, ada_b.reshape(depth, 1, n))


def _inproj_kernel(first, ntp, *refs):
    if first:
        xp_ref, xs_ref, lng_ref, lnb_ref = refs[:4]
        xn_ref = refs[-8]
        x = jnp.where(pl.program_id(0) >= ntp, xs_ref[...], xp_ref[...])
        x = _ln(x, lng_ref[...], lnb_ref[...])
        xn_ref[...] = x
        refs = refs[4:-8] + refs[-7:]
    else:
        x = refs[0][...]
        refs = refs[1:]
    (sc_ref, sh_ref, w_ref, gg_ref, gb_ref, wcat_ref, bs_ref, nb_ref,
     q_ref, k_ref, v_ref, kf_ref, vf_ref, gv_ref, bn_ref) = refs
    h = _modulate(x, sc_ref, sh_ref).astype(BF16)

    q = _dot(h, w_ref[:, 0:D_A])
    q_ref[...] = (q * (ATTN_SCALE * LOG2E)).astype(BF16)
    k = _dot(h, w_ref[:, D_A:2 * D_A])
    k_ref[...] = k.astype(BF16)
    kf_ref[...] = k
    v = _dot(h, w_ref[:, 2 * D_A:3 * D_A])
    v_ref[...] = v.astype(BF16)
    vf_ref[...] = v

    u = jax.nn.gelu(_dot(h, w_ref[:, 3 * D_A:3 * D_A + D_B]), approximate=True)
    vb = jax.nn.gelu(_dot(h, w_ref[:, 3 * D_A + D_B:3 * D_A + 2 * D_B]), approximate=True)
    vn = _ln(vb, gg_ref[...], gb_ref[...])
    gv_ref[...] = vn

    lane = lax.broadcasted_iota(jnp.int32, (MLP_CHUNK, LANES), 1)
    lo = lane < GROUP_DIM
    vnb = vn.astype(BF16)
    zero = jnp.zeros((MLP_CHUNK, LANES), BF16)
    rows = []
    for c in range(ROW_TILE // MLP_CHUNK):
        r0 = c * MLP_CHUNK
        cols = []
        for p in range(D_B // LANES):
            slab = vnb[r0:r0 + MLP_CHUNK, p * LANES:(p + 1) * LANES]
            rhs = jnp.concatenate([jnp.where(lo, slab, zero), jnp.where(lo, zero, slab)], axis=0)
            mixed = _dot(wcat_ref[0, p], rhs) + bs_ref[0, :, p * LANES:(p + 1) * LANES]
            cols.append(u[r0:r0 + MLP_CHUNK, p * LANES:(p + 1) * LANES] * mixed)
        rows.append(jnp.concatenate(cols, axis=1))
    bo = jnp.concatenate(rows, axis=0)
    bn_ref[...] = _rms(bo, nb_ref[...]).astype(BF16)


def _inproj_call(layer, x_in, mods, w_bf, gg, gb, wcat, bs, nb, *, n_prompt_tiles, tiles_per_batch, n_batch,
                 n_sample_rows):
    first = len(x_in) == 4
    d = x_in[0].shape[1]
    t = mods.shape[2] * CHUNK
    nt = t // ROW_TILE
    ntp = n_prompt_tiles
    n_tail = n_batch + (nt - ntp)

    def tile(i):
        return (i, 0)

    def const2(i):
        return (0, 0)

    def selmap4(i):
        return (jnp.where(i >= ntp, 1, 0), 0, 0, 0)

    def selmap3(i):
        return (jnp.where(i >= ntp, 1, 0), 0, 0)

    def tailmap(i):
        return (jnp.where(i < ntp, i // tiles_per_batch, n_batch + i - ntp), 0)

    def gvmap(i):
        return (jnp.maximum(i - ntp, 0), 0)

    if first:
        x_specs = [
            pl.BlockSpec((ROW_TILE, d), lambda i: (jnp.minimum(i, ntp - 1), 0)),
            pl.BlockSpec((ROW_TILE, d), lambda i: (jnp.maximum(i - ntp, 0), 0)),
            pl.BlockSpec((1, d), const2),
            pl.BlockSpec((1, d), const2),
        ]
    else:
        x_specs = [pl.BlockSpec((ROW_TILE, d), tile)]
    in_specs = x_specs + [
        _mod_spec(layer, SC1, ROW_TILE, d),
        _mod_spec(layer, SH1, ROW_TILE, d),
        pl.BlockSpec(w_bf.shape, const2),
        pl.BlockSpec((1, D_B), const2),
        pl.BlockSpec((1, D_B), const2),
        pl.BlockSpec((1,) + wcat.shape[1:], selmap4),
        pl.BlockSpec((1, MLP_CHUNK, D_B), selmap3),
        pl.BlockSpec((1, D_B), const2),
    ]
    out_specs = [
        pl.BlockSpec((ROW_TILE, D_A), tile),
        pl.BlockSpec((ROW_TILE, D_A), tile),
        pl.BlockSpec((ROW_TILE, D_A), tile),
        pl.BlockSpec((ROW_TILE, D_A), tailmap),
        pl.BlockSpec((ROW_TILE, D_A), tailmap),
        pl.BlockSpec((ROW_TILE, D_B), gvmap),
        pl.BlockSpec((ROW_TILE, D_B), tile),
    ]
    out_shape = [
        jax.ShapeDtypeStruct((t, D_A), BF16),
        jax.ShapeDtypeStruct((t, D_A), BF16),
        jax.ShapeDtypeStruct((t, D_A), BF16),
        jax.ShapeDtypeStruct((n_tail * ROW_TILE, D_A), F32),
        jax.ShapeDtypeStruct((n_tail * ROW_TILE, D_A), F32),
        jax.ShapeDtypeStruct((n_sample_rows, D_B), F32),
        jax.ShapeDtypeStruct((t, D_B), BF16),
    ]
    if first:
        out_specs = [pl.BlockSpec((ROW_TILE, d), tile)] + out_specs
        out_shape = [jax.ShapeDtypeStruct((t, d), F32)] + out_shape
    return pl.pallas_call(
        functools.partial(_inproj_kernel, first, ntp),
        grid=(nt,),
        in_specs=in_specs,
        out_specs=out_specs,
        out_shape=out_shape,
        compiler_params=_cparams("arbitrary"),
        name="in_proj_first" if first else "in_proj",
    )(*x_in, mods, mods, w_bf, gg, gb, wcat, bs, nb)


def _attend(q_ref, kv_slabs, bias_of, part_ok, na):
    r = q_ref.shape[0]
    nt_dims = (((1,), (1,)), ((), ()))
    lo = lax.broadcasted_iota(jnp.int32, (1, LANES), 1) < HEAD_DIM
    lo_t = lax.broadcasted_iota(jnp.int32, (LANES, 1), 0) < HEAD_DIM
    outs = []
    for p in range(D_A // LANES):
        qp = q_ref[:, p * LANES:(p + 1) * LANES]
        slabs = kv_slabs(p)
        o_pair = jnp.zeros((r, LANES), F32)
        for half in range(2):
            h = 2 * p + half
            keep = lo if half == 0 else jnp.logical_not(lo)
            keep_t = lo_t if half == 0 else jnp.logical_not(lo_t)
            qh = jnp.where(keep, qp, jnp.zeros_like(qp))
            ss = []
            m = None
            for j, (kp, _, transposed) in enumerate(slabs):
                if transposed:
                    s = _dot(qh, kp)
                else:
                    s = lax.dot_general(qh, kp, nt_dims, preferred_element_type=F32)
                s = s + bias_of(h, j)
                if part_ok is not None and part_ok[j] is not None:
                    s = jnp.where(part_ok[j], s, NEG)
                mj = s.max(axis=1, keepdims=True)
                m = mj if m is None else jnp.maximum(m, mj)
                ss.append(s)
            acc = jnp.zeros((r, LANES), F32)
            for s, (_, vp, transposed) in zip(ss, slabs):
                e = jnp.exp2(s - m).astype(BF16)
                if transposed:
                    vh = jnp.where(keep_t, vp, jnp.ones_like(vp))
                    acc = acc + lax.dot_general(e, vh, nt_dims, preferred_element_type=F32)
                else:
                    acc = acc + _dot(e, jnp.where(keep, vp, jnp.ones_like(vp)))
            denom = acc[:, HEAD_DIM:HEAD_DIM + 1] if half == 0 else acc[:, 0:1]
            o_pair = jnp.where(keep, acc / denom, o_pair)
        outs.append(o_pair)
    a = jnp.concatenate(outs, axis=1)
    return _rms(a, na)


def _attn_prompt_kernel(q_ref, k0_ref, k1_ref, k2_ref, v0_ref, v1_ref, v2_ref, bias_ref, na_ref, o_ref):
    j = pl.program_id(1)
    k_refs = (k0_ref, k1_ref, k2_ref)
    v_refs = (v0_ref, v1_ref, v2_ref)

    def kv_slabs(p):
        sl = slice(p * LANES, (p + 1) * LANES)
        return [(kr[:, sl], vr[:, sl], False) for kr, vr in zip(k_refs, v_refs)]

    def bias_of(h, b):
        return bias_ref[h, :, b * Q_TILE:(b + 1) * Q_TILE]

    @pl.when(j >= 2)
    def _():
        o_ref[...] = _attend(q_ref, kv_slabs, bias_of, None, na_ref[...]).astype(BF16)

    @pl.when(j < 2)
    def _():
        part_ok = [j >= 2, j >= 1, None]
        o_ref[...] = _attend(q_ref, kv_slabs, bias_of, part_ok, na_ref[...]).astype(BF16)


def _attn_prompt_call(q, k, v, bias, na, *, n_batch, q_tiles_per_batch):
    nq = q_tiles_per_batch

    def qmap(b, j):
        return (b * nq + j, 0)

    def kmap(off):
        return lambda b, j: (b * nq + jnp.maximum(j - off, 0), 0)

    blk = pl.BlockSpec((Q_TILE, D_A), qmap)
    kv_specs = [pl.BlockSpec((Q_TILE, D_A), kmap(off)) for off in (2, 1, 0)]
    return pl.pallas_call(
        _attn_prompt_kernel,
        grid=(n_batch, nq),
        in_specs=[blk] + kv_specs + kv_specs + [
            pl.BlockSpec(bias.shape, lambda b, j: (0, 0, 0)),
            pl.BlockSpec((1, D_A), lambda b, j: (0, 0)),
        ],
        out_specs=blk,
        out_shape=jax.ShapeDtypeStruct((n_batch * nq * Q_TILE, D_A), BF16),
        compiler_params=_cparams("arbitrary", "arbitrary"),
        name="attn_prompt",
    )(q, k, k, k, v, v, v, bias, na)


def _attn_sample_kernel(q_ref, kn_ref, vn_ref, ck_ref, cv_ref, bias_c_ref, bias_n_ref, na_ref, o_ref):
    def kv_slabs(p):
        sl = slice(p * LANES, (p + 1) * LANES)

        def cached(c_ref):
            return c_ref[2 * p:2 * p + 2].reshape(LANES, c_ref.shape[2]).astype(BF16)
        return [(cached(ck_ref), cached(cv_ref), True), (kn_ref[:, sl], vn_ref[:, sl], False)]

    def bias_of(h, b):
        return bias_c_ref[h] if b == 0 else bias_n_ref[h]

    o_ref[...] = _attend(q_ref, kv_slabs, bias_of, None, na_ref[...]).astype(BF16)


def _attn_sample_call(layer, q, k, v, cache_kt, cache_vt, bias_c, bias_n, na, *, first_chunk):
    _, nb, nh, hd, win = cache_kt.shape
    blk = pl.BlockSpec((CHUNK, D_A), lambda b: (first_chunk + b, 0))
    cache = pl.BlockSpec((None, None, nh, hd, win), lambda b: (layer, b, 0, 0, 0))
    return pl.pallas_call(
        _attn_sample_kernel,
        grid=(nb,),
        in_specs=[blk, blk, blk, cache, cache,
                  pl.BlockSpec(bias_c.shape, lambda b: (0, 0, 0)),
                  pl.BlockSpec(bias_n.shape, lambda b: (0, 0, 0)),
                  pl.BlockSpec((1, D_A), lambda b: (0, 0))],
        out_specs=pl.BlockSpec((CHUNK, D_A), lambda b: (b, 0)),
        out_shape=jax.ShapeDtypeStruct((nb * CHUNK, D_A), BF16),
        compiler_params=_cparams("arbitrary"),
        name="attn_sample",
    )(q, k, v, cache_kt, cache_vt, bias_c, bias_n, na)


def _outproj_kernel(route, ntp, x_ref, anp_ref, ans_ref, bn_ref, w_ref, g1_ref, sc_ref, sh_ref, lg_ref, lb_ref, *rest):
    if route:
        rc_ref, x1_ref, hp_ref, rt_ref = rest
    else:
        x1_ref, hp_ref = rest
    an = jnp.where(pl.program_id(0) >= ntp, ans_ref[...], anp_ref[...])
    mix = _dot(an, w_ref[0:D_A, :]) + _dot(bn_ref[...], w_ref[D_A:D_A + D_B, :])
    x1 = _ln(ALPHA * x_ref[...] + _gate_rows(mix, g1_ref), lg_ref[...], lb_ref[...])
    x1_ref[...] = x1
    h2 = _modulate(x1, sc_ref, sh_ref)
    hp_ref[...] = h2
    if route:
        h_hi, h_lo = _split_bf16(h2)
        both = _dot(h_hi, rc_ref[...])
        logits = both[:, :LANES] + both[:, LANES:] + _dot(h_lo, rc_ref[:, :LANES])
        lane = lax.broadcasted_iota(jnp.int32, logits.shape, 1)
        logits = jnp.where(lane < N_EXPERTS, logits, -jnp.inf)
        m1 = logits.max(axis=1, keepdims=True)
        i1 = jnp.where(logits == m1, lane, LANES).min(axis=1, keepdims=True)
        rest_l = jnp.where(lane == i1, -jnp.inf, logits)
        m2 = rest_l.max(axis=1, keepdims=True)
        i2 = jnp.where(rest_l == m2, lane, LANES).min(axis=1, keepdims=True)
        e2 = jnp.exp(m2 - m1)
        w1 = 1.0 / (1.0 + e2)
        w2 = e2 / (1.0 + e2)
        rt = jnp.where(lane == 0, i1.astype(F32),
                       jnp.where(lane == 1, i2.astype(F32),
                                 jnp.where(lane == 2, w1, jnp.where(lane == 3, w2, 0.0))))
        rt_ref[...] = rt


def _outproj_call(layer, route, x, an_p, an_s, bn, w_bf, mods, lg, lb, r_cat=None):
    t, d = x.shape
    nt = t // ROW_TILE
    ntp = an_p.shape[0] // ROW_TILE

    def tile(i):
        return (i, 0)

    def const2(i):
        return (0, 0)

    in_specs = [
        pl.BlockSpec((ROW_TILE, d), tile),
        pl.BlockSpec((ROW_TILE, D_A), lambda i: (jnp.minimum(i, ntp - 1), 0)),
        pl.BlockSpec((ROW_TILE, D_A), lambda i: (jnp.maximum(i - ntp, 0), 0)),
        pl.BlockSpec((ROW_TILE, D_B), tile),
        pl.BlockSpec(w_bf.shape, const2),
        _mod_spec(layer, G1, ROW_TILE, d),
        _mod_spec(layer, SC2, ROW_TILE, d),
        _mod_spec(layer, SH2, ROW_TILE, d),
        pl.BlockSpec((1, d), const2),
        pl.BlockSpec((1, d), const2),
    ]
    out_specs = [pl.BlockSpec((ROW_TILE, d), tile), pl.BlockSpec((ROW_TILE, d), tile)]
    out_shape = [jax.ShapeDtypeStruct((t, d), F32), jax.ShapeDtypeStruct((t, d), F32)]
    args = [x, an_p, an_s, bn, w_bf, mods, mods, mods, lg, lb]
    if route:
        in_specs.append(pl.BlockSpec(r_cat.shape, const2))
        out_specs.append(pl.BlockSpec((ROW_TILE, LANES), tile))
        out_shape.append(jax.ShapeDtypeStruct((t, LANES), F32))
        args.append(r_cat)
    return pl.pallas_call(
        functools.partial(_outproj_kernel, route, ntp),
        grid=(nt,),
        in_specs=in_specs,
        out_specs=out_specs,
        out_shape=out_shape,
        compiler_params=_cparams("arbitrary"),
        name="out_proj_route" if route else "out_proj",
    )(*args)


def _ffn_kernel(close, te_ref, tv_ref, x_ref, wg_ref, wu_ref, wd_ref, *rest):
    if close:
        x1_ref, g2_ref, lg_ref, lb_ref, o_ref, xb_ref = rest
    else:
        o_ref, xb_ref = rest
    del te_ref
    i = pl.program_id(0)
    j = pl.program_id(1)
    valid = tv_ref[i]
    nsub = (valid + FFN_SUB - 1) // FFN_SUB

    @pl.when(j == 0)
    def _():
        xb_ref[...] = x_ref[...].astype(BF16)
        o_ref[...] = jnp.zeros(o_ref.shape, F32)

    def run(n_sub):
        wg = wg_ref[0].astype(BF16)
        wu = wu_ref[0].astype(BF16)
        wd = wd_ref[0].astype(BF16)
        for s in range(n_sub):
            rows = slice(s * FFN_SUB, (s + 1) * FFN_SUB)
            xs = xb_ref[rows, :]
            g = _dot(xs, wg)
            u = _dot(xs, wu)
            a = (g * jax.nn.sigmoid(g) * u).astype(BF16)
            o_ref[rows, :] += _dot(a, wd)

    for n_sub in range(1, x_ref.shape[0] // FFN_SUB + 1):
        pl.when(nsub == n_sub)(functools.partial(run, n_sub))

    if close:
        @pl.when(j == pl.num_programs(1) - 1)
        def _():
            o_ref[...] = _ln(ALPHA * x1_ref[...] + _gate_rows(o_ref[...], g2_ref), lg_ref[...], lb_ref[...])


def _ffn_call(tile_expert, tile_valid, xs, wg, wu, wd, tile_rows, closing=None):
    p, d = xs.shape
    f = wg.shape[2]
    nt = p // tile_rows
    nc = f // FFN_COLS

    def active_col(i, j, tv):
        return jnp.where(tv[i] > 0, j, nc - 1)

    in_specs = [
        pl.BlockSpec((tile_rows, d), lambda i, j, te, tv: (i, 0)),
        pl.BlockSpec((1, d, FFN_COLS), lambda i, j, te, tv: (te[i], 0, active_col(i, j, tv))),
        pl.BlockSpec((1, d, FFN_COLS), lambda i, j, te, tv: (te[i], 0, active_col(i, j, tv))),
        pl.BlockSpec((1, FFN_COLS, d), lambda i, j, te, tv: (te[i], active_col(i, j, tv), 0)),
    ]
    args = [tile_expert, tile_valid, xs, wg, wu, wd]
    if closing is not None:
        layer, x1, mods, lg, lb = closing
        in_specs += [
            pl.BlockSpec((tile_rows, d), lambda i, j, te, tv: (i, 0)),
            _mod_spec(layer, G2, tile_rows, d),
            pl.BlockSpec((1, d), lambda i, j, te, tv: (0, 0)),
            pl.BlockSpec((1, d), lambda i, j, te, tv: (0, 0)),
        ]
        args += [x1, mods, lg, lb]
    return pl.pallas_call(
        functools.partial(_ffn_kernel, closing is not None),
        grid_spec=pltpu.PrefetchScalarGridSpec(
            num_scalar_prefetch=2,
            grid=(nt, nc),
            in_specs=in_specs,
            out_specs=pl.BlockSpec((tile_rows, d), lambda i, j, te, tv: (i, 0)),
            scratch_shapes=[pltpu.VMEM((tile_rows, d), BF16)],
        ),
        out_shape=jax.ShapeDtypeStruct((p, d), F32),
        compiler_params=_cparams("arbitrary", "arbitrary"),
        name="swiglu_close" if closing is not None else "swiglu_grouped",
    )(*args)


def _dispatch_kernel(pos_ref, src_ref, init_ref, dst_ref, sem):
    del init_ref

    def issue(r, carry):
        row = src_ref.at[pl.ds(r, 1), :]
        pltpu.make_async_copy(row, dst_ref.at[pl.ds(pos_ref[0, 2 * r], 1), :], sem).start()
        pltpu.make_async_copy(row, dst_ref.at[pl.ds(pos_ref[0, 2 * r + 1], 1), :], sem).start()
        return carry
    lax.fori_loop(0, ROW_TILE, issue, 0, unroll=8)
    for _ in range(2):
        pltpu.make_async_copy(src_ref, dst_ref.at[pl.ds(0, ROW_TILE), :], sem).wait()


def _dispatch_call(pos_tiles, hp, n_rows_sorted):
    t, d = hp.shape
    nt = t // ROW_TILE
    return pl.pallas_call(
        _dispatch_kernel,
        grid=(nt,),
        in_specs=[
            pl.BlockSpec((None, 1, 2 * ROW_TILE), lambda i: (i, 0, 0), memory_space=pltpu.SMEM),
            pl.BlockSpec((ROW_TILE, d), lambda i: (i, 0)),
            pl.BlockSpec(memory_space=pl.ANY),
        ],
        out_specs=pl.BlockSpec(memory_space=pl.ANY),
        out_shape=jax.ShapeDtypeStruct((n_rows_sorted, d), hp.dtype),
        scratch_shapes=[pltpu.SemaphoreType.DMA(())],
        input_output_aliases={2: 0},
        compiler_params=_cparams("arbitrary"),
        name="dispatch_rows",
    )(pos_tiles, hp, jnp.zeros((n_rows_sorted, d), hp.dtype))


def _final_moe_kernel(ntp, pos_ref, x_ref, rt_ref, g2_ref, lg_ref, lb_ref, ys_ref, op_ref, os_ref, ybuf, sem):
    def issue(r, carry):
        pltpu.make_async_copy(ys_ref.at[pl.ds(pos_ref[0, 2 * r], 1), :], ybuf.at[0, pl.ds(r, 1), :], sem).start()
        pltpu.make_async_copy(ys_ref.at[pl.ds(pos_ref[0, 2 * r + 1], 1), :], ybuf.at[1, pl.ds(r, 1), :], sem).start()
        return carry
    lax.fori_loop(0, ROW_TILE, issue, 0, unroll=8)
    for k in range(2):
        pltpu.make_async_copy(ys_ref.at[pl.ds(0, ROW_TILE), :], ybuf.at[k], sem).wait()
    rt = rt_ref[...]
    f = rt[:, 2:3] * ybuf[0] + rt[:, 3:4] * ybuf[1]
    y = _ln(ALPHA * x_ref[...] + _gate_rows(f, g2_ref), lg_ref[...], lb_ref[...])
    i = pl.program_id(0)

    @pl.when(i < ntp)
    def _():
        op_ref[...] = y

    @pl.when(i >= ntp)
    def _():
        os_ref[...] = y


def _final_moe_call(layer, pos_tiles, x1, rt, mods, lg, lb, ys, *, n_prompt_tiles):
    t, d = x1.shape
    ntp = n_prompt_tiles
    tile = pl.BlockSpec((ROW_TILE, d), lambda i: (i, 0))
    vec = pl.BlockSpec((1, d), lambda i: (0, 0))
    return pl.pallas_call(
        functools.partial(_final_moe_kernel, ntp),
        grid=(t // ROW_TILE,),
        in_specs=[
            pl.BlockSpec((None, 1, 2 * ROW_TILE), lambda i: (i, 0, 0), memory_space=pltpu.SMEM),
            tile,
            pl.BlockSpec((ROW_TILE, LANES), lambda i: (i, 0)),
            _mod_spec(layer, G2, ROW_TILE, d),
            vec, vec,
            pl.BlockSpec(memory_space=pl.ANY),
        ],
        out_specs=[pl.BlockSpec((ROW_TILE, d), lambda i: (jnp.minimum(i, ntp - 1), 0)),
                   pl.BlockSpec((ROW_TILE, d), lambda i: (jnp.maximum(i - ntp, 0), 0))],
        out_shape=[jax.ShapeDtypeStruct((ntp * ROW_TILE, d), F32),
                   jax.ShapeDtypeStruct((t - ntp * ROW_TILE, d), F32)],
        scratch_shapes=[pltpu.VMEM((2, ROW_TILE, d), F32), pltpu.SemaphoreType.DMA(())],
        compiler_params=_cparams("arbitrary"),
        name="final_moe",
    )(pos_tiles, x1, rt, mods, lg, lb, ys)


def _rel_bias(table, n_q, n_k, banded):
    r = jnp.arange(n_q)[:, None]
    w = jnp.arange(n_k)[None, :]
    n_diag = n_q + n_k - 1
    k = jnp.arange(n_diag)
    diag = table[:, jnp.clip(LEFT_CTX + (n_q - 1) - k, -REL_CLIP, REL_CLIP) + REL_CLIP].astype(F32) * LOG2E
    padded = jnp.concatenate([diag, jnp.zeros((diag.shape[0], 1), F32)], axis=1)
    skew = jnp.tile(padded, (1, n_q))[:, :n_q * n_diag].reshape(-1, n_q, n_diag)
    bias = skew[:, :, n_q - 1:n_q - 1 + n_k]
    if banded:
        qa = r // CHUNK
        kc = w // CHUNK
        vis = (kc >= qa) & (kc <= qa + LEFT_CHUNKS)
        bias = jnp.where(vis[None], bias, NEG)
    return bias


def _gating_weights(w_s, b_s):
    n = MLP_CHUNK
    tril = jnp.tril(jnp.ones((n, n), bool))
    wm = jnp.where(tril[None], w_s, 0.0)
    h = CHUNK
    top = wm[:, :h, :h]
    z = jnp.zeros_like(top)
    wm_s = jnp.concatenate([jnp.concatenate([top, z], 2), jnp.concatenate([z, top], 2)], 1)
    both = jnp.stack([wm, wm_s])
    wcat = jnp.concatenate([both[:, 0::2], both[:, 1::2]], axis=-1)
    bias_p = jnp.repeat(jnp.transpose(b_s), GROUP_DIM, axis=1)
    bias_s = jnp.concatenate([bias_p[:h], bias_p[:h]], 0)
    return wcat.astype(BF16), jnp.stack([bias_p, bias_s]).astype(F32)


def _route_plan(rt, n_tiles, tile_rows):
    t = rt.shape[0]
    e_flat = rt[:, 0:2].astype(jnp.int32).reshape(-1)
    oh = (e_flat[:, None] == jnp.arange(N_EXPERTS)[None, :]).astype(jnp.int32)
    csum = jnp.cumsum(oh, axis=0)
    rank = jnp.sum(oh * (csum - 1), axis=1)
    counts = csum[-1]
    tiles_e = (counts + tile_rows - 1) // tile_rows
    tile_end = jnp.cumsum(tiles_e)
    tile_start = tile_end - tiles_e
    pos = (tile_start * tile_rows)[e_flat] + rank
    ti = jnp.arange(n_tiles)
    te = jnp.minimum(jnp.sum((ti[:, None] >= tile_end[None, :]).astype(jnp.int32), axis=1), N_EXPERTS - 1)
    tv = jnp.clip(counts[te] - (ti - tile_start[te]) * tile_rows, 0, tile_rows)
    tv = jnp.where(ti < tile_end[-1], tv, 0)
    last_e = te[jnp.maximum(tile_end[-1] - 1, 0)]
    te = jnp.where(ti < tile_end[-1], te, last_e)
    return pos.reshape(t // ROW_TILE, 1, 2 * ROW_TILE).astype(jnp.int32), te.astype(jnp.int32), tv.astype(jnp.int32)


def kernel(x_prompt, x_sample, cache_k, cache_v, c_prompt, c_sample, ln_in_g, ln_in_b, w_in, w_out, rel_bias_table, gmlp_ln_g, gmlp_ln_b, gmlp_w_s, gmlp_b_s, out_norm_a, out_norm_b, ada_w, ada_b, ln1_g, ln1_b, ln2_g, ln2_b, ffn_w_gate, ffn_w_up, ffn_w_down, moe_router, moe_w_gate, moe_w_up, moe_w_down):
    nb, seq, d = x_prompt.shape
    ns, dseq, _ = x_sample.shape
    depth = w_in.shape[0]
    tp = nb * seq
    ts = ns * dseq
    t = tp + ts
    assert depth == DEPTH and dseq == CHUNK and seq % ROW_TILE == 0 and ts % ROW_TILE == 0
    assert t % FFN_TILE == 0 and cache_k.shape[2] == LEFT_CTX
    ntp = tp // ROW_TILE

    n_cond = nb + ns
    c_all = jnp.concatenate([c_prompt, c_sample], axis=0)
    c_pad = jnp.pad(c_all, ((0, (-n_cond) % 8), (0, 0)))
    mods = _ada_call(c_pad, ada_w, ada_b)
    mods = jnp.transpose(mods[:, :n_cond].reshape(depth, n_cond, 6, d), (0, 2, 1, 3))
    cpb = seq // CHUNK
    mods_p = jnp.broadcast_to(mods[:, :, :nb, None, :], (depth, 6, nb, cpb, d)).reshape(depth, 6, nb * cpb, d)
    mods = jnp.concatenate([mods_p, mods[:, :, nb:]], axis=2)

    def row(v):
        return v.reshape(1, -1)

    w_in_bf = w_in.astype(BF16)
    w_out_bf = w_out.astype(BF16)
    cache_kt = jnp.transpose(cache_k, (0, 1, 3, 4, 2))
    cache_vt = jnp.transpose(cache_v, (0, 1, 3, 4, 2))
    k_tail, v_tail, gv_rows = [], [], []
    for l in range(depth):
        wcat, bs = _gating_weights(gmlp_w_s[l], gmlp_b_s[l])
        if l == 0:
            x_in = (x_prompt.reshape(tp, d), x_sample.reshape(ts, d), row(ln_in_g), row(ln_in_b))
        else:
            x_in = (x,)
        res = _inproj_call(l, x_in, mods, w_in_bf[l], row(gmlp_ln_g[l]), row(gmlp_ln_b[l]), wcat, bs,
                           row(out_norm_b[l]),
                           n_prompt_tiles=ntp, tiles_per_batch=seq // ROW_TILE, n_batch=nb, n_sample_rows=ts)
        if l == 0:
            x, q, k, v, kf, vf, gv, bn = res
        else:
            q, k, v, kf, vf, gv, bn = res
        k_tail.append(kf)
        v_tail.append(vf)
        gv_rows.append(gv)

        table = rel_bias_table[l]
        bias_p = _rel_bias(table, Q_TILE, 3 * Q_TILE, True)
        bias_s = _rel_bias(table, CHUNK, LEFT_CTX + CHUNK, False)
        na = row(out_norm_a[l])
        an_p = _attn_prompt_call(q, k, v, bias_p, na, n_batch=nb, q_tiles_per_batch=seq // Q_TILE)
        an_s = _attn_sample_call(l, q, k, v, cache_kt, cache_vt, bias_s[:, :, :LEFT_CTX], bias_s[:, :, LEFT_CTX:], na,
                                 first_chunk=tp // CHUNK)

        lg1, lb1, lg2, lb2 = row(ln1_g[l]), row(ln1_b[l]), row(ln2_g[l]), row(ln2_b[l])
        i = l // 2
        if l % 2 == 0:
            x1, hp = _outproj_call(l, False, x, an_p, an_s, bn, w_out_bf[l], mods, lg1, lb1)
            nt = t // FFN_TILE
            te = jnp.zeros((nt,), jnp.int32)
            tv = jnp.full((nt,), FFN_TILE, jnp.int32)
            x = _ffn_call(te, tv, hp, ffn_w_gate[i:i + 1], ffn_w_up[i:i + 1], ffn_w_down[i:i + 1], FFN_TILE,
                          closing=(l, x1, mods, lg2, lb2))
        else:
            r_pad = jnp.pad(moe_router[i], ((0, 0), (0, LANES - N_EXPERTS)))
            r_hi = r_pad.astype(BF16)
            r_lo = (r_pad - r_hi.astype(F32)).astype(BF16)
            r_cat = jnp.concatenate([r_hi, r_lo], axis=1)
            x1, hp, rt = _outproj_call(l, True, x, an_p, an_s, bn, w_out_bf[l], mods, lg1, lb1, r_cat)
            n_tiles = -(-2 * t // MOE_TILE) + N_EXPERTS
            pos, te, tv = _route_plan(rt, n_tiles, MOE_TILE)
            xs = _dispatch_call(pos, hp, n_tiles * MOE_TILE)
            ys = _ffn_call(te, tv, xs, moe_w_gate[i], moe_w_up[i], moe_w_down[i], MOE_TILE)
            y_p, y_s = _final_moe_call(l, pos, x1, rt, mods, lg2, lb2, ys, n_prompt_tiles=ntp)

    y_prompt = y_p.reshape(nb, seq, d)
    y_sample = y_s.reshape(ns, dseq, d)

    def tails(rows):
        kp = jnp.stack([r[:nb * ROW_TILE].reshape(nb, ROW_TILE, N_HEADS, HEAD_DIM) for r in rows])
        ksn = jnp.stack([r[nb * ROW_TILE:].reshape(ns, dseq, N_HEADS, HEAD_DIM) for r in rows])
        return kp, ksn

    k_prompt_new, k_sample_new = tails(k_tail)
    v_prompt_new, v_sample_new = tails(v_tail)
    gmlp_v_sample_new = jnp.stack([g.reshape(ns, dseq, D_B) for g in gv_rows])
    return (y_prompt, y_sample, k_prompt_new, v_prompt_new, k_sample_new, v_sample_new, gmlp_v_sample_new)
```

```python
import functools

import jax
import jax.numpy as jnp
from jax import lax
from jax.experimental import pallas as pl
from jax.experimental.pallas import tpu as pltpu

CHUNK = 64
LEFT_CHUNKS = 8
LEFT_CTX = LEFT_CHUNKS * CHUNK
N_HEADS = 8
HEAD_DIM = 64
D_A = N_HEADS * HEAD_DIM
N_GROUPS = 8
GROUP_DIM = 64
D_B = N_GROUPS * GROUP_DIM
MLP_CHUNK = 128
REL_CLIP = 128
N_EXPERTS = 8
DEPTH = 2
ALPHA = (2 * DEPTH) ** 0.25
LN_EPS = 1e-5
ATTN_SCALE = HEAD_DIM ** -0.5
LOG2E = 1.4426950408889634
NEG = -1e30
SH1, SC1, G1, SH2, SC2, G2 = range(6)

LANES = 128
ROW_TILE = 512
Q_TILE = 256
FFN_TILE = 1024
MOE_TILE = 2048
FFN_SUB = 512
FFN_COLS = 256
VMEM_LIMIT = 56 * 1024 * 1024

BF16 = jnp.bfloat16
F32 = jnp.float32


def _cparams(*sem):
    return pltpu.CompilerParams(dimension_semantics=sem, vmem_limit_bytes=VMEM_LIMIT)


def _ln(x, g, b):
    mu = jnp.mean(x, axis=-1, keepdims=True)
    xc = x - mu
    var = jnp.mean(xc * xc, axis=-1, keepdims=True)
    return xc * lax.rsqrt(var + LN_EPS) * g + b


def _rms(x, g):
    return x * lax.rsqrt(jnp.mean(x * x, axis=-1, keepdims=True) + LN_EPS) * g


def _modulate(x, sc_ref, sh_ref):
    parts = []
    for c in range(x.shape[0] // CHUNK):
        xc = x[c * CHUNK:(c + 1) * CHUNK]
        parts.append(xc * (1.0 + sc_ref[c:c + 1, :]) + sh_ref[c:c + 1, :])
    return jnp.concatenate(parts, axis=0)


def _gate_rows(x, g_ref):
    parts = []
    for c in range(x.shape[0] // CHUNK):
        parts.append(x[c * CHUNK:(c + 1) * CHUNK] * (1.0 + g_ref[c:c + 1, :]))
    return jnp.concatenate(parts, axis=0)


def _mod_spec(layer, comp, rows, d):
    return pl.BlockSpec((None, None, rows // CHUNK, d), lambda i, *_: (layer, comp, i, 0))


def _split_bf16(x):
    hi = x.astype(BF16)
    lo = (x - hi.astype(F32)).astype(BF16)
    return hi, lo


def _dot(a, b):
    return jnp.dot(a, b, preferred_element_type=F32)


def _ada_kernel(c_ref, w_ref, b_ref, o_ref):
    c = c_ref[...]
    s = c * jax.nn.sigmoid(c)
    s_hi, s_lo = _split_bf16(s)
    w_hi, w_lo = _split_bf16(w_ref[0])
    o_ref[0] = _dot(s_hi, w_hi) + _dot(s_lo, w_hi) + _dot(s_hi, w_lo) + b_ref[0]


def _ada_call(c_pad, ada_w, ada_b):
    depth, d, n = ada_w.shape
    rows = c_pad.shape[0]
    tn = n // 4
    return pl.pallas_call(
        _ada_kernel,
        grid=(depth, n // tn),
        in_specs=[
            pl.BlockSpec((rows, d), lambda l, j: (0, 0)),
            pl.BlockSpec((1, d, tn), lambda l, j: (l, 0, j)),
            pl.BlockSpec((1, 1, tn), lambda l, j: (l, 0, j)),
        ],
        out_specs=pl.BlockSpec((1, rows, tn), lambda l, j: (l, 0, j)),
        out_shape=jax.ShapeDtypeStruct((depth, rows, n), F32),
        compiler_params=_cparams("arbitrary", "arbitrary"),
        name="ada_mod",
    )(c_pad, ada_w, ada_b.reshape(depth, 1, n))


def _inproj_kernel(first, ntp, *refs):
    if first:
        xp_ref, xs_ref, lng_ref, lnb_ref = refs[:4]
        xn_ref = refs[-8]
        x = jnp.where(pl.program_id(0) >= ntp, xs_ref[...], xp_ref[...])
        x = _ln(x, lng_ref[...], lnb_ref[...])
        xn_ref[...] = x
        refs = refs[4:-8] + refs[-7:]
    else:
        x = refs[0][...]
        refs = refs[1:]
    (sc_ref, sh_ref, w_ref, gg_ref, gb_ref, wcat_ref, bs_ref, nb_ref,
     q_ref, k_ref, v_ref, kf_ref, vf_ref, gv_ref, bn_ref) = refs
    h = _modulate(x, sc_ref, sh_ref).astype(BF16)

    q = _dot(h, w_ref[:, 0:D_A])
    q_ref[...] = (q * (ATTN_SCALE * LOG2E)).astype(BF16)
    k = _dot(h, w_ref[:, D_A:2 * D_A])
    k_ref[...] = k.astype(BF16)
    kf_ref[...] = k
    v = _dot(h, w_ref[:, 2 * D_A:3 * D_A])
    v_ref[...] = v.astype(BF16)
    vf_ref[...] = v

    u = jax.nn.gelu(_dot(h, w_ref[:, 3 * D_A:3 * D_A + D_B]), approximate=True)
    vb = jax.nn.gelu(_dot(h, w_ref[:, 3 * D_A + D_B:3 * D_A + 2 * D_B]), approximate=True)
    vn = _ln(vb, gg_ref[...], gb_ref[...])
    gv_ref[...] = vn

    lane = lax.broadcasted_iota(jnp.int32, (MLP_CHUNK, LANES), 1)
    lo = lane < GROUP_DIM
    vnb = vn.astype(BF16)
    zero = jnp.zeros((MLP_CHUNK, LANES), BF16)
    rows = []
    for c in range(ROW_TILE // MLP_CHUNK):
        r0 = c * MLP_CHUNK
        cols = []
        for p in range(D_B // LANES):
            slab = vnb[r0:r0 + MLP_CHUNK, p * LANES:(p + 1) * LANES]
            rhs = jnp.concatenate([jnp.where(lo, slab, zero), jnp.where(lo, zero, slab)], axis=0)
            mixed = _dot(wcat_ref[0, p], rhs) + bs_ref[0, :, p * LANES:(p + 1) * LANES]
            cols.append(u[r0:r0 + MLP_CHUNK, p * LANES:(p + 1) * LANES] * mixed)
        rows.append(jnp.concatenate(cols, axis=1))
    bo = jnp.concatenate(rows, axis=0)
    bn_ref[...] = _rms(bo, nb_ref[...]).astype(BF16)


def _inproj_call(layer, x_in, mods, w_bf, gg, gb, wcat, bs, nb, *, n_prompt_tiles, tiles_per_batch, n_batch,
                 n_sample_rows):
    first = len(x_in) == 4
    d = x_in[0].shape[1]
    t = mods.shape[2] * CHUNK
    nt = t // ROW_TILE
    ntp = n_prompt_tiles
    n_tail = n_batch + (nt - ntp)

    def tile(i):
        return (i, 0)

    def const2(i):
        return (0, 0)

    def selmap4(i):
        return (jnp.where(i >= ntp, 1, 0), 0, 0, 0)

    def selmap3(i):
        return (jnp.where(i >= ntp, 1, 0), 0, 0)

    def tailmap(i):
        return (jnp.where(i < ntp, i // tiles_per_batch, n_batch + i - ntp), 0)

    def gvmap(i):
        return (jnp.maximum(i - ntp, 0), 0)

    if first:
        x_specs = [
            pl.BlockSpec((ROW_TILE, d), lambda i: (jnp.minimum(i, ntp - 1), 0)),
            pl.BlockSpec((ROW_TILE, d), lambda i: (jnp.maximum(i - ntp, 0), 0)),
            pl.BlockSpec((1, d), const2),
            pl.BlockSpec((1, d), const2),
        ]
    else:
        x_specs = [pl.BlockSpec((ROW_TILE, d), tile)]
    in_specs = x_specs + [
        _mod_spec(layer, SC1, ROW_TILE, d),
        _mod_spec(layer, SH1, ROW_TILE, d),
        pl.BlockSpec(w_bf.shape, const2),
        pl.BlockSpec((1, D_B), const2),
        pl.BlockSpec((1, D_B), const2),
        pl.BlockSpec((1,) + wcat.shape[1:], selmap4),
        pl.BlockSpec((1, MLP_CHUNK, D_B), selmap3),
        pl.BlockSpec((1, D_B), const2),
    ]
    out_specs = [
        pl.BlockSpec((ROW_TILE, D_A), tile),
        pl.BlockSpec((ROW_TILE, D_A), tile),
        pl.BlockSpec((ROW_TILE, D_A), tile),
        pl.BlockSpec((ROW_TILE, D_A), tailmap),
        pl.BlockSpec((ROW_TILE, D_A), tailmap),
        pl.BlockSpec((ROW_TILE, D_B), gvmap),
        pl.BlockSpec((ROW_TILE, D_B), tile),
    ]
    out_shape = [
        jax.ShapeDtypeStruct((t, D_A), BF16),
        jax.ShapeDtypeStruct((t, D_A), BF16),
        jax.ShapeDtypeStruct((t, D_A), BF16),
        jax.ShapeDtypeStruct((n_tail * ROW_TILE, D_A), F32),
        jax.ShapeDtypeStruct((n_tail * ROW_TILE, D_A), F32),
        jax.ShapeDtypeStruct((n_sample_rows, D_B), F32),
        jax.ShapeDtypeStruct((t, D_B), BF16),
    ]
    if first:
        out_specs = [pl.BlockSpec((ROW_TILE, d), tile)] + out_specs
        out_shape = [jax.ShapeDtypeStruct((t, d), F32)] + out_shape
    return pl.pallas_call(
        functools.partial(_inproj_kernel, first, ntp),
        grid=(nt,),
        in_specs=in_specs,
        out_specs=out_specs,
        out_shape=out_shape,
        compiler_params=_cparams("arbitrary"),
        name="in_proj_first" if first else "in_proj",
    )(*x_in, mods, mods, w_bf, gg, gb, wcat, bs, nb)


def _attend(q_ref, kv_slabs, bias_of, part_ok, na):
    r = q_ref.shape[0]
    nt_dims = (((1,), (1,)), ((), ()))
    lo = lax.broadcasted_iota(jnp.int32, (1, LANES), 1) < HEAD_DIM
    lo_t = lax.broadcasted_iota(jnp.int32, (LANES, 1), 0) < HEAD_DIM
    outs = []
    for p in range(D_A // LANES):
        qp = q_ref[:, p * LANES:(p + 1) * LANES]
        slabs = kv_slabs(p)
        o_pair = jnp.zeros((r, LANES), F32)
        for half in range(2):
            h = 2 * p + half
            keep = lo if half == 0 else jnp.logical_not(lo)
            keep_t = lo_t if half == 0 else jnp.logical_not(lo_t)
            qh = jnp.where(keep, qp, jnp.zeros_like(qp))
            ss = []
            m = None
            for j, (kp, _, transposed) in enumerate(slabs):
                if transposed:
                    s = _dot(qh, kp)
                else:
                    s = lax.dot_general(qh, kp, nt_dims, preferred_element_type=F32)
                s = s + bias_of(h, j)
                if part_ok is not None and part_ok[j] is not None:
                    s = jnp.where(part_ok[j], s, NEG)
                mj = s.max(axis=1, keepdims=True)
                m = mj if m is None else jnp.maximum(m, mj)
                ss.append(s)
            acc = jnp.zeros((r, LANES), F32)
            for s, (_, vp, transposed) in zip(ss, slabs):
                e = jnp.exp2(s - m).astype(BF16)
                if transposed:
                    vh = jnp.where(keep_t, vp, jnp.ones_like(vp))
                    acc = acc + lax.dot_general(e, vh, nt_dims, preferred_element_type=F32)
                else:
                    acc = acc + _dot(e, jnp.where(keep, vp, jnp.ones_like(vp)))
            denom = acc[:, HEAD_DIM:HEAD_DIM + 1] if half == 0 else acc[:, 0:1]
            o_pair = jnp.where(keep, acc / denom, o_pair)
        outs.append(o_pair)
    a = jnp.concatenate(outs, axis=1)
    return _rms(a, na)


def _attn_prompt_kernel(q_ref, k0_ref, k1_ref, k2_ref, v0_ref, v1_ref, v2_ref, bias_ref, na_ref, o_ref):
    j = pl.program_id(1)
    k_refs = (k0_ref, k1_ref, k2_ref)
    v_refs = (v0_ref, v1_ref, v2_ref)

    def kv_slabs(p):
        sl = slice(p * LANES, (p + 1) * LANES)
        return [(kr[:, sl], vr[:, sl], False) for kr, vr in zip(k_refs, v_refs)]

    def bias_of(h, b):
        return bias_ref[h, :, b * Q_TILE:(b + 1) * Q_TILE]

    @pl.when(j >= 2)
    def _():
        o_ref[...] = _attend(q_ref, kv_slabs, bias_of, None, na_ref[...]).astype(BF16)

    @pl.when(j < 2)
    def _():
        part_ok = [j >= 2, j >= 1, None]
        o_ref[...] = _attend(q_ref, kv_slabs, bias_of, part_ok, na_ref[...]).astype(BF16)


def _attn_prompt_call(q, k, v, bias, na, *, n_batch, q_tiles_per_batch):
    nq = q_tiles_per_batch

    def qmap(b, j):
        return (b * nq + j, 0)

    def kmap(off):
        return lambda b, j: (b * nq + jnp.maximum(j - off, 0), 0)

    blk = pl.BlockSpec((Q_TILE, D_A), qmap)
    kv_specs = [pl.BlockSpec((Q_TILE, D_A), kmap(off)) for off in (2, 1, 0)]
    return pl.pallas_call(
        _attn_prompt_kernel,
        grid=(n_batch, nq),
        in_specs=[blk] + kv_specs + kv_specs + [
            pl.BlockSpec(bias.shape, lambda b, j: (0, 0, 0)),
            pl.BlockSpec((1, D_A), lambda b, j: (0, 0)),
        ],
        out_specs=blk,
        out_shape=jax.ShapeDtypeStruct((n_batch * nq * Q_TILE, D_A), BF16),
        compiler_params=_cparams("arbitrary", "arbitrary"),
        name="attn_prompt",
    )(q, k, k, k, v, v, v, bias, na)


def _attn_sample_kernel(q_ref, kn_ref, vn_ref, ck_ref, cv_ref, bias_c_ref, bias_n_ref, na_ref, o_ref):
    def kv_slabs(p):
        sl = slice(p * LANES, (p + 1) * LANES)

        def cached(c_ref):
            return c_ref[2 * p:2 * p + 2].reshape(LANES, c_ref.shape[2]).astype(BF16)
        return [(cached(ck_ref), cached(cv_ref), True), (kn_ref[:, sl], vn_ref[:, sl], False)]

    def bias_of(h, b):
        return bias_c_ref[h] if b == 0 else bias_n_ref[h]

    o_ref[...] = _attend(q_ref, kv_slabs, bias_of, None, na_ref[...]).astype(BF16)


def _attn_sample_call(layer, q, k, v, cache_kt, cache_vt, bias_c, bias_n, na, *, first_chunk):
    _, nb, nh, hd, win = cache_kt.shape
    blk = pl.BlockSpec((CHUNK, D_A), lambda b: (first_chunk + b, 0))
    cache = pl.BlockSpec((None, None, nh, hd, win), lambda b: (layer, b, 0, 0, 0))
    return pl.pallas_call(
        _attn_sample_kernel,
        grid=(nb,),
        in_specs=[blk, blk, blk, cache, cache,
                  pl.BlockSpec(bias_c.shape, lambda b: (0, 0, 0)),
                  pl.BlockSpec(bias_n.shape, lambda b: (0, 0, 0)),
                  pl.BlockSpec((1, D_A), lambda b: (0, 0))],
        out_specs=pl.BlockSpec((CHUNK, D_A), lambda b: (b, 0)),
        out_shape=jax.ShapeDtypeStruct((nb * CHUNK, D_A), BF16),
        compiler_params=_cparams("arbitrary"),
        name="attn_sample",
    )(q, k, v, cache_kt, cache_vt, bias_c, bias_n, na)


def _outproj_kernel(route, ntp, x_ref, anp_ref, ans_ref, bn_ref, w_ref, g1_ref, sc_ref, sh_ref, lg_ref, lb_ref, *rest):
    if route:
        rc_ref, x1_ref, hp_ref, rt_ref = rest
    else:
        x1_ref, hp_ref = rest
    an = jnp.where(pl.program_id(0) >= ntp, ans_ref[...], anp_ref[...])
    mix = _dot(an, w_ref[0:D_A, :]) + _dot(bn_ref[...], w_ref[D_A:D_A + D_B, :])
    x1 = _ln(ALPHA * x_ref[...] + _gate_rows(mix, g1_ref), lg_ref[...], lb_ref[...])
    x1_ref[...] = x1
    h2 = _modulate(x1, sc_ref, sh_ref)
    hp_ref[...] = h2
    if route:
        h_hi, h_lo = _split_bf16(h2)
        both = _dot(h_hi, rc_ref[...])
        logits = both[:, :LANES] + both[:, LANES:] + _dot(h_lo, rc_ref[:, :LANES])
        lane = lax.broadcasted_iota(jnp.int32, logits.shape, 1)
        logits = jnp.where(lane < N_EXPERTS, logits, -jnp.inf)
        m1 = logits.max(axis=1, keepdims=True)
        i1 = jnp.where(logits == m1, lane, LANES).min(axis=1, keepdims=True)
        rest_l = jnp.where(lane == i1, -jnp.inf, logits)
        m2 = rest_l.max(axis=1, keepdims=True)
        i2 = jnp.where(rest_l == m2, lane, LANES).min(axis=1, keepdims=True)
        e2 = jnp.exp(m2 - m1)
        w1 = 1.0 / (1.0 + e2)
        w2 = e2 / (1.0 + e2)
        rt = jnp.where(lane == 0, i1.astype(F32),
                       jnp.where(lane == 1, i2.astype(F32),
                                 jnp.where(lane == 2, w1, jnp.where(lane == 3, w2, 0.0))))
        rt_ref[...] = rt


def _outproj_call(layer, route, x, an_p, an_s, bn, w_bf, mods, lg, lb, r_cat=None):
    t, d = x.shape
    nt = t // ROW_TILE
    ntp = an_p.shape[0] // ROW_TILE

    def tile(i):
        return (i, 0)

    def const2(i):
        return (0, 0)

    in_specs = [
        pl.BlockSpec((ROW_TILE, d), tile),
        pl.BlockSpec((ROW_TILE, D_A), lambda i: (jnp.minimum(i, ntp - 1), 0)),
        pl.BlockSpec((ROW_TILE, D_A), lambda i: (jnp.maximum(i - ntp, 0), 0)),
        pl.BlockSpec((ROW_TILE, D_B), tile),
        pl.BlockSpec(w_bf.shape, const2),
        _mod_spec(layer, G1, ROW_TILE, d),
        _mod_spec(layer, SC2, ROW_TILE, d),
        _mod_spec(layer, SH2, ROW_TILE, d),
        pl.BlockSpec((1, d), const2),
        pl.BlockSpec((1, d), const2),
    ]
    out_specs = [pl.BlockSpec((ROW_TILE, d), tile), pl.BlockSpec((ROW_TILE, d), tile)]
    out_shape = [jax.ShapeDtypeStruct((t, d), F32), jax.ShapeDtypeStruct((t, d), F32)]
    args = [x, an_p, an_s, bn, w_bf, mods, mods, mods, lg, lb]
    if route:
        in_specs.append(pl.BlockSpec(r_cat.shape, const2))
        out_specs.append(pl.BlockSpec((ROW_TILE, LANES), tile))
        out_shape.append(jax.ShapeDtypeStruct((t, LANES), F32))
        args.append(r_cat)
    return pl.pallas_call(
        functools.partial(_outproj_kernel, route, ntp),
        grid=(nt,),
        in_specs=in_specs,
        out_specs=out_specs,
        out_shape=out_shape,
        compiler_params=_cparams("arbitrary"),
        name="out_proj_route" if route else "out_proj",
    )(*args)


def _swiglu_chains(n_sub, xb_ref, wg_ref, wu_ref, wd_ref, o_ref, extra=None):
    wg = wg_ref[0].astype(BF16)
    wu = wu_ref[0].astype(BF16)
    wd = wd_ref[0].astype(BF16)
    if extra is not None:
        extra()
    for s in range(n_sub):
        rows = slice(s * FFN_SUB, (s + 1) * FFN_SUB)
        xs = xb_ref[rows, :]
        g = _dot(xs, wg)
        u = _dot(xs, wu)
        a = (g * jax.nn.sigmoid(g) * u).astype(BF16)
        o_ref[rows, :] += _dot(a, wd)


def _ffn_kernel(close, te_ref, tv_ref, x_ref, wg_ref, wu_ref, wd_ref, *rest):
    if close:
        x1_ref, g2_ref, lg_ref, lb_ref, o_ref, xb_ref = rest
    else:
        o_ref, xb_ref = rest
    del te_ref
    i = pl.program_id(0)
    j = pl.program_id(1)
    valid = tv_ref[i]
    nsub = (valid + FFN_SUB - 1) // FFN_SUB

    @pl.when(j == 0)
    def _():
        xb_ref[...] = x_ref[...].astype(BF16)
        o_ref[...] = jnp.zeros(o_ref.shape, F32)

    for n_sub in range(1, x_ref.shape[0] // FFN_SUB + 1):
        pl.when(nsub == n_sub)(functools.partial(_swiglu_chains, n_sub, xb_ref, wg_ref, wu_ref, wd_ref, o_ref))

    if close:
        @pl.when(j == pl.num_programs(1) - 1)
        def _():
            o_ref[...] = _ln(ALPHA * x1_ref[...] + _gate_rows(o_ref[...], g2_ref), lg_ref[...], lb_ref[...])


def _ffn_call(tile_expert, tile_valid, xs, wg, wu, wd, tile_rows, closing=None):
    p, d = xs.shape
    f = wg.shape[2]
    nt = p // tile_rows
    nc = f // FFN_COLS

    def active_col(i, j, tv):
        return jnp.where(tv[i] > 0, j, nc - 1)

    in_specs = [
        pl.BlockSpec((tile_rows, d), lambda i, j, te, tv: (i, 0)),
        pl.BlockSpec((1, d, FFN_COLS), lambda i, j, te, tv: (te[i], 0, active_col(i, j, tv))),
        pl.BlockSpec((1, d, FFN_COLS), lambda i, j, te, tv: (te[i], 0, active_col(i, j, tv))),
        pl.BlockSpec((1, FFN_COLS, d), lambda i, j, te, tv: (te[i], active_col(i, j, tv), 0)),
    ]
    args = [tile_expert, tile_valid, xs, wg, wu, wd]
    if closing is not None:
        layer, x1, mods, lg, lb = closing
        in_specs += [
            pl.BlockSpec((tile_rows, d), lambda i, j, te, tv: (i, 0)),
            _mod_spec(layer, G2, tile_rows, d),
            pl.BlockSpec((1, d), lambda i, j, te, tv: (0, 0)),
            pl.BlockSpec((1, d), lambda i, j, te, tv: (0, 0)),
        ]
        args += [x1, mods, lg, lb]
    return pl.pallas_call(
        functools.partial(_ffn_kernel, closing is not None),
        grid_spec=pltpu.PrefetchScalarGridSpec(
            num_scalar_prefetch=2,
            grid=(nt, nc),
            in_specs=in_specs,
            out_specs=pl.BlockSpec((tile_rows, d), lambda i, j, te, tv: (i, 0)),
            scratch_shapes=[pltpu.VMEM((tile_rows, d), BF16)],
        ),
        out_shape=jax.ShapeDtypeStruct((p, d), F32),
        compiler_params=_cparams("arbitrary", "arbitrary"),
        name="swiglu_close" if closing is not None else "swiglu_grouped",
    )(*args)


def _moe_ffn_kernel(per_step, te_ref, tv_ref, src_ref, h_ref, wg_ref, wu_ref, wd_ref, o_ref, xbuf, xb_ref, sem):
    del te_ref
    g = pl.program_id(0)
    j = pl.program_id(1)
    nt = pl.num_programs(0) - 1
    nc = pl.num_programs(1)
    tile_rows = o_ref.shape[0]

    def gathers(row):
        computes = jnp.where(row >= 1, tv_ref[jnp.maximum(row - 1, 0)], 0) > 0
        fetches = jnp.where(row < nt, tv_ref[jnp.clip(row, 0, nt - 1)], 0) > 0
        return computes | fetches

    def wait_all(slot):
        pltpu.make_async_copy(h_ref.at[pl.ds(0, xbuf.shape[1]), :], xbuf.at[slot], sem.at[slot]).wait()

    valid = jnp.where(g >= 1, tv_ref[jnp.maximum(g - 1, 0)], 0)
    nsub = (valid + FFN_SUB - 1) // FFN_SUB
    slot = g % 2

    @pl.when(j == 0)
    def _():
        o_ref[...] = jnp.zeros(o_ref.shape, F32)

        @pl.when((g >= 1) & gathers(g - 1))
        def _():
            wait_all(1 - slot)

        @pl.when(valid > 0)
        def _():
            xb_ref[...] = xbuf[1 - slot, 0:tile_rows, :].astype(BF16)

    def issue():
        for k in range(per_step):
            r = j * per_step + k
            tok = src_ref[0, jnp.minimum(r, tile_rows - 1)]
            pltpu.make_async_copy(h_ref.at[pl.ds(tok, 1), :], xbuf.at[slot, pl.ds(r, 1), :], sem.at[slot]).start()

    pl.when((nsub == 0) & gathers(g))(issue)
    for n_sub in range(1, tile_rows // FFN_SUB + 1):
        pl.when(nsub == n_sub)(
            functools.partial(_swiglu_chains, n_sub, xb_ref, wg_ref, wu_ref, wd_ref, o_ref, issue))

    @pl.when((g == nt) & (j == nc - 1) & gathers(g))
    def _():
        wait_all(slot)


def _moe_ffn_call(tile_expert, tile_valid, src_tiles, h, wg, wu, wd):
    nt, _, tile_rows = src_tiles.shape
    d = h.shape[1]
    f = wg.shape[2]
    nc = f // FFN_COLS
    per_step = -(-tile_rows // (8 * nc)) * 8

    def tile_c(g):
        return jnp.maximum(g - 1, 0)

    def col(g, j, tv):
        active = (g >= 1) & (tv[tile_c(g)] > 0)
        return jnp.where(active, j, jnp.where(g == 0, 0, nc - 1))

    return pl.pallas_call(
        functools.partial(_moe_ffn_kernel, per_step),
        grid_spec=pltpu.PrefetchScalarGridSpec(
            num_scalar_prefetch=2,
            grid=(nt + 1, nc),
            in_specs=[
                pl.BlockSpec((None, 1, tile_rows), lambda g, j, te, tv: (jnp.minimum(g, nt - 1), 0, 0),
                             memory_space=pltpu.SMEM),
                pl.BlockSpec(memory_space=pl.ANY),
                pl.BlockSpec((1, d, FFN_COLS), lambda g, j, te, tv: (te[tile_c(g)], 0, col(g, j, tv))),
                pl.BlockSpec((1, d, FFN_COLS), lambda g, j, te, tv: (te[tile_c(g)], 0, col(g, j, tv))),
                pl.BlockSpec((1, FFN_COLS, d), lambda g, j, te, tv: (te[tile_c(g)], col(g, j, tv), 0)),
            ],
            out_specs=pl.BlockSpec((tile_rows, d), lambda g, j, te, tv: (tile_c(g), 0)),
            scratch_shapes=[pltpu.VMEM((2, nc * per_step, d), F32), pltpu.VMEM((tile_rows, d), BF16),
                            pltpu.SemaphoreType.DMA((2,))],
        ),
        out_shape=jax.ShapeDtypeStruct((nt * tile_rows, d), F32),
        compiler_params=_cparams("arbitrary", "arbitrary"),
        name="swiglu_routed",
    )(tile_expert, tile_valid, src_tiles, h, wg, wu, wd)


def _final_moe_kernel(ntp, pos_ref, x_ref, rt_ref, g2_ref, lg_ref, lb_ref, ys_ref, op_ref, os_ref, ybuf, sem):
    def issue(r, carry):
        pltpu.make_async_copy(ys_ref.at[pl.ds(pos_ref[0, 2 * r], 1), :], ybuf.at[0, pl.ds(r, 1), :], sem).start()
        pltpu.make_async_copy(ys_ref.at[pl.ds(pos_ref[0, 2 * r + 1], 1), :], ybuf.at[1, pl.ds(r, 1), :], sem).start()
        return carry
    lax.fori_loop(0, ROW_TILE, issue, 0, unroll=8)
    for k in range(2):
        pltpu.make_async_copy(ys_ref.at[pl.ds(0, ROW_TILE), :], ybuf.at[k], sem).wait()
    rt = rt_ref[...]
    f = rt[:, 2:3] * ybuf[0] + rt[:, 3:4] * ybuf[1]
    y = _ln(ALPHA * x_ref[...] + _gate_rows(f, g2_ref), lg_ref[...], lb_ref[...])
    i = pl.program_id(0)

    @pl.when(i < ntp)
    def _():
        op_ref[...] = y

    @pl.when(i >= ntp)
    def _():
        os_ref[...] = y


def _final_moe_call(layer, pos_tiles, x1, rt, mods, lg, lb, ys, *, n_prompt_tiles):
    t, d = x1.shape
    ntp = n_prompt_tiles
    tile = pl.BlockSpec((ROW_TILE, d), lambda i: (i, 0))
    vec = pl.BlockSpec((1, d), lambda i: (0, 0))
    return pl.pallas_call(
        functools.partial(_final_moe_kernel, ntp),
        grid=(t // ROW_TILE,),
        in_specs=[
            pl.BlockSpec((None, 1, 2 * ROW_TILE), lambda i: (i, 0, 0), memory_space=pltpu.SMEM),
            tile,
            pl.BlockSpec((ROW_TILE, LANES), lambda i: (i, 0)),
            _mod_spec(layer, G2, ROW_TILE, d),
            vec, vec,
            pl.BlockSpec(memory_space=pl.ANY),
        ],
        out_specs=[pl.BlockSpec((ROW_TILE, d), lambda i: (jnp.minimum(i, ntp - 1), 0)),
                   pl.BlockSpec((ROW_TILE, d), lambda i: (jnp.maximum(i - ntp, 0), 0))],
        out_shape=[jax.ShapeDtypeStruct((ntp * ROW_TILE, d), F32),
                   jax.ShapeDtypeStruct((t - ntp * ROW_TILE, d), F32)],
        scratch_shapes=[pltpu.VMEM((2, ROW_TILE, d), F32), pltpu.SemaphoreType.DMA(())],
        compiler_params=_cparams("arbitrary"),
        name="final_moe",
    )(pos_tiles, x1, rt, mods, lg, lb, ys)


def _rel_bias(table, n_q, n_k, banded):
    r = jnp.arange(n_q)[:, None]
    w = jnp.arange(n_k)[None, :]
    n_diag = n_q + n_k - 1
    k = jnp.arange(n_diag)
    diag = table[:, jnp.clip(LEFT_CTX + (n_q - 1) - k, -REL_CLIP, REL_CLIP) + REL_CLIP].astype(F32) * LOG2E
    padded = jnp.concatenate([diag, jnp.zeros((diag.shape[0], 1), F32)], axis=1)
    skew = jnp.tile(padded, (1, n_q))[:, :n_q * n_diag].reshape(-1, n_q, n_diag)
    bias = skew[:, :, n_q - 1:n_q - 1 + n_k]
    if banded:
        qa = r // CHUNK
        kc = w // CHUNK
        vis = (kc >= qa) & (kc <= qa + LEFT_CHUNKS)
        bias = jnp.where(vis[None], bias, NEG)
    return bias


def _gating_weights(w_s, b_s):
    n = MLP_CHUNK
    tril = jnp.tril(jnp.ones((n, n), bool))
    wm = jnp.where(tril[None], w_s, 0.0)
    h = CHUNK
    top = wm[:, :h, :h]
    z = jnp.zeros_like(top)
    wm_s = jnp.concatenate([jnp.concatenate([top, z], 2), jnp.concatenate([z, top], 2)], 1)
    both = jnp.stack([wm, wm_s])
    wcat = jnp.concatenate([both[:, 0::2], both[:, 1::2]], axis=-1)
    bias_p = jnp.repeat(jnp.transpose(b_s), GROUP_DIM, axis=1)
    bias_s = jnp.concatenate([bias_p[:h], bias_p[:h]], 0)
    return wcat.astype(BF16), jnp.stack([bias_p, bias_s]).astype(F32)


def _route_plan(rt, n_tiles, tile_rows):
    t = rt.shape[0]
    e_flat = rt[:, 0:2].astype(jnp.int32).reshape(-1)
    oh = (e_flat[:, None] == jnp.arange(N_EXPERTS)[None, :]).astype(jnp.int32)
    csum = jnp.cumsum(oh, axis=0)
    rank = jnp.sum(oh * (csum - 1), axis=1)
    counts = csum[-1]
    tiles_e = (counts + tile_rows - 1) // tile_rows
    tile_end = jnp.cumsum(tiles_e)
    tile_start = tile_end - tiles_e
    pos = (tile_start * tile_rows)[e_flat] + rank
    ti = jnp.arange(n_tiles)
    te = jnp.minimum(jnp.sum((ti[:, None] >= tile_end[None, :]).astype(jnp.int32), axis=1), N_EXPERTS - 1)
    tv = jnp.clip(counts[te] - (ti - tile_start[te]) * tile_rows, 0, tile_rows)
    tv = jnp.where(ti < tile_end[-1], tv, 0)
    last_e = te[jnp.maximum(tile_end[-1] - 1, 0)]
    te = jnp.where(ti < tile_end[-1], te, last_e)
    src = jnp.zeros((n_tiles * tile_rows,), jnp.int32).at[pos].set(jnp.arange(2 * t, dtype=jnp.int32) // 2,
                                                                  unique_indices=True)
    return (pos.reshape(t // ROW_TILE, 1, 2 * ROW_TILE).astype(jnp.int32), src.reshape(n_tiles, 1, tile_rows),
            te.astype(jnp.int32), tv.astype(jnp.int32))


def kernel(x_prompt, x_sample, cache_k, cache_v, c_prompt, c_sample, ln_in_g, ln_in_b, w_in, w_out, rel_bias_table, gmlp_ln_g, gmlp_ln_b, gmlp_w_s, gmlp_b_s, out_norm_a, out_norm_b, ada_w, ada_b, ln1_g, ln1_b, ln2_g, ln2_b, ffn_w_gate, ffn_w_up, ffn_w_down, moe_router, moe_w_gate, moe_w_up, moe_w_down):
    nb, seq, d = x_prompt.shape
    ns, dseq, _ = x_sample.shape
    depth = w_in.shape[0]
    tp = nb * seq
    ts = ns * dseq
    t = tp + ts
    assert depth == DEPTH and dseq == CHUNK and seq % ROW_TILE == 0 and ts % ROW_TILE == 0
    assert t % FFN_TILE == 0 and cache_k.shape[2] == LEFT_CTX
    ntp = tp // ROW_TILE

    n_cond = nb + ns
    c_all = jnp.concatenate([c_prompt, c_sample], axis=0)
    c_pad = jnp.pad(c_all, ((0, (-n_cond) % 8), (0, 0)))
    mods = _ada_call(c_pad, ada_w, ada_b)
    mods = jnp.transpose(mods[:, :n_cond].reshape(depth, n_cond, 6, d), (0, 2, 1, 3))
    cpb = seq // CHUNK
    mods_p = jnp.broadcast_to(mods[:, :, :nb, None, :], (depth, 6, nb, cpb, d)).reshape(depth, 6, nb * cpb, d)
    mods = jnp.concatenate([mods_p, mods[:, :, nb:]], axis=2)

    def row(v):
        return v.reshape(1, -1)

    w_in_bf = w_in.astype(BF16)
    w_out_bf = w_out.astype(BF16)
    cache_kt = jnp.transpose(cache_k, (0, 1, 3, 4, 2))
    cache_vt = jnp.transpose(cache_v, (0, 1, 3, 4, 2))
    k_tail, v_tail, gv_rows = [], [], []
    for l in range(depth):
        wcat, bs = _gating_weights(gmlp_w_s[l], gmlp_b_s[l])
        if l == 0:
            x_in = (x_prompt.reshape(tp, d), x_sample.reshape(ts, d), row(ln_in_g), row(ln_in_b))
        else:
            x_in = (x,)
        res = _inproj_call(l, x_in, mods, w_in_bf[l], row(gmlp_ln_g[l]), row(gmlp_ln_b[l]), wcat, bs,
                           row(out_norm_b[l]),
                           n_prompt_tiles=ntp, tiles_per_batch=seq // ROW_TILE, n_batch=nb, n_sample_rows=ts)
        if l == 0:
            x, q, k, v, kf, vf, gv, bn = res
        else:
            q, k, v, kf, vf, gv, bn = res
        k_tail.append(kf)
        v_tail.append(vf)
        gv_rows.append(gv)

        table = rel_bias_table[l]
        bias_p = _rel_bias(table, Q_TILE, 3 * Q_TILE, True)
        bias_s = _rel_bias(table, CHUNK, LEFT_CTX + CHUNK, False)
        na = row(out_norm_a[l])
        an_p = _attn_prompt_call(q, k, v, bias_p, na, n_batch=nb, q_tiles_per_batch=seq // Q_TILE)
        an_s = _attn_sample_call(l, q, k, v, cache_kt, cache_vt, bias_s[:, :, :LEFT_CTX], bias_s[:, :, LEFT_CTX:], na,
                                 first_chunk=tp // CHUNK)

        lg1, lb1, lg2, lb2 = row(ln1_g[l]), row(ln1_b[l]), row(ln2_g[l]), row(ln2_b[l])
        i = l // 2
        if l % 2 == 0:
            x1, hp = _outproj_call(l, False, x, an_p, an_s, bn, w_out_bf[l], mods, lg1, lb1)
            nt = t // FFN_TILE
            te = jnp.zeros((nt,), jnp.int32)
            tv = jnp.full((nt,), FFN_TILE, jnp.int32)
            x = _ffn_call(te, tv, hp, ffn_w_gate[i:i + 1], ffn_w_up[i:i + 1], ffn_w_down[i:i + 1], FFN_TILE,
                          closing=(l, x1, mods, lg2, lb2))
        else:
            r_pad = jnp.pad(moe_router[i], ((0, 0), (0, LANES - N_EXPERTS)))
            r_hi = r_pad.astype(BF16)
            r_lo = (r_pad - r_hi.astype(F32)).astype(BF16)
            r_cat = jnp.concatenate([r_hi, r_lo], axis=1)
            x1, hp, rt = _outproj_call(l, True, x, an_p, an_s, bn, w_out_bf[l], mods, lg1, lb1, r_cat)
            n_tiles = -(-2 * t // MOE_TILE) + N_EXPERTS
            pos, src, te, tv = _route_plan(rt, n_tiles, MOE_TILE)
            ys = _moe_ffn_call(te, tv, src, hp, moe_w_gate[i], moe_w_up[i], moe_w_down[i])
            y_p, y_s = _final_moe_call(l, pos, x1, rt, mods, lg2, lb2, ys, n_prompt_tiles=ntp)

    y_prompt = y_p.reshape(nb, seq, d)
    y_sample = y_s.reshape(ns, dseq, d)

    def tails(rows):
        kp = jnp.stack([r[:nb * ROW_TILE].reshape(nb, ROW_TILE, N_HEADS, HEAD_DIM) for r in rows])
        ksn = jnp.stack([r[nb * ROW_TILE:].reshape(ns, dseq, N_HEADS, HEAD_DIM) for r in rows])
        return kp, ksn

    k_prompt_new, k_sample_new = tails(k_tail)
    v_prompt_new, v_sample_new = tails(v_tail)
    gmlp_v_sample_new = jnp.stack([g.reshape(ns, dseq, D_B) for g in gv_rows])
    return (y_prompt, y_sample, k_prompt_new, v_prompt_new, k_sample_new, v_sample_new, gmlp_v_sample_new)
```

```python
import functools

import jax
import jax.numpy as jnp
from jax import lax
from jax.experimental import pallas as pl
from jax.experimental.pallas import tpu as pltpu

CHUNK = 64
LEFT_CHUNKS = 8
LEFT_CTX = LEFT_CHUNKS * CHUNK
N_HEADS = 8
HEAD_DIM = 64
D_A = N_HEADS * HEAD_DIM
N_GROUPS = 8
GROUP_DIM = 64
D_B = N_GROUPS * GROUP_DIM
MLP_CHUNK = 128
REL_CLIP = 128
N_EXPERTS = 8
DEPTH = 2
ALPHA = (2 * DEPTH) ** 0.25
LN_EPS = 1e-5
ATTN_SCALE = HEAD_DIM ** -0.5
LOG2E = 1.4426950408889634
NEG = -1e30
SH1, SC1, G1, SH2, SC2, G2 = range(6)

LANES = 128
ROW_TILE = 512
Q_TILE = 256
FFN_TILE = 1024
MOE_TILE = 2048
FFN_SUB = 512
FFN_COLS = 256
VMEM_LIMIT = 56 * 1024 * 1024

BF16 = jnp.bfloat16
F32 = jnp.float32


def _cparams(*sem):
    return pltpu.CompilerParams(dimension_semantics=sem, vmem_limit_bytes=VMEM_LIMIT)


def _ln(x, g, b):
    mu = jnp.mean(x, axis=-1, keepdims=True)
    xc = x - mu
    var = jnp.mean(xc * xc, axis=-1, keepdims=True)
    return xc * lax.rsqrt(var + LN_EPS) * g + b


def _rms(x, g):
    return x * lax.rsqrt(jnp.mean(x * x, axis=-1, keepdims=True) + LN_EPS) * g


def _modulate(x, sc_ref, sh_ref):
    parts = []
    for c in range(x.shape[0] // CHUNK):
        xc = x[c * CHUNK:(c + 1) * CHUNK]
        parts.append(xc * (1.0 + sc_ref[c:c + 1, :]) + sh_ref[c:c + 1, :])
    return jnp.concatenate(parts, axis=0)


def _gate_rows(x, g_ref):
    parts = []
    for c in range(x.shape[0] // CHUNK):
        parts.append(x[c * CHUNK:(c + 1) * CHUNK] * (1.0 + g_ref[c:c + 1, :]))
    return jnp.concatenate(parts, axis=0)


def _mod_spec(layer, comp, rows, d):
    return pl.BlockSpec((None, None, rows // CHUNK, d), lambda i, *_: (layer, comp, i, 0))


def _split_bf16(x):
    hi = x.astype(BF16)
    lo = (x - hi.astype(F32)).astype(BF16)
    return hi, lo


def _dot(a, b):
    return jnp.dot(a, b, preferred_element_type=F32)


def _ada_kernel(c_ref, w_ref, b_ref, o_ref):
    c = c_ref[...]
    s = c * jax.nn.sigmoid(c)
    s_hi, s_lo = _split_bf16(s)
    w_hi, w_lo = _split_bf16(w_ref[0])
    o_ref[0] = _dot(s_hi, w_hi) + _dot(s_lo, w_hi) + _dot(s_hi, w_lo) + b_ref[0]


def _ada_call(c_pad, ada_w, ada_b):
    depth, d, n = ada_w.shape
    rows = c_pad.shape[0]
    tn = n // 4
    return pl.pallas_call(
        _ada_kernel,
        grid=(depth, n // tn),
        in_specs=[
            pl.BlockSpec((rows, d), lambda l, j: (0, 0)),
            pl.BlockSpec((1, d, tn), lambda l, j: (l, 0, j)),
            pl.BlockSpec((1, 1, tn), lambda l, j: (l, 0, j)),
        ],
        out_specs=pl.BlockSpec((1, rows, tn), lambda l, j: (l, 0, j)),
        out_shape=jax.ShapeDtypeStruct((depth, rows, n), F32),
        compiler_params=_cparams("arbitrary", "arbitrary"),
        name="ada_mod",
    )(c_pad, ada_w, ada_b.reshape(depth, 1, n))


def _inproj_kernel(first, ntp, *refs):
    if first:
        xp_ref, xs_ref, lng_ref, lnb_ref = refs[:4]
        xn_ref = refs[-8]
        x = jnp.where(pl.program_id(0) >= ntp, xs_ref[...], xp_ref[...])
        x = _ln(x, lng_ref[...], lnb_ref[...])
        xn_ref[...] = x
        refs = refs[4:-8] + refs[-7:]
    else:
        x = refs[0][...]
        refs = refs[1:]
    (sc_ref, sh_ref, w_ref, gg_ref, gb_ref, wcat_ref, bs_ref, nb_ref,
     q_ref, k_ref, v_ref, kf_ref, vf_ref, gv_ref, bn_ref) = refs
    h = _modulate(x, sc_ref, sh_ref).astype(BF16)

    q = _dot(h, w_ref[:, 0:D_A])
    q_ref[...] = (q * (ATTN_SCALE * LOG2E)).astype(BF16)
    k = _dot(h, w_ref[:, D_A:2 * D_A])
    k_ref[...] = k.astype(BF16)
    kf_ref[...] = k
    v = _dot(h, w_ref[:, 2 * D_A:3 * D_A])
    v_ref[...] = v.astype(BF16)
    vf_ref[...] = v

    u = jax.nn.gelu(_dot(h, w_ref[:, 3 * D_A:3 * D_A + D_B]), approximate=True)
    vb = jax.nn.gelu(_dot(h, w_ref[:, 3 * D_A + D_B:3 * D_A + 2 * D_B]), approximate=True)
    vn = _ln(vb, gg_ref[...], gb_ref[...])
    gv_ref[...] = vn

    lane = lax.broadcasted_iota(jnp.int32, (MLP_CHUNK, LANES), 1)
    lo = lane < GROUP_DIM
    vnb = vn.astype(BF16)
    zero = jnp.zeros((MLP_CHUNK, LANES), BF16)
    n_chunks = ROW_TILE // MLP_CHUNK
    mixed = []
    for p in range(D_B // LANES):
        stacks = []
        for c in range(n_chunks):
            slab = vnb[c * MLP_CHUNK:(c + 1) * MLP_CHUNK, p * LANES:(p + 1) * LANES]
            stacks.append(jnp.concatenate([jnp.where(lo, slab, zero), jnp.where(lo, zero, slab)], axis=0))
        mixed.append(_dot(wcat_ref[0, p], jnp.concatenate(stacks, axis=1)))
    rows = []
    for c in range(n_chunks):
        r0 = c * MLP_CHUNK
        mix_c = jnp.concatenate([m[:, c * LANES:(c + 1) * LANES] for m in mixed], axis=1) + bs_ref[0]
        rows.append(u[r0:r0 + MLP_CHUNK, :] * mix_c)
    bo = jnp.concatenate(rows, axis=0)
    bn_ref[...] = _rms(bo, nb_ref[...]).astype(BF16)


def _inproj_call(layer, x_in, mods, w_bf, gg, gb, wcat, bs, nb, *, n_prompt_tiles, tiles_per_batch, n_batch,
                 n_sample_rows):
    first = len(x_in) == 4
    d = x_in[0].shape[1]
    t = mods.shape[2] * CHUNK
    nt = t // ROW_TILE
    ntp = n_prompt_tiles
    n_tail = n_batch + (nt - ntp)

    def tile(i):
        return (i, 0)

    def const2(i):
        return (0, 0)

    def selmap4(i):
        return (jnp.where(i >= ntp, 1, 0), 0, 0, 0)

    def selmap3(i):
        return (jnp.where(i >= ntp, 1, 0), 0, 0)

    def tailmap(i):
        return (jnp.where(i < ntp, i // tiles_per_batch, n_batch + i - ntp), 0)

    def gvmap(i):
        return (jnp.maximum(i - ntp, 0), 0)

    if first:
        x_specs = [
            pl.BlockSpec((ROW_TILE, d), lambda i: (jnp.minimum(i, ntp - 1), 0)),
            pl.BlockSpec((ROW_TILE, d), lambda i: (jnp.maximum(i - ntp, 0), 0)),
            pl.BlockSpec((1, d), const2),
            pl.BlockSpec((1, d), const2),
        ]
    else:
        x_specs = [pl.BlockSpec((ROW_TILE, d), tile)]
    in_specs = x_specs + [
        _mod_spec(layer, SC1, ROW_TILE, d),
        _mod_spec(layer, SH1, ROW_TILE, d),
        pl.BlockSpec(w_bf.shape, const2),
        pl.BlockSpec((1, D_B), const2),
        pl.BlockSpec((1, D_B), const2),
        pl.BlockSpec((1,) + wcat.shape[1:], selmap4),
        pl.BlockSpec((1, MLP_CHUNK, D_B), selmap3),
        pl.BlockSpec((1, D_B), const2),
    ]
    out_specs = [
        pl.BlockSpec((ROW_TILE, D_A), tile),
        pl.BlockSpec((ROW_TILE, D_A), tile),
        pl.BlockSpec((ROW_TILE, D_A), tile),
        pl.BlockSpec((ROW_TILE, D_A), tailmap),
        pl.BlockSpec((ROW_TILE, D_A), tailmap),
        pl.BlockSpec((ROW_TILE, D_B), gvmap),
        pl.BlockSpec((ROW_TILE, D_B), tile),
    ]
    out_shape = [
        jax.ShapeDtypeStruct((t, D_A), BF16),
        jax.ShapeDtypeStruct((t, D_A), BF16),
        jax.ShapeDtypeStruct((t, D_A), BF16),
        jax.ShapeDtypeStruct((n_tail * ROW_TILE, D_A), F32),
        jax.ShapeDtypeStruct((n_tail * ROW_TILE, D_A), F32),
        jax.ShapeDtypeStruct((n_sample_rows, D_B), F32),
        jax.ShapeDtypeStruct((t, D_B), BF16),
    ]
    if first:
        out_specs = [pl.BlockSpec((ROW_TILE, d), tile)] + out_specs
        out_shape = [jax.ShapeDtypeStruct((t, d), F32)] + out_shape
    return pl.pallas_call(
        functools.partial(_inproj_kernel, first, ntp),
        grid=(nt,),
        in_specs=in_specs,
        out_specs=out_specs,
        out_shape=out_shape,
        compiler_params=_cparams("arbitrary"),
        name="in_proj_first" if first else "in_proj",
    )(*x_in, mods, mods, w_bf, gg, gb, wcat, bs, nb)


def _row_max(pieces):
    folded = None
    m = None
    for s in pieces:
        if s.shape[1] % LANES == 0:
            for c0 in range(0, s.shape[1], LANES):
                piece = s[:, c0:c0 + LANES]
                folded = piece if folded is None else jnp.maximum(folded, piece)
        else:
            mj = s.max(axis=1, keepdims=True)
            m = mj if m is None else jnp.maximum(m, mj)
    if folded is not None:
        mj = folded.max(axis=1, keepdims=True)
        m = mj if m is None else jnp.maximum(m, mj)
    return m


def _attend(q_of, r, kv_slabs, bias_of, part_rows, part_ok, na):
    nt_dims = (((1,), (1,)), ((), ()))
    lo = lax.broadcasted_iota(jnp.int32, (1, LANES), 1) < HEAD_DIM
    lo_t = lax.broadcasted_iota(jnp.int32, (LANES, 1), 0) < HEAD_DIM
    bounds = sorted({0, r} | {b for rows in part_rows for b in rows})
    segs = list(zip(bounds[:-1], bounds[1:]))

    def covered(seg, rows):
        return rows[0] <= seg[0] and seg[1] <= rows[1]

    outs = []
    for p in range(D_A // LANES):
        qp = q_of(p)
        slabs = kv_slabs(p)
        accs = []
        for half in range(2):
            h = 2 * p + half
            keep = lo if half == 0 else jnp.logical_not(lo)
            keep_t = lo_t if half == 0 else jnp.logical_not(lo_t)
            qh = jnp.where(keep, qp, jnp.zeros_like(qp))
            ss = []
            for j, (kp, _, transposed) in enumerate(slabs):
                qj = qh[part_rows[j][0]:part_rows[j][1]]
                if transposed:
                    s = _dot(qj, kp)
                else:
                    s = lax.dot_general(qj, kp, nt_dims, preferred_element_type=F32)
                s = s + bias_of(h, j)
                if part_ok is not None and part_ok[j] is not None:
                    s = jnp.where(part_ok[j], s, NEG)
                ss.append(s)
            m_segs = [_row_max([s[a - rows[0]:b - rows[0]] for s, rows in zip(ss, part_rows) if covered((a, b), rows)])
                      for a, b in segs]
            acc_segs = [jnp.zeros((b - a, LANES), F32) for a, b in segs]
            for s, (_, vp, transposed), rows in zip(ss, slabs, part_rows):
                m = jnp.concatenate([m_segs[i] for i, seg in enumerate(segs) if covered(seg, rows)], axis=0)
                e = jnp.exp2(s - m).astype(BF16)
                if transposed:
                    vh = jnp.where(keep_t, vp, jnp.ones_like(vp))
                    c = lax.dot_general(e, vh, nt_dims, preferred_element_type=F32)
                else:
                    c = _dot(e, jnp.where(keep, vp, jnp.ones_like(vp)))
                for i, (a, b) in enumerate(segs):
                    if covered((a, b), rows):
                        acc_segs[i] = acc_segs[i] + c[a - rows[0]:b - rows[0]]
            accs.append(jnp.concatenate(acc_segs, axis=0))
        num = jnp.where(lo, accs[0], accs[1])
        den = pltpu.roll(jnp.where(lo, accs[1], accs[0]), HEAD_DIM, 1)
        outs.append(num / den)
    a = jnp.concatenate(outs, axis=1)
    return _rms(a, na)


def _attn_prompt_kernel(q_ref, k0_ref, k1_ref, k2_ref, k3_ref, v0_ref, v1_ref, v2_ref, v3_ref, bias_ref, na_ref,
                        o_ref):
    j = pl.program_id(1)
    k_refs = (k0_ref, k1_ref, k2_ref, k3_ref)
    v_refs = (v0_ref, v1_ref, v2_ref, v3_ref)
    part_rows = [(0, Q_TILE), (0, 2 * Q_TILE), (0, 2 * Q_TILE), (Q_TILE, 2 * Q_TILE)]

    def q_of(p):
        return q_ref[:, p * LANES:(p + 1) * LANES]

    def kv_slabs(p):
        sl = slice(p * LANES, (p + 1) * LANES)
        return [(kr[:, sl], vr[:, sl], False) for kr, vr in zip(k_refs, v_refs)]

    def bias_of(h, b):
        first = bias_ref[h, :, b * Q_TILE:(b + 1) * Q_TILE] if b < 3 else None
        second = bias_ref[h, :, (b - 1) * Q_TILE:b * Q_TILE] if b > 0 else None
        if first is None:
            return second
        if second is None:
            return first
        return jnp.concatenate([first, second], axis=0)

    def run(part_ok):
        o_ref[...] = _attend(q_of, 2 * Q_TILE, kv_slabs, bias_of, part_rows, part_ok, na_ref[...]).astype(BF16)

    pl.when(j >= 1)(functools.partial(run, None))
    pl.when(j < 1)(functools.partial(run, [j >= 1, j >= 1, None, None]))


def _attn_prompt_call(q, k, v, bias, na, *, n_batch, q_tiles_per_batch):
    nq = q_tiles_per_batch // 2

    def kmap(part):
        return lambda b, j: (2 * b * nq + jnp.maximum(2 * j - 2 + part, 0), 0)

    blk = pl.BlockSpec((2 * Q_TILE, D_A), lambda b, j: (b * nq + j, 0))
    kv_specs = [pl.BlockSpec((Q_TILE, D_A), kmap(part)) for part in range(4)]
    return pl.pallas_call(
        _attn_prompt_kernel,
        grid=(n_batch, nq),
        in_specs=[blk] + kv_specs + kv_specs + [
            pl.BlockSpec(bias.shape, lambda b, j: (0, 0, 0)),
            pl.BlockSpec((1, D_A), lambda b, j: (0, 0)),
        ],
        out_specs=blk,
        out_shape=jax.ShapeDtypeStruct((n_batch * nq * 2 * Q_TILE, D_A), BF16),
        compiler_params=_cparams("arbitrary", "arbitrary"),
        name="attn_prompt",
    )(q, k, k, k, k, v, v, v, v, bias, na)


def _attn_sample_kernel(q_ref, kn_ref, vn_ref, ck_ref, cv_ref, bias_c_ref, bias_n_ref, na_ref, o_ref):
    def kv_slabs(p):
        sl = slice(p * LANES, (p + 1) * LANES)

        def cached(c_ref):
            return c_ref[2 * p:2 * p + 2].reshape(LANES, c_ref.shape[2]).astype(BF16)
        return [(cached(ck_ref), cached(cv_ref), True), (kn_ref[:, sl], vn_ref[:, sl], False)]

    def bias_of(h, b):
        return bias_c_ref[h] if b == 0 else bias_n_ref[h]

    def q_of(p):
        return q_ref[:, p * LANES:(p + 1) * LANES]

    r = q_ref.shape[0]
    o_ref[...] = _attend(q_of, r, kv_slabs, bias_of, [(0, r), (0, r)], None, na_ref[...]).astype(BF16)


def _attn_sample_call(layer, q, k, v, cache_kt, cache_vt, bias_c, bias_n, na, *, first_chunk):
    _, nb, nh, hd, win = cache_kt.shape
    blk = pl.BlockSpec((CHUNK, D_A), lambda b: (first_chunk + b, 0))
    cache = pl.BlockSpec((None, None, nh, hd, win), lambda b: (layer, b, 0, 0, 0))
    return pl.pallas_call(
        _attn_sample_kernel,
        grid=(nb,),
        in_specs=[blk, blk, blk, cache, cache,
                  pl.BlockSpec(bias_c.shape, lambda b: (0, 0, 0)),
                  pl.BlockSpec(bias_n.shape, lambda b: (0, 0, 0)),
                  pl.BlockSpec((1, D_A), lambda b: (0, 0))],
        out_specs=pl.BlockSpec((CHUNK, D_A), lambda b: (b, 0)),
        out_shape=jax.ShapeDtypeStruct((nb * CHUNK, D_A), BF16),
        compiler_params=_cparams("arbitrary"),
        name="attn_sample",
    )(q, k, v, cache_kt, cache_vt, bias_c, bias_n, na)


def _outproj_kernel(route, ntp, x_ref, anp_ref, ans_ref, bn_ref, w_ref, g1_ref, sc_ref, sh_ref, lg_ref, lb_ref, *rest):
    if route:
        rc_ref, x1_ref, hp_ref, rt_ref = rest
    else:
        x1_ref, hp_ref = rest
    an = jnp.where(pl.program_id(0) >= ntp, ans_ref[...], anp_ref[...])
    mix = _dot(an, w_ref[0:D_A, :]) + _dot(bn_ref[...], w_ref[D_A:D_A + D_B, :])
    x1 = _ln(ALPHA * x_ref[...] + _gate_rows(mix, g1_ref), lg_ref[...], lb_ref[...])
    x1_ref[...] = x1
    h2 = _modulate(x1, sc_ref, sh_ref)
    hp_ref[...] = h2
    if route:
        h_hi, h_lo = _split_bf16(h2)
        both = _dot(h_hi, rc_ref[...])
        logits = both[:, :LANES] + both[:, LANES:] + _dot(h_lo, rc_ref[:, :LANES])
        lane = lax.broadcasted_iota(jnp.int32, logits.shape, 1)
        logits = jnp.where(lane < N_EXPERTS, logits, -jnp.inf)
        m1 = logits.max(axis=1, keepdims=True)
        i1 = jnp.where(logits == m1, lane, LANES).min(axis=1, keepdims=True)
        rest_l = jnp.where(lane == i1, -jnp.inf, logits)
        m2 = rest_l.max(axis=1, keepdims=True)
        i2 = jnp.where(rest_l == m2, lane, LANES).min(axis=1, keepdims=True)
        e2 = jnp.exp(m2 - m1)
        w1 = 1.0 / (1.0 + e2)
        w2 = e2 / (1.0 + e2)
        rt = jnp.where(lane == 0, i1.astype(F32),
                       jnp.where(lane == 1, i2.astype(F32),
                                 jnp.where(lane == 2, w1, jnp.where(lane == 3, w2, 0.0))))
        rt_ref[...] = rt


def _outproj_call(layer, route, x, an_p, an_s, bn, w_bf, mods, lg, lb, r_cat=None):
    t, d = x.shape
    nt = t // ROW_TILE
    ntp = an_p.shape[0] // ROW_TILE

    def tile(i):
        return (i, 0)

    def const2(i):
        return (0, 0)

    in_specs = [
        pl.BlockSpec((ROW_TILE, d), tile),
        pl.BlockSpec((ROW_TILE, D_A), lambda i: (jnp.minimum(i, ntp - 1), 0)),
        pl.BlockSpec((ROW_TILE, D_A), lambda i: (jnp.maximum(i - ntp, 0), 0)),
        pl.BlockSpec((ROW_TILE, D_B), tile),
        pl.BlockSpec(w_bf.shape, const2),
        _mod_spec(layer, G1, ROW_TILE, d),
        _mod_spec(layer, SC2, ROW_TILE, d),
        _mod_spec(layer, SH2, ROW_TILE, d),
        pl.BlockSpec((1, d), const2),
        pl.BlockSpec((1, d), const2),
    ]
    out_specs = [pl.BlockSpec((ROW_TILE, d), tile), pl.BlockSpec((ROW_TILE, d), tile)]
    out_shape = [jax.ShapeDtypeStruct((t, d), F32), jax.ShapeDtypeStruct((t, d), F32)]
    args = [x, an_p, an_s, bn, w_bf, mods, mods, mods, lg, lb]
    if route:
        in_specs.append(pl.BlockSpec(r_cat.shape, const2))
        out_specs.append(pl.BlockSpec((ROW_TILE, LANES), tile))
        out_shape.append(jax.ShapeDtypeStruct((t, LANES), F32))
        args.append(r_cat)
    return pl.pallas_call(
        functools.partial(_outproj_kernel, route, ntp),
        grid=(nt,),
        in_specs=in_specs,
        out_specs=out_specs,
        out_shape=out_shape,
        compiler_params=_cparams("arbitrary"),
        name="out_proj_route" if route else "out_proj",
    )(*args)


def _swiglu_chains(n_sub, xb_ref, wg_ref, wu_ref, wd_ref, o_ref):
    wg = wg_ref[0].astype(BF16)
    wu = wu_ref[0].astype(BF16)
    wd = wd_ref[0].astype(BF16)
    for s in range(n_sub):
        rows = slice(s * FFN_SUB, (s + 1) * FFN_SUB)
        xs = xb_ref[rows, :]
        g = _dot(xs, wg)
        u = _dot(xs, wu)
        a = (g * jax.nn.sigmoid(g) * u).astype(BF16)
        o_ref[rows, :] += _dot(a, wd)


def _ffn_kernel(close, te_ref, tv_ref, x_ref, wg_ref, wu_ref, wd_ref, *rest):
    if close:
        x1_ref, g2_ref, lg_ref, lb_ref, o_ref, xb_ref = rest
    else:
        o_ref, xb_ref = rest
    del te_ref
    i = pl.program_id(0)
    j = pl.program_id(1)
    valid = tv_ref[i]
    nsub = (valid + FFN_SUB - 1) // FFN_SUB

    @pl.when(j == 0)
    def _():
        xb_ref[...] = x_ref[...].astype(BF16)
        o_ref[...] = jnp.zeros(o_ref.shape, F32)

    for n_sub in range(1, x_ref.shape[0] // FFN_SUB + 1):
        pl.when(nsub == n_sub)(functools.partial(_swiglu_chains, n_sub, xb_ref, wg_ref, wu_ref, wd_ref, o_ref))

    if close:
        @pl.when(j == pl.num_programs(1) - 1)
        def _():
            o_ref[...] = _ln(ALPHA * x1_ref[...] + _gate_rows(o_ref[...], g2_ref), lg_ref[...], lb_ref[...])


def _ffn_call(tile_expert, tile_valid, xs, wg, wu, wd, tile_rows, closing=None):
    p, d = xs.shape
    f = wg.shape[2]
    nt = p // tile_rows
    nc = f // FFN_COLS

    def active_col(i, j, tv):
        return jnp.where(tv[i] > 0, j, nc - 1)

    in_specs = [
        pl.BlockSpec((tile_rows, d), lambda i, j, te, tv: (i, 0)),
        pl.BlockSpec((1, d, FFN_COLS), lambda i, j, te, tv: (te[i], 0, active_col(i, j, tv))),
        pl.BlockSpec((1, d, FFN_COLS), lambda i, j, te, tv: (te[i], 0, active_col(i, j, tv))),
        pl.BlockSpec((1, FFN_COLS, d), lambda i, j, te, tv: (te[i], active_col(i, j, tv), 0)),
    ]
    args = [tile_expert, tile_valid, xs, wg, wu, wd]
    if closing is not None:
        layer, x1, mods, lg, lb = closing
        in_specs += [
            pl.BlockSpec((tile_rows, d), lambda i, j, te, tv: (i, 0)),
            _mod_spec(layer, G2, tile_rows, d),
            pl.BlockSpec((1, d), lambda i, j, te, tv: (0, 0)),
            pl.BlockSpec((1, d), lambda i, j, te, tv: (0, 0)),
        ]
        args += [x1, mods, lg, lb]
    return pl.pallas_call(
        functools.partial(_ffn_kernel, closing is not None),
        grid_spec=pltpu.PrefetchScalarGridSpec(
            num_scalar_prefetch=2,
            grid=(nt, nc),
            in_specs=in_specs,
            out_specs=pl.BlockSpec((tile_rows, d), lambda i, j, te, tv: (i, 0)),
            scratch_shapes=[pltpu.VMEM((tile_rows, d), BF16)],
        ),
        out_shape=jax.ShapeDtypeStruct((p, d), F32),
        compiler_params=_cparams("arbitrary", "arbitrary"),
        name="swiglu_close" if closing is not None else "swiglu_grouped",
    )(*args)


def _dispatch_kernel(pos_ref, src_ref, init_ref, dst_ref, sem):
    del init_ref

    def issue(r, carry):
        row = src_ref.at[pl.ds(r, 1), :]
        pltpu.make_async_copy(row, dst_ref.at[pl.ds(pos_ref[0, 2 * r], 1), :], sem).start()
        pltpu.make_async_copy(row, dst_ref.at[pl.ds(pos_ref[0, 2 * r + 1], 1), :], sem).start()
        return carry
    lax.fori_loop(0, ROW_TILE, issue, 0, unroll=8)
    for _ in range(2):
        pltpu.make_async_copy(src_ref, dst_ref.at[pl.ds(0, ROW_TILE), :], sem).wait()


def _dispatch_call(pos_tiles, hp, n_rows_sorted):
    t, d = hp.shape
    nt = t // ROW_TILE
    return pl.pallas_call(
        _dispatch_kernel,
        grid=(nt,),
        in_specs=[
            pl.BlockSpec((None, 1, 2 * ROW_TILE), lambda i: (i, 0, 0), memory_space=pltpu.SMEM),
            pl.BlockSpec((ROW_TILE, d), lambda i: (i, 0)),
            pl.BlockSpec(memory_space=pl.ANY),
        ],
        out_specs=pl.BlockSpec(memory_space=pl.ANY),
        out_shape=jax.ShapeDtypeStruct((n_rows_sorted, d), hp.dtype),
        scratch_shapes=[pltpu.SemaphoreType.DMA(())],
        input_output_aliases={2: 0},
        compiler_params=_cparams("arbitrary"),
        name="dispatch_rows",
    )(pos_tiles, hp, jnp.zeros((n_rows_sorted, d), hp.dtype))


def _final_moe_kernel(ntp, pos_ref, x_ref, rt_ref, g2_ref, lg_ref, lb_ref, ys_ref, op_ref, os_ref, ybuf, sem):
    def issue(r, carry):
        pltpu.make_async_copy(ys_ref.at[pl.ds(pos_ref[0, 2 * r], 1), :], ybuf.at[0, pl.ds(r, 1), :], sem).start()
        pltpu.make_async_copy(ys_ref.at[pl.ds(pos_ref[0, 2 * r + 1], 1), :], ybuf.at[1, pl.ds(r, 1), :], sem).start()
        return carry
    lax.fori_loop(0, ROW_TILE, issue, 0, unroll=8)
    for k in range(2):
        pltpu.make_async_copy(ys_ref.at[pl.ds(0, ROW_TILE), :], ybuf.at[k], sem).wait()
    rt = rt_ref[...]
    f = rt[:, 2:3] * ybuf[0] + rt[:, 3:4] * ybuf[1]
    y = _ln(ALPHA * x_ref[...] + _gate_rows(f, g2_ref), lg_ref[...], lb_ref[...])
    i = pl.program_id(0)

    @pl.when(i < ntp)
    def _():
        op_ref[...] = y

    @pl.when(i >= ntp)
    def _():
        os_ref[...] = y


def _final_moe_call(layer, pos_tiles, x1, rt, mods, lg, lb, ys, *, n_prompt_tiles):
    t, d = x1.shape
    ntp = n_prompt_tiles
    tile = pl.BlockSpec((ROW_TILE, d), lambda i: (i, 0))
    vec = pl.BlockSpec((1, d), lambda i: (0, 0))
    return pl.pallas_call(
        functools.partial(_final_moe_kernel, ntp),
        grid=(t // ROW_TILE,),
        in_specs=[
            pl.BlockSpec((None, 1, 2 * ROW_TILE), lambda i: (i, 0, 0), memory_space=pltpu.SMEM),
            tile,
            pl.BlockSpec((ROW_TILE, LANES), lambda i: (i, 0)),
            _mod_spec(layer, G2, ROW_TILE, d),
            vec, vec,
            pl.BlockSpec(memory_space=pl.ANY),
        ],
        out_specs=[pl.BlockSpec((ROW_TILE, d), lambda i: (jnp.minimum(i, ntp - 1), 0)),
                   pl.BlockSpec((ROW_TILE, d), lambda i: (jnp.maximum(i - ntp, 0), 0))],
        out_shape=[jax.ShapeDtypeStruct((ntp * ROW_TILE, d), F32),
                   jax.ShapeDtypeStruct((t - ntp * ROW_TILE, d), F32)],
        scratch_shapes=[pltpu.VMEM((2, ROW_TILE, d), F32), pltpu.SemaphoreType.DMA(())],
        compiler_params=_cparams("arbitrary"),
        name="final_moe",
    )(pos_tiles, x1, rt, mods, lg, lb, ys)


def _rel_bias(table, n_q, n_k, banded):
    r = jnp.arange(n_q)[:, None]
    w = jnp.arange(n_k)[None, :]
    n_diag = n_q + n_k - 1
    k = jnp.arange(n_diag)
    diag = table[:, jnp.clip(LEFT_CTX + (n_q - 1) - k, -REL_CLIP, REL_CLIP) + REL_CLIP].astype(F32) * LOG2E
    padded = jnp.concatenate([diag, jnp.zeros((diag.shape[0], 1), F32)], axis=1)
    skew = jnp.tile(padded, (1, n_q))[:, :n_q * n_diag].reshape(-1, n_q, n_diag)
    bias = skew[:, :, n_q - 1:n_q - 1 + n_k]
    if banded:
        qa = r // CHUNK
        kc = w // CHUNK
        vis = (kc >= qa) & (kc <= qa + LEFT_CHUNKS)
        bias = jnp.where(vis[None], bias, NEG)
    return bias


def _gating_weights(w_s, b_s):
    n = MLP_CHUNK
    tril = jnp.tril(jnp.ones((n, n), bool))
    wm = jnp.where(tril[None], w_s, 0.0)
    h = CHUNK
    top = wm[:, :h, :h]
    z = jnp.zeros_like(top)
    wm_s = jnp.concatenate([jnp.concatenate([top, z], 2), jnp.concatenate([z, top], 2)], 1)
    both = jnp.stack([wm, wm_s])
    wcat = jnp.concatenate([both[:, 0::2], both[:, 1::2]], axis=-1)
    bias_p = jnp.repeat(jnp.transpose(b_s), GROUP_DIM, axis=1)
    bias_s = jnp.concatenate([bias_p[:h], bias_p[:h]], 0)
    return wcat.astype(BF16), jnp.stack([bias_p, bias_s]).astype(F32)


def _route_plan(rt, n_tiles, tile_rows):
    t = rt.shape[0]
    e_flat = rt[:, 0:2].astype(jnp.int32).reshape(-1)
    oh = (e_flat[:, None] == jnp.arange(N_EXPERTS)[None, :]).astype(jnp.int32)
    csum = jnp.cumsum(oh, axis=0)
    rank = jnp.sum(oh * (csum - 1), axis=1)
    counts = csum[-1]
    tiles_e = (counts + tile_rows - 1) // tile_rows
    tile_end = jnp.cumsum(tiles_e)
    tile_start = tile_end - tiles_e
    pos = (tile_start * tile_rows)[e_flat] + rank
    ti = jnp.arange(n_tiles)
    te = jnp.minimum(jnp.sum((ti[:, None] >= tile_end[None, :]).astype(jnp.int32), axis=1), N_EXPERTS - 1)
    tv = jnp.clip(counts[te] - (ti - tile_start[te]) * tile_rows, 0, tile_rows)
    tv = jnp.where(ti < tile_end[-1], tv, 0)
    last_e = te[jnp.maximum(tile_end[-1] - 1, 0)]
    te = jnp.where(ti < tile_end[-1], te, last_e)
    return pos.reshape(t // ROW_TILE, 1, 2 * ROW_TILE).astype(jnp.int32), te.astype(jnp.int32), tv.astype(jnp.int32)


def kernel(x_prompt, x_sample, cache_k, cache_v, c_prompt, c_sample, ln_in_g, ln_in_b, w_in, w_out, rel_bias_table, gmlp_ln_g, gmlp_ln_b, gmlp_w_s, gmlp_b_s, out_norm_a, out_norm_b, ada_w, ada_b, ln1_g, ln1_b, ln2_g, ln2_b, ffn_w_gate, ffn_w_up, ffn_w_down, moe_router, moe_w_gate, moe_w_up, moe_w_down):
    nb, seq, d = x_prompt.shape
    ns, dseq, _ = x_sample.shape
    depth = w_in.shape[0]
    tp = nb * seq
    ts = ns * dseq
    t = tp + ts
    assert depth == DEPTH and dseq == CHUNK and seq % ROW_TILE == 0 and ts % ROW_TILE == 0
    assert t % FFN_TILE == 0 and cache_k.shape[2] == LEFT_CTX
    ntp = tp // ROW_TILE

    n_cond = nb + ns
    c_all = jnp.concatenate([c_prompt, c_sample], axis=0)
    c_pad = jnp.pad(c_all, ((0, (-n_cond) % 8), (0, 0)))
    mods = _ada_call(c_pad, ada_w, ada_b)
    mods = jnp.transpose(mods[:, :n_cond].reshape(depth, n_cond, 6, d), (0, 2, 1, 3))
    cpb = seq // CHUNK
    mods_p = jnp.broadcast_to(mods[:, :, :nb, None, :], (depth, 6, nb, cpb, d)).reshape(depth, 6, nb * cpb, d)
    mods = jnp.concatenate([mods_p, mods[:, :, nb:]], axis=2)

    def row(v):
        return v.reshape(1, -1)

    w_in_bf = w_in.astype(BF16)
    w_out_bf = w_out.astype(BF16)
    cache_kt = jnp.transpose(cache_k, (0, 1, 3, 4, 2))
    cache_vt = jnp.transpose(cache_v, (0, 1, 3, 4, 2))
    k_tail, v_tail, gv_rows = [], [], []
    for l in range(depth):
        wcat, bs = _gating_weights(gmlp_w_s[l], gmlp_b_s[l])
        if l == 0:
            x_in = (x_prompt.reshape(tp, d), x_sample.reshape(ts, d), row(ln_in_g), row(ln_in_b))
        else:
            x_in = (x,)
        res = _inproj_call(l, x_in, mods, w_in_bf[l], row(gmlp_ln_g[l]), row(gmlp_ln_b[l]), wcat, bs,
                           row(out_norm_b[l]),
                           n_prompt_tiles=ntp, tiles_per_batch=seq // ROW_TILE, n_batch=nb, n_sample_rows=ts)
        if l == 0:
            x, q, k, v, kf, vf, gv, bn = res
        else:
            q, k, v, kf, vf, gv, bn = res
        k_tail.append(kf)
        v_tail.append(vf)
        gv_rows.append(gv)

        table = rel_bias_table[l]
        bias_p = _rel_bias(table, Q_TILE, 3 * Q_TILE, True)
        bias_s = _rel_bias(table, CHUNK, LEFT_CTX + CHUNK, False)
        na = row(out_norm_a[l])
        an_p = _attn_prompt_call(q, k, v, bias_p, na, n_batch=nb, q_tiles_per_batch=seq // Q_TILE)
        an_s = _attn_sample_call(l, q, k, v, cache_kt, cache_vt, bias_s[:, :, :LEFT_CTX], bias_s[:, :, LEFT_CTX:], na,
                                 first_chunk=tp // CHUNK)

        lg1, lb1, lg2, lb2 = row(ln1_g[l]), row(ln1_b[l]), row(ln2_g[l]), row(ln2_b[l])
        i = l // 2
        if l % 2 == 0:
            x1, hp = _outproj_call(l, False, x, an_p, an_s, bn, w_out_bf[l], mods, lg1, lb1)
            nt = t // FFN_TILE
            te = jnp.zeros((nt,), jnp.int32)
            tv = jnp.full((nt,), FFN_TILE, jnp.int32)
            x = _ffn_call(te, tv, hp, ffn_w_gate[i:i + 1], ffn_w_up[i:i + 1], ffn_w_down[i:i + 1], FFN_TILE,
                          closing=(l, x1, mods, lg2, lb2))
        else:
            r_pad = jnp.pad(moe_router[i], ((0, 0), (0, LANES - N_EXPERTS)))
            r_hi = r_pad.astype(BF16)
            r_lo = (r_pad - r_hi.astype(F32)).astype(BF16)
            r_cat = jnp.concatenate([r_hi, r_lo], axis=1)
            x1, hp, rt = _outproj_call(l, True, x, an_p, an_s, bn, w_out_bf[l], mods, lg1, lb1, r_cat)
            n_tiles = -(-2 * t // MOE_TILE) + N_EXPERTS
            pos, te, tv = _route_plan(rt, n_tiles, MOE_TILE)
            xs = _dispatch_call(pos, hp, n_tiles * MOE_TILE)
            ys = _ffn_call(te, tv, xs, moe_w_gate[i], moe_w_up[i], moe_w_down[i], MOE_TILE)
            y_p, y_s = _final_moe_call(l, pos, x1, rt, mods, lg2, lb2, ys, n_prompt_tiles=ntp)

    y_prompt = y_p.reshape(nb, seq, d)
    y_sample = y_s.reshape(ns, dseq, d)

    def tails(rows):
        kp = jnp.stack([r[:nb * ROW_TILE].reshape(nb, ROW_TILE, N_HEADS, HEAD_DIM) for r in rows])
        ksn = jnp.stack([r[nb * ROW_TILE:].reshape(ns, dseq, N_HEADS, HEAD_DIM) for r in rows])
        return kp, ksn

    k_prompt_new, k_sample_new = tails(k_tail)
    v_prompt_new, v_sample_new = tails(v_tail)
    gmlp_v_sample_new = jnp.stack([g.reshape(ns, dseq, D_B) for g in gv_rows])
    return (y_prompt, y_sample, k_prompt_new, v_prompt_new, k_sample_new, v_sample_new, gmlp_v_sample_new)
```

```python
import functools

import jax
import jax.numpy as jnp
from jax import lax
from jax.experimental import pallas as pl
from jax.experimental.pallas import tpu as pltpu

CHUNK = 64
LEFT_CHUNKS = 8
LEFT_CTX = LEFT_CHUNKS * CHUNK
N_HEADS = 8
HEAD_DIM = 64
D_A = N_HEADS * HEAD_DIM
N_GROUPS = 8
GROUP_DIM = 64
D_B = N_GROUPS * GROUP_DIM
MLP_CHUNK = 128
REL_CLIP = 128
N_EXPERTS = 8
DEPTH = 2
ALPHA = (2 * DEPTH) ** 0.25
LN_EPS = 1e-5
ATTN_SCALE = HEAD_DIM ** -0.5
LOG2E = 1.4426950408889634
NEG = -1e30
SH1, SC1, G1, SH2, SC2, G2 = range(6)

LANES = 128
ROW_TILE = 512
Q_TILE = 256
FFN_SPLIT = 1536
MOE_TILE = 2048
FFN_SUB = 512
FFN_COLS = 256
VMEM_LIMIT = 56 * 1024 * 1024

BF16 = jnp.bfloat16
F32 = jnp.float32


def _cparams(*sem):
    return pltpu.CompilerParams(dimension_semantics=sem, vmem_limit_bytes=VMEM_LIMIT)


def _ln(x, g, b):
    mu = jnp.mean(x, axis=-1, keepdims=True)
    xc = x - mu
    var = jnp.mean(xc * xc, axis=-1, keepdims=True)
    return xc * lax.rsqrt(var + LN_EPS) * g + b


def _rms(x, g):
    return x * lax.rsqrt(jnp.mean(x * x, axis=-1, keepdims=True) + LN_EPS) * g


def _modulate(x, sc_ref, sh_ref):
    parts = []
    for c in range(x.shape[0] // CHUNK):
        xc = x[c * CHUNK:(c + 1) * CHUNK]
        parts.append(xc * (1.0 + sc_ref[c:c + 1, :]) + sh_ref[c:c + 1, :])
    return jnp.concatenate(parts, axis=0)


def _gate_rows(x, g_ref):
    parts = []
    for c in range(x.shape[0] // CHUNK):
        parts.append(x[c * CHUNK:(c + 1) * CHUNK] * (1.0 + g_ref[c:c + 1, :]))
    return jnp.concatenate(parts, axis=0)


def _mod_spec(layer, comp, rows, d):
    return pl.BlockSpec((None, None, rows // CHUNK, d), lambda i, *_: (layer, comp, i, 0))


def _split_bf16(x):
    hi = x.astype(BF16)
    lo = (x - hi.astype(F32)).astype(BF16)
    return hi, lo


def _dot(a, b):
    return jnp.dot(a, b, preferred_element_type=F32)


def _ada_kernel(c_ref, w_ref, b_ref, o_ref):
    c = c_ref[...]
    s = c * jax.nn.sigmoid(c)
    s_hi, s_lo = _split_bf16(s)
    w_hi, w_lo = _split_bf16(w_ref[0])
    o_ref[0] = _dot(s_hi, w_hi) + _dot(s_lo, w_hi) + _dot(s_hi, w_lo) + b_ref[0]


def _ada_call(c_pad, ada_w, ada_b):
    depth, d, n = ada_w.shape
    rows = c_pad.shape[0]
    tn = n // 4
    return pl.pallas_call(
        _ada_kernel,
        grid=(depth, n // tn),
        in_specs=[
            pl.BlockSpec((rows, d), lambda l, j: (0, 0)),
            pl.BlockSpec((1, d, tn), lambda l, j: (l, 0, j)),
            pl.BlockSpec((1, 1, tn), lambda l, j: (l, 0, j)),
        ],
        out_specs=pl.BlockSpec((1, rows, tn), lambda l, j: (l, 0, j)),
        out_shape=jax.ShapeDtypeStruct((depth, rows, n), F32),
        compiler_params=_cparams("arbitrary", "arbitrary"),
        name="ada_mod",
    )(c_pad, ada_w, ada_b.reshape(depth, 1, n))


def _inproj_kernel(first, ntp, *refs):
    if first:
        xp_ref, xs_ref, lng_ref, lnb_ref = refs[:4]
        xn_ref = refs[-8]
        x = jnp.where(pl.program_id(0) >= ntp, xs_ref[...], xp_ref[...])
        x = _ln(x, lng_ref[...], lnb_ref[...])
        xn_ref[...] = x
        refs = refs[4:-8] + refs[-7:]
    else:
        x = refs[0][...]
        refs = refs[1:]
    (sc_ref, sh_ref, w_ref, gg_ref, gb_ref, wcat_ref, bs_ref, nb_ref,
     q_ref, k_ref, v_ref, kf_ref, vf_ref, gv_ref, bn_ref) = refs
    h = _modulate(x, sc_ref, sh_ref).astype(BF16)

    q = _dot(h, w_ref[:, 0:D_A])
    q_ref[...] = (q * (ATTN_SCALE * LOG2E)).astype(BF16)
    k = _dot(h, w_ref[:, D_A:2 * D_A])
    k_ref[...] = k.astype(BF16)
    kf_ref[...] = k
    v = _dot(h, w_ref[:, 2 * D_A:3 * D_A])
    v_ref[...] = v.astype(BF16)
    vf_ref[...] = v

    u = jax.nn.gelu(_dot(h, w_ref[:, 3 * D_A:3 * D_A + D_B]), approximate=True)
    vb = jax.nn.gelu(_dot(h, w_ref[:, 3 * D_A + D_B:3 * D_A + 2 * D_B]), approximate=True)
    vn = _ln(vb, gg_ref[...], gb_ref[...])
    gv_ref[...] = vn

    lane = lax.broadcasted_iota(jnp.int32, (MLP_CHUNK, LANES), 1)
    lo = lane < GROUP_DIM
    vnb = vn.astype(BF16)
    zero = jnp.zeros((MLP_CHUNK, LANES), BF16)
    n_chunks = ROW_TILE // MLP_CHUNK
    mixed = []
    for p in range(D_B // LANES):
        stacks = []
        for c in range(n_chunks):
            slab = vnb[c * MLP_CHUNK:(c + 1) * MLP_CHUNK, p * LANES:(p + 1) * LANES]
            stacks.append(jnp.concatenate([jnp.where(lo, slab, zero), jnp.where(lo, zero, slab)], axis=0))
        mixed.append(_dot(wcat_ref[0, p], jnp.concatenate(stacks, axis=1)))
    rows = []
    for c in range(n_chunks):
        r0 = c * MLP_CHUNK
        mix_c = jnp.concatenate([m[:, c * LANES:(c + 1) * LANES] for m in mixed], axis=1) + bs_ref[0]
        rows.append(u[r0:r0 + MLP_CHUNK, :] * mix_c)
    bo = jnp.concatenate(rows, axis=0)
    bn_ref[...] = _rms(bo, nb_ref[...]).astype(BF16)


def _inproj_call(layer, x_in, mods, w_bf, gg, gb, wcat, bs, nb, *, n_prompt_tiles, tiles_per_batch, n_batch,
                 n_sample_rows):
    first = len(x_in) == 4
    d = x_in[0].shape[1]
    t = mods.shape[2] * CHUNK
    nt = t // ROW_TILE
    ntp = n_prompt_tiles
    n_tail = n_batch + (nt - ntp)

    def tile(i):
        return (i, 0)

    def const2(i):
        return (0, 0)

    def selmap4(i):
        return (jnp.where(i >= ntp, 1, 0), 0, 0, 0)

    def selmap3(i):
        return (jnp.where(i >= ntp, 1, 0), 0, 0)

    def tailmap(i):
        return (jnp.where(i < ntp, i // tiles_per_batch, n_batch + i - ntp), 0)

    def gvmap(i):
        return (jnp.maximum(i - ntp, 0), 0)

    if first:
        x_specs = [
            pl.BlockSpec((ROW_TILE, d), lambda i: (jnp.minimum(i, ntp - 1), 0)),
            pl.BlockSpec((ROW_TILE, d), lambda i: (jnp.maximum(i - ntp, 0), 0)),
            pl.BlockSpec((1, d), const2),
            pl.BlockSpec((1, d), const2),
        ]
    else:
        x_specs = [pl.BlockSpec((ROW_TILE, d), tile)]
    in_specs = x_specs + [
        _mod_spec(layer, SC1, ROW_TILE, d),
        _mod_spec(layer, SH1, ROW_TILE, d),
        pl.BlockSpec(w_bf.shape, const2),
        pl.BlockSpec((1, D_B), const2),
        pl.BlockSpec((1, D_B), const2),
        pl.BlockSpec((1,) + wcat.shape[1:], selmap4),
        pl.BlockSpec((1, MLP_CHUNK, D_B), selmap3),
        pl.BlockSpec((1, D_B), const2),
    ]
    out_specs = [
        pl.BlockSpec((ROW_TILE, D_A), tile),
        pl.BlockSpec((ROW_TILE, D_A), tile),
        pl.BlockSpec((ROW_TILE, D_A), tile),
        pl.BlockSpec((ROW_TILE, D_A), tailmap),
        pl.BlockSpec((ROW_TILE, D_A), tailmap),
        pl.BlockSpec((ROW_TILE, D_B), gvmap),
        pl.BlockSpec((ROW_TILE, D_B), tile),
    ]
    out_shape = [
        jax.ShapeDtypeStruct((t, D_A), BF16),
        jax.ShapeDtypeStruct((t, D_A), BF16),
        jax.ShapeDtypeStruct((t, D_A), BF16),
        jax.ShapeDtypeStruct((n_tail * ROW_TILE, D_A), F32),
        jax.ShapeDtypeStruct((n_tail * ROW_TILE, D_A), F32),
        jax.ShapeDtypeStruct((n_sample_rows, D_B), F32),
        jax.ShapeDtypeStruct((t, D_B), BF16),
    ]
    if first:
        out_specs = [pl.BlockSpec((ROW_TILE, d), tile)] + out_specs
        out_shape = [jax.ShapeDtypeStruct((t, d), F32)] + out_shape
    return pl.pallas_call(
        functools.partial(_inproj_kernel, first, ntp),
        grid=(nt,),
        in_specs=in_specs,
        out_specs=out_specs,
        out_shape=out_shape,
        compiler_params=_cparams("arbitrary"),
        name="in_proj_first" if first else "in_proj",
    )(*x_in, mods, mods, w_bf, gg, gb, wcat, bs, nb)


def _row_max(pieces):
    folded = None
    m = None
    for s in pieces:
        if s.shape[1] % LANES == 0:
            for c0 in range(0, s.shape[1], LANES):
                piece = s[:, c0:c0 + LANES]
                folded = piece if folded is None else jnp.maximum(folded, piece)
        else:
            mj = s.max(axis=1, keepdims=True)
            m = mj if m is None else jnp.maximum(m, mj)
    if folded is not None:
        mj = folded.max(axis=1, keepdims=True)
        m = mj if m is None else jnp.maximum(m, mj)
    return m


def _attend(q_of, r, kv_slabs, bias_of, part_rows, part_ok, na):
    nt_dims = (((1,), (1,)), ((), ()))
    lo = lax.broadcasted_iota(jnp.int32, (1, LANES), 1) < HEAD_DIM
    lo_t = lax.broadcasted_iota(jnp.int32, (LANES, 1), 0) < HEAD_DIM
    bounds = sorted({0, r} | {b for rows in part_rows for b in rows})
    segs = list(zip(bounds[:-1], bounds[1:]))

    def covered(seg, rows):
        return rows[0] <= seg[0] and seg[1] <= rows[1]

    outs = []
    for p in range(D_A // LANES):
        qp = q_of(p)
        slabs = kv_slabs(p)
        accs = []
        for half in range(2):
            h = 2 * p + half
            keep = lo if half == 0 else jnp.logical_not(lo)
            keep_t = lo_t if half == 0 else jnp.logical_not(lo_t)
            qh = jnp.where(keep, qp, jnp.zeros_like(qp))
            ss = []
            for j, (kp, _, transposed) in enumerate(slabs):
                qj = qh[part_rows[j][0]:part_rows[j][1]]
                if transposed:
                    s = _dot(qj, kp)
                else:
                    s = lax.dot_general(qj, kp, nt_dims, preferred_element_type=F32)
                s = s + bias_of(h, j)
                if part_ok is not None and part_ok[j] is not None:
                    s = jnp.where(part_ok[j], s, NEG)
                ss.append(s)
            m_segs = [_row_max([s[a - rows[0]:b - rows[0]] for s, rows in zip(ss, part_rows) if covered((a, b), rows)])
                      for a, b in segs]
            acc_segs = [jnp.zeros((b - a, LANES), F32) for a, b in segs]
            for s, (_, vp, transposed), rows in zip(ss, slabs, part_rows):
                m = jnp.concatenate([m_segs[i] for i, seg in enumerate(segs) if covered(seg, rows)], axis=0)
                e = jnp.exp2(s - m).astype(BF16)
                if transposed:
                    vh = jnp.where(keep_t, vp, jnp.ones_like(vp))
                    c = lax.dot_general(e, vh, nt_dims, preferred_element_type=F32)
                else:
                    c = _dot(e, jnp.where(keep, vp, jnp.ones_like(vp)))
                for i, (a, b) in enumerate(segs):
                    if covered((a, b), rows):
                        acc_segs[i] = acc_segs[i] + c[a - rows[0]:b - rows[0]]
            accs.append(jnp.concatenate(acc_segs, axis=0))
        num = jnp.where(lo, accs[0], accs[1])
        den = pltpu.roll(jnp.where(lo, accs[1], accs[0]), HEAD_DIM, 1)
        outs.append(num / den)
    a = jnp.concatenate(outs, axis=1)
    return _rms(a, na)


def _attn_prompt_kernel(q_ref, k0_ref, k1_ref, k2_ref, k3_ref, v0_ref, v1_ref, v2_ref, v3_ref, bias_ref, na_ref,
                        o_ref):
    j = pl.program_id(1)
    k_refs = (k0_ref, k1_ref, k2_ref, k3_ref)
    v_refs = (v0_ref, v1_ref, v2_ref, v3_ref)
    part_rows = [(0, Q_TILE), (0, 2 * Q_TILE), (0, 2 * Q_TILE), (Q_TILE, 2 * Q_TILE)]

    def q_of(p):
        return q_ref[:, p * LANES:(p + 1) * LANES]

    def kv_slabs(p):
        sl = slice(p * LANES, (p + 1) * LANES)
        return [(kr[:, sl], vr[:, sl], False) for kr, vr in zip(k_refs, v_refs)]

    def bias_of(h, b):
        first = bias_ref[h, :, b * Q_TILE:(b + 1) * Q_TILE] if b < 3 else None
        second = bias_ref[h, :, (b - 1) * Q_TILE:b * Q_TILE] if b > 0 else None
        if first is None:
            return second
        if second is None:
            return first
        return jnp.concatenate([first, second], axis=0)

    def run(part_ok):
        o_ref[...] = _attend(q_of, 2 * Q_TILE, kv_slabs, bias_of, part_rows, part_ok, na_ref[...]).astype(BF16)

    pl.when(j >= 1)(functools.partial(run, None))
    pl.when(j < 1)(functools.partial(run, [j >= 1, j >= 1, None, None]))


def _attn_prompt_call(q, k, v, bias, na, *, n_batch, q_tiles_per_batch):
    nq = q_tiles_per_batch // 2

    def kmap(part):
        return lambda b, j: (2 * b * nq + jnp.maximum(2 * j - 2 + part, 0), 0)

    blk = pl.BlockSpec((2 * Q_TILE, D_A), lambda b, j: (b * nq + j, 0))
    kv_specs = [pl.BlockSpec((Q_TILE, D_A), kmap(part)) for part in range(4)]
    return pl.pallas_call(
        _attn_prompt_kernel,
        grid=(n_batch, nq),
        in_specs=[blk] + kv_specs + kv_specs + [
            pl.BlockSpec(bias.shape, lambda b, j: (0, 0, 0)),
            pl.BlockSpec((1, D_A), lambda b, j: (0, 0)),
        ],
        out_specs=blk,
        out_shape=jax.ShapeDtypeStruct((n_batch * nq * 2 * Q_TILE, D_A), BF16),
        compiler_params=_cparams("arbitrary", "arbitrary"),
        name="attn_prompt",
    )(q, k, k, k, k, v, v, v, v, bias, na)


def _attn_sample_kernel(q_ref, kn_ref, vn_ref, ck_ref, cv_ref, bias_c_ref, bias_n_ref, na_ref, o_ref):
    def kv_slabs(p):
        sl = slice(p * LANES, (p + 1) * LANES)

        def cached(c_ref):
            return c_ref[2 * p:2 * p + 2].reshape(LANES, c_ref.shape[2]).astype(BF16)
        return [(cached(ck_ref), cached(cv_ref), True), (kn_ref[:, sl], vn_ref[:, sl], False)]

    def bias_of(h, b):
        return bias_c_ref[h] if b == 0 else bias_n_ref[h]

    def q_of(p):
        return q_ref[:, p * LANES:(p + 1) * LANES]

    r = q_ref.shape[0]
    o_ref[...] = _attend(q_of, r, kv_slabs, bias_of, [(0, r), (0, r)], None, na_ref[...]).astype(BF16)


def _attn_sample_call(layer, q, k, v, cache_kt, cache_vt, bias_c, bias_n, na, *, first_chunk):
    _, nb, nh, hd, win = cache_kt.shape
    blk = pl.BlockSpec((CHUNK, D_A), lambda b: (first_chunk + b, 0))
    cache = pl.BlockSpec((None, None, nh, hd, win), lambda b: (layer, b, 0, 0, 0))
    return pl.pallas_call(
        _attn_sample_kernel,
        grid=(nb,),
        in_specs=[blk, blk, blk, cache, cache,
                  pl.BlockSpec(bias_c.shape, lambda b: (0, 0, 0)),
                  pl.BlockSpec(bias_n.shape, lambda b: (0, 0, 0)),
                  pl.BlockSpec((1, D_A), lambda b: (0, 0))],
        out_specs=pl.BlockSpec((CHUNK, D_A), lambda b: (b, 0)),
        out_shape=jax.ShapeDtypeStruct((nb * CHUNK, D_A), BF16),
        compiler_params=_cparams("arbitrary"),
        name="attn_sample",
    )(q, k, v, cache_kt, cache_vt, bias_c, bias_n, na)


def _outproj_kernel(route, ntp, x_ref, anp_ref, ans_ref, bn_ref, w_ref, g1_ref, sc_ref, sh_ref, lg_ref, lb_ref, *rest):
    if route:
        rc_ref, x1_ref, hp_ref, rt_ref = rest
    else:
        x1_ref, hp_ref = rest
    an = jnp.where(pl.program_id(0) >= ntp, ans_ref[...], anp_ref[...])
    mix = _dot(an, w_ref[0:D_A, :]) + _dot(bn_ref[...], w_ref[D_A:D_A + D_B, :])
    x1 = _ln(ALPHA * x_ref[...] + _gate_rows(mix, g1_ref), lg_ref[...], lb_ref[...])
    x1_ref[...] = x1
    h2 = _modulate(x1, sc_ref, sh_ref)
    hp_ref[...] = h2.astype(hp_ref.dtype)
    if route:
        h_hi, h_lo = _split_bf16(h2)
        both = _dot(h_hi, rc_ref[...])
        logits = both[:, :LANES] + both[:, LANES:] + _dot(h_lo, rc_ref[:, :LANES])
        lane = lax.broadcasted_iota(jnp.int32, logits.shape, 1)
        logits = jnp.where(lane < N_EXPERTS, logits, -jnp.inf)
        m1 = logits.max(axis=1, keepdims=True)
        i1 = jnp.where(logits == m1, lane, LANES).min(axis=1, keepdims=True)
        rest_l = jnp.where(lane == i1, -jnp.inf, logits)
        m2 = rest_l.max(axis=1, keepdims=True)
        i2 = jnp.where(rest_l == m2, lane, LANES).min(axis=1, keepdims=True)
        e2 = jnp.exp(m2 - m1)
        w1 = 1.0 / (1.0 + e2)
        w2 = e2 / (1.0 + e2)
        rt = jnp.where(lane == 0, i1.astype(F32),
                       jnp.where(lane == 1, i2.astype(F32),
                                 jnp.where(lane == 2, w1, jnp.where(lane == 3, w2, 0.0))))
        rt_ref[...] = rt


def _outproj_call(layer, route, x, an_p, an_s, bn, w_bf, mods, lg, lb, r_cat=None):
    t, d = x.shape
    nt = t // ROW_TILE
    ntp = an_p.shape[0] // ROW_TILE

    def tile(i):
        return (i, 0)

    def const2(i):
        return (0, 0)

    in_specs = [
        pl.BlockSpec((ROW_TILE, d), tile),
        pl.BlockSpec((ROW_TILE, D_A), lambda i: (jnp.minimum(i, ntp - 1), 0)),
        pl.BlockSpec((ROW_TILE, D_A), lambda i: (jnp.maximum(i - ntp, 0), 0)),
        pl.BlockSpec((ROW_TILE, D_B), tile),
        pl.BlockSpec(w_bf.shape, const2),
        _mod_spec(layer, G1, ROW_TILE, d),
        _mod_spec(layer, SC2, ROW_TILE, d),
        _mod_spec(layer, SH2, ROW_TILE, d),
        pl.BlockSpec((1, d), const2),
        pl.BlockSpec((1, d), const2),
    ]
    out_specs = [pl.BlockSpec((ROW_TILE, d), tile), pl.BlockSpec((ROW_TILE, d), tile)]
    out_shape = [jax.ShapeDtypeStruct((t, d), F32), jax.ShapeDtypeStruct((t, d), F32 if route else BF16)]
    args = [x, an_p, an_s, bn, w_bf, mods, mods, mods, lg, lb]
    if route:
        in_specs.append(pl.BlockSpec(r_cat.shape, const2))
        out_specs.append(pl.BlockSpec((ROW_TILE, LANES), tile))
        out_shape.append(jax.ShapeDtypeStruct((t, LANES), F32))
        args.append(r_cat)
    return pl.pallas_call(
        functools.partial(_outproj_kernel, route, ntp),
        grid=(nt,),
        in_specs=in_specs,
        out_specs=out_specs,
        out_shape=out_shape,
        compiler_params=_cparams("arbitrary"),
        name="out_proj_route" if route else "out_proj",
    )(*args)


def _swiglu_chains(n_sub, xb_ref, wg_ref, wu_ref, wd_ref, o_ref):
    wg = wg_ref[0].astype(BF16)
    wu = wu_ref[0].astype(BF16)
    wd = wd_ref[0].astype(BF16)
    for s in range(n_sub):
        rows = slice(s * FFN_SUB, (s + 1) * FFN_SUB)
        xs = xb_ref[rows, :]
        g = _dot(xs, wg)
        u = _dot(xs, wu)
        a = (g * jax.nn.sigmoid(g) * u).astype(BF16)
        o_ref[rows, :] += _dot(a, wd)


def _ffn_kernel(te_ref, tv_ref, x_ref, wg_ref, wu_ref, wd_ref, o_ref, xb_ref):
    del te_ref
    i = pl.program_id(0)
    j = pl.program_id(1)
    valid = tv_ref[i]
    nsub = (valid + FFN_SUB - 1) // FFN_SUB

    @pl.when(j == 0)
    def _():
        xb_ref[...] = x_ref[...].astype(BF16)
        o_ref[...] = jnp.zeros(o_ref.shape, F32)

    for n_sub in range(1, x_ref.shape[0] // FFN_SUB + 1):
        pl.when(nsub == n_sub)(functools.partial(_swiglu_chains, n_sub, xb_ref, wg_ref, wu_ref, wd_ref, o_ref))


def _ffn_call(tile_expert, tile_valid, xs, wg, wu, wd, tile_rows):
    p, d = xs.shape
    f = wg.shape[2]
    nt = p // tile_rows
    nc = f // FFN_COLS

    def active_col(i, j, tv):
        return jnp.where(tv[i] > 0, j, nc - 1)

    in_specs = [
        pl.BlockSpec((tile_rows, d), lambda i, j, te, tv: (i, 0)),
        pl.BlockSpec((1, d, FFN_COLS), lambda i, j, te, tv: (te[i], 0, active_col(i, j, tv))),
        pl.BlockSpec((1, d, FFN_COLS), lambda i, j, te, tv: (te[i], 0, active_col(i, j, tv))),
        pl.BlockSpec((1, FFN_COLS, d), lambda i, j, te, tv: (te[i], active_col(i, j, tv), 0)),
    ]
    return pl.pallas_call(
        _ffn_kernel,
        grid_spec=pltpu.PrefetchScalarGridSpec(
            num_scalar_prefetch=2,
            grid=(nt, nc),
            in_specs=in_specs,
            out_specs=pl.BlockSpec((tile_rows, d), lambda i, j, te, tv: (i, 0)),
            scratch_shapes=[pltpu.VMEM((tile_rows, d), BF16)],
        ),
        out_shape=jax.ShapeDtypeStruct((p, d), F32),
        compiler_params=_cparams("arbitrary", "arbitrary"),
        name="swiglu_grouped",
    )(tile_expert, tile_valid, xs, wg, wu, wd)


def _dense_ffn_kernel(x_ref, wg_ref, wu_ref, wd_ref, x1_ref, g2_ref, lg_ref, lb_ref, o_ref):
    x = x_ref[...]
    f = wg_ref.shape[1]
    acc = None
    for c0 in range(0, f, FFN_SPLIT):
        c1 = min(c0 + FFN_SPLIT, f)
        g = _dot(x, wg_ref[:, c0:c1])
        u = _dot(x, wu_ref[:, c0:c1])
        a = (g * jax.nn.sigmoid(g) * u).astype(BF16)
        part = _dot(a, wd_ref[c0:c1, :])
        acc = part if acc is None else acc + part
    o_ref[...] = _ln(ALPHA * x1_ref[...] + _gate_rows(acc, g2_ref), lg_ref[...], lb_ref[...])


def _dense_ffn_call(layer, hp, wg_bf, wu_bf, wd_bf, x1, mods, lg, lb):
    t, d = x1.shape
    f = wg_bf.shape[1]
    tile = pl.BlockSpec((ROW_TILE, d), lambda i: (i, 0))
    vec = pl.BlockSpec((1, d), lambda i: (0, 0))

    def resident(shape):
        return pl.BlockSpec(shape, lambda i: (0, 0), pipeline_mode=pl.Buffered(1))

    return pl.pallas_call(
        _dense_ffn_kernel,
        grid=(t // ROW_TILE,),
        in_specs=[tile, resident((d, f)), resident((d, f)), resident((f, d)), tile,
                  _mod_spec(layer, G2, ROW_TILE, d), vec, vec],
        out_specs=tile,
        out_shape=jax.ShapeDtypeStruct((t, d), F32),
        compiler_params=_cparams("arbitrary"),
        name="swiglu_dense",
    )(hp, wg_bf, wu_bf, wd_bf, x1, mods, lg, lb)


def _dispatch_kernel(pos_ref, src_ref, init_ref, dst_ref, sem):
    del init_ref

    def issue(r, carry):
        row = src_ref.at[pl.ds(r, 1), :]
        pltpu.make_async_copy(row, dst_ref.at[pl.ds(pos_ref[0, 2 * r], 1), :], sem).start()
        pltpu.make_async_copy(row, dst_ref.at[pl.ds(pos_ref[0, 2 * r + 1], 1), :], sem).start()
        return carry
    lax.fori_loop(0, ROW_TILE, issue, 0, unroll=8)
    for _ in range(2):
        pltpu.make_async_copy(src_ref, dst_ref.at[pl.ds(0, ROW_TILE), :], sem).wait()


def _dispatch_call(pos_tiles, hp, n_rows_sorted):
    t, d = hp.shape
    nt = t // ROW_TILE
    return pl.pallas_call(
        _dispatch_kernel,
        grid=(nt,),
        in_specs=[
            pl.BlockSpec((None, 1, 2 * ROW_TILE), lambda i: (i, 0, 0), memory_space=pltpu.SMEM),
            pl.BlockSpec((ROW_TILE, d), lambda i: (i, 0)),
            pl.BlockSpec(memory_space=pl.ANY),
        ],
        out_specs=pl.BlockSpec(memory_space=pl.ANY),
        out_shape=jax.ShapeDtypeStruct((n_rows_sorted, d), hp.dtype),
        scratch_shapes=[pltpu.SemaphoreType.DMA(())],
        input_output_aliases={2: 0},
        compiler_params=_cparams("arbitrary"),
        name="dispatch_rows",
    )(pos_tiles, hp, jnp.zeros((n_rows_sorted, d), hp.dtype))


def _final_moe_kernel(ntp, pos_ref, x_ref, rt_ref, g2_ref, lg_ref, lb_ref, ys_ref, op_ref, os_ref, ybuf, sem):
    def issue(r, carry):
        pltpu.make_async_copy(ys_ref.at[pl.ds(pos_ref[0, 2 * r], 1), :], ybuf.at[0, pl.ds(r, 1), :], sem).start()
        pltpu.make_async_copy(ys_ref.at[pl.ds(pos_ref[0, 2 * r + 1], 1), :], ybuf.at[1, pl.ds(r, 1), :], sem).start()
        return carry
    lax.fori_loop(0, ROW_TILE, issue, 0, unroll=8)
    for k in range(2):
        pltpu.make_async_copy(ys_ref.at[pl.ds(0, ROW_TILE), :], ybuf.at[k], sem).wait()
    rt = rt_ref[...]
    f = rt[:, 2:3] * ybuf[0] + rt[:, 3:4] * ybuf[1]
    y = _ln(ALPHA * x_ref[...] + _gate_rows(f, g2_ref), lg_ref[...], lb_ref[...])
    i = pl.program_id(0)

    @pl.when(i < ntp)
    def _():
        op_ref[...] = y

    @pl.when(i >= ntp)
    def _():
        os_ref[...] = y


def _final_moe_call(layer, pos_tiles, x1, rt, mods, lg, lb, ys, *, n_prompt_tiles):
    t, d = x1.shape
    ntp = n_prompt_tiles
    tile = pl.BlockSpec((ROW_TILE, d), lambda i: (i, 0))
    vec = pl.BlockSpec((1, d), lambda i: (0, 0))
    return pl.pallas_call(
        functools.partial(_final_moe_kernel, ntp),
        grid=(t // ROW_TILE,),
        in_specs=[
            pl.BlockSpec((None, 1, 2 * ROW_TILE), lambda i: (i, 0, 0), memory_space=pltpu.SMEM),
            tile,
            pl.BlockSpec((ROW_TILE, LANES), lambda i: (i, 0)),
            _mod_spec(layer, G2, ROW_TILE, d),
            vec, vec,
            pl.BlockSpec(memory_space=pl.ANY),
        ],
        out_specs=[pl.BlockSpec((ROW_TILE, d), lambda i: (jnp.minimum(i, ntp - 1), 0)),
                   pl.BlockSpec((ROW_TILE, d), lambda i: (jnp.maximum(i - ntp, 0), 0))],
        out_shape=[jax.ShapeDtypeStruct((ntp * ROW_TILE, d), F32),
                   jax.ShapeDtypeStruct((t - ntp * ROW_TILE, d), F32)],
        scratch_shapes=[pltpu.VMEM((2, ROW_TILE, d), F32), pltpu.SemaphoreType.DMA(())],
        compiler_params=_cparams("arbitrary"),
        name="final_moe",
    )(pos_tiles, x1, rt, mods, lg, lb, ys)


def _rel_bias(table, n_q, n_k, banded):
    r = jnp.arange(n_q)[:, None]
    w = jnp.arange(n_k)[None, :]
    n_diag = n_q + n_k - 1
    k = jnp.arange(n_diag)
    diag = table[:, jnp.clip(LEFT_CTX + (n_q - 1) - k, -REL_CLIP, REL_CLIP) + REL_CLIP].astype(F32) * LOG2E
    padded = jnp.concatenate([diag, jnp.zeros((diag.shape[0], 1), F32)], axis=1)
    skew = jnp.tile(padded, (1, n_q))[:, :n_q * n_diag].reshape(-1, n_q, n_diag)
    bias = skew[:, :, n_q - 1:n_q - 1 + n_k]
    if banded:
        qa = r // CHUNK
        kc = w // CHUNK
        vis = (kc >= qa) & (kc <= qa + LEFT_CHUNKS)
        bias = jnp.where(vis[None], bias, NEG)
    return bias


def _gating_weights(w_s, b_s):
    n = MLP_CHUNK
    tril = jnp.tril(jnp.ones((n, n), bool))
    wm = jnp.where(tril[None], w_s, 0.0)
    h = CHUNK
    top = wm[:, :h, :h]
    z = jnp.zeros_like(top)
    wm_s = jnp.concatenate([jnp.concatenate([top, z], 2), jnp.concatenate([z, top], 2)], 1)
    both = jnp.stack([wm, wm_s])
    wcat = jnp.concatenate([both[:, 0::2], both[:, 1::2]], axis=-1)
    bias_p = jnp.repeat(jnp.transpose(b_s), GROUP_DIM, axis=1)
    bias_s = jnp.concatenate([bias_p[:h], bias_p[:h]], 0)
    return wcat.astype(BF16), jnp.stack([bias_p, bias_s]).astype(F32)


def _route_plan(rt, n_tiles, tile_rows):
    t = rt.shape[0]
    e_flat = rt[:, 0:2].astype(jnp.int32).reshape(-1)
    oh = (e_flat[:, None] == jnp.arange(N_EXPERTS)[None, :]).astype(jnp.int32)
    csum = jnp.cumsum(oh, axis=0)
    rank = jnp.sum(oh * (csum - 1), axis=1)
    counts = csum[-1]
    tiles_e = (counts + tile_rows - 1) // tile_rows
    tile_end = jnp.cumsum(tiles_e)
    tile_start = tile_end - tiles_e
    pos = (tile_start * tile_rows)[e_flat] + rank
    ti = jnp.arange(n_tiles)
    te = jnp.minimum(jnp.sum((ti[:, None] >= tile_end[None, :]).astype(jnp.int32), axis=1), N_EXPERTS - 1)
    tv = jnp.clip(counts[te] - (ti - tile_start[te]) * tile_rows, 0, tile_rows)
    tv = jnp.where(ti < tile_end[-1], tv, 0)
    last_e = te[jnp.maximum(tile_end[-1] - 1, 0)]
    te = jnp.where(ti < tile_end[-1], te, last_e)
    return pos.reshape(t // ROW_TILE, 1, 2 * ROW_TILE).astype(jnp.int32), te.astype(jnp.int32), tv.astype(jnp.int32)


def kernel(x_prompt, x_sample, cache_k, cache_v, c_prompt, c_sample, ln_in_g, ln_in_b, w_in, w_out, rel_bias_table, gmlp_ln_g, gmlp_ln_b, gmlp_w_s, gmlp_b_s, out_norm_a, out_norm_b, ada_w, ada_b, ln1_g, ln1_b, ln2_g, ln2_b, ffn_w_gate, ffn_w_up, ffn_w_down, moe_router, moe_w_gate, moe_w_up, moe_w_down):
    nb, seq, d = x_prompt.shape
    ns, dseq, _ = x_sample.shape
    depth = w_in.shape[0]
    tp = nb * seq
    ts = ns * dseq
    t = tp + ts
    assert depth == DEPTH and dseq == CHUNK and seq % ROW_TILE == 0 and ts % ROW_TILE == 0
    assert seq % (2 * Q_TILE) == 0 and cache_k.shape[2] == LEFT_CTX
    ntp = tp // ROW_TILE

    n_cond = nb + ns
    c_all = jnp.concatenate([c_prompt, c_sample], axis=0)
    c_pad = jnp.pad(c_all, ((0, (-n_cond) % 8), (0, 0)))
    mods = _ada_call(c_pad, ada_w, ada_b)
    mods = jnp.transpose(mods[:, :n_cond].reshape(depth, n_cond, 6, d), (0, 2, 1, 3))
    cpb = seq // CHUNK
    mods_p = jnp.broadcast_to(mods[:, :, :nb, None, :], (depth, 6, nb, cpb, d)).reshape(depth, 6, nb * cpb, d)
    mods = jnp.concatenate([mods_p, mods[:, :, nb:]], axis=2)

    def row(v):
        return v.reshape(1, -1)

    w_in_bf = w_in.astype(BF16)
    w_out_bf = w_out.astype(BF16)
    cache_kt = jnp.transpose(cache_k, (0, 1, 3, 4, 2))
    cache_vt = jnp.transpose(cache_v, (0, 1, 3, 4, 2))
    k_tail, v_tail, gv_rows = [], [], []
    for l in range(depth):
        wcat, bs = _gating_weights(gmlp_w_s[l], gmlp_b_s[l])
        if l == 0:
            x_in = (x_prompt.reshape(tp, d), x_sample.reshape(ts, d), row(ln_in_g), row(ln_in_b))
        else:
            x_in = (x,)
        res = _inproj_call(l, x_in, mods, w_in_bf[l], row(gmlp_ln_g[l]), row(gmlp_ln_b[l]), wcat, bs,
                           row(out_norm_b[l]),
                           n_prompt_tiles=ntp, tiles_per_batch=seq // ROW_TILE, n_batch=nb, n_sample_rows=ts)
        if l == 0:
            x, q, k, v, kf, vf, gv, bn = res
        else:
            q, k, v, kf, vf, gv, bn = res
        k_tail.append(kf)
        v_tail.append(vf)
        gv_rows.append(gv)

        table = rel_bias_table[l]
        bias_p = _rel_bias(table, Q_TILE, 3 * Q_TILE, True)
        bias_s = _rel_bias(table, CHUNK, LEFT_CTX + CHUNK, False)
        na = row(out_norm_a[l])
        an_p = _attn_prompt_call(q, k, v, bias_p, na, n_batch=nb, q_tiles_per_batch=seq // Q_TILE)
        an_s = _attn_sample_call(l, q, k, v, cache_kt, cache_vt, bias_s[:, :, :LEFT_CTX], bias_s[:, :, LEFT_CTX:], na,
                                 first_chunk=tp // CHUNK)

        lg1, lb1, lg2, lb2 = row(ln1_g[l]), row(ln1_b[l]), row(ln2_g[l]), row(ln2_b[l])
        i = l // 2
        if l % 2 == 0:
            x1, hp = _outproj_call(l, False, x, an_p, an_s, bn, w_out_bf[l], mods, lg1, lb1)
            x = _dense_ffn_call(l, hp, ffn_w_gate[i].astype(BF16), ffn_w_up[i].astype(BF16),
                                ffn_w_down[i].astype(BF16), x1, mods, lg2, lb2)
        else:
            r_pad = jnp.pad(moe_router[i], ((0, 0), (0, LANES - N_EXPERTS)))
            r_hi = r_pad.astype(BF16)
            r_lo = (r_pad - r_hi.astype(F32)).astype(BF16)
            r_cat = jnp.concatenate([r_hi, r_lo], axis=1)
            x1, hp, rt = _outproj_call(l, True, x, an_p, an_s, bn, w_out_bf[l], mods, lg1, lb1, r_cat)
            n_tiles = -(-2 * t // MOE_TILE) + N_EXPERTS
            pos, te, tv = _route_plan(rt, n_tiles, MOE_TILE)
            xs = _dispatch_call(pos, hp, n_tiles * MOE_TILE)
            ys = _ffn_call(te, tv, xs, moe_w_gate[i], moe_w_up[i], moe_w_down[i], MOE_TILE)
            y_p, y_s = _final_moe_call(l, pos, x1, rt, mods, lg2, lb2, ys, n_prompt_tiles=ntp)

    y_prompt = y_p.reshape(nb, seq, d)
    y_sample = y_s.reshape(ns, dseq, d)

    def tails(rows):
        kp = jnp.stack([r[:nb * ROW_TILE].reshape(nb, ROW_TILE, N_HEADS, HEAD_DIM) for r in rows])
        ksn = jnp.stack([r[nb * ROW_TILE:].reshape(ns, dseq, N_HEADS, HEAD_DIM) for r in rows])
        return kp, ksn

    k_prompt_new, k_sample_new = tails(k_tail)
    v_prompt_new, v_sample_new = tails(v_tail)
    gmlp_v_sample_new = jnp.stack([g.reshape(ns, dseq, D_B) for g in gv_rows])
    return (y_prompt, y_sample, k_prompt_new, v_prompt_new, k_sample_new, v_sample_new, gmlp_v_sample_new)
```

```python
import functools

import jax
import jax.numpy as jnp
from jax import lax
from jax.experimental import pallas as pl
from jax.experimental.pallas import tpu as pltpu

CHUNK = 64
LEFT_CHUNKS = 8
LEFT_CTX = LEFT_CHUNKS * CHUNK
N_HEADS = 8
HEAD_DIM = 64
D_A = N_HEADS * HEAD_DIM
N_GROUPS = 8
GROUP_DIM = 64
D_B = N_GROUPS * GROUP_DIM
MLP_CHUNK = 128
REL_CLIP = 128
N_EXPERTS = 8
DEPTH = 2
ALPHA = (2 * DEPTH) ** 0.25
LN_EPS = 1e-5
ATTN_SCALE = HEAD_DIM ** -0.5
LOG2E = 1.4426950408889634
NEG = -1e30
SH1, SC1, G1, SH2, SC2, G2 = range(6)

LANES = 128
ROW_TILE = 512
Q_TILE = 256
Q_GROUP = 4
FFN_SPLIT = 1536
MOE_TILE = 2048
FFN_SUB = 512
FFN_COLS = 256
VMEM_LIMIT = 56 * 1024 * 1024

BF16 = jnp.bfloat16
F32 = jnp.float32


def _cparams(*sem):
    return pltpu.CompilerParams(dimension_semantics=sem, vmem_limit_bytes=VMEM_LIMIT)


def _ln(x, g, b):
    mu = jnp.mean(x, axis=-1, keepdims=True)
    xc = x - mu
    var = jnp.mean(xc * xc, axis=-1, keepdims=True)
    return xc * lax.rsqrt(var + LN_EPS) * g + b


def _rms(x, g):
    return x * lax.rsqrt(jnp.mean(x * x, axis=-1, keepdims=True) + LN_EPS) * g


def _modulate(x, sc_ref, sh_ref):
    parts = []
    for c in range(x.shape[0] // CHUNK):
        xc = x[c * CHUNK:(c + 1) * CHUNK]
        parts.append(xc * (1.0 + sc_ref[c:c + 1, :]) + sh_ref[c:c + 1, :])
    return jnp.concatenate(parts, axis=0)


def _gate_rows(x, g_ref):
    parts = []
    for c in range(x.shape[0] // CHUNK):
        parts.append(x[c * CHUNK:(c + 1) * CHUNK] * (1.0 + g_ref[c:c + 1, :]))
    return jnp.concatenate(parts, axis=0)


def _mod_spec(layer, comp, rows, d):
    return pl.BlockSpec((None, None, rows // CHUNK, d), lambda i, *_: (layer, comp, i, 0))


def _split_bf16(x):
    hi = x.astype(BF16)
    lo = (x - hi.astype(F32)).astype(BF16)
    return hi, lo


def _dot(a, b):
    return jnp.dot(a, b, preferred_element_type=F32)


def _ada_kernel(c_ref, w_ref, b_ref, o_ref):
    c = c_ref[...]
    s = c * jax.nn.sigmoid(c)
    s_hi, s_lo = _split_bf16(s)
    w_hi, w_lo = _split_bf16(w_ref[0])
    o_ref[0] = _dot(s_hi, w_hi) + _dot(s_lo, w_hi) + _dot(s_hi, w_lo) + b_ref[0]


def _ada_call(c_pad, ada_w, ada_b):
    depth, d, n = ada_w.shape
    rows = c_pad.shape[0]
    tn = n // 4
    return pl.pallas_call(
        _ada_kernel,
        grid=(depth, n // tn),
        in_specs=[
            pl.BlockSpec((rows, d), lambda l, j: (0, 0)),
            pl.BlockSpec((1, d, tn), lambda l, j: (l, 0, j)),
            pl.BlockSpec((1, 1, tn), lambda l, j: (l, 0, j)),
        ],
        out_specs=pl.BlockSpec((1, rows, tn), lambda l, j: (l, 0, j)),
        out_shape=jax.ShapeDtypeStruct((depth, rows, n), F32),
        compiler_params=_cparams("arbitrary", "arbitrary"),
        name="ada_mod",
    )(c_pad, ada_w, ada_b.reshape(depth, 1, n))


def _inproj_kernel(first, ntp, *refs):
    if first:
        xp_ref, xs_ref, lng_ref, lnb_ref = refs[:4]
        xn_ref = refs[-8]
        x = jnp.where(pl.program_id(0) >= ntp, xs_ref[...], xp_ref[...])
        x = _ln(x, lng_ref[...], lnb_ref[...])
        xn_ref[...] = x
        refs = refs[4:-8] + refs[-7:]
    else:
        x = refs[0][...]
        refs = refs[1:]
    (sc_ref, sh_ref, w_ref, gg_ref, gb_ref, wcat_ref, bs_ref, nb_ref,
     q_ref, k_ref, v_ref, kf_ref, vf_ref, gv_ref, bn_ref) = refs
    h = _modulate(x, sc_ref, sh_ref).astype(BF16)

    q = _dot(h, w_ref[:, 0:D_A])
    q_ref[...] = (q * (ATTN_SCALE * LOG2E)).astype(BF16)
    k = _dot(h, w_ref[:, D_A:2 * D_A])
    k_ref[...] = k.astype(BF16)
    kf_ref[...] = k
    v = _dot(h, w_ref[:, 2 * D_A:3 * D_A])
    v_ref[...] = v.astype(BF16)
    vf_ref[...] = v

    u = jax.nn.gelu(_dot(h, w_ref[:, 3 * D_A:3 * D_A + D_B]), approximate=True)
    vb = jax.nn.gelu(_dot(h, w_ref[:, 3 * D_A + D_B:3 * D_A + 2 * D_B]), approximate=True)
    vn = _ln(vb, gg_ref[...], gb_ref[...])
    gv_ref[...] = vn

    lane = lax.broadcasted_iota(jnp.int32, (MLP_CHUNK, LANES), 1)
    lo = lane < GROUP_DIM
    vnb = vn.astype(BF16)
    zero = jnp.zeros((MLP_CHUNK, LANES), BF16)
    n_chunks = ROW_TILE // MLP_CHUNK
    mixed = []
    for p in range(D_B // LANES):
        stacks = []
        for c in range(n_chunks):
            slab = vnb[c * MLP_CHUNK:(c + 1) * MLP_CHUNK, p * LANES:(p + 1) * LANES]
            stacks.append(jnp.concatenate([jnp.where(lo, slab, zero), jnp.where(lo, zero, slab)], axis=0))
        mixed.append(_dot(wcat_ref[0, p], jnp.concatenate(stacks, axis=1)))
    rows = []
    for c in range(n_chunks):
        r0 = c * MLP_CHUNK
        mix_c = jnp.concatenate([m[:, c * LANES:(c + 1) * LANES] for m in mixed], axis=1) + bs_ref[0]
        rows.append(u[r0:r0 + MLP_CHUNK, :] * mix_c)
    bo = jnp.concatenate(rows, axis=0)
    bn_ref[...] = _rms(bo, nb_ref[...]).astype(BF16)


def _inproj_call(layer, x_in, mods, w_bf, gg, gb, wcat, bs, nb, *, n_prompt_tiles, tiles_per_batch, n_batch,
                 n_sample_rows):
    first = len(x_in) == 4
    d = x_in[0].shape[1]
    t = mods.shape[2] * CHUNK
    nt = t // ROW_TILE
    ntp = n_prompt_tiles
    n_tail = n_batch + (nt - ntp)

    def tile(i):
        return (i, 0)

    def const2(i):
        return (0, 0)

    def selmap4(i):
        return (jnp.where(i >= ntp, 1, 0), 0, 0, 0)

    def selmap3(i):
        return (jnp.where(i >= ntp, 1, 0), 0, 0)

    def tailmap(i):
        return (jnp.where(i < ntp, i // tiles_per_batch, n_batch + i - ntp), 0)

    def gvmap(i):
        return (jnp.maximum(i - ntp, 0), 0)

    if first:
        x_specs = [
            pl.BlockSpec((ROW_TILE, d), lambda i: (jnp.minimum(i, ntp - 1), 0)),
            pl.BlockSpec((ROW_TILE, d), lambda i: (jnp.maximum(i - ntp, 0), 0)),
            pl.BlockSpec((1, d), const2),
            pl.BlockSpec((1, d), const2),
        ]
    else:
        x_specs = [pl.BlockSpec((ROW_TILE, d), tile)]
    in_specs = x_specs + [
        _mod_spec(layer, SC1, ROW_TILE, d),
        _mod_spec(layer, SH1, ROW_TILE, d),
        pl.BlockSpec(w_bf.shape, const2),
        pl.BlockSpec((1, D_B), const2),
        pl.BlockSpec((1, D_B), const2),
        pl.BlockSpec((1,) + wcat.shape[1:], selmap4),
        pl.BlockSpec((1, MLP_CHUNK, D_B), selmap3),
        pl.BlockSpec((1, D_B), const2),
    ]
    out_specs = [
        pl.BlockSpec((ROW_TILE, D_A), tile),
        pl.BlockSpec((ROW_TILE, D_A), tile),
        pl.BlockSpec((ROW_TILE, D_A), tile),
        pl.BlockSpec((ROW_TILE, D_A), tailmap),
        pl.BlockSpec((ROW_TILE, D_A), tailmap),
        pl.BlockSpec((ROW_TILE, D_B), gvmap),
        pl.BlockSpec((ROW_TILE, D_B), tile),
    ]
    out_shape = [
        jax.ShapeDtypeStruct((t, D_A), BF16),
        jax.ShapeDtypeStruct((t, D_A), BF16),
        jax.ShapeDtypeStruct((t, D_A), BF16),
        jax.ShapeDtypeStruct((n_tail * ROW_TILE, D_A), F32),
        jax.ShapeDtypeStruct((n_tail * ROW_TILE, D_A), F32),
        jax.ShapeDtypeStruct((n_sample_rows, D_B), F32),
        jax.ShapeDtypeStruct((t, D_B), BF16),
    ]
    if first:
        out_specs = [pl.BlockSpec((ROW_TILE, d), tile)] + out_specs
        out_shape = [jax.ShapeDtypeStruct((t, d), F32)] + out_shape
    return pl.pallas_call(
        functools.partial(_inproj_kernel, first, ntp),
        grid=(nt,),
        in_specs=in_specs,
        out_specs=out_specs,
        out_shape=out_shape,
        compiler_params=_cparams("arbitrary"),
        name="in_proj_first" if first else "in_proj",
    )(*x_in, mods, mods, w_bf, gg, gb, wcat, bs, nb)


def _row_max(pieces):
    folded = None
    m = None
    for s in pieces:
        if s.shape[1] % LANES == 0:
            for c0 in range(0, s.shape[1], LANES):
                piece = s[:, c0:c0 + LANES]
                folded = piece if folded is None else jnp.maximum(folded, piece)
        else:
            mj = s.max(axis=1, keepdims=True)
            m = mj if m is None else jnp.maximum(m, mj)
    if folded is not None:
        mj = folded.max(axis=1, keepdims=True)
        m = mj if m is None else jnp.maximum(m, mj)
    return m


def _attend(q_of, r, kv_slabs, bias_of, part_rows, part_ok, na):
    nt_dims = (((1,), (1,)), ((), ()))
    lo = lax.broadcasted_iota(jnp.int32, (1, LANES), 1) < HEAD_DIM
    lo_t = lax.broadcasted_iota(jnp.int32, (LANES, 1), 0) < HEAD_DIM
    bounds = sorted({0, r} | {b for rows in part_rows for b in rows})
    segs = list(zip(bounds[:-1], bounds[1:]))

    def covered(seg, rows):
        return rows[0] <= seg[0] and seg[1] <= rows[1]

    outs = []
    for p in range(D_A // LANES):
        qp = q_of(p)
        slabs = kv_slabs(p)
        accs = []
        for half in range(2):
            h = 2 * p + half
            keep = lo if half == 0 else jnp.logical_not(lo)
            keep_t = lo_t if half == 0 else jnp.logical_not(lo_t)
            qh = jnp.where(keep, qp, jnp.zeros_like(qp))
            ss = []
            for j, (kp, _, transposed) in enumerate(slabs):
                qj = qh[part_rows[j][0]:part_rows[j][1]]
                if transposed:
                    s = _dot(qj, kp)
                else:
                    s = lax.dot_general(qj, kp, nt_dims, preferred_element_type=F32)
                s = s + bias_of(h, j)
                if part_ok is not None and part_ok[j] is not None:
                    s = jnp.where(part_ok[j], s, NEG)
                ss.append(s)
            m_segs = [_row_max([s[a - rows[0]:b - rows[0]] for s, rows in zip(ss, part_rows) if covered((a, b), rows)])
                      for a, b in segs]
            acc_segs = [jnp.zeros((b - a, LANES), F32) for a, b in segs]
            for s, (_, vp, transposed), rows in zip(ss, slabs, part_rows):
                m = jnp.concatenate([m_segs[i] for i, seg in enumerate(segs) if covered(seg, rows)], axis=0)
                e = jnp.exp2(s - m).astype(BF16)
                if transposed:
                    vh = jnp.where(keep_t, vp, jnp.ones_like(vp))
                    c = lax.dot_general(e, vh, nt_dims, preferred_element_type=F32)
                else:
                    c = _dot(e, jnp.where(keep, vp, jnp.ones_like(vp)))
                for i, (a, b) in enumerate(segs):
                    if covered((a, b), rows):
                        acc_segs[i] = acc_segs[i] + c[a - rows[0]:b - rows[0]]
            accs.append(jnp.concatenate(acc_segs, axis=0))
        num = jnp.where(lo, accs[0], accs[1])
        den = pltpu.roll(jnp.where(lo, accs[1], accs[0]), HEAD_DIM, 1)
        outs.append(num / den)
    a = jnp.concatenate(outs, axis=1)
    return _rms(a, na)


def _attn_prompt_kernel(q_ref, *refs):
    n_parts = Q_GROUP + 2
    k_refs = refs[:n_parts]
    v_refs = refs[n_parts:2 * n_parts]
    bias_ref, na_ref, o_ref = refs[2 * n_parts:]
    j = pl.program_id(1)
    tiles_of = [[t for t in range(Q_GROUP) if t <= b <= t + 2] for b in range(n_parts)]
    part_rows = [(ts[0] * Q_TILE, (ts[-1] + 1) * Q_TILE) for ts in tiles_of]

    def q_of(p):
        return q_ref[:, p * LANES:(p + 1) * LANES]

    def kv_slabs(p):
        sl = slice(p * LANES, (p + 1) * LANES)
        return [(kr[:, sl], vr[:, sl], False) for kr, vr in zip(k_refs, v_refs)]

    def bias_of(h, b):
        return jnp.concatenate([bias_ref[h, :, (b - t) * Q_TILE:(b - t + 1) * Q_TILE] for t in tiles_of[b]], axis=0)

    def run(part_ok):
        out = _attend(q_of, Q_GROUP * Q_TILE, kv_slabs, bias_of, part_rows, part_ok, na_ref[...])
        o_ref[...] = out.astype(BF16)

    pl.when(j >= 1)(functools.partial(run, None))
    pl.when(j < 1)(functools.partial(run, [j >= 1, j >= 1] + [None] * Q_GROUP))


def _attn_prompt_call(q, k, v, bias, na, *, n_batch, q_tiles_per_batch):
    nq = q_tiles_per_batch // Q_GROUP

    def kmap(part):
        return lambda b, j: (b * q_tiles_per_batch + jnp.maximum(Q_GROUP * j - 2 + part, 0), 0)

    blk = pl.BlockSpec((Q_GROUP * Q_TILE, D_A), lambda b, j: (b * nq + j, 0))
    kv_specs = [pl.BlockSpec((Q_TILE, D_A), kmap(part)) for part in range(Q_GROUP + 2)]
    return pl.pallas_call(
        _attn_prompt_kernel,
        grid=(n_batch, nq),
        in_specs=[blk] + kv_specs + kv_specs + [
            pl.BlockSpec(bias.shape, lambda b, j: (0, 0, 0)),
            pl.BlockSpec((1, D_A), lambda b, j: (0, 0)),
        ],
        out_specs=blk,
        out_shape=jax.ShapeDtypeStruct((n_batch * nq * Q_GROUP * Q_TILE, D_A), BF16),
        compiler_params=_cparams("arbitrary", "arbitrary"),
        name="attn_prompt",
    )(q, *([k] * (Q_GROUP + 2)), *([v] * (Q_GROUP + 2)), bias, na)


def _attn_sample_kernel(q_ref, kn_ref, vn_ref, ck_ref, cv_ref, bias_c_ref, bias_n_ref, na_ref, o_ref):
    def kv_slabs(p):
        sl = slice(p * LANES, (p + 1) * LANES)

        def cached(c_ref):
            return c_ref[2 * p:2 * p + 2].reshape(LANES, c_ref.shape[2]).astype(BF16)
        return [(cached(ck_ref), cached(cv_ref), True), (kn_ref[:, sl], vn_ref[:, sl], False)]

    def bias_of(h, b):
        return bias_c_ref[h] if b == 0 else bias_n_ref[h]

    def q_of(p):
        return q_ref[:, p * LANES:(p + 1) * LANES]

    r = q_ref.shape[0]
    o_ref[...] = _attend(q_of, r, kv_slabs, bias_of, [(0, r), (0, r)], None, na_ref[...]).astype(BF16)


def _attn_sample_call(layer, q, k, v, cache_kt, cache_vt, bias_c, bias_n, na, *, first_chunk):
    _, nb, nh, hd, win = cache_kt.shape
    blk = pl.BlockSpec((CHUNK, D_A), lambda b: (first_chunk + b, 0))
    cache = pl.BlockSpec((None, None, nh, hd, win), lambda b: (layer, b, 0, 0, 0))
    return pl.pallas_call(
        _attn_sample_kernel,
        grid=(nb,),
        in_specs=[blk, blk, blk, cache, cache,
                  pl.BlockSpec(bias_c.shape, lambda b: (0, 0, 0)),
                  pl.BlockSpec(bias_n.shape, lambda b: (0, 0, 0)),
                  pl.BlockSpec((1, D_A), lambda b: (0, 0))],
        out_specs=pl.BlockSpec((CHUNK, D_A), lambda b: (b, 0)),
        out_shape=jax.ShapeDtypeStruct((nb * CHUNK, D_A), BF16),
        compiler_params=_cparams("arbitrary"),
        name="attn_sample",
    )(q, k, v, cache_kt, cache_vt, bias_c, bias_n, na)


def _outproj_kernel(route, ntp, x_ref, anp_ref, ans_ref, bn_ref, w_ref, g1_ref, sc_ref, sh_ref, lg_ref, lb_ref, *rest):
    if route:
        rc_ref, x1_ref, hp_ref, rt_ref = rest
    else:
        x1_ref, hp_ref = rest
    an = jnp.where(pl.program_id(0) >= ntp, ans_ref[...], anp_ref[...])
    mix = _dot(an, w_ref[0:D_A, :]) + _dot(bn_ref[...], w_ref[D_A:D_A + D_B, :])
    x1 = _ln(ALPHA * x_ref[...] + _gate_rows(mix, g1_ref), lg_ref[...], lb_ref[...])
    x1_ref[...] = x1
    h2 = _modulate(x1, sc_ref, sh_ref)
    hp_ref[...] = h2.astype(hp_ref.dtype)
    if route:
        h_hi, h_lo = _split_bf16(h2)
        both = _dot(h_hi, rc_ref[...])
        logits = both[:, :LANES] + both[:, LANES:] + _dot(h_lo, rc_ref[:, :LANES])
        lane = lax.broadcasted_iota(jnp.int32, logits.shape, 1)
        logits = jnp.where(lane < N_EXPERTS, logits, -jnp.inf)
        m1 = logits.max(axis=1, keepdims=True)
        i1 = jnp.where(logits == m1, lane, LANES).min(axis=1, keepdims=True)
        rest_l = jnp.where(lane == i1, -jnp.inf, logits)
        m2 = rest_l.max(axis=1, keepdims=True)
        i2 = jnp.where(rest_l == m2, lane, LANES).min(axis=1, keepdims=True)
        e2 = jnp.exp(m2 - m1)
        w1 = 1.0 / (1.0 + e2)
        w2 = e2 / (1.0 + e2)
        rt = jnp.where(lane == 0, i1.astype(F32),
                       jnp.where(lane == 1, i2.astype(F32),
                                 jnp.where(lane == 2, w1, jnp.where(lane == 3, w2, 0.0))))
        rt_ref[...] = rt


def _outproj_call(layer, route, x, an_p, an_s, bn, w_bf, mods, lg, lb, r_cat=None):
    t, d = x.shape
    nt = t // ROW_TILE
    ntp = an_p.shape[0] // ROW_TILE

    def tile(i):
        return (i, 0)

    def const2(i):
        return (0, 0)

    in_specs = [
        pl.BlockSpec((ROW_TILE, d), tile),
        pl.BlockSpec((ROW_TILE, D_A), lambda i: (jnp.minimum(i, ntp - 1), 0)),
        pl.BlockSpec((ROW_TILE, D_A), lambda i: (jnp.maximum(i - ntp, 0), 0)),
        pl.BlockSpec((ROW_TILE, D_B), tile),
        pl.BlockSpec(w_bf.shape, const2),
        _mod_spec(layer, G1, ROW_TILE, d),
        _mod_spec(layer, SC2, ROW_TILE, d),
        _mod_spec(layer, SH2, ROW_TILE, d),
        pl.BlockSpec((1, d), const2),
        pl.BlockSpec((1, d), const2),
    ]
    out_specs = [pl.BlockSpec((ROW_TILE, d), tile), pl.BlockSpec((ROW_TILE, d), tile)]
    out_shape = [jax.ShapeDtypeStruct((t, d), F32), jax.ShapeDtypeStruct((t, d), F32 if route else BF16)]
    args = [x, an_p, an_s, bn, w_bf, mods, mods, mods, lg, lb]
    if route:
        in_specs.append(pl.BlockSpec(r_cat.shape, const2))
        out_specs.append(pl.BlockSpec((ROW_TILE, LANES), tile))
        out_shape.append(jax.ShapeDtypeStruct((t, LANES), F32))
        args.append(r_cat)
    return pl.pallas_call(
        functools.partial(_outproj_kernel, route, ntp),
        grid=(nt,),
        in_specs=in_specs,
        out_specs=out_specs,
        out_shape=out_shape,
        compiler_params=_cparams("arbitrary"),
        name="out_proj_route" if route else "out_proj",
    )(*args)


def _swiglu_chains(n_sub, xb_ref, wg_ref, wu_ref, wd_ref, o_ref):
    wg = wg_ref[0].astype(BF16)
    wu = wu_ref[0].astype(BF16)
    wd = wd_ref[0].astype(BF16)
    for s in range(n_sub):
        rows = slice(s * FFN_SUB, (s + 1) * FFN_SUB)
        xs = xb_ref[rows, :]
        g = _dot(xs, wg)
        u = _dot(xs, wu)
        a = (g * jax.nn.sigmoid(g) * u).astype(BF16)
        o_ref[rows, :] += _dot(a, wd)


def _ffn_kernel(te_ref, tv_ref, x_ref, wg_ref, wu_ref, wd_ref, o_ref, xb_ref):
    del te_ref
    i = pl.program_id(0)
    j = pl.program_id(1)
    valid = tv_ref[i]
    nsub = (valid + FFN_SUB - 1) // FFN_SUB

    @pl.when(j == 0)
    def _():
        xb_ref[...] = x_ref[...].astype(BF16)
        o_ref[...] = jnp.zeros(o_ref.shape, F32)

    for n_sub in range(1, x_ref.shape[0] // FFN_SUB + 1):
        pl.when(nsub == n_sub)(functools.partial(_swiglu_chains, n_sub, xb_ref, wg_ref, wu_ref, wd_ref, o_ref))


def _ffn_call(tile_expert, tile_valid, xs, wg, wu, wd, tile_rows):
    p, d = xs.shape
    f = wg.shape[2]
    nt = p // tile_rows
    nc = f // FFN_COLS

    def active_col(i, j, tv):
        return jnp.where(tv[i] > 0, j, nc - 1)

    in_specs = [
        pl.BlockSpec((tile_rows, d), lambda i, j, te, tv: (i, 0)),
        pl.BlockSpec((1, d, FFN_COLS), lambda i, j, te, tv: (te[i], 0, active_col(i, j, tv))),
        pl.BlockSpec((1, d, FFN_COLS), lambda i, j, te, tv: (te[i], 0, active_col(i, j, tv))),
        pl.BlockSpec((1, FFN_COLS, d), lambda i, j, te, tv: (te[i], active_col(i, j, tv), 0)),
    ]
    return pl.pallas_call(
        _ffn_kernel,
        grid_spec=pltpu.PrefetchScalarGridSpec(
            num_scalar_prefetch=2,
            grid=(nt, nc),
            in_specs=in_specs,
            out_specs=pl.BlockSpec((tile_rows, d), lambda i, j, te, tv: (i, 0)),
            scratch_shapes=[pltpu.VMEM((tile_rows, d), BF16)],
        ),
        out_shape=jax.ShapeDtypeStruct((p, d), F32),
        compiler_params=_cparams("arbitrary", "arbitrary"),
        name="swiglu_grouped",
    )(tile_expert, tile_valid, xs, wg, wu, wd)


def _dense_ffn_kernel(x_ref, wg_ref, wu_ref, wd_ref, x1_ref, g2_ref, lg_ref, lb_ref, o_ref):
    x = x_ref[...]
    f = wg_ref.shape[1]
    acc = None
    for c0 in range(0, f, FFN_SPLIT):
        c1 = min(c0 + FFN_SPLIT, f)
        g = _dot(x, wg_ref[:, c0:c1])
        u = _dot(x, wu_ref[:, c0:c1])
        a = (g * jax.nn.sigmoid(g) * u).astype(BF16)
        part = _dot(a, wd_ref[c0:c1, :])
        acc = part if acc is None else acc + part
    o_ref[...] = _ln(ALPHA * x1_ref[...] + _gate_rows(acc, g2_ref), lg_ref[...], lb_ref[...])


def _dense_ffn_call(layer, hp, wg_bf, wu_bf, wd_bf, x1, mods, lg, lb):
    t, d = x1.shape
    f = wg_bf.shape[1]
    tile = pl.BlockSpec((ROW_TILE, d), lambda i: (i, 0))
    vec = pl.BlockSpec((1, d), lambda i: (0, 0))

    def resident(shape):
        return pl.BlockSpec(shape, lambda i: (0, 0), pipeline_mode=pl.Buffered(1))

    return pl.pallas_call(
        _dense_ffn_kernel,
        grid=(t // ROW_TILE,),
        in_specs=[tile, resident((d, f)), resident((d, f)), resident((f, d)), tile,
                  _mod_spec(layer, G2, ROW_TILE, d), vec, vec],
        out_specs=tile,
        out_shape=jax.ShapeDtypeStruct((t, d), F32),
        compiler_params=_cparams("arbitrary"),
        name="swiglu_dense",
    )(hp, wg_bf, wu_bf, wd_bf, x1, mods, lg, lb)


def _dispatch_kernel(pos_ref, src_ref, init_ref, dst_ref, sem):
    del init_ref

    def issue(r, carry):
        row = src_ref.at[pl.ds(r, 1), :]
        pltpu.make_async_copy(row, dst_ref.at[pl.ds(pos_ref[0, 2 * r], 1), :], sem).start()
        pltpu.make_async_copy(row, dst_ref.at[pl.ds(pos_ref[0, 2 * r + 1], 1), :], sem).start()
        return carry
    lax.fori_loop(0, ROW_TILE, issue, 0, unroll=8)
    for _ in range(2):
        pltpu.make_async_copy(src_ref, dst_ref.at[pl.ds(0, ROW_TILE), :], sem).wait()


def _dispatch_call(pos_tiles, hp, n_rows_sorted):
    t, d = hp.shape
    nt = t // ROW_TILE
    return pl.pallas_call(
        _dispatch_kernel,
        grid=(nt,),
        in_specs=[
            pl.BlockSpec((None, 1, 2 * ROW_TILE), lambda i: (i, 0, 0), memory_space=pltpu.SMEM),
            pl.BlockSpec((ROW_TILE, d), lambda i: (i, 0)),
            pl.BlockSpec(memory_space=pl.ANY),
        ],
        out_specs=pl.BlockSpec(memory_space=pl.ANY),
        out_shape=jax.ShapeDtypeStruct((n_rows_sorted, d), hp.dtype),
        scratch_shapes=[pltpu.SemaphoreType.DMA(())],
        input_output_aliases={2: 0},
        compiler_params=_cparams("arbitrary"),
        name="dispatch_rows",
    )(pos_tiles, hp, jnp.zeros((n_rows_sorted, d), hp.dtype))


def _final_moe_kernel(ntp, pos_ref, x_ref, rt_ref, g2_ref, lg_ref, lb_ref, ys_ref, op_ref, os_ref, ybuf, sem):
    def issue(r, carry):
        pltpu.make_async_copy(ys_ref.at[pl.ds(pos_ref[0, 2 * r], 1), :], ybuf.at[0, pl.ds(r, 1), :], sem).start()
        pltpu.make_async_copy(ys_ref.at[pl.ds(pos_ref[0, 2 * r + 1], 1), :], ybuf.at[1, pl.ds(r, 1), :], sem).start()
        return carry
    lax.fori_loop(0, ROW_TILE, issue, 0, unroll=8)
    for k in range(2):
        pltpu.make_async_copy(ys_ref.at[pl.ds(0, ROW_TILE), :], ybuf.at[k], sem).wait()
    rt = rt_ref[...]
    f = rt[:, 2:3] * ybuf[0] + rt[:, 3:4] * ybuf[1]
    y = _ln(ALPHA * x_ref[...] + _gate_rows(f, g2_ref), lg_ref[...], lb_ref[...])
    i = pl.program_id(0)

    @pl.when(i < ntp)
    def _():
        op_ref[...] = y

    @pl.when(i >= ntp)
    def _():
        os_ref[...] = y


def _final_moe_call(layer, pos_tiles, x1, rt, mods, lg, lb, ys, *, n_prompt_tiles):
    t, d = x1.shape
    ntp = n_prompt_tiles
    tile = pl.BlockSpec((ROW_TILE, d), lambda i: (i, 0))
    vec = pl.BlockSpec((1, d), lambda i: (0, 0))
    return pl.pallas_call(
        functools.partial(_final_moe_kernel, ntp),
        grid=(t // ROW_TILE,),
        in_specs=[
            pl.BlockSpec((None, 1, 2 * ROW_TILE), lambda i: (i, 0, 0), memory_space=pltpu.SMEM),
            tile,
            pl.BlockSpec((ROW_TILE, LANES), lambda i: (i, 0)),
            _mod_spec(layer, G2, ROW_TILE, d),
            vec, vec,
            pl.BlockSpec(memory_space=pl.ANY),
        ],
        out_specs=[pl.BlockSpec((ROW_TILE, d), lambda i: (jnp.minimum(i, ntp - 1), 0)),
                   pl.BlockSpec((ROW_TILE, d), lambda i: (jnp.maximum(i - ntp, 0), 0))],
        out_shape=[jax.ShapeDtypeStruct((ntp * ROW_TILE, d), F32),
                   jax.ShapeDtypeStruct((t - ntp * ROW_TILE, d), F32)],
        scratch_shapes=[pltpu.VMEM((2, ROW_TILE, d), F32), pltpu.SemaphoreType.DMA(())],
        compiler_params=_cparams("arbitrary"),
        name="final_moe",
    )(pos_tiles, x1, rt, mods, lg, lb, ys)


def _rel_bias(table, n_q, n_k, banded):
    r = jnp.arange(n_q)[:, None]
    w = jnp.arange(n_k)[None, :]
    n_diag = n_q + n_k - 1
    k = jnp.arange(n_diag)
    diag = table[:, jnp.clip(LEFT_CTX + (n_q - 1) - k, -REL_CLIP, REL_CLIP) + REL_CLIP].astype(F32) * LOG2E
    padded = jnp.concatenate([diag, jnp.zeros((diag.shape[0], 1), F32)], axis=1)
    skew = jnp.tile(padded, (1, n_q))[:, :n_q * n_diag].reshape(-1, n_q, n_diag)
    bias = skew[:, :, n_q - 1:n_q - 1 + n_k]
    if banded:
        qa = r // CHUNK
        kc = w // CHUNK
        vis = (kc >= qa) & (kc <= qa + LEFT_CHUNKS)
        bias = jnp.where(vis[None], bias, NEG)
    return bias


def _gating_weights(w_s, b_s):
    n = MLP_CHUNK
    tril = jnp.tril(jnp.ones((n, n), bool))
    wm = jnp.where(tril[None], w_s, 0.0)
    h = CHUNK
    top = wm[:, :h, :h]
    z = jnp.zeros_like(top)
    wm_s = jnp.concatenate([jnp.concatenate([top, z], 2), jnp.concatenate([z, top], 2)], 1)
    both = jnp.stack([wm, wm_s])
    wcat = jnp.concatenate([both[:, 0::2], both[:, 1::2]], axis=-1)
    bias_p = jnp.repeat(jnp.transpose(b_s), GROUP_DIM, axis=1)
    bias_s = jnp.concatenate([bias_p[:h], bias_p[:h]], 0)
    return wcat.astype(BF16), jnp.stack([bias_p, bias_s]).astype(F32)


def _route_plan(rt, n_tiles, tile_rows):
    t = rt.shape[0]
    e_flat = rt[:, 0:2].astype(jnp.int32).reshape(-1)
    oh = (e_flat[:, None] == jnp.arange(N_EXPERTS)[None, :]).astype(jnp.int32)
    csum = jnp.cumsum(oh, axis=0)
    rank = jnp.sum(oh * (csum - 1), axis=1)
    counts = csum[-1]
    tiles_e = (counts + tile_rows - 1) // tile_rows
    tile_end = jnp.cumsum(tiles_e)
    tile_start = tile_end - tiles_e
    pos = (tile_start * tile_rows)[e_flat] + rank
    ti = jnp.arange(n_tiles)
    te = jnp.minimum(jnp.sum((ti[:, None] >= tile_end[None, :]).astype(jnp.int32), axis=1), N_EXPERTS - 1)
    tv = jnp.clip(counts[te] - (ti - tile_start[te]) * tile_rows, 0, tile_rows)
    tv = jnp.where(ti < tile_end[-1], tv, 0)
    last_e = te[jnp.maximum(tile_end[-1] - 1, 0)]
    te = jnp.where(ti < tile_end[-1], te, last_e)
    return pos.reshape(t // ROW_TILE, 1, 2 * ROW_TILE).astype(jnp.int32), te.astype(jnp.int32), tv.astype(jnp.int32)


def kernel(x_prompt, x_sample, cache_k, cache_v, c_prompt, c_sample, ln_in_g, ln_in_b, w_in, w_out, rel_bias_table, gmlp_ln_g, gmlp_ln_b, gmlp_w_s, gmlp_b_s, out_norm_a, out_norm_b, ada_w, ada_b, ln1_g, ln1_b, ln2_g, ln2_b, ffn_w_gate, ffn_w_up, ffn_w_down, moe_router, moe_w_gate, moe_w_up, moe_w_down):
    nb, seq, d = x_prompt.shape
    ns, dseq, _ = x_sample.shape
    depth = w_in.shape[0]
    tp = nb * seq
    ts = ns * dseq
    t = tp + ts
    assert depth == DEPTH and dseq == CHUNK and seq % ROW_TILE == 0 and ts % ROW_TILE == 0
    assert seq % (Q_GROUP * Q_TILE) == 0 and cache_k.shape[2] == LEFT_CTX
    ntp = tp // ROW_TILE

    n_cond = nb + ns
    c_all = jnp.concatenate([c_prompt, c_sample], axis=0)
    c_pad = jnp.pad(c_all, ((0, (-n_cond) % 8), (0, 0)))
    mods = _ada_call(c_pad, ada_w, ada_b)
    mods = jnp.transpose(mods[:, :n_cond].reshape(depth, n_cond, 6, d), (0, 2, 1, 3))
    cpb = seq // CHUNK
    mods_p = jnp.broadcast_to(mods[:, :, :nb, None, :], (depth, 6, nb, cpb, d)).reshape(depth, 6, nb * cpb, d)
    mods = jnp.concatenate([mods_p, mods[:, :, nb:]], axis=2)

    def row(v):
        return v.reshape(1, -1)

    w_in_bf = w_in.astype(BF16)
    w_out_bf = w_out.astype(BF16)
    cache_kt = jnp.transpose(cache_k, (0, 1, 3, 4, 2))
    cache_vt = jnp.transpose(cache_v, (0, 1, 3, 4, 2))
    k_tail, v_tail, gv_rows = [], [], []
    for l in range(depth):
        wcat, bs = _gating_weights(gmlp_w_s[l], gmlp_b_s[l])
        if l == 0:
            x_in = (x_prompt.reshape(tp, d), x_sample.reshape(ts, d), row(ln_in_g), row(ln_in_b))
        else:
            x_in = (x,)
        res = _inproj_call(l, x_in, mods, w_in_bf[l], row(gmlp_ln_g[l]), row(gmlp_ln_b[l]), wcat, bs,
                           row(out_norm_b[l]),
                           n_prompt_tiles=ntp, tiles_per_batch=seq // ROW_TILE, n_batch=nb, n_sample_rows=ts)
        if l == 0:
            x, q, k, v, kf, vf, gv, bn = res
        else:
            q, k, v, kf, vf, gv, bn = res
        k_tail.append(kf)
        v_tail.append(vf)
        gv_rows.append(gv)

        table = rel_bias_table[l]
        bias_p = _rel_bias(table, Q_TILE, 3 * Q_TILE, True)
        bias_s = _rel_bias(table, CHUNK, LEFT_CTX + CHUNK, False)
        na = row(out_norm_a[l])
        an_p = _attn_prompt_call(q, k, v, bias_p, na, n_batch=nb, q_tiles_per_batch=seq // Q_TILE)
        an_s = _attn_sample_call(l, q, k, v, cache_kt, cache_vt, bias_s[:, :, :LEFT_CTX], bias_s[:, :, LEFT_CTX:], na,
                                 first_chunk=tp // CHUNK)

        lg1, lb1, lg2, lb2 = row(ln1_g[l]), row(ln1_b[l]), row(ln2_g[l]), row(ln2_b[l])
        i = l // 2
        if l % 2 == 0:
            x1, hp = _outproj_call(l, False, x, an_p, an_s, bn, w_out_bf[l], mods, lg1, lb1)
            x = _dense_ffn_call(l, hp, ffn_w_gate[i].astype(BF16), ffn_w_up[i].astype(BF16),
                                ffn_w_down[i].astype(BF16), x1, mods, lg2, lb2)
        else:
            r_pad = jnp.pad(moe_router[i], ((0, 0), (0, LANES - N_EXPERTS)))
            r_hi = r_pad.astype(BF16)
            r_lo = (r_pad - r_hi.astype(F32)).astype(BF16)
            r_cat = jnp.concatenate([r_hi, r_lo], axis=1)
            x1, hp, rt = _outproj_call(l, True, x, an_p, an_s, bn, w_out_bf[l], mods, lg1, lb1, r_cat)
            n_tiles = -(-2 * t // MOE_TILE) + N_EXPERTS
            pos, te, tv = _route_plan(rt, n_tiles, MOE_TILE)
            xs = _dispatch_call(pos, hp, n_tiles * MOE_TILE)
            ys = _ffn_call(te, tv, xs, moe_w_gate[i], moe_w_up[i], moe_w_down[i], MOE_TILE)
            y_p, y_s = _final_moe_call(l, pos, x1, rt, mods, lg2, lb2, ys, n_prompt_tiles=ntp)

    y_prompt = y_p.reshape(nb, seq, d)
    y_sample = y_s.reshape(ns, dseq, d)

    def tails(rows):
        kp = jnp.stack([r[:nb * ROW_TILE].reshape(nb, ROW_TILE, N_HEADS, HEAD_DIM) for r in rows])
        ksn = jnp.stack([r[nb * ROW_TILE:].reshape(ns, dseq, N_HEADS, HEAD_DIM) for r in rows])
        return kp, ksn

    k_prompt_new, k_sample_new = tails(k_tail)
    v_prompt_new, v_sample_new = tails(v_tail)
    gmlp_v_sample_new = jnp.stack([g.reshape(ns, dseq, D_B) for g in gv_rows])
    return (y_prompt, y_sample, k_prompt_new, v_prompt_new, k_sample_new, v_sample_new, gmlp_v_sample_new)
```

```python
import functools

import jax
import jax.numpy as jnp
from jax import lax
from jax.experimental import pallas as pl
from jax.experimental.pallas import tpu as pltpu

CHUNK = 64
LEFT_CHUNKS = 8
LEFT_CTX = LEFT_CHUNKS * CHUNK
N_HEADS = 8
HEAD_DIM = 64
D_A = N_HEADS * HEAD_DIM
N_GROUPS = 8
GROUP_DIM = 64
D_B = N_GROUPS * GROUP_DIM
MLP_CHUNK = 128
REL_CLIP = 128
N_EXPERTS = 8
DEPTH = 2
ALPHA = (2 * DEPTH) ** 0.25
LN_EPS = 1e-5
ATTN_SCALE = HEAD_DIM ** -0.5
LOG2E = 1.4426950408889634
NEG = -1e30
SH1, SC1, G1, SH2, SC2, G2 = range(6)

LANES = 128
ROW_TILE = 512
Q_TILE = 256
Q_GROUP = 4
ZERO_ROWS_MAX = 1024
FFN_SPLIT = 1536
MOE_TILE = 2048
FFN_SUB = 512
FFN_COLS = 256
VMEM_LIMIT = 56 * 1024 * 1024

BF16 = jnp.bfloat16
F32 = jnp.float32


def _cparams(*sem):
    return pltpu.CompilerParams(dimension_semantics=sem, vmem_limit_bytes=VMEM_LIMIT)


def _ln(x, g, b):
    mu = jnp.mean(x, axis=-1, keepdims=True)
    xc = x - mu
    var = jnp.mean(xc * xc, axis=-1, keepdims=True)
    return xc * lax.rsqrt(var + LN_EPS) * g + b


def _rms(x, g):
    return x * lax.rsqrt(jnp.mean(x * x, axis=-1, keepdims=True) + LN_EPS) * g


def _modulate(x, sc_ref, sh_ref):
    parts = []
    for c in range(x.shape[0] // CHUNK):
        xc = x[c * CHUNK:(c + 1) * CHUNK]
        parts.append(xc * (1.0 + sc_ref[c:c + 1, :]) + sh_ref[c:c + 1, :])
    return jnp.concatenate(parts, axis=0)


def _gate_rows(x, g_ref):
    parts = []
    for c in range(x.shape[0] // CHUNK):
        parts.append(x[c * CHUNK:(c + 1) * CHUNK] * (1.0 + g_ref[c:c + 1, :]))
    return jnp.concatenate(parts, axis=0)


def _mod_spec(layer, comp, rows, d):
    return pl.BlockSpec((None, None, rows // CHUNK, d), lambda i, *_: (layer, comp, i, 0))


def _split_bf16(x):
    hi = x.astype(BF16)
    lo = (x - hi.astype(F32)).astype(BF16)
    return hi, lo


def _dot(a, b):
    return jnp.dot(a, b, preferred_element_type=F32)


def _ada_kernel(c_ref, w_ref, b_ref, o_ref):
    c = c_ref[...]
    s = c * jax.nn.sigmoid(c)
    s_hi, s_lo = _split_bf16(s)
    w_hi, w_lo = _split_bf16(w_ref[0])
    o_ref[0] = _dot(s_hi, w_hi) + _dot(s_lo, w_hi) + _dot(s_hi, w_lo) + b_ref[0]


def _ada_call(c_pad, ada_w, ada_b):
    depth, d, n = ada_w.shape
    rows = c_pad.shape[0]
    tn = n // 4
    return pl.pallas_call(
        _ada_kernel,
        grid=(depth, n // tn),
        in_specs=[
            pl.BlockSpec((rows, d), lambda l, j: (0, 0)),
            pl.BlockSpec((1, d, tn), lambda l, j: (l, 0, j)),
            pl.BlockSpec((1, 1, tn), lambda l, j: (l, 0, j)),
        ],
        out_specs=pl.BlockSpec((1, rows, tn), lambda l, j: (l, 0, j)),
        out_shape=jax.ShapeDtypeStruct((depth, rows, n), F32),
        compiler_params=_cparams("arbitrary", "arbitrary"),
        name="ada_mod",
    )(c_pad, ada_w, ada_b.reshape(depth, 1, n))


def _inproj_kernel(first, ntp, *refs):
    if first:
        xp_ref, xs_ref, lng_ref, lnb_ref = refs[:4]
        xn_ref = refs[-8]
        x = jnp.where(pl.program_id(0) >= ntp, xs_ref[...], xp_ref[...])
        x = _ln(x, lng_ref[...], lnb_ref[...])
        xn_ref[...] = x
        refs = refs[4:-8] + refs[-7:]
    else:
        x = refs[0][...]
        refs = refs[1:]
    (sc_ref, sh_ref, w_ref, gg_ref, gb_ref, wcat_ref, bs_ref, nb_ref,
     q_ref, k_ref, v_ref, kf_ref, vf_ref, gv_ref, bn_ref) = refs
    h = _modulate(x, sc_ref, sh_ref).astype(BF16)

    q = _dot(h, w_ref[:, 0:D_A])
    q_ref[...] = (q * (ATTN_SCALE * LOG2E)).astype(BF16)
    k = _dot(h, w_ref[:, D_A:2 * D_A])
    k_ref[...] = k.astype(BF16)
    kf_ref[...] = k
    v = _dot(h, w_ref[:, 2 * D_A:3 * D_A])
    v_ref[...] = v.astype(BF16)
    vf_ref[...] = v

    u = jax.nn.gelu(_dot(h, w_ref[:, 3 * D_A:3 * D_A + D_B]), approximate=True)
    vb = jax.nn.gelu(_dot(h, w_ref[:, 3 * D_A + D_B:3 * D_A + 2 * D_B]), approximate=True)
    vn = _ln(vb, gg_ref[...], gb_ref[...])
    gv_ref[...] = vn

    lane = lax.broadcasted_iota(jnp.int32, (MLP_CHUNK, LANES), 1)
    lo = lane < GROUP_DIM
    vnb = vn.astype(BF16)
    zero = jnp.zeros((MLP_CHUNK, LANES), BF16)
    n_chunks = ROW_TILE // MLP_CHUNK
    mixed = []
    for p in range(D_B // LANES):
        stacks = []
        for c in range(n_chunks):
            slab = vnb[c * MLP_CHUNK:(c + 1) * MLP_CHUNK, p * LANES:(p + 1) * LANES]
            stacks.append(jnp.concatenate([jnp.where(lo, slab, zero), jnp.where(lo, zero, slab)], axis=0))
        mixed.append(_dot(wcat_ref[0, p], jnp.concatenate(stacks, axis=1)))
    rows = []
    for c in range(n_chunks):
        r0 = c * MLP_CHUNK
        mix_c = jnp.concatenate([m[:, c * LANES:(c + 1) * LANES] for m in mixed], axis=1) + bs_ref[0]
        rows.append(u[r0:r0 + MLP_CHUNK, :] * mix_c)
    bo = jnp.concatenate(rows, axis=0)
    bn_ref[...] = _rms(bo, nb_ref[...]).astype(BF16)


def _inproj_call(layer, x_in, mods, w_bf, gg, gb, wcat, bs, nb, *, n_prompt_tiles, tiles_per_batch, n_batch,
                 n_sample_rows):
    first = len(x_in) == 4
    d = x_in[0].shape[1]
    t = mods.shape[2] * CHUNK
    nt = t // ROW_TILE
    ntp = n_prompt_tiles
    n_tail = n_batch + (nt - ntp)

    def tile(i):
        return (i, 0)

    def const2(i):
        return (0, 0)

    def selmap4(i):
        return (jnp.where(i >= ntp, 1, 0), 0, 0, 0)

    def selmap3(i):
        return (jnp.where(i >= ntp, 1, 0), 0, 0)

    def tailmap(i):
        return (jnp.where(i < ntp, i // tiles_per_batch, n_batch + i - ntp), 0)

    def gvmap(i):
        return (jnp.maximum(i - ntp, 0), 0)

    if first:
        x_specs = [
            pl.BlockSpec((ROW_TILE, d), lambda i: (jnp.minimum(i, ntp - 1), 0)),
            pl.BlockSpec((ROW_TILE, d), lambda i: (jnp.maximum(i - ntp, 0), 0)),
            pl.BlockSpec((1, d), const2),
            pl.BlockSpec((1, d), const2),
        ]
    else:
        x_specs = [pl.BlockSpec((ROW_TILE, d), tile)]
    in_specs = x_specs + [
        _mod_spec(layer, SC1, ROW_TILE, d),
        _mod_spec(layer, SH1, ROW_TILE, d),
        pl.BlockSpec(w_bf.shape, const2),
        pl.BlockSpec((1, D_B), const2),
        pl.BlockSpec((1, D_B), const2),
        pl.BlockSpec((1,) + wcat.shape[1:], selmap4),
        pl.BlockSpec((1, MLP_CHUNK, D_B), selmap3),
        pl.BlockSpec((1, D_B), const2),
    ]
    out_specs = [
        pl.BlockSpec((ROW_TILE, D_A), tile),
        pl.BlockSpec((ROW_TILE, D_A), tile),
        pl.BlockSpec((ROW_TILE, D_A), tile),
        pl.BlockSpec((ROW_TILE, D_A), tailmap),
        pl.BlockSpec((ROW_TILE, D_A), tailmap),
        pl.BlockSpec((ROW_TILE, D_B), gvmap),
        pl.BlockSpec((ROW_TILE, D_B), tile),
    ]
    out_shape = [
        jax.ShapeDtypeStruct((t, D_A), BF16),
        jax.ShapeDtypeStruct((t, D_A), BF16),
        jax.ShapeDtypeStruct((t, D_A), BF16),
        jax.ShapeDtypeStruct((n_tail * ROW_TILE, D_A), F32),
        jax.ShapeDtypeStruct((n_tail * ROW_TILE, D_A), F32),
        jax.ShapeDtypeStruct((n_sample_rows, D_B), F32),
        jax.ShapeDtypeStruct((t, D_B), BF16),
    ]
    if first:
        out_specs = [pl.BlockSpec((ROW_TILE, d), tile)] + out_specs
        out_shape = [jax.ShapeDtypeStruct((t, d), F32)] + out_shape
    return pl.pallas_call(
        functools.partial(_inproj_kernel, first, ntp),
        grid=(nt,),
        in_specs=in_specs,
        out_specs=out_specs,
        out_shape=out_shape,
        compiler_params=_cparams("arbitrary"),
        name="in_proj_first" if first else "in_proj",
    )(*x_in, mods, mods, w_bf, gg, gb, wcat, bs, nb)


def _row_max(pieces):
    folded = None
    m = None
    for s in pieces:
        if s.shape[1] % LANES == 0:
            for c0 in range(0, s.shape[1], LANES):
                piece = s[:, c0:c0 + LANES]
                folded = piece if folded is None else jnp.maximum(folded, piece)
        else:
            mj = s.max(axis=1, keepdims=True)
            m = mj if m is None else jnp.maximum(m, mj)
    if folded is not None:
        mj = folded.max(axis=1, keepdims=True)
        m = mj if m is None else jnp.maximum(m, mj)
    return m


def _attend(q_of, r, kv_slabs, bias_of, part_rows, part_ok, na):
    nt_dims = (((1,), (1,)), ((), ()))
    lo = lax.broadcasted_iota(jnp.int32, (1, LANES), 1) < HEAD_DIM
    lo_t = lax.broadcasted_iota(jnp.int32, (LANES, 1), 0) < HEAD_DIM
    bounds = sorted({0, r} | {b for rows in part_rows for b in rows})
    segs = list(zip(bounds[:-1], bounds[1:]))

    def covered(seg, rows):
        return rows[0] <= seg[0] and seg[1] <= rows[1]

    outs = []
    for p in range(D_A // LANES):
        qp = q_of(p)
        slabs = kv_slabs(p)
        accs = []
        for half in range(2):
            h = 2 * p + half
            keep = lo if half == 0 else jnp.logical_not(lo)
            keep_t = lo_t if half == 0 else jnp.logical_not(lo_t)
            qh = jnp.where(keep, qp, jnp.zeros_like(qp))
            ss = []
            for j, (kp, _, transposed) in enumerate(slabs):
                qj = qh[part_rows[j][0]:part_rows[j][1]]
                if transposed:
                    s = _dot(qj, kp)
                else:
                    s = lax.dot_general(qj, kp, nt_dims, preferred_element_type=F32)
                s = s + bias_of(h, j)
                if part_ok is not None and part_ok[j] is not None:
                    s = jnp.where(part_ok[j], s, NEG)
                ss.append(s)
            m_segs = [_row_max([s[a - rows[0]:b - rows[0]] for s, rows in zip(ss, part_rows) if covered((a, b), rows)])
                      for a, b in segs]
            acc_segs = [jnp.zeros((b - a, LANES), F32) for a, b in segs]
            for s, (_, vp, transposed), rows in zip(ss, slabs, part_rows):
                m = jnp.concatenate([m_segs[i] for i, seg in enumerate(segs) if covered(seg, rows)], axis=0)
                e = jnp.exp2(s - m).astype(BF16)
                if transposed:
                    vh = jnp.where(keep_t, vp, jnp.ones_like(vp))
                    c = lax.dot_general(e, vh, nt_dims, preferred_element_type=F32)
                else:
                    c = _dot(e, jnp.where(keep, vp, jnp.ones_like(vp)))
                for i, (a, b) in enumerate(segs):
                    if covered((a, b), rows):
                        acc_segs[i] = acc_segs[i] + c[a - rows[0]:b - rows[0]]
            accs.append(jnp.concatenate(acc_segs, axis=0))
        num = jnp.where(lo, accs[0], accs[1])
        den = pltpu.roll(jnp.where(lo, accs[1], accs[0]), HEAD_DIM, 1)
        outs.append(num / den)
    a = jnp.concatenate(outs, axis=1)
    return _rms(a, na)


def _attn_prompt_kernel(zero_fill, q_ref, *refs):
    n_parts = Q_GROUP + 2
    k_refs = refs[:n_parts]
    v_refs = refs[n_parts:2 * n_parts]
    bias_ref, na_ref, o_ref = refs[2 * n_parts:2 * n_parts + 3]
    j = pl.program_id(1)
    if zero_fill:
        z_hbm, zbuf, zsem = refs[2 * n_parts + 3:]
        step = pl.program_id(0) * pl.num_programs(1) + j
        zr = zbuf.shape[0]
        zbuf[...] = jnp.zeros(zbuf.shape, zbuf.dtype)
        z_copies = [pltpu.make_async_copy(zbuf, z_hbm.at[pl.ds((step * zero_fill + c) * zr, zr), :], zsem)
                    for c in range(zero_fill)]
        for cp in z_copies:
            cp.start()
    tiles_of = [[t for t in range(Q_GROUP) if t <= b <= t + 2] for b in range(n_parts)]
    part_rows = [(ts[0] * Q_TILE, (ts[-1] + 1) * Q_TILE) for ts in tiles_of]

    def q_of(p):
        return q_ref[:, p * LANES:(p + 1) * LANES]

    def kv_slabs(p):
        sl = slice(p * LANES, (p + 1) * LANES)
        return [(kr[:, sl], vr[:, sl], False) for kr, vr in zip(k_refs, v_refs)]

    def bias_of(h, b):
        return jnp.concatenate([bias_ref[h, :, (b - t) * Q_TILE:(b - t + 1) * Q_TILE] for t in tiles_of[b]], axis=0)

    def run(part_ok):
        out = _attend(q_of, Q_GROUP * Q_TILE, kv_slabs, bias_of, part_rows, part_ok, na_ref[...])
        o_ref[...] = out.astype(BF16)

    pl.when(j >= 1)(functools.partial(run, None))
    pl.when(j < 1)(functools.partial(run, [j >= 1, j >= 1] + [None] * Q_GROUP))
    if zero_fill:
        for cp in z_copies:
            cp.wait()


def _attn_prompt_call(q, k, v, bias, na, *, n_batch, q_tiles_per_batch, zeros_shape=None):
    nq = q_tiles_per_batch // Q_GROUP

    def kmap(part):
        return lambda b, j: (b * q_tiles_per_batch + jnp.maximum(Q_GROUP * j - 2 + part, 0), 0)

    blk = pl.BlockSpec((Q_GROUP * Q_TILE, D_A), lambda b, j: (b * nq + j, 0))
    kv_specs = [pl.BlockSpec((Q_TILE, D_A), kmap(part)) for part in range(Q_GROUP + 2)]
    out_specs = [blk]
    out_shape = [jax.ShapeDtypeStruct((n_batch * nq * Q_GROUP * Q_TILE, D_A), BF16)]
    scratch = []
    copies = 0
    if zeros_shape is not None:
        rows, cols = zeros_shape
        per_step, rem = divmod(rows, n_batch * nq)
        assert rem == 0 and per_step % 8 == 0
        zr = max(r for r in range(8, ZERO_ROWS_MAX + 1, 8) if per_step % r == 0)
        copies = per_step // zr
        out_specs.append(pl.BlockSpec(memory_space=pl.ANY))
        out_shape.append(jax.ShapeDtypeStruct((rows, cols), F32))
        scratch = [pltpu.VMEM((zr, cols), F32), pltpu.SemaphoreType.DMA(())]
    res = pl.pallas_call(
        functools.partial(_attn_prompt_kernel, copies),
        grid=(n_batch, nq),
        in_specs=[blk] + kv_specs + kv_specs + [
            pl.BlockSpec(bias.shape, lambda b, j: (0, 0, 0)),
            pl.BlockSpec((1, D_A), lambda b, j: (0, 0)),
        ],
        out_specs=out_specs,
        out_shape=out_shape,
        scratch_shapes=scratch,
        compiler_params=_cparams("arbitrary", "arbitrary"),
        name="attn_prompt_zeros" if zeros_shape is not None else "attn_prompt",
    )(q, *([k] * (Q_GROUP + 2)), *([v] * (Q_GROUP + 2)), bias, na)
    return res if zeros_shape is not None else res[0]


def _attn_sample_kernel(q_ref, kn_ref, vn_ref, ck_ref, cv_ref, bias_c_ref, bias_n_ref, na_ref, o_ref):
    def kv_slabs(p):
        sl = slice(p * LANES, (p + 1) * LANES)

        def cached(c_ref):
            return c_ref[2 * p:2 * p + 2].reshape(LANES, c_ref.shape[2]).astype(BF16)
        return [(cached(ck_ref), cached(cv_ref), True), (kn_ref[:, sl], vn_ref[:, sl], False)]

    def bias_of(h, b):
        return bias_c_ref[h] if b == 0 else bias_n_ref[h]

    def q_of(p):
        return q_ref[:, p * LANES:(p + 1) * LANES]

    r = q_ref.shape[0]
    o_ref[...] = _attend(q_of, r, kv_slabs, bias_of, [(0, r), (0, r)], None, na_ref[...]).astype(BF16)


def _attn_sample_call(layer, q, k, v, cache_kt, cache_vt, bias_c, bias_n, na, *, first_chunk):
    _, nb, nh, hd, win = cache_kt.shape
    blk = pl.BlockSpec((CHUNK, D_A), lambda b: (first_chunk + b, 0))
    cache = pl.BlockSpec((None, None, nh, hd, win), lambda b: (layer, b, 0, 0, 0))
    return pl.pallas_call(
        _attn_sample_kernel,
        grid=(nb,),
        in_specs=[blk, blk, blk, cache, cache,
                  pl.BlockSpec(bias_c.shape, lambda b: (0, 0, 0)),
                  pl.BlockSpec(bias_n.shape, lambda b: (0, 0, 0)),
                  pl.BlockSpec((1, D_A), lambda b: (0, 0))],
        out_specs=pl.BlockSpec((CHUNK, D_A), lambda b: (b, 0)),
        out_shape=jax.ShapeDtypeStruct((nb * CHUNK, D_A), BF16),
        compiler_params=_cparams("arbitrary"),
        name="attn_sample",
    )(q, k, v, cache_kt, cache_vt, bias_c, bias_n, na)


def _outproj_kernel(route, ntp, x_ref, anp_ref, ans_ref, bn_ref, w_ref, g1_ref, sc_ref, sh_ref, lg_ref, lb_ref, *rest):
    if route:
        rc_ref, x1_ref, hp_ref, rt_ref = rest
    else:
        x1_ref, hp_ref = rest
    an = jnp.where(pl.program_id(0) >= ntp, ans_ref[...], anp_ref[...])
    mix = _dot(an, w_ref[0:D_A, :]) + _dot(bn_ref[...], w_ref[D_A:D_A + D_B, :])
    x1 = _ln(ALPHA * x_ref[...] + _gate_rows(mix, g1_ref), lg_ref[...], lb_ref[...])
    x1_ref[...] = x1
    h2 = _modulate(x1, sc_ref, sh_ref)
    hp_ref[...] = h2.astype(hp_ref.dtype)
    if route:
        h_hi, h_lo = _split_bf16(h2)
        both = _dot(h_hi, rc_ref[...])
        logits = both[:, :LANES] + both[:, LANES:] + _dot(h_lo, rc_ref[:, :LANES])
        lane = lax.broadcasted_iota(jnp.int32, logits.shape, 1)
        logits = jnp.where(lane < N_EXPERTS, logits, -jnp.inf)
        m1 = logits.max(axis=1, keepdims=True)
        i1 = jnp.where(logits == m1, lane, LANES).min(axis=1, keepdims=True)
        rest_l = jnp.where(lane == i1, -jnp.inf, logits)
        m2 = rest_l.max(axis=1, keepdims=True)
        i2 = jnp.where(rest_l == m2, lane, LANES).min(axis=1, keepdims=True)
        e2 = jnp.exp(m2 - m1)
        w1 = 1.0 / (1.0 + e2)
        w2 = e2 / (1.0 + e2)
        rt = jnp.where(lane == 0, i1.astype(F32),
                       jnp.where(lane == 1, i2.astype(F32),
                                 jnp.where(lane == 2, w1, jnp.where(lane == 3, w2, 0.0))))
        rt_ref[...] = rt


def _outproj_call(layer, route, x, an_p, an_s, bn, w_bf, mods, lg, lb, r_cat=None):
    t, d = x.shape
    nt = t // ROW_TILE
    ntp = an_p.shape[0] // ROW_TILE

    def tile(i):
        return (i, 0)

    def const2(i):
        return (0, 0)

    in_specs = [
        pl.BlockSpec((ROW_TILE, d), tile),
        pl.BlockSpec((ROW_TILE, D_A), lambda i: (jnp.minimum(i, ntp - 1), 0)),
        pl.BlockSpec((ROW_TILE, D_A), lambda i: (jnp.maximum(i - ntp, 0), 0)),
        pl.BlockSpec((ROW_TILE, D_B), tile),
        pl.BlockSpec(w_bf.shape, const2),
        _mod_spec(layer, G1, ROW_TILE, d),
        _mod_spec(layer, SC2, ROW_TILE, d),
        _mod_spec(layer, SH2, ROW_TILE, d),
        pl.BlockSpec((1, d), const2),
        pl.BlockSpec((1, d), const2),
    ]
    out_specs = [pl.BlockSpec((ROW_TILE, d), tile), pl.BlockSpec((ROW_TILE, d), tile)]
    out_shape = [jax.ShapeDtypeStruct((t, d), F32), jax.ShapeDtypeStruct((t, d), F32 if route else BF16)]
    args = [x, an_p, an_s, bn, w_bf, mods, mods, mods, lg, lb]
    if route:
        in_specs.append(pl.BlockSpec(r_cat.shape, const2))
        out_specs.append(pl.BlockSpec((ROW_TILE, LANES), tile))
        out_shape.append(jax.ShapeDtypeStruct((t, LANES), F32))
        args.append(r_cat)
    return pl.pallas_call(
        functools.partial(_outproj_kernel, route, ntp),
        grid=(nt,),
        in_specs=in_specs,
        out_specs=out_specs,
        out_shape=out_shape,
        compiler_params=_cparams("arbitrary"),
        name="out_proj_route" if route else "out_proj",
    )(*args)


def _swiglu_chains(n_sub, xb_ref, wg_ref, wu_ref, wd_ref, o_ref):
    wg = wg_ref[0].astype(BF16)
    wu = wu_ref[0].astype(BF16)
    wd = wd_ref[0].astype(BF16)
    for s in range(n_sub):
        rows = slice(s * FFN_SUB, (s + 1) * FFN_SUB)
        xs = xb_ref[rows, :]
        g = _dot(xs, wg)
        u = _dot(xs, wu)
        a = (g * jax.nn.sigmoid(g) * u).astype(BF16)
        o_ref[rows, :] += _dot(a, wd)


def _ffn_kernel(te_ref, tv_ref, x_ref, wg_ref, wu_ref, wd_ref, o_ref, xb_ref):
    del te_ref
    i = pl.program_id(0)
    j = pl.program_id(1)
    valid = tv_ref[i]
    nsub = (valid + FFN_SUB - 1) // FFN_SUB

    @pl.when(j == 0)
    def _():
        xb_ref[...] = x_ref[...].astype(BF16)
        o_ref[...] = jnp.zeros(o_ref.shape, F32)

    for n_sub in range(1, x_ref.shape[0] // FFN_SUB + 1):
        pl.when(nsub == n_sub)(functools.partial(_swiglu_chains, n_sub, xb_ref, wg_ref, wu_ref, wd_ref, o_ref))


def _ffn_call(tile_expert, tile_valid, xs, wg, wu, wd, tile_rows):
    p, d = xs.shape
    f = wg.shape[2]
    nt = p // tile_rows
    nc = f // FFN_COLS

    def active_col(i, j, tv):
        return jnp.where(tv[i] > 0, j, nc - 1)

    in_specs = [
        pl.BlockSpec((tile_rows, d), lambda i, j, te, tv: (i, 0)),
        pl.BlockSpec((1, d, FFN_COLS), lambda i, j, te, tv: (te[i], 0, active_col(i, j, tv))),
        pl.BlockSpec((1, d, FFN_COLS), lambda i, j, te, tv: (te[i], 0, active_col(i, j, tv))),
        pl.BlockSpec((1, FFN_COLS, d), lambda i, j, te, tv: (te[i], active_col(i, j, tv), 0)),
    ]
    return pl.pallas_call(
        _ffn_kernel,
        grid_spec=pltpu.PrefetchScalarGridSpec(
            num_scalar_prefetch=2,
            grid=(nt, nc),
            in_specs=in_specs,
            out_specs=pl.BlockSpec((tile_rows, d), lambda i, j, te, tv: (i, 0)),
            scratch_shapes=[pltpu.VMEM((tile_rows, d), BF16)],
        ),
        out_shape=jax.ShapeDtypeStruct((p, d), F32),
        compiler_params=_cparams("arbitrary", "arbitrary"),
        name="swiglu_grouped",
    )(tile_expert, tile_valid, xs, wg, wu, wd)


def _dense_ffn_kernel(x_ref, wg_ref, wu_ref, wd_ref, x1_ref, g2_ref, lg_ref, lb_ref, o_ref):
    x = x_ref[...]
    f = wg_ref.shape[1]
    acc = None
    for c0 in range(0, f, FFN_SPLIT):
        c1 = min(c0 + FFN_SPLIT, f)
        g = _dot(x, wg_ref[:, c0:c1])
        u = _dot(x, wu_ref[:, c0:c1])
        a = (g * jax.nn.sigmoid(g) * u).astype(BF16)
        part = _dot(a, wd_ref[c0:c1, :])
        acc = part if acc is None else acc + part
    o_ref[...] = _ln(ALPHA * x1_ref[...] + _gate_rows(acc, g2_ref), lg_ref[...], lb_ref[...])


def _dense_ffn_call(layer, hp, wg_bf, wu_bf, wd_bf, x1, mods, lg, lb):
    t, d = x1.shape
    f = wg_bf.shape[1]
    tile = pl.BlockSpec((ROW_TILE, d), lambda i: (i, 0))
    vec = pl.BlockSpec((1, d), lambda i: (0, 0))

    def resident(shape):
        return pl.BlockSpec(shape, lambda i: (0, 0), pipeline_mode=pl.Buffered(1))

    return pl.pallas_call(
        _dense_ffn_kernel,
        grid=(t // ROW_TILE,),
        in_specs=[tile, resident((d, f)), resident((d, f)), resident((f, d)), tile,
                  _mod_spec(layer, G2, ROW_TILE, d), vec, vec],
        out_specs=tile,
        out_shape=jax.ShapeDtypeStruct((t, d), F32),
        compiler_params=_cparams("arbitrary"),
        name="swiglu_dense",
    )(hp, wg_bf, wu_bf, wd_bf, x1, mods, lg, lb)


def _dispatch_kernel(pos_ref, src_ref, init_ref, dst_ref, sem):
    del init_ref

    def issue(r, carry):
        row = src_ref.at[pl.ds(r, 1), :]
        pltpu.make_async_copy(row, dst_ref.at[pl.ds(pos_ref[0, 2 * r], 1), :], sem).start()
        pltpu.make_async_copy(row, dst_ref.at[pl.ds(pos_ref[0, 2 * r + 1], 1), :], sem).start()
        return carry
    lax.fori_loop(0, ROW_TILE, issue, 0, unroll=8)
    for _ in range(2):
        pltpu.make_async_copy(src_ref, dst_ref.at[pl.ds(0, ROW_TILE), :], sem).wait()


def _dispatch_call(pos_tiles, hp, zeros):
    t, d = hp.shape
    n_rows_sorted = zeros.shape[0]
    nt = t // ROW_TILE
    return pl.pallas_call(
        _dispatch_kernel,
        grid=(nt,),
        in_specs=[
            pl.BlockSpec((None, 1, 2 * ROW_TILE), lambda i: (i, 0, 0), memory_space=pltpu.SMEM),
            pl.BlockSpec((ROW_TILE, d), lambda i: (i, 0)),
            pl.BlockSpec(memory_space=pl.ANY),
        ],
        out_specs=pl.BlockSpec(memory_space=pl.ANY),
        out_shape=jax.ShapeDtypeStruct((n_rows_sorted, d), hp.dtype),
        scratch_shapes=[pltpu.SemaphoreType.DMA(())],
        input_output_aliases={2: 0},
        compiler_params=_cparams("arbitrary"),
        name="dispatch_rows",
    )(pos_tiles, hp, zeros)


def _final_moe_kernel(ntp, pos_ref, x_ref, rt_ref, g2_ref, lg_ref, lb_ref, ys_ref, op_ref, os_ref, ybuf, sem):
    def issue(r, carry):
        pltpu.make_async_copy(ys_ref.at[pl.ds(pos_ref[0, 2 * r], 1), :], ybuf.at[0, pl.ds(r, 1), :], sem).start()
        pltpu.make_async_copy(ys_ref.at[pl.ds(pos_ref[0, 2 * r + 1], 1), :], ybuf.at[1, pl.ds(r, 1), :], sem).start()
        return carry
    lax.fori_loop(0, ROW_TILE, issue, 0, unroll=8)
    for k in range(2):
        pltpu.make_async_copy(ys_ref.at[pl.ds(0, ROW_TILE), :], ybuf.at[k], sem).wait()
    rt = rt_ref[...]
    f = rt[:, 2:3] * ybuf[0] + rt[:, 3:4] * ybuf[1]
    y = _ln(ALPHA * x_ref[...] + _gate_rows(f, g2_ref), lg_ref[...], lb_ref[...])
    i = pl.program_id(0)

    @pl.when(i < ntp)
    def _():
        op_ref[...] = y

    @pl.when(i >= ntp)
    def _():
        os_ref[...] = y


def _final_moe_call(layer, pos_tiles, x1, rt, mods, lg, lb, ys, *, n_prompt_tiles):
    t, d = x1.shape
    ntp = n_prompt_tiles
    tile = pl.BlockSpec((ROW_TILE, d), lambda i: (i, 0))
    vec = pl.BlockSpec((1, d), lambda i: (0, 0))
    return pl.pallas_call(
        functools.partial(_final_moe_kernel, ntp),
        grid=(t // ROW_TILE,),
        in_specs=[
            pl.BlockSpec((None, 1, 2 * ROW_TILE), lambda i: (i, 0, 0), memory_space=pltpu.SMEM),
            tile,
            pl.BlockSpec((ROW_TILE, LANES), lambda i: (i, 0)),
            _mod_spec(layer, G2, ROW_TILE, d),
            vec, vec,
            pl.BlockSpec(memory_space=pl.ANY),
        ],
        out_specs=[pl.BlockSpec((ROW_TILE, d), lambda i: (jnp.minimum(i, ntp - 1), 0)),
                   pl.BlockSpec((ROW_TILE, d), lambda i: (jnp.maximum(i - ntp, 0), 0))],
        out_shape=[jax.ShapeDtypeStruct((ntp * ROW_TILE, d), F32),
                   jax.ShapeDtypeStruct((t - ntp * ROW_TILE, d), F32)],
        scratch_shapes=[pltpu.VMEM((2, ROW_TILE, d), F32), pltpu.SemaphoreType.DMA(())],
        compiler_params=_cparams("arbitrary"),
        name="final_moe",
    )(pos_tiles, x1, rt, mods, lg, lb, ys)


def _rel_bias(table, n_q, n_k, banded):
    r = jnp.arange(n_q)[:, None]
    w = jnp.arange(n_k)[None, :]
    n_diag = n_q + n_k - 1
    k = jnp.arange(n_diag)
    diag = table[:, jnp.clip(LEFT_CTX + (n_q - 1) - k, -REL_CLIP, REL_CLIP) + REL_CLIP].astype(F32) * LOG2E
    padded = jnp.concatenate([diag, jnp.zeros((diag.shape[0], 1), F32)], axis=1)
    skew = jnp.tile(padded, (1, n_q))[:, :n_q * n_diag].reshape(-1, n_q, n_diag)
    bias = skew[:, :, n_q - 1:n_q - 1 + n_k]
    if banded:
        qa = r // CHUNK
        kc = w // CHUNK
        vis = (kc >= qa) & (kc <= qa + LEFT_CHUNKS)
        bias = jnp.where(vis[None], bias, NEG)
    return bias


def _gating_weights(w_s, b_s):
    n = MLP_CHUNK
    tril = jnp.tril(jnp.ones((n, n), bool))
    wm = jnp.where(tril[None], w_s, 0.0)
    h = CHUNK
    top = wm[:, :h, :h]
    z = jnp.zeros_like(top)
    wm_s = jnp.concatenate([jnp.concatenate([top, z], 2), jnp.concatenate([z, top], 2)], 1)
    both = jnp.stack([wm, wm_s])
    wcat = jnp.concatenate([both[:, 0::2], both[:, 1::2]], axis=-1)
    bias_p = jnp.repeat(jnp.transpose(b_s), GROUP_DIM, axis=1)
    bias_s = jnp.concatenate([bias_p[:h], bias_p[:h]], 0)
    return wcat.astype(BF16), jnp.stack([bias_p, bias_s]).astype(F32)


def _route_plan(rt, n_tiles, tile_rows):
    t = rt.shape[0]
    e_flat = rt[:, 0:2].astype(jnp.int32).reshape(-1)
    oh = (e_flat[:, None] == jnp.arange(N_EXPERTS)[None, :]).astype(jnp.int32)
    csum = jnp.cumsum(oh, axis=0)
    rank = jnp.sum(oh * (csum - 1), axis=1)
    counts = csum[-1]
    tiles_e = (counts + tile_rows - 1) // tile_rows
    tile_end = jnp.cumsum(tiles_e)
    tile_start = tile_end - tiles_e
    pos = (tile_start * tile_rows)[e_flat] + rank
    ti = jnp.arange(n_tiles)
    te = jnp.minimum(jnp.sum((ti[:, None] >= tile_end[None, :]).astype(jnp.int32), axis=1), N_EXPERTS - 1)
    tv = jnp.clip(counts[te] - (ti - tile_start[te]) * tile_rows, 0, tile_rows)
    tv = jnp.where(ti < tile_end[-1], tv, 0)
    last_e = te[jnp.maximum(tile_end[-1] - 1, 0)]
    te = jnp.where(ti < tile_end[-1], te, last_e)
    return pos.reshape(t // ROW_TILE, 1, 2 * ROW_TILE).astype(jnp.int32), te.astype(jnp.int32), tv.astype(jnp.int32)


def kernel(x_prompt, x_sample, cache_k, cache_v, c_prompt, c_sample, ln_in_g, ln_in_b, w_in, w_out, rel_bias_table, gmlp_ln_g, gmlp_ln_b, gmlp_w_s, gmlp_b_s, out_norm_a, out_norm_b, ada_w, ada_b, ln1_g, ln1_b, ln2_g, ln2_b, ffn_w_gate, ffn_w_up, ffn_w_down, moe_router, moe_w_gate, moe_w_up, moe_w_down):
    nb, seq, d = x_prompt.shape
    ns, dseq, _ = x_sample.shape
    depth = w_in.shape[0]
    tp = nb * seq
    ts = ns * dseq
    t = tp + ts
    assert depth == DEPTH and dseq == CHUNK and seq % ROW_TILE == 0 and ts % ROW_TILE == 0
    assert seq % (Q_GROUP * Q_TILE) == 0 and cache_k.shape[2] == LEFT_CTX
    ntp = tp // ROW_TILE

    n_cond = nb + ns
    c_all = jnp.concatenate([c_prompt, c_sample], axis=0)
    c_pad = jnp.pad(c_all, ((0, (-n_cond) % 8), (0, 0)))
    mods = _ada_call(c_pad, ada_w, ada_b)
    mods = jnp.transpose(mods[:, :n_cond].reshape(depth, n_cond, 6, d), (0, 2, 1, 3))
    cpb = seq // CHUNK
    mods_p = jnp.broadcast_to(mods[:, :, :nb, None, :], (depth, 6, nb, cpb, d)).reshape(depth, 6, nb * cpb, d)
    mods = jnp.concatenate([mods_p, mods[:, :, nb:]], axis=2)

    def row(v):
        return v.reshape(1, -1)

    w_in_bf = w_in.astype(BF16)
    w_out_bf = w_out.astype(BF16)
    cache_kt = jnp.transpose(cache_k, (0, 1, 3, 4, 2))
    cache_vt = jnp.transpose(cache_v, (0, 1, 3, 4, 2))
    k_tail, v_tail, gv_rows = [], [], []
    for l in range(depth):
        wcat, bs = _gating_weights(gmlp_w_s[l], gmlp_b_s[l])
        if l == 0:
            x_in = (x_prompt.reshape(tp, d), x_sample.reshape(ts, d), row(ln_in_g), row(ln_in_b))
        else:
            x_in = (x,)
        res = _inproj_call(l, x_in, mods, w_in_bf[l], row(gmlp_ln_g[l]), row(gmlp_ln_b[l]), wcat, bs,
                           row(out_norm_b[l]),
                           n_prompt_tiles=ntp, tiles_per_batch=seq // ROW_TILE, n_batch=nb, n_sample_rows=ts)
        if l == 0:
            x, q, k, v, kf, vf, gv, bn = res
        else:
            q, k, v, kf, vf, gv, bn = res
        k_tail.append(kf)
        v_tail.append(vf)
        gv_rows.append(gv)

        table = rel_bias_table[l]
        bias_p = _rel_bias(table, Q_TILE, 3 * Q_TILE, True)
        bias_s = _rel_bias(table, CHUNK, LEFT_CTX + CHUNK, False)
        na = row(out_norm_a[l])
        routed = l % 2 == 1
        n_tiles = -(-2 * t // MOE_TILE) + N_EXPERTS
        an_p = _attn_prompt_call(q, k, v, bias_p, na, n_batch=nb, q_tiles_per_batch=seq // Q_TILE,
                                 zeros_shape=(n_tiles * MOE_TILE, d) if routed else None)
        if routed:
            an_p, sorted_init = an_p
        an_s = _attn_sample_call(l, q, k, v, cache_kt, cache_vt, bias_s[:, :, :LEFT_CTX], bias_s[:, :, LEFT_CTX:], na,
                                 first_chunk=tp // CHUNK)

        lg1, lb1, lg2, lb2 = row(ln1_g[l]), row(ln1_b[l]), row(ln2_g[l]), row(ln2_b[l])
        i = l // 2
        if not routed:
            x1, hp = _outproj_call(l, False, x, an_p, an_s, bn, w_out_bf[l], mods, lg1, lb1)
            x = _dense_ffn_call(l, hp, ffn_w_gate[i].astype(BF16), ffn_w_up[i].astype(BF16),
                                ffn_w_down[i].astype(BF16), x1, mods, lg2, lb2)
        else:
            r_pad = jnp.pad(moe_router[i], ((0, 0), (0, LANES - N_EXPERTS)))
            r_hi = r_pad.astype(BF16)
            r_lo = (r_pad - r_hi.astype(F32)).astype(BF16)
            r_cat = jnp.concatenate([r_hi, r_lo], axis=1)
            x1, hp, rt = _outproj_call(l, True, x, an_p, an_s, bn, w_out_bf[l], mods, lg1, lb1, r_cat)
            pos, te, tv = _route_plan(rt, n_tiles, MOE_TILE)
            xs = _dispatch_call(pos, hp, sorted_init)
            ys = _ffn_call(te, tv, xs, moe_w_gate[i], moe_w_up[i], moe_w_down[i], MOE_TILE)
            y_p, y_s = _final_moe_call(l, pos, x1, rt, mods, lg2, lb2, ys, n_prompt_tiles=ntp)

    y_prompt = y_p.reshape(nb, seq, d)
    y_sample = y_s.reshape(ns, dseq, d)

    def tails(rows):
        kp = jnp.stack([r[:nb * ROW_TILE].reshape(nb, ROW_TILE, N_HEADS, HEAD_DIM) for r in rows])
        ksn = jnp.stack([r[nb * ROW_TILE:].reshape(ns, dseq, N_HEADS, HEAD_DIM) for r in rows])
        return kp, ksn

    k_prompt_new, k_sample_new = tails(k_tail)
    v_prompt_new, v_sample_new = tails(v_tail)
    gmlp_v_sample_new = jnp.stack([g.reshape(ns, dseq, D_B) for g in gv_rows])
    return (y_prompt, y_sample, k_prompt_new, v_prompt_new, k_sample_new, v_sample_new, gmlp_v_sample_new)
```

```python
import functools

import jax
import jax.numpy as jnp
from jax import lax
from jax.experimental import pallas as pl
from jax.experimental.pallas import tpu as pltpu

CHUNK = 64
LEFT_CHUNKS = 8
LEFT_CTX = LEFT_CHUNKS * CHUNK
N_HEADS = 8
HEAD_DIM = 64
D_A = N_HEADS * HEAD_DIM
N_GROUPS = 8
GROUP_DIM = 64
D_B = N_GROUPS * GROUP_DIM
MLP_CHUNK = 128
REL_CLIP = 128
N_EXPERTS = 8
DEPTH = 2
ALPHA = (2 * DEPTH) ** 0.25
LN_EPS = 1e-5
ATTN_SCALE = HEAD_DIM ** -0.5
LOG2E = 1.4426950408889634
NEG = -1e30
SH1, SC1, G1, SH2, SC2, G2 = range(6)

LANES = 128
ROW_TILE = 512
IN_PART = 512
Q_TILE = 256
Q_GROUP = 4
ZERO_ROWS_MAX = 1024
FFN_SPLIT = 1536
MOE_TILE = 2048
FFN_SUB = 512
FFN_COLS = 256
VMEM_LIMIT = 56 * 1024 * 1024

BF16 = jnp.bfloat16
F32 = jnp.float32


def _cparams(*sem):
    return pltpu.CompilerParams(dimension_semantics=sem, vmem_limit_bytes=VMEM_LIMIT)


def _ln(x, g, b):
    mu = jnp.mean(x, axis=-1, keepdims=True)
    xc = x - mu
    var = jnp.mean(xc * xc, axis=-1, keepdims=True)
    return xc * lax.rsqrt(var + LN_EPS) * g + b


def _rms(x, g):
    return x * lax.rsqrt(jnp.mean(x * x, axis=-1, keepdims=True) + LN_EPS) * g


def _modulate(x, sc_ref, sh_ref):
    parts = []
    for c in range(x.shape[0] // CHUNK):
        xc = x[c * CHUNK:(c + 1) * CHUNK]
        parts.append(xc * (1.0 + sc_ref[c:c + 1, :]) + sh_ref[c:c + 1, :])
    return jnp.concatenate(parts, axis=0)


def _gate_rows(x, g_ref):
    parts = []
    for c in range(x.shape[0] // CHUNK):
        parts.append(x[c * CHUNK:(c + 1) * CHUNK] * (1.0 + g_ref[c:c + 1, :]))
    return jnp.concatenate(parts, axis=0)


def _mod_spec(layer, comp, rows, d):
    return pl.BlockSpec((None, None, rows // CHUNK, d), lambda i, *_: (layer, comp, i, 0))


def _split_bf16(x):
    hi = x.astype(BF16)
    lo = (x - hi.astype(F32)).astype(BF16)
    return hi, lo


def _dot(a, b):
    return jnp.dot(a, b, preferred_element_type=F32)


def _ada_kernel(c_ref, w_ref, b_ref, o_ref):
    c = c_ref[...]
    s = c * jax.nn.sigmoid(c)
    s_hi, s_lo = _split_bf16(s)
    w_hi, w_lo = _split_bf16(w_ref[0])
    o_ref[0] = _dot(s_hi, w_hi) + _dot(s_lo, w_hi) + _dot(s_hi, w_lo) + b_ref[0]


def _ada_call(c_pad, ada_w, ada_b):
    depth, d, n = ada_w.shape
    rows = c_pad.shape[0]
    tn = n // 4
    return pl.pallas_call(
        _ada_kernel,
        grid=(depth, n // tn),
        in_specs=[
            pl.BlockSpec((rows, d), lambda l, j: (0, 0)),
            pl.BlockSpec((1, d, tn), lambda l, j: (l, 0, j)),
            pl.BlockSpec((1, 1, tn), lambda l, j: (l, 0, j)),
        ],
        out_specs=pl.BlockSpec((1, rows, tn), lambda l, j: (l, 0, j)),
        out_shape=jax.ShapeDtypeStruct((depth, rows, n), F32),
        compiler_params=_cparams("arbitrary", "arbitrary"),
        name="ada_mod",
    )(c_pad, ada_w, ada_b.reshape(depth, 1, n))


def _inproj_kernel(first, ntp, *refs):
    if first:
        xp_ref, xs_ref, lng_ref, lnb_ref = refs[:4]
        xn_ref = refs[-8]
        refs = refs[4:-8] + refs[-7:]
    else:
        x_ref = refs[0]
        refs = refs[1:]
    (sc_ref, sh_ref, w_ref, gg_ref, gb_ref, wcat_ref, bs_ref, nb_ref,
     q_ref, k_ref, v_ref, kf_ref, vf_ref, gv_ref, bn_ref) = refs
    lane = lax.broadcasted_iota(jnp.int32, (MLP_CHUNK, LANES), 1)
    lo = lane < GROUP_DIM
    zero = jnp.zeros((MLP_CHUNK, LANES), BF16)

    def part(r0, n_rows):
        rows = slice(r0, r0 + n_rows)
        mod_rows = slice(r0 // CHUNK, (r0 + n_rows) // CHUNK)
        if first:
            x = jnp.where(pl.program_id(0) >= ntp, xs_ref[rows, :], xp_ref[rows, :])
            x = _ln(x, lng_ref[...], lnb_ref[...])
            xn_ref[rows, :] = x
        else:
            x = x_ref[rows, :]
        h = _modulate(x, sc_ref.at[mod_rows, :], sh_ref.at[mod_rows, :]).astype(BF16)

        q = _dot(h, w_ref[:, 0:D_A])
        q_ref[rows, :] = (q * (ATTN_SCALE * LOG2E)).astype(BF16)
        k = _dot(h, w_ref[:, D_A:2 * D_A])
        k_ref[rows, :] = k.astype(BF16)
        kf_ref[rows, :] = k
        v = _dot(h, w_ref[:, 2 * D_A:3 * D_A])
        v_ref[rows, :] = v.astype(BF16)
        vf_ref[rows, :] = v

        u = jax.nn.gelu(_dot(h, w_ref[:, 3 * D_A:3 * D_A + D_B]), approximate=True)
        vb = jax.nn.gelu(_dot(h, w_ref[:, 3 * D_A + D_B:3 * D_A + 2 * D_B]), approximate=True)
        vn = _ln(vb, gg_ref[...], gb_ref[...])
        gv_ref[rows, :] = vn

        vnb = vn.astype(BF16)
        n_chunks = n_rows // MLP_CHUNK
        mixed = []
        for p in range(D_B // LANES):
            stacks = []
            for c in range(n_chunks):
                slab = vnb[c * MLP_CHUNK:(c + 1) * MLP_CHUNK, p * LANES:(p + 1) * LANES]
                stacks.append(jnp.concatenate([jnp.where(lo, slab, zero), jnp.where(lo, zero, slab)], axis=0))
            mixed.append(_dot(wcat_ref[0, p], jnp.concatenate(stacks, axis=1)))
        outs = []
        for c in range(n_chunks):
            c0 = c * MLP_CHUNK
            mix_c = jnp.concatenate([m[:, c * LANES:(c + 1) * LANES] for m in mixed], axis=1) + bs_ref[0]
            outs.append(u[c0:c0 + MLP_CHUNK, :] * mix_c)
        bo = jnp.concatenate(outs, axis=0)
        bn_ref[rows, :] = _rms(bo, nb_ref[...]).astype(BF16)

    for r0 in range(0, ROW_TILE, IN_PART):
        part(r0, IN_PART)


def _inproj_call(layer, x_in, mods, w_bf, gg, gb, wcat, bs, nb, *, n_prompt_tiles, tiles_per_batch, n_batch,
                 n_sample_rows):
    first = len(x_in) == 4
    d = x_in[0].shape[1]
    t = mods.shape[2] * CHUNK
    nt = t // ROW_TILE
    ntp = n_prompt_tiles
    n_tail = n_batch + (nt - ntp)

    def tile(i):
        return (i, 0)

    def const2(i):
        return (0, 0)

    def selmap4(i):
        return (jnp.where(i >= ntp, 1, 0), 0, 0, 0)

    def selmap3(i):
        return (jnp.where(i >= ntp, 1, 0), 0, 0)

    def tailmap(i):
        return (jnp.where(i < ntp, i // tiles_per_batch, n_batch + i - ntp), 0)

    def gvmap(i):
        return (jnp.maximum(i - ntp, 0), 0)

    if first:
        x_specs = [
            pl.BlockSpec((ROW_TILE, d), lambda i: (jnp.minimum(i, ntp - 1), 0)),
            pl.BlockSpec((ROW_TILE, d), lambda i: (jnp.maximum(i - ntp, 0), 0)),
            pl.BlockSpec((1, d), const2),
            pl.BlockSpec((1, d), const2),
        ]
    else:
        x_specs = [pl.BlockSpec((ROW_TILE, d), tile)]
    in_specs = x_specs + [
        _mod_spec(layer, SC1, ROW_TILE, d),
        _mod_spec(layer, SH1, ROW_TILE, d),
        pl.BlockSpec(w_bf.shape, const2),
        pl.BlockSpec((1, D_B), const2),
        pl.BlockSpec((1, D_B), const2),
        pl.BlockSpec((1,) + wcat.shape[1:], selmap4),
        pl.BlockSpec((1, MLP_CHUNK, D_B), selmap3),
        pl.BlockSpec((1, D_B), const2),
    ]
    out_specs = [
        pl.BlockSpec((ROW_TILE, D_A), tile),
        pl.BlockSpec((ROW_TILE, D_A), tile),
        pl.BlockSpec((ROW_TILE, D_A), tile),
        pl.BlockSpec((ROW_TILE, D_A), tailmap),
        pl.BlockSpec((ROW_TILE, D_A), tailmap),
        pl.BlockSpec((ROW_TILE, D_B), gvmap),
        pl.BlockSpec((ROW_TILE, D_B), tile),
    ]
    out_shape = [
        jax.ShapeDtypeStruct((t, D_A), BF16),
        jax.ShapeDtypeStruct((t, D_A), BF16),
        jax.ShapeDtypeStruct((t, D_A), BF16),
        jax.ShapeDtypeStruct((n_tail * ROW_TILE, D_A), F32),
        jax.ShapeDtypeStruct((n_tail * ROW_TILE, D_A), F32),
        jax.ShapeDtypeStruct((n_sample_rows, D_B), F32),
        jax.ShapeDtypeStruct((t, D_B), BF16),
    ]
    if first:
        out_specs = [pl.BlockSpec((ROW_TILE, d), tile)] + out_specs
        out_shape = [jax.ShapeDtypeStruct((t, d), F32)] + out_shape
    return pl.pallas_call(
        functools.partial(_inproj_kernel, first, ntp),
        grid=(nt,),
        in_specs=in_specs,
        out_specs=out_specs,
        out_shape=out_shape,
        compiler_params=_cparams("arbitrary"),
        name="in_proj_first" if first else "in_proj",
    )(*x_in, mods, mods, w_bf, gg, gb, wcat, bs, nb)


def _row_max(pieces):
    folded = None
    m = None
    for s in pieces:
        if s.shape[1] % LANES == 0:
            for c0 in range(0, s.shape[1], LANES):
                piece = s[:, c0:c0 + LANES]
                folded = piece if folded is None else jnp.maximum(folded, piece)
        else:
            mj = s.max(axis=1, keepdims=True)
            m = mj if m is None else jnp.maximum(m, mj)
    if folded is not None:
        mj = folded.max(axis=1, keepdims=True)
        m = mj if m is None else jnp.maximum(m, mj)
    return m


def _attend(q_of, r, kv_slabs, bias_of, part_rows, part_ok, na):
    nt_dims = (((1,), (1,)), ((), ()))
    lo = lax.broadcasted_iota(jnp.int32, (1, LANES), 1) < HEAD_DIM
    lo_t = lax.broadcasted_iota(jnp.int32, (LANES, 1), 0) < HEAD_DIM
    bounds = sorted({0, r} | {b for rows in part_rows for b in rows})
    segs = list(zip(bounds[:-1], bounds[1:]))

    def covered(seg, rows):
        return rows[0] <= seg[0] and seg[1] <= rows[1]

    outs = []
    for p in range(D_A // LANES):
        qp = q_of(p)
        slabs = kv_slabs(p)
        accs = []
        for half in range(2):
            h = 2 * p + half
            keep = lo if half == 0 else jnp.logical_not(lo)
            keep_t = lo_t if half == 0 else jnp.logical_not(lo_t)
            qh = jnp.where(keep, qp, jnp.zeros_like(qp))
            ss = []
            for j, (kp, _, transposed) in enumerate(slabs):
                qj = qh[part_rows[j][0]:part_rows[j][1]]
                if transposed:
                    s = _dot(qj, kp)
                else:
                    s = lax.dot_general(qj, kp, nt_dims, preferred_element_type=F32)
                s = s + bias_of(h, j)
                if part_ok is not None and part_ok[j] is not None:
                    s = jnp.where(part_ok[j], s, NEG)
                ss.append(s)
            m_segs = [_row_max([s[a - rows[0]:b - rows[0]] for s, rows in zip(ss, part_rows) if covered((a, b), rows)])
                      for a, b in segs]
            acc_segs = [jnp.zeros((b - a, LANES), F32) for a, b in segs]
            for s, (_, vp, transposed), rows in zip(ss, slabs, part_rows):
                m = jnp.concatenate([m_segs[i] for i, seg in enumerate(segs) if covered(seg, rows)], axis=0)
                e = jnp.exp2(s - m).astype(BF16)
                if transposed:
                    vh = jnp.where(keep_t, vp, jnp.ones_like(vp))
                    c = lax.dot_general(e, vh, nt_dims, preferred_element_type=F32)
                else:
                    c = _dot(e, jnp.where(keep, vp, jnp.ones_like(vp)))
                for i, (a, b) in enumerate(segs):
                    if covered((a, b), rows):
                        acc_segs[i] = acc_segs[i] + c[a - rows[0]:b - rows[0]]
            accs.append(jnp.concatenate(acc_segs, axis=0))
        num = jnp.where(lo, accs[0], accs[1])
        den = pltpu.roll(jnp.where(lo, accs[1], accs[0]), HEAD_DIM, 1)
        outs.append(num / den)
    a = jnp.concatenate(outs, axis=1)
    return _rms(a, na)


def _attn_prompt_kernel(zero_fill, q_ref, *refs):
    n_parts = Q_GROUP + 2
    k_refs = refs[:n_parts]
    v_refs = refs[n_parts:2 * n_parts]
    bias_ref, na_ref, o_ref = refs[2 * n_parts:2 * n_parts + 3]
    j = pl.program_id(1)
    if zero_fill:
        z_hbm, zbuf, zsem = refs[2 * n_parts + 3:]
        step = pl.program_id(0) * pl.num_programs(1) + j
        zr = zbuf.shape[0]
        zbuf[...] = jnp.zeros(zbuf.shape, zbuf.dtype)
        z_copies = [pltpu.make_async_copy(zbuf, z_hbm.at[pl.ds((step * zero_fill + c) * zr, zr), :], zsem)
                    for c in range(zero_fill)]
        for cp in z_copies:
            cp.start()
    tiles_of = [[t for t in range(Q_GROUP) if t <= b <= t + 2] for b in range(n_parts)]
    part_rows = [(ts[0] * Q_TILE, (ts[-1] + 1) * Q_TILE) for ts in tiles_of]

    def q_of(p):
        return q_ref[:, p * LANES:(p + 1) * LANES]

    def kv_slabs(p):
        sl = slice(p * LANES, (p + 1) * LANES)
        return [(kr[:, sl], vr[:, sl], False) for kr, vr in zip(k_refs, v_refs)]

    def bias_of(h, b):
        return jnp.concatenate([bias_ref[h, :, (b - t) * Q_TILE:(b - t + 1) * Q_TILE] for t in tiles_of[b]], axis=0)

    def run(part_ok):
        out = _attend(q_of, Q_GROUP * Q_TILE, kv_slabs, bias_of, part_rows, part_ok, na_ref[...])
        o_ref[...] = out.astype(BF16)

    pl.when(j >= 1)(functools.partial(run, None))
    pl.when(j < 1)(functools.partial(run, [j >= 1, j >= 1] + [None] * Q_GROUP))
    if zero_fill:
        for cp in z_copies:
            cp.wait()


def _attn_prompt_call(q, k, v, bias, na, *, n_batch, q_tiles_per_batch, zeros_shape=None):
    nq = q_tiles_per_batch // Q_GROUP

    def kmap(part):
        return lambda b, j: (b * q_tiles_per_batch + jnp.maximum(Q_GROUP * j - 2 + part, 0), 0)

    blk = pl.BlockSpec((Q_GROUP * Q_TILE, D_A), lambda b, j: (b * nq + j, 0))
    kv_specs = [pl.BlockSpec((Q_TILE, D_A), kmap(part)) for part in range(Q_GROUP + 2)]
    out_specs = [blk]
    out_shape = [jax.ShapeDtypeStruct((n_batch * nq * Q_GROUP * Q_TILE, D_A), BF16)]
    scratch = []
    copies = 0
    if zeros_shape is not None:
        rows, cols = zeros_shape
        per_step, rem = divmod(rows, n_batch * nq)
        assert rem == 0 and per_step % 8 == 0
        zr = max(r for r in range(8, ZERO_ROWS_MAX + 1, 8) if per_step % r == 0)
        copies = per_step // zr
        out_specs.append(pl.BlockSpec(memory_space=pl.ANY))
        out_shape.append(jax.ShapeDtypeStruct((rows, cols), F32))
        scratch = [pltpu.VMEM((zr, cols), F32), pltpu.SemaphoreType.DMA(())]
    res = pl.pallas_call(
        functools.partial(_attn_prompt_kernel, copies),
        grid=(n_batch, nq),
        in_specs=[blk] + kv_specs + kv_specs + [
            pl.BlockSpec(bias.shape, lambda b, j: (0, 0, 0)),
            pl.BlockSpec((1, D_A), lambda b, j: (0, 0)),
        ],
        out_specs=out_specs,
        out_shape=out_shape,
        scratch_shapes=scratch,
        compiler_params=_cparams("arbitrary", "arbitrary"),
        name="attn_prompt_zeros" if zeros_shape is not None else "attn_prompt",
    )(q, *([k] * (Q_GROUP + 2)), *([v] * (Q_GROUP + 2)), bias, na)
    return res if zeros_shape is not None else res[0]


def _attn_sample_kernel(q_ref, kn_ref, vn_ref, ck_ref, cv_ref, bias_c_ref, bias_n_ref, na_ref, o_ref):
    def kv_slabs(p):
        sl = slice(p * LANES, (p + 1) * LANES)

        def cached(c_ref):
            return c_ref[2 * p:2 * p + 2].reshape(LANES, c_ref.shape[2]).astype(BF16)
        return [(cached(ck_ref), cached(cv_ref), True), (kn_ref[:, sl], vn_ref[:, sl], False)]

    def bias_of(h, b):
        return bias_c_ref[h] if b == 0 else bias_n_ref[h]

    def q_of(p):
        return q_ref[:, p * LANES:(p + 1) * LANES]

    r = q_ref.shape[0]
    o_ref[...] = _attend(q_of, r, kv_slabs, bias_of, [(0, r), (0, r)], None, na_ref[...]).astype(BF16)


def _attn_sample_call(layer, q, k, v, cache_kt, cache_vt, bias_c, bias_n, na, *, first_chunk):
    _, nb, nh, hd, win = cache_kt.shape
    blk = pl.BlockSpec((CHUNK, D_A), lambda b: (first_chunk + b, 0))
    cache = pl.BlockSpec((None, None, nh, hd, win), lambda b: (layer, b, 0, 0, 0))
    return pl.pallas_call(
        _attn_sample_kernel,
        grid=(nb,),
        in_specs=[blk, blk, blk, cache, cache,
                  pl.BlockSpec(bias_c.shape, lambda b: (0, 0, 0)),
                  pl.BlockSpec(bias_n.shape, lambda b: (0, 0, 0)),
                  pl.BlockSpec((1, D_A), lambda b: (0, 0))],
        out_specs=pl.BlockSpec((CHUNK, D_A), lambda b: (b, 0)),
        out_shape=jax.ShapeDtypeStruct((nb * CHUNK, D_A), BF16),
        compiler_params=_cparams("arbitrary"),
        name="attn_sample",
    )(q, k, v, cache_kt, cache_vt, bias_c, bias_n, na)


def _outproj_kernel(route, ntp, x_ref, anp_ref, ans_ref, bn_ref, w_ref, g1_ref, sc_ref, sh_ref, lg_ref, lb_ref, *rest):
    if route:
        rc_ref, x1_ref, hp_ref, rt_ref, et_ref = rest
    else:
        x1_ref, hp_ref = rest
    an = jnp.where(pl.program_id(0) >= ntp, ans_ref[...], anp_ref[...])
    mix = _dot(an, w_ref[0:D_A, :]) + _dot(bn_ref[...], w_ref[D_A:D_A + D_B, :])
    x1 = _ln(ALPHA * x_ref[...] + _gate_rows(mix, g1_ref), lg_ref[...], lb_ref[...])
    x1_ref[...] = x1
    h2 = _modulate(x1, sc_ref, sh_ref)
    hp_ref[...] = h2.astype(hp_ref.dtype)
    if route:
        h_hi, h_lo = _split_bf16(h2)
        both = _dot(h_hi, rc_ref[...])
        logits = both[:, :LANES] + both[:, LANES:] + _dot(h_lo, rc_ref[:, :LANES])
        lane = lax.broadcasted_iota(jnp.int32, logits.shape, 1)
        logits = jnp.where(lane < N_EXPERTS, logits, -jnp.inf)
        m1 = logits.max(axis=1, keepdims=True)
        i1 = jnp.where(logits == m1, lane, LANES).min(axis=1, keepdims=True)
        rest_l = jnp.where(lane == i1, -jnp.inf, logits)
        m2 = rest_l.max(axis=1, keepdims=True)
        i2 = jnp.where(rest_l == m2, lane, LANES).min(axis=1, keepdims=True)
        e2 = jnp.exp(m2 - m1)
        w1 = 1.0 / (1.0 + e2)
        w2 = e2 / (1.0 + e2)
        rt = jnp.where(lane == 0, i1.astype(F32),
                       jnp.where(lane == 1, i2.astype(F32),
                                 jnp.where(lane == 2, w1, jnp.where(lane == 3, w2, 0.0))))
        rt_ref[...] = rt
        et_ref[...] = jnp.transpose(rt)[0:8, :]


def _outproj_call(layer, route, x, an_p, an_s, bn, w_bf, mods, lg, lb, r_cat=None):
    t, d = x.shape
    nt = t // ROW_TILE
    ntp = an_p.shape[0] // ROW_TILE

    def tile(i):
        return (i, 0)

    def const2(i):
        return (0, 0)

    in_specs = [
        pl.BlockSpec((ROW_TILE, d), tile),
        pl.BlockSpec((ROW_TILE, D_A), lambda i: (jnp.minimum(i, ntp - 1), 0)),
        pl.BlockSpec((ROW_TILE, D_A), lambda i: (jnp.maximum(i - ntp, 0), 0)),
        pl.BlockSpec((ROW_TILE, D_B), tile),
        pl.BlockSpec(w_bf.shape, const2),
        _mod_spec(layer, G1, ROW_TILE, d),
        _mod_spec(layer, SC2, ROW_TILE, d),
        _mod_spec(layer, SH2, ROW_TILE, d),
        pl.BlockSpec((1, d), const2),
        pl.BlockSpec((1, d), const2),
    ]
    out_specs = [pl.BlockSpec((ROW_TILE, d), tile), pl.BlockSpec((ROW_TILE, d), tile)]
    out_shape = [jax.ShapeDtypeStruct((t, d), F32), jax.ShapeDtypeStruct((t, d), F32 if route else BF16)]
    args = [x, an_p, an_s, bn, w_bf, mods, mods, mods, lg, lb]
    if route:
        in_specs.append(pl.BlockSpec(r_cat.shape, const2))
        out_specs += [pl.BlockSpec((ROW_TILE, LANES), tile), pl.BlockSpec((None, 8, ROW_TILE), lambda i: (i, 0, 0))]
        out_shape += [jax.ShapeDtypeStruct((t, LANES), F32), jax.ShapeDtypeStruct((nt, 8, ROW_TILE), F32)]
        args.append(r_cat)
    return pl.pallas_call(
        functools.partial(_outproj_kernel, route, ntp),
        grid=(nt,),
        in_specs=in_specs,
        out_specs=out_specs,
        out_shape=out_shape,
        compiler_params=_cparams("arbitrary"),
        name="out_proj_route" if route else "out_proj",
    )(*args)


def _swiglu_chains(n_sub, xb_ref, wg_ref, wu_ref, wd_ref, o_ref):
    wg = wg_ref[0].astype(BF16)
    wu = wu_ref[0].astype(BF16)
    wd = wd_ref[0].astype(BF16)
    for s in range(n_sub):
        rows = slice(s * FFN_SUB, (s + 1) * FFN_SUB)
        xs = xb_ref[rows, :]
        g = _dot(xs, wg)
        u = _dot(xs, wu)
        a = (g * jax.nn.sigmoid(g) * u).astype(BF16)
        o_ref[rows, :] += _dot(a, wd)


def _ffn_kernel(te_ref, tv_ref, x_ref, wg_ref, wu_ref, wd_ref, o_ref, xb_ref):
    del te_ref
    i = pl.program_id(0)
    j = pl.program_id(1)
    valid = tv_ref[i]
    nsub = (valid + FFN_SUB - 1) // FFN_SUB

    @pl.when(j == 0)
    def _():
        xb_ref[...] = x_ref[...].astype(BF16)
        o_ref[...] = jnp.zeros(o_ref.shape, F32)

    for n_sub in range(1, x_ref.shape[0] // FFN_SUB + 1):
        pl.when(nsub == n_sub)(functools.partial(_swiglu_chains, n_sub, xb_ref, wg_ref, wu_ref, wd_ref, o_ref))


def _ffn_call(tile_expert, tile_valid, xs, wg, wu, wd, tile_rows):
    p, d = xs.shape
    f = wg.shape[2]
    nt = p // tile_rows
    nc = f // FFN_COLS

    def active_col(i, j, tv):
        return jnp.where(tv[i] > 0, j, nc - 1)

    in_specs = [
        pl.BlockSpec((tile_rows, d), lambda i, j, te, tv: (i, 0)),
        pl.BlockSpec((1, d, FFN_COLS), lambda i, j, te, tv: (te[i], 0, active_col(i, j, tv))),
        pl.BlockSpec((1, d, FFN_COLS), lambda i, j, te, tv: (te[i], 0, active_col(i, j, tv))),
        pl.BlockSpec((1, FFN_COLS, d), lambda i, j, te, tv: (te[i], active_col(i, j, tv), 0)),
    ]
    return pl.pallas_call(
        _ffn_kernel,
        grid_spec=pltpu.PrefetchScalarGridSpec(
            num_scalar_prefetch=2,
            grid=(nt, nc),
            in_specs=in_specs,
            out_specs=pl.BlockSpec((tile_rows, d), lambda i, j, te, tv: (i, 0)),
            scratch_shapes=[pltpu.VMEM((tile_rows, d), BF16)],
        ),
        out_shape=jax.ShapeDtypeStruct((p, d), F32),
        compiler_params=_cparams("arbitrary", "arbitrary"),
        name="swiglu_grouped",
    )(tile_expert, tile_valid, xs, wg, wu, wd)


def _dense_ffn_kernel(x_ref, wg_ref, wu_ref, wd_ref, x1_ref, g2_ref, lg_ref, lb_ref, o_ref):
    x = x_ref[...]
    f = wg_ref.shape[1]
    acc = None
    for c0 in range(0, f, FFN_SPLIT):
        c1 = min(c0 + FFN_SPLIT, f)
        g = _dot(x, wg_ref[:, c0:c1])
        u = _dot(x, wu_ref[:, c0:c1])
        a = (g * jax.nn.sigmoid(g) * u).astype(BF16)
        part = _dot(a, wd_ref[c0:c1, :])
        acc = part if acc is None else acc + part
    o_ref[...] = _ln(ALPHA * x1_ref[...] + _gate_rows(acc, g2_ref), lg_ref[...], lb_ref[...])


def _dense_ffn_call(layer, hp, wg_bf, wu_bf, wd_bf, x1, mods, lg, lb):
    t, d = x1.shape
    f = wg_bf.shape[1]
    tile = pl.BlockSpec((ROW_TILE, d), lambda i: (i, 0))
    vec = pl.BlockSpec((1, d), lambda i: (0, 0))

    def resident(shape):
        return pl.BlockSpec(shape, lambda i: (0, 0), pipeline_mode=pl.Buffered(1))

    return pl.pallas_call(
        _dense_ffn_kernel,
        grid=(t // ROW_TILE,),
        in_specs=[tile, resident((d, f)), resident((d, f)), resident((f, d)), tile,
                  _mod_spec(layer, G2, ROW_TILE, d), vec, vec],
        out_specs=tile,
        out_shape=jax.ShapeDtypeStruct((t, d), F32),
        compiler_params=_cparams("arbitrary"),
        name="swiglu_dense",
    )(hp, wg_bf, wu_bf, wd_bf, x1, mods, lg, lb)


def _dispatch_kernel(pos_ref, src_ref, init_ref, dst_ref, sem):
    del init_ref

    def issue(r, carry):
        row = src_ref.at[pl.ds(r, 1), :]
        pltpu.make_async_copy(row, dst_ref.at[pl.ds(pos_ref[0, r], 1), :], sem).start()
        pltpu.make_async_copy(row, dst_ref.at[pl.ds(pos_ref[0, ROW_TILE + r], 1), :], sem).start()
        return carry
    lax.fori_loop(0, ROW_TILE, issue, 0, unroll=8)
    for _ in range(2):
        pltpu.make_async_copy(src_ref, dst_ref.at[pl.ds(0, ROW_TILE), :], sem).wait()


def _dispatch_call(pos_tiles, hp, zeros):
    t, d = hp.shape
    n_rows_sorted = zeros.shape[0]
    nt = t // ROW_TILE
    return pl.pallas_call(
        _dispatch_kernel,
        grid=(nt,),
        in_specs=[
            pl.BlockSpec((None, 1, 2 * ROW_TILE), lambda i: (i, 0, 0), memory_space=pltpu.SMEM),
            pl.BlockSpec((ROW_TILE, d), lambda i: (i, 0)),
            pl.BlockSpec(memory_space=pl.ANY),
        ],
        out_specs=pl.BlockSpec(memory_space=pl.ANY),
        out_shape=jax.ShapeDtypeStruct((n_rows_sorted, d), hp.dtype),
        scratch_shapes=[pltpu.SemaphoreType.DMA(())],
        input_output_aliases={2: 0},
        compiler_params=_cparams("arbitrary"),
        name="dispatch_rows",
    )(pos_tiles, hp, zeros)


def _final_moe_kernel(ntp, pos_ref, x_ref, rt_ref, g2_ref, lg_ref, lb_ref, ys_ref, op_ref, os_ref, ybuf, sem):
    def issue(r, carry):
        pltpu.make_async_copy(ys_ref.at[pl.ds(pos_ref[0, r], 1), :], ybuf.at[0, pl.ds(r, 1), :], sem).start()
        pltpu.make_async_copy(ys_ref.at[pl.ds(pos_ref[0, ROW_TILE + r], 1), :], ybuf.at[1, pl.ds(r, 1), :],
                              sem).start()
        return carry
    lax.fori_loop(0, ROW_TILE, issue, 0, unroll=8)
    for k in range(2):
        pltpu.make_async_copy(ys_ref.at[pl.ds(0, ROW_TILE), :], ybuf.at[k], sem).wait()
    rt = rt_ref[...]
    f = rt[:, 2:3] * ybuf[0] + rt[:, 3:4] * ybuf[1]
    y = _ln(ALPHA * x_ref[...] + _gate_rows(f, g2_ref), lg_ref[...], lb_ref[...])
    i = pl.program_id(0)

    @pl.when(i < ntp)
    def _():
        op_ref[...] = y

    @pl.when(i >= ntp)
    def _():
        os_ref[...] = y


def _final_moe_call(layer, pos_tiles, x1, rt, mods, lg, lb, ys, *, n_prompt_tiles):
    t, d = x1.shape
    ntp = n_prompt_tiles
    tile = pl.BlockSpec((ROW_TILE, d), lambda i: (i, 0))
    vec = pl.BlockSpec((1, d), lambda i: (0, 0))
    return pl.pallas_call(
        functools.partial(_final_moe_kernel, ntp),
        grid=(t // ROW_TILE,),
        in_specs=[
            pl.BlockSpec((None, 1, 2 * ROW_TILE), lambda i: (i, 0, 0), memory_space=pltpu.SMEM),
            tile,
            pl.BlockSpec((ROW_TILE, LANES), lambda i: (i, 0)),
            _mod_spec(layer, G2, ROW_TILE, d),
            vec, vec,
            pl.BlockSpec(memory_space=pl.ANY),
        ],
        out_specs=[pl.BlockSpec((ROW_TILE, d), lambda i: (jnp.minimum(i, ntp - 1), 0)),
                   pl.BlockSpec((ROW_TILE, d), lambda i: (jnp.maximum(i - ntp, 0), 0))],
        out_shape=[jax.ShapeDtypeStruct((ntp * ROW_TILE, d), F32),
                   jax.ShapeDtypeStruct((t - ntp * ROW_TILE, d), F32)],
        scratch_shapes=[pltpu.VMEM((2, ROW_TILE, d), F32), pltpu.SemaphoreType.DMA(())],
        compiler_params=_cparams("arbitrary"),
        name="final_moe",
    )(pos_tiles, x1, rt, mods, lg, lb, ys)


def _rel_bias(table, n_q, n_k, banded):
    r = jnp.arange(n_q)[:, None]
    w = jnp.arange(n_k)[None, :]
    n_diag = n_q + n_k - 1
    k = jnp.arange(n_diag)
    diag = table[:, jnp.clip(LEFT_CTX + (n_q - 1) - k, -REL_CLIP, REL_CLIP) + REL_CLIP].astype(F32) * LOG2E
    padded = jnp.concatenate([diag, jnp.zeros((diag.shape[0], 1), F32)], axis=1)
    skew = jnp.tile(padded, (1, n_q))[:, :n_q * n_diag].reshape(-1, n_q, n_diag)
    bias = skew[:, :, n_q - 1:n_q - 1 + n_k]
    if banded:
        qa = r // CHUNK
        kc = w // CHUNK
        vis = (kc >= qa) & (kc <= qa + LEFT_CHUNKS)
        bias = jnp.where(vis[None], bias, NEG)
    return bias


def _gating_weights(w_s, b_s):
    n = MLP_CHUNK
    tril = jnp.tril(jnp.ones((n, n), bool))
    wm = jnp.where(tril[None], w_s, 0.0)
    h = CHUNK
    top = wm[:, :h, :h]
    z = jnp.zeros_like(top)
    wm_s = jnp.concatenate([jnp.concatenate([top, z], 2), jnp.concatenate([z, top], 2)], 1)
    both = jnp.stack([wm, wm_s])
    wcat = jnp.concatenate([both[:, 0::2], both[:, 1::2]], axis=-1)
    bias_p = jnp.repeat(jnp.transpose(b_s), GROUP_DIM, axis=1)
    bias_s = jnp.concatenate([bias_p[:h], bias_p[:h]], 0)
    return wcat.astype(BF16), jnp.stack([bias_p, bias_s]).astype(F32)


def _route_plan(et, n_tiles, tile_rows):
    nt = et.shape[0]
    e0 = et[:, 0, :].astype(jnp.int32)
    e1 = et[:, 1, :].astype(jnp.int32)
    experts = jnp.arange(N_EXPERTS, dtype=jnp.int32)[:, None, None]
    is0 = e0[None] == experts
    is1 = e1[None] == experts
    hit = (is0 | is1).astype(jnp.int32)
    in_tile = jnp.cumsum(hit, axis=2)
    tile_tot = in_tile[:, :, -1]
    before_tile = jnp.cumsum(tile_tot, axis=1) - tile_tot
    before = before_tile[:, :, None] + in_tile - hit
    counts = jnp.sum(tile_tot, axis=1)
    tiles_e = (counts + tile_rows - 1) // tile_rows
    tile_end = jnp.cumsum(tiles_e)
    tile_start = tile_end - tiles_e
    slot = before + (tile_start * tile_rows)[:, None, None]
    pos0 = jnp.sum(jnp.where(is0, slot, 0), axis=0)
    pos1 = jnp.sum(jnp.where(is1, slot, 0), axis=0)
    pos = jnp.concatenate([pos0, pos1], axis=1).reshape(nt, 1, 2 * ROW_TILE)
    ti = jnp.arange(n_tiles)
    te = jnp.minimum(jnp.sum((ti[:, None] >= tile_end[None, :]).astype(jnp.int32), axis=1), N_EXPERTS - 1)
    tv = jnp.clip(counts[te] - (ti - tile_start[te]) * tile_rows, 0, tile_rows)
    tv = jnp.where(ti < tile_end[-1], tv, 0)
    last_e = te[jnp.maximum(tile_end[-1] - 1, 0)]
    te = jnp.where(ti < tile_end[-1], te, last_e)
    return pos.astype(jnp.int32), te.astype(jnp.int32), tv.astype(jnp.int32)


def kernel(x_prompt, x_sample, cache_k, cache_v, c_prompt, c_sample, ln_in_g, ln_in_b, w_in, w_out, rel_bias_table, gmlp_ln_g, gmlp_ln_b, gmlp_w_s, gmlp_b_s, out_norm_a, out_norm_b, ada_w, ada_b, ln1_g, ln1_b, ln2_g, ln2_b, ffn_w_gate, ffn_w_up, ffn_w_down, moe_router, moe_w_gate, moe_w_up, moe_w_down):
    nb, seq, d = x_prompt.shape
    ns, dseq, _ = x_sample.shape
    depth = w_in.shape[0]
    tp = nb * seq
    ts = ns * dseq
    t = tp + ts
    assert depth == DEPTH and dseq == CHUNK and seq % ROW_TILE == 0 and ts % ROW_TILE == 0
    assert seq % (Q_GROUP * Q_TILE) == 0 and cache_k.shape[2] == LEFT_CTX
    ntp = tp // ROW_TILE

    n_cond = nb + ns
    c_all = jnp.concatenate([c_prompt, c_sample], axis=0)
    c_pad = jnp.pad(c_all, ((0, (-n_cond) % 8), (0, 0)))
    mods = _ada_call(c_pad, ada_w, ada_b)
    mods = jnp.transpose(mods[:, :n_cond].reshape(depth, n_cond, 6, d), (0, 2, 1, 3))
    cpb = seq // CHUNK
    mods_p = jnp.broadcast_to(mods[:, :, :nb, None, :], (depth, 6, nb, cpb, d)).reshape(depth, 6, nb * cpb, d)
    mods = jnp.concatenate([mods_p, mods[:, :, nb:]], axis=2)

    def row(v):
        return v.reshape(1, -1)

    w_in_bf = w_in.astype(BF16)
    w_out_bf = w_out.astype(BF16)
    cache_kt = jnp.transpose(cache_k, (0, 1, 3, 4, 2))
    cache_vt = jnp.transpose(cache_v, (0, 1, 3, 4, 2))
    k_tail, v_tail, gv_rows = [], [], []
    for l in range(depth):
        wcat, bs = _gating_weights(gmlp_w_s[l], gmlp_b_s[l])
        if l == 0:
            x_in = (x_prompt.reshape(tp, d), x_sample.reshape(ts, d), row(ln_in_g), row(ln_in_b))
        else:
            x_in = (x,)
        res = _inproj_call(l, x_in, mods, w_in_bf[l], row(gmlp_ln_g[l]), row(gmlp_ln_b[l]), wcat, bs,
                           row(out_norm_b[l]),
                           n_prompt_tiles=ntp, tiles_per_batch=seq // ROW_TILE, n_batch=nb, n_sample_rows=ts)
        if l == 0:
            x, q, k, v, kf, vf, gv, bn = res
        else:
            q, k, v, kf, vf, gv, bn = res
        k_tail.append(kf)
        v_tail.append(vf)
        gv_rows.append(gv)

        table = rel_bias_table[l]
        bias_p = _rel_bias(table, Q_TILE, 3 * Q_TILE, True)
        bias_s = _rel_bias(table, CHUNK, LEFT_CTX + CHUNK, False)
        na = row(out_norm_a[l])
        routed = l % 2 == 1
        n_tiles = -(-2 * t // MOE_TILE) + N_EXPERTS
        an_p = _attn_prompt_call(q, k, v, bias_p, na, n_batch=nb, q_tiles_per_batch=seq // Q_TILE,
                                 zeros_shape=(n_tiles * MOE_TILE, d) if routed else None)
        if routed:
            an_p, sorted_init = an_p
        an_s = _attn_sample_call(l, q, k, v, cache_kt, cache_vt, bias_s[:, :, :LEFT_CTX], bias_s[:, :, LEFT_CTX:], na,
                                 first_chunk=tp // CHUNK)

        lg1, lb1, lg2, lb2 = row(ln1_g[l]), row(ln1_b[l]), row(ln2_g[l]), row(ln2_b[l])
        i = l // 2
        if not routed:
            x1, hp = _outproj_call(l, False, x, an_p, an_s, bn, w_out_bf[l], mods, lg1, lb1)
            x = _dense_ffn_call(l, hp, ffn_w_gate[i].astype(BF16), ffn_w_up[i].astype(BF16),
                                ffn_w_down[i].astype(BF16), x1, mods, lg2, lb2)
        else:
            r_pad = jnp.pad(moe_router[i], ((0, 0), (0, LANES - N_EXPERTS)))
            r_hi = r_pad.astype(BF16)
            r_lo = (r_pad - r_hi.astype(F32)).astype(BF16)
            r_cat = jnp.concatenate([r_hi, r_lo], axis=1)
            x1, hp, rt, et = _outproj_call(l, True, x, an_p, an_s, bn, w_out_bf[l], mods, lg1, lb1, r_cat)
            pos, te, tv = _route_plan(et, n_tiles, MOE_TILE)
            xs = _dispatch_call(pos, hp, sorted_init)
            ys = _ffn_call(te, tv, xs, moe_w_gate[i], moe_w_up[i], moe_w_down[i], MOE_TILE)
            y_p, y_s = _final_moe_call(l, pos, x1, rt, mods, lg2, lb2, ys, n_prompt_tiles=ntp)

    y_prompt = y_p.reshape(nb, seq, d)
    y_sample = y_s.reshape(ns, dseq, d)

    def tails(rows):
        kp = jnp.stack([r[:nb * ROW_TILE].reshape(nb, ROW_TILE, N_HEADS, HEAD_DIM) for r in rows])
        ksn = jnp.stack([r[nb * ROW_TILE:].reshape(ns, dseq, N_HEADS, HEAD_DIM) for r in rows])
        return kp, ksn

    k_prompt_new, k_sample_new = tails(k_tail)
    v_prompt_new, v_sample_new = tails(v_tail)
    gmlp_v_sample_new = jnp.stack([g.reshape(ns, dseq, D_B) for g in gv_rows])
    return (y_prompt, y_sample, k_prompt_new, v_prompt_new, k_sample_new, v_sample_new, gmlp_v_sample_new)
```

```python
import functools

import jax
import jax.numpy as jnp
from jax import lax
from jax.experimental import pallas as pl
from jax.experimental.pallas import tpu as pltpu

CHUNK = 64
LEFT_CHUNKS = 8
LEFT_CTX = LEFT_CHUNKS * CHUNK
N_HEADS = 8
HEAD_DIM = 64
D_A = N_HEADS * HEAD_DIM
N_GROUPS = 8
GROUP_DIM = 64
D_B = N_GROUPS * GROUP_DIM
MLP_CHUNK = 128
REL_CLIP = 128
N_EXPERTS = 8
DEPTH = 2
ALPHA = (2 * DEPTH) ** 0.25
LN_EPS = 1e-5
ATTN_SCALE = HEAD_DIM ** -0.5
LOG2E = 1.4426950408889634
NEG = -1e30
SH1, SC1, G1, SH2, SC2, G2 = range(6)

LANES = 128
ROW_TILE = 512
IN_PART = 512
Q_TILE = 256
Q_GROUP = 4
ZERO_ROWS_MAX = 1024
FFN_SPLIT = 1536
MOE_TILE = 2048
FFN_SUB = 512
FFN_COLS = 256
VMEM_LIMIT = 56 * 1024 * 1024

BF16 = jnp.bfloat16
F32 = jnp.float32


def _cparams(*sem):
    return pltpu.CompilerParams(dimension_semantics=sem, vmem_limit_bytes=VMEM_LIMIT)


def _ln(x, g, b):
    mu = jnp.mean(x, axis=-1, keepdims=True)
    xc = x - mu
    var = jnp.mean(xc * xc, axis=-1, keepdims=True)
    return xc * lax.rsqrt(var + LN_EPS) * g + b


def _rms(x, g):
    return x * lax.rsqrt(jnp.mean(x * x, axis=-1, keepdims=True) + LN_EPS) * g


def _modulate(x, sc_ref, sh_ref):
    parts = []
    for c in range(x.shape[0] // CHUNK):
        xc = x[c * CHUNK:(c + 1) * CHUNK]
        parts.append(xc * (1.0 + sc_ref[c:c + 1, :]) + sh_ref[c:c + 1, :])
    return jnp.concatenate(parts, axis=0)


def _gate_rows(x, g_ref):
    parts = []
    for c in range(x.shape[0] // CHUNK):
        parts.append(x[c * CHUNK:(c + 1) * CHUNK] * (1.0 + g_ref[c:c + 1, :]))
    return jnp.concatenate(parts, axis=0)


def _mod_spec(layer, comp, rows, d):
    return pl.BlockSpec((None, None, rows // CHUNK, d), lambda i, *_: (layer, comp, i, 0))


def _split_bf16(x):
    hi = x.astype(BF16)
    lo = (x - hi.astype(F32)).astype(BF16)
    return hi, lo


def _dot(a, b):
    return jnp.dot(a, b, preferred_element_type=F32)


def _ada_kernel(c_ref, w_ref, b_ref, o_ref):
    c = c_ref[...]
    s = c * jax.nn.sigmoid(c)
    s_hi, s_lo = _split_bf16(s)
    w_hi, w_lo = _split_bf16(w_ref[0])
    o_ref[0] = _dot(s_hi, w_hi) + _dot(s_lo, w_hi) + _dot(s_hi, w_lo) + b_ref[0]


def _ada_call(c_pad, ada_w, ada_b):
    depth, d, n = ada_w.shape
    rows = c_pad.shape[0]
    tn = n // 4
    return pl.pallas_call(
        _ada_kernel,
        grid=(depth, n // tn),
        in_specs=[
            pl.BlockSpec((rows, d), lambda l, j: (0, 0)),
            pl.BlockSpec((1, d, tn), lambda l, j: (l, 0, j)),
            pl.BlockSpec((1, 1, tn), lambda l, j: (l, 0, j)),
        ],
        out_specs=pl.BlockSpec((1, rows, tn), lambda l, j: (l, 0, j)),
        out_shape=jax.ShapeDtypeStruct((depth, rows, n), F32),
        compiler_params=_cparams("arbitrary", "arbitrary"),
        name="ada_mod",
    )(c_pad, ada_w, ada_b.reshape(depth, 1, n))


def _inproj_kernel(first, ntp, *refs):
    if first:
        xp_ref, xs_ref, lng_ref, lnb_ref = refs[:4]
        xn_ref = refs[-8]
        refs = refs[4:-8] + refs[-7:]
    else:
        x_ref = refs[0]
        refs = refs[1:]
    (sc_ref, sh_ref, w_ref, gg_ref, gb_ref, wcat_ref, bs_ref, nb_ref,
     q_ref, k_ref, v_ref, kf_ref, vf_ref, gv_ref, bn_ref) = refs
    lane = lax.broadcasted_iota(jnp.int32, (MLP_CHUNK, LANES), 1)
    lo = lane < GROUP_DIM
    zero = jnp.zeros((MLP_CHUNK, LANES), BF16)

    def part(r0, n_rows):
        rows = slice(r0, r0 + n_rows)
        mod_rows = slice(r0 // CHUNK, (r0 + n_rows) // CHUNK)
        if first:
            x = jnp.where(pl.program_id(0) >= ntp, xs_ref[rows, :], xp_ref[rows, :])
            x = _ln(x, lng_ref[...], lnb_ref[...])
            xn_ref[rows, :] = x
        else:
            x = x_ref[rows, :]
        h = _modulate(x, sc_ref.at[mod_rows, :], sh_ref.at[mod_rows, :]).astype(BF16)

        q = _dot(h, w_ref[:, 0:D_A])
        q_ref[rows, :] = (q * (ATTN_SCALE * LOG2E)).astype(BF16)
        k = _dot(h, w_ref[:, D_A:2 * D_A])
        k_ref[rows, :] = k.astype(BF16)
        kf_ref[rows, :] = k
        v = _dot(h, w_ref[:, 2 * D_A:3 * D_A])
        v_ref[rows, :] = v.astype(BF16)
        vf_ref[rows, :] = v

        u = jax.nn.gelu(_dot(h, w_ref[:, 3 * D_A:3 * D_A + D_B]), approximate=True)
        vb = jax.nn.gelu(_dot(h, w_ref[:, 3 * D_A + D_B:3 * D_A + 2 * D_B]), approximate=True)
        vn = _ln(vb, gg_ref[...], gb_ref[...])
        gv_ref[rows, :] = vn

        vnb = vn.astype(BF16)
        n_chunks = n_rows // MLP_CHUNK
        mixed = []
        for p in range(D_B // LANES):
            stacks = []
            for c in range(n_chunks):
                slab = vnb[c * MLP_CHUNK:(c + 1) * MLP_CHUNK, p * LANES:(p + 1) * LANES]
                stacks.append(jnp.concatenate([jnp.where(lo, slab, zero), jnp.where(lo, zero, slab)], axis=0))
            mixed.append(_dot(wcat_ref[0, p], jnp.concatenate(stacks, axis=1)))
        outs = []
        for c in range(n_chunks):
            c0 = c * MLP_CHUNK
            mix_c = jnp.concatenate([m[:, c * LANES:(c + 1) * LANES] for m in mixed], axis=1) + bs_ref[0]
            outs.append(u[c0:c0 + MLP_CHUNK, :] * mix_c)
        bo = jnp.concatenate(outs, axis=0)
        bn_ref[rows, :] = _rms(bo, nb_ref[...]).astype(BF16)

    for r0 in range(0, ROW_TILE, IN_PART):
        part(r0, IN_PART)


def _inproj_call(layer, x_in, mods, w_bf, gg, gb, wcat, bs, nb, *, n_prompt_tiles, tiles_per_batch, n_batch,
                 n_sample_rows):
    first = len(x_in) == 4
    d = x_in[0].shape[1]
    t = mods.shape[2] * CHUNK
    nt = t // ROW_TILE
    ntp = n_prompt_tiles
    n_tail = n_batch + (nt - ntp)

    def tile(i):
        return (i, 0)

    def const2(i):
        return (0, 0)

    def selmap4(i):
        return (jnp.where(i >= ntp, 1, 0), 0, 0, 0)

    def selmap3(i):
        return (jnp.where(i >= ntp, 1, 0), 0, 0)

    def tailmap(i):
        return (jnp.where(i < ntp, i // tiles_per_batch, n_batch + i - ntp), 0)

    def gvmap(i):
        return (jnp.maximum(i - ntp, 0), 0)

    if first:
        x_specs = [
            pl.BlockSpec((ROW_TILE, d), lambda i: (jnp.minimum(i, ntp - 1), 0)),
            pl.BlockSpec((ROW_TILE, d), lambda i: (jnp.maximum(i - ntp, 0), 0)),
            pl.BlockSpec((1, d), const2),
            pl.BlockSpec((1, d), const2),
        ]
    else:
        x_specs = [pl.BlockSpec((ROW_TILE, d), tile)]
    in_specs = x_specs + [
        _mod_spec(layer, SC1, ROW_TILE, d),
        _mod_spec(layer, SH1, ROW_TILE, d),
        pl.BlockSpec((None,) + w_bf.shape[1:], lambda i: (layer, 0, 0)),
        pl.BlockSpec((1, D_B), const2),
        pl.BlockSpec((1, D_B), const2),
        pl.BlockSpec((1,) + wcat.shape[1:], selmap4),
        pl.BlockSpec((1, MLP_CHUNK, D_B), selmap3),
        pl.BlockSpec((1, D_B), const2),
    ]
    out_specs = [
        pl.BlockSpec((ROW_TILE, D_A), tile),
        pl.BlockSpec((ROW_TILE, D_A), tile),
        pl.BlockSpec((ROW_TILE, D_A), tile),
        pl.BlockSpec((ROW_TILE, D_A), tailmap),
        pl.BlockSpec((ROW_TILE, D_A), tailmap),
        pl.BlockSpec((ROW_TILE, D_B), gvmap),
        pl.BlockSpec((ROW_TILE, D_B), tile),
    ]
    out_shape = [
        jax.ShapeDtypeStruct((t, D_A), BF16),
        jax.ShapeDtypeStruct((t, D_A), BF16),
        jax.ShapeDtypeStruct((t, D_A), BF16),
        jax.ShapeDtypeStruct((n_tail * ROW_TILE, D_A), F32),
        jax.ShapeDtypeStruct((n_tail * ROW_TILE, D_A), F32),
        jax.ShapeDtypeStruct((n_sample_rows, D_B), F32),
        jax.ShapeDtypeStruct((t, D_B), BF16),
    ]
    if first:
        out_specs = [pl.BlockSpec((ROW_TILE, d), tile)] + out_specs
        out_shape = [jax.ShapeDtypeStruct((t, d), F32)] + out_shape
    return pl.pallas_call(
        functools.partial(_inproj_kernel, first, ntp),
        grid=(nt,),
        in_specs=in_specs,
        out_specs=out_specs,
        out_shape=out_shape,
        compiler_params=_cparams("arbitrary"),
        name="in_proj_first" if first else "in_proj",
    )(*x_in, mods, mods, w_bf, gg, gb, wcat, bs, nb)


def _row_max(pieces):
    folded = None
    m = None
    for s in pieces:
        if s.shape[1] % LANES == 0:
            for c0 in range(0, s.shape[1], LANES):
                piece = s[:, c0:c0 + LANES]
                folded = piece if folded is None else jnp.maximum(folded, piece)
        else:
            mj = s.max(axis=1, keepdims=True)
            m = mj if m is None else jnp.maximum(m, mj)
    if folded is not None:
        mj = folded.max(axis=1, keepdims=True)
        m = mj if m is None else jnp.maximum(m, mj)
    return m


def _attend(q_of, r, kv_slabs, bias_of, part_rows, part_ok, na):
    nt_dims = (((1,), (1,)), ((), ()))
    lo = lax.broadcasted_iota(jnp.int32, (1, LANES), 1) < HEAD_DIM
    lo_t = lax.broadcasted_iota(jnp.int32, (LANES, 1), 0) < HEAD_DIM
    bounds = sorted({0, r} | {b for rows in part_rows for b in rows})
    segs = list(zip(bounds[:-1], bounds[1:]))

    def covered(seg, rows):
        return rows[0] <= seg[0] and seg[1] <= rows[1]

    outs = []
    for p in range(D_A // LANES):
        qp = q_of(p)
        slabs = kv_slabs(p)
        accs = []
        for half in range(2):
            h = 2 * p + half
            keep = lo if half == 0 else jnp.logical_not(lo)
            keep_t = lo_t if half == 0 else jnp.logical_not(lo_t)
            qh = jnp.where(keep, qp, jnp.zeros_like(qp))
            ss = []
            for j, (kp, _, transposed) in enumerate(slabs):
                qj = qh[part_rows[j][0]:part_rows[j][1]]
                if transposed:
                    s = _dot(qj, kp)
                else:
                    s = lax.dot_general(qj, kp, nt_dims, preferred_element_type=F32)
                s = s + bias_of(h, j)
                if part_ok is not None and part_ok[j] is not None:
                    s = jnp.where(part_ok[j], s, NEG)
                ss.append(s)
            m_segs = [_row_max([s[a - rows[0]:b - rows[0]] for s, rows in zip(ss, part_rows) if covered((a, b), rows)])
                      for a, b in segs]
            acc_segs = [jnp.zeros((b - a, LANES), F32) for a, b in segs]
            for s, (_, vp, transposed), rows in zip(ss, slabs, part_rows):
                m = jnp.concatenate([m_segs[i] for i, seg in enumerate(segs) if covered(seg, rows)], axis=0)
                e = jnp.exp2(s - m).astype(BF16)
                if transposed:
                    vh = jnp.where(keep_t, vp, jnp.ones_like(vp))
                    c = lax.dot_general(e, vh, nt_dims, preferred_element_type=F32)
                else:
                    c = _dot(e, jnp.where(keep, vp, jnp.ones_like(vp)))
                for i, (a, b) in enumerate(segs):
                    if covered((a, b), rows):
                        acc_segs[i] = acc_segs[i] + c[a - rows[0]:b - rows[0]]
            accs.append(jnp.concatenate(acc_segs, axis=0))
        num = jnp.where(lo, accs[0], accs[1])
        den = pltpu.roll(jnp.where(lo, accs[1], accs[0]), HEAD_DIM, 1)
        outs.append(num / den)
    a = jnp.concatenate(outs, axis=1)
    return _rms(a, na)


def _attn_prompt_kernel(zero_fill, q_ref, *refs):
    n_parts = Q_GROUP + 2
    k_refs = refs[:n_parts]
    v_refs = refs[n_parts:2 * n_parts]
    diag_ref, na_ref, o_ref = refs[2 * n_parts:2 * n_parts + 3]
    rest = refs[2 * n_parts + 3:]
    j = pl.program_id(1)
    if zero_fill:
        z_hbm, bias_ref, zbuf, zsem = rest
    else:
        bias_ref, = rest

    @pl.when((pl.program_id(0) == 0) & (j == 0))
    def _():
        shape = (Q_TILE, 3 * Q_TILE)
        q_chunk = lax.broadcasted_iota(jnp.int32, shape, 0) // CHUNK
        k_chunk = lax.broadcasted_iota(jnp.int32, shape, 1) // CHUNK
        visible = (k_chunk >= q_chunk) & (k_chunk <= q_chunk + LEFT_CHUNKS)
        for h in range(N_HEADS):
            rows = jnp.broadcast_to(diag_ref[h:h + 1, :], (Q_TILE, diag_ref.shape[1]))
            skew = pltpu.roll(rows, 0, 1, stride=1, stride_axis=0)
            bias_ref[h] = jnp.where(visible, skew[:, Q_TILE:], NEG)

    if zero_fill:
        step = pl.program_id(0) * pl.num_programs(1) + j
        zr = zbuf.shape[0]
        zbuf[...] = jnp.zeros(zbuf.shape, zbuf.dtype)
        z_copies = [pltpu.make_async_copy(zbuf, z_hbm.at[pl.ds((step * zero_fill + c) * zr, zr), :], zsem)
                    for c in range(zero_fill)]
        for cp in z_copies:
            cp.start()
    tiles_of = [[t for t in range(Q_GROUP) if t <= b <= t + 2] for b in range(n_parts)]
    part_rows = [(ts[0] * Q_TILE, (ts[-1] + 1) * Q_TILE) for ts in tiles_of]

    def q_of(p):
        return q_ref[:, p * LANES:(p + 1) * LANES]

    def kv_slabs(p):
        sl = slice(p * LANES, (p + 1) * LANES)
        return [(kr[:, sl], vr[:, sl], False) for kr, vr in zip(k_refs, v_refs)]

    def bias_of(h, b):
        return jnp.concatenate([bias_ref[h, :, (b - t) * Q_TILE:(b - t + 1) * Q_TILE] for t in tiles_of[b]], axis=0)

    def run(part_ok):
        out = _attend(q_of, Q_GROUP * Q_TILE, kv_slabs, bias_of, part_rows, part_ok, na_ref[...])
        o_ref[...] = out.astype(BF16)

    pl.when(j >= 1)(functools.partial(run, None))
    pl.when(j < 1)(functools.partial(run, [j >= 1, j >= 1] + [None] * Q_GROUP))
    if zero_fill:
        for cp in z_copies:
            cp.wait()


def _attn_prompt_call(q, k, v, diag, na, *, n_batch, q_tiles_per_batch, zeros_shape=None):
    nq = q_tiles_per_batch // Q_GROUP

    def kmap(part):
        return lambda b, j: (b * q_tiles_per_batch + jnp.maximum(Q_GROUP * j - 2 + part, 0), 0)

    blk = pl.BlockSpec((Q_GROUP * Q_TILE, D_A), lambda b, j: (b * nq + j, 0))
    kv_specs = [pl.BlockSpec((Q_TILE, D_A), kmap(part)) for part in range(Q_GROUP + 2)]
    out_specs = [blk]
    out_shape = [jax.ShapeDtypeStruct((n_batch * nq * Q_GROUP * Q_TILE, D_A), BF16)]
    scratch = [pltpu.VMEM((N_HEADS, Q_TILE, 3 * Q_TILE), F32)]
    copies = 0
    if zeros_shape is not None:
        rows, cols = zeros_shape
        per_step, rem = divmod(rows, n_batch * nq)
        assert rem == 0 and per_step % 8 == 0
        zr = max(r for r in range(8, ZERO_ROWS_MAX + 1, 8) if per_step % r == 0)
        copies = per_step // zr
        out_specs.append(pl.BlockSpec(memory_space=pl.ANY))
        out_shape.append(jax.ShapeDtypeStruct((rows, cols), F32))
        scratch += [pltpu.VMEM((zr, cols), F32), pltpu.SemaphoreType.DMA(())]
    res = pl.pallas_call(
        functools.partial(_attn_prompt_kernel, copies),
        grid=(n_batch, nq),
        in_specs=[blk] + kv_specs + kv_specs + [
            pl.BlockSpec(diag.shape, lambda b, j: (0, 0)),
            pl.BlockSpec((1, D_A), lambda b, j: (0, 0)),
        ],
        out_specs=out_specs,
        out_shape=out_shape,
        scratch_shapes=scratch,
        compiler_params=_cparams("arbitrary", "arbitrary"),
        name="attn_prompt_zeros" if zeros_shape is not None else "attn_prompt",
    )(q, *([k] * (Q_GROUP + 2)), *([v] * (Q_GROUP + 2)), diag, na)
    return res if zeros_shape is not None else res[0]


def _attn_sample_kernel(q_ref, kn_ref, vn_ref, ck_ref, cv_ref, bias_c_ref, bias_n_ref, na_ref, o_ref):
    def kv_slabs(p):
        sl = slice(p * LANES, (p + 1) * LANES)

        def cached(c_ref):
            return c_ref[2 * p:2 * p + 2].reshape(LANES, c_ref.shape[2]).astype(BF16)
        return [(cached(ck_ref), cached(cv_ref), True), (kn_ref[:, sl], vn_ref[:, sl], False)]

    def bias_of(h, b):
        return bias_c_ref[h] if b == 0 else bias_n_ref[h]

    def q_of(p):
        return q_ref[:, p * LANES:(p + 1) * LANES]

    r = q_ref.shape[0]
    o_ref[...] = _attend(q_of, r, kv_slabs, bias_of, [(0, r), (0, r)], None, na_ref[...]).astype(BF16)


def _attn_sample_call(layer, q, k, v, cache_kt, cache_vt, bias_c, bias_n, na, *, first_chunk):
    _, nb, nh, hd, win = cache_kt.shape
    blk = pl.BlockSpec((CHUNK, D_A), lambda b: (first_chunk + b, 0))
    cache = pl.BlockSpec((None, None, nh, hd, win), lambda b: (layer, b, 0, 0, 0))
    return pl.pallas_call(
        _attn_sample_kernel,
        grid=(nb,),
        in_specs=[blk, blk, blk, cache, cache,
                  pl.BlockSpec(bias_c.shape, lambda b: (0, 0, 0)),
                  pl.BlockSpec(bias_n.shape, lambda b: (0, 0, 0)),
                  pl.BlockSpec((1, D_A), lambda b: (0, 0))],
        out_specs=pl.BlockSpec((CHUNK, D_A), lambda b: (b, 0)),
        out_shape=jax.ShapeDtypeStruct((nb * CHUNK, D_A), BF16),
        compiler_params=_cparams("arbitrary"),
        name="attn_sample",
    )(q, k, v, cache_kt, cache_vt, bias_c, bias_n, na)


def _outproj_kernel(route, ntp, x_ref, anp_ref, ans_ref, bn_ref, w_ref, g1_ref, sc_ref, sh_ref, lg_ref, lb_ref, *rest):
    if route:
        rc_ref, x1_ref, hp_ref, rt_ref, et_ref = rest
    else:
        x1_ref, hp_ref = rest
    an = jnp.where(pl.program_id(0) >= ntp, ans_ref[...], anp_ref[...])
    mix = _dot(an, w_ref[0:D_A, :]) + _dot(bn_ref[...], w_ref[D_A:D_A + D_B, :])
    x1 = _ln(ALPHA * x_ref[...] + _gate_rows(mix, g1_ref), lg_ref[...], lb_ref[...])
    x1_ref[...] = x1
    h2 = _modulate(x1, sc_ref, sh_ref)
    hp_ref[...] = h2.astype(hp_ref.dtype)
    if route:
        h_hi, h_lo = _split_bf16(h2)
        both = _dot(h_hi, rc_ref[...])
        logits = both[:, :LANES] + both[:, LANES:] + _dot(h_lo, rc_ref[:, :LANES])
        lane = lax.broadcasted_iota(jnp.int32, logits.shape, 1)
        logits = jnp.where(lane < N_EXPERTS, logits, -jnp.inf)
        m1 = logits.max(axis=1, keepdims=True)
        i1 = jnp.where(logits == m1, lane, LANES).min(axis=1, keepdims=True)
        rest_l = jnp.where(lane == i1, -jnp.inf, logits)
        m2 = rest_l.max(axis=1, keepdims=True)
        i2 = jnp.where(rest_l == m2, lane, LANES).min(axis=1, keepdims=True)
        e2 = jnp.exp(m2 - m1)
        w1 = 1.0 / (1.0 + e2)
        w2 = e2 / (1.0 + e2)
        rt = jnp.where(lane == 0, i1.astype(F32),
                       jnp.where(lane == 1, i2.astype(F32),
                                 jnp.where(lane == 2, w1, jnp.where(lane == 3, w2, 0.0))))
        rt_ref[...] = rt
        et_ref[...] = jnp.transpose(rt)[0:8, :]


def _outproj_call(layer, route, x, an_p, an_s, bn, w_bf, mods, lg, lb, r_cat=None):
    t, d = x.shape
    nt = t // ROW_TILE
    ntp = an_p.shape[0] // ROW_TILE

    def tile(i):
        return (i, 0)

    def const2(i):
        return (0, 0)

    in_specs = [
        pl.BlockSpec((ROW_TILE, d), tile),
        pl.BlockSpec((ROW_TILE, D_A), lambda i: (jnp.minimum(i, ntp - 1), 0)),
        pl.BlockSpec((ROW_TILE, D_A), lambda i: (jnp.maximum(i - ntp, 0), 0)),
        pl.BlockSpec((ROW_TILE, D_B), tile),
        pl.BlockSpec((None,) + w_bf.shape[1:], lambda i: (layer, 0, 0)),
        _mod_spec(layer, G1, ROW_TILE, d),
        _mod_spec(layer, SC2, ROW_TILE, d),
        _mod_spec(layer, SH2, ROW_TILE, d),
        pl.BlockSpec((1, d), const2),
        pl.BlockSpec((1, d), const2),
    ]
    out_specs = [pl.BlockSpec((ROW_TILE, d), tile), pl.BlockSpec((ROW_TILE, d), tile)]
    out_shape = [jax.ShapeDtypeStruct((t, d), F32), jax.ShapeDtypeStruct((t, d), F32 if route else BF16)]
    args = [x, an_p, an_s, bn, w_bf, mods, mods, mods, lg, lb]
    if route:
        in_specs.append(pl.BlockSpec(r_cat.shape, const2))
        out_specs += [pl.BlockSpec((ROW_TILE, LANES), tile), pl.BlockSpec((None, 8, ROW_TILE), lambda i: (i, 0, 0))]
        out_shape += [jax.ShapeDtypeStruct((t, LANES), F32), jax.ShapeDtypeStruct((nt, 8, ROW_TILE), F32)]
        args.append(r_cat)
    return pl.pallas_call(
        functools.partial(_outproj_kernel, route, ntp),
        grid=(nt,),
        in_specs=in_specs,
        out_specs=out_specs,
        out_shape=out_shape,
        compiler_params=_cparams("arbitrary"),
        name="out_proj_route" if route else "out_proj",
    )(*args)


def _swiglu_chains(n_sub, xb_ref, wg_ref, wu_ref, wd_ref, o_ref):
    wg = wg_ref[0].astype(BF16)
    wu = wu_ref[0].astype(BF16)
    wd = wd_ref[0].astype(BF16)
    for s in range(n_sub):
        rows = slice(s * FFN_SUB, (s + 1) * FFN_SUB)
        xs = xb_ref[rows, :]
        g = _dot(xs, wg)
        u = _dot(xs, wu)
        a = (g * jax.nn.sigmoid(g) * u).astype(BF16)
        o_ref[rows, :] += _dot(a, wd)


def _ffn_kernel(te_ref, tv_ref, x_ref, wg_ref, wu_ref, wd_ref, o_ref, xb_ref):
    del te_ref
    i = pl.program_id(0)
    j = pl.program_id(1)
    valid = tv_ref[i]
    nsub = (valid + FFN_SUB - 1) // FFN_SUB

    @pl.when(j == 0)
    def _():
        xb_ref[...] = x_ref[...].astype(BF16)
        o_ref[...] = jnp.zeros(o_ref.shape, F32)

    for n_sub in range(1, x_ref.shape[0] // FFN_SUB + 1):
        pl.when(nsub == n_sub)(functools.partial(_swiglu_chains, n_sub, xb_ref, wg_ref, wu_ref, wd_ref, o_ref))


def _ffn_call(tile_expert, tile_valid, xs, wg, wu, wd, tile_rows):
    p, d = xs.shape
    f = wg.shape[2]
    nt = p // tile_rows
    nc = f // FFN_COLS

    def active_col(i, j, tv):
        return jnp.where(tv[i] > 0, j, nc - 1)

    in_specs = [
        pl.BlockSpec((tile_rows, d), lambda i, j, te, tv: (i, 0)),
        pl.BlockSpec((1, d, FFN_COLS), lambda i, j, te, tv: (te[i], 0, active_col(i, j, tv))),
        pl.BlockSpec((1, d, FFN_COLS), lambda i, j, te, tv: (te[i], 0, active_col(i, j, tv))),
        pl.BlockSpec((1, FFN_COLS, d), lambda i, j, te, tv: (te[i], active_col(i, j, tv), 0)),
    ]
    return pl.pallas_call(
        _ffn_kernel,
        grid_spec=pltpu.PrefetchScalarGridSpec(
            num_scalar_prefetch=2,
            grid=(nt, nc),
            in_specs=in_specs,
            out_specs=pl.BlockSpec((tile_rows, d), lambda i, j, te, tv: (i, 0)),
            scratch_shapes=[pltpu.VMEM((tile_rows, d), BF16)],
        ),
        out_shape=jax.ShapeDtypeStruct((p, d), F32),
        compiler_params=_cparams("arbitrary", "arbitrary"),
        name="swiglu_grouped",
    )(tile_expert, tile_valid, xs, wg, wu, wd)


def _dense_ffn_kernel(x_ref, wg_ref, wu_ref, wd_ref, x1_ref, g2_ref, lg_ref, lb_ref, o_ref):
    x = x_ref[...]
    f = wg_ref.shape[1]
    acc = None
    for c0 in range(0, f, FFN_SPLIT):
        c1 = min(c0 + FFN_SPLIT, f)
        g = _dot(x, wg_ref[:, c0:c1])
        u = _dot(x, wu_ref[:, c0:c1])
        a = (g * jax.nn.sigmoid(g) * u).astype(BF16)
        part = _dot(a, wd_ref[c0:c1, :])
        acc = part if acc is None else acc + part
    o_ref[...] = _ln(ALPHA * x1_ref[...] + _gate_rows(acc, g2_ref), lg_ref[...], lb_ref[...])


def _dense_ffn_call(layer, hp, wg_bf, wu_bf, wd_bf, x1, mods, lg, lb):
    t, d = x1.shape
    f = wg_bf.shape[1]
    tile = pl.BlockSpec((ROW_TILE, d), lambda i: (i, 0))
    vec = pl.BlockSpec((1, d), lambda i: (0, 0))

    def resident(shape):
        return pl.BlockSpec(shape, lambda i: (0, 0), pipeline_mode=pl.Buffered(1))

    return pl.pallas_call(
        _dense_ffn_kernel,
        grid=(t // ROW_TILE,),
        in_specs=[tile, resident((d, f)), resident((d, f)), resident((f, d)), tile,
                  _mod_spec(layer, G2, ROW_TILE, d), vec, vec],
        out_specs=tile,
        out_shape=jax.ShapeDtypeStruct((t, d), F32),
        compiler_params=_cparams("arbitrary"),
        name="swiglu_dense",
    )(hp, wg_bf, wu_bf, wd_bf, x1, mods, lg, lb)


def _dispatch_kernel(pos_ref, src_ref, init_ref, dst_ref, sem):
    del init_ref

    def issue(r, carry):
        row = src_ref.at[pl.ds(r, 1), :]
        pltpu.make_async_copy(row, dst_ref.at[pl.ds(pos_ref[0, r], 1), :], sem).start()
        pltpu.make_async_copy(row, dst_ref.at[pl.ds(pos_ref[0, ROW_TILE + r], 1), :], sem).start()
        return carry
    lax.fori_loop(0, ROW_TILE, issue, 0, unroll=8)
    for _ in range(2):
        pltpu.make_async_copy(src_ref, dst_ref.at[pl.ds(0, ROW_TILE), :], sem).wait()


def _dispatch_call(pos_tiles, hp, zeros):
    t, d = hp.shape
    n_rows_sorted = zeros.shape[0]
    nt = t // ROW_TILE
    return pl.pallas_call(
        _dispatch_kernel,
        grid=(nt,),
        in_specs=[
            pl.BlockSpec((None, 1, 2 * ROW_TILE), lambda i: (i, 0, 0), memory_space=pltpu.SMEM),
            pl.BlockSpec((ROW_TILE, d), lambda i: (i, 0)),
            pl.BlockSpec(memory_space=pl.ANY),
        ],
        out_specs=pl.BlockSpec(memory_space=pl.ANY),
        out_shape=jax.ShapeDtypeStruct((n_rows_sorted, d), hp.dtype),
        scratch_shapes=[pltpu.SemaphoreType.DMA(())],
        input_output_aliases={2: 0},
        compiler_params=_cparams("arbitrary"),
        name="dispatch_rows",
    )(pos_tiles, hp, zeros)


def _final_moe_kernel(ntp, pos_ref, x_ref, rt_ref, g2_ref, lg_ref, lb_ref, ys_ref, op_ref, os_ref, ybuf, sem):
    def issue(r, carry):
        pltpu.make_async_copy(ys_ref.at[pl.ds(pos_ref[0, r], 1), :], ybuf.at[0, pl.ds(r, 1), :], sem).start()
        pltpu.make_async_copy(ys_ref.at[pl.ds(pos_ref[0, ROW_TILE + r], 1), :], ybuf.at[1, pl.ds(r, 1), :],
                              sem).start()
        return carry
    lax.fori_loop(0, ROW_TILE, issue, 0, unroll=8)
    for k in range(2):
        pltpu.make_async_copy(ys_ref.at[pl.ds(0, ROW_TILE), :], ybuf.at[k], sem).wait()
    rt = rt_ref[...]
    f = rt[:, 2:3] * ybuf[0] + rt[:, 3:4] * ybuf[1]
    y = _ln(ALPHA * x_ref[...] + _gate_rows(f, g2_ref), lg_ref[...], lb_ref[...])
    i = pl.program_id(0)

    @pl.when(i < ntp)
    def _():
        op_ref[...] = y

    @pl.when(i >= ntp)
    def _():
        os_ref[...] = y


def _final_moe_call(layer, pos_tiles, x1, rt, mods, lg, lb, ys, *, n_prompt_tiles):
    t, d = x1.shape
    ntp = n_prompt_tiles
    tile = pl.BlockSpec((ROW_TILE, d), lambda i: (i, 0))
    vec = pl.BlockSpec((1, d), lambda i: (0, 0))
    return pl.pallas_call(
        functools.partial(_final_moe_kernel, ntp),
        grid=(t // ROW_TILE,),
        in_specs=[
            pl.BlockSpec((None, 1, 2 * ROW_TILE), lambda i: (i, 0, 0), memory_space=pltpu.SMEM),
            tile,
            pl.BlockSpec((ROW_TILE, LANES), lambda i: (i, 0)),
            _mod_spec(layer, G2, ROW_TILE, d),
            vec, vec,
            pl.BlockSpec(memory_space=pl.ANY),
        ],
        out_specs=[pl.BlockSpec((ROW_TILE, d), lambda i: (jnp.minimum(i, ntp - 1), 0)),
                   pl.BlockSpec((ROW_TILE, d), lambda i: (jnp.maximum(i - ntp, 0), 0))],
        out_shape=[jax.ShapeDtypeStruct((ntp * ROW_TILE, d), F32),
                   jax.ShapeDtypeStruct((t - ntp * ROW_TILE, d), F32)],
        scratch_shapes=[pltpu.VMEM((2, ROW_TILE, d), F32), pltpu.SemaphoreType.DMA(())],
        compiler_params=_cparams("arbitrary"),
        name="final_moe",
    )(pos_tiles, x1, rt, mods, lg, lb, ys)


def _bias_diagonals(table, n_q, n_k):
    k = jnp.arange(n_q + n_k - 1)
    return table[:, jnp.clip(LEFT_CTX + (n_q - 1) - k, -REL_CLIP, REL_CLIP) + REL_CLIP].astype(F32) * LOG2E


def _rel_bias(table, n_q, n_k):
    diag = _bias_diagonals(table, n_q, n_k)
    n_diag = diag.shape[1]
    padded = jnp.concatenate([diag, jnp.zeros((diag.shape[0], 1), F32)], axis=1)
    skew = jnp.tile(padded, (1, n_q))[:, :n_q * n_diag].reshape(-1, n_q, n_diag)
    return skew[:, :, n_q - 1:n_q - 1 + n_k]


def _gating_weights(w_s, b_s):
    n = MLP_CHUNK
    tril = jnp.tril(jnp.ones((n, n), bool))
    wm = jnp.where(tril[None], w_s, 0.0)
    h = CHUNK
    top = wm[:, :h, :h]
    z = jnp.zeros_like(top)
    wm_s = jnp.concatenate([jnp.concatenate([top, z], 2), jnp.concatenate([z, top], 2)], 1)
    both = jnp.stack([wm, wm_s])
    wcat = jnp.concatenate([both[:, 0::2], both[:, 1::2]], axis=-1)
    bias_p = jnp.repeat(jnp.transpose(b_s), GROUP_DIM, axis=1)
    bias_s = jnp.concatenate([bias_p[:h], bias_p[:h]], 0)
    return wcat.astype(BF16), jnp.stack([bias_p, bias_s]).astype(F32)


def _route_plan(et, n_tiles, tile_rows):
    nt = et.shape[0]
    e0 = et[:, 0, :].astype(jnp.int32)
    e1 = et[:, 1, :].astype(jnp.int32)
    experts = jnp.arange(N_EXPERTS, dtype=jnp.int32)[:, None, None]
    is0 = e0[None] == experts
    is1 = e1[None] == experts
    hit = (is0 | is1).astype(jnp.int32)
    in_tile = jnp.cumsum(hit, axis=2)
    tile_tot = in_tile[:, :, -1]
    before_tile = jnp.cumsum(tile_tot, axis=1) - tile_tot
    before = before_tile[:, :, None] + in_tile - hit
    counts = jnp.sum(tile_tot, axis=1)
    tiles_e = (counts + tile_rows - 1) // tile_rows
    tile_end = jnp.cumsum(tiles_e)
    tile_start = tile_end - tiles_e
    slot = before + (tile_start * tile_rows)[:, None, None]
    pos0 = jnp.sum(jnp.where(is0, slot, 0), axis=0)
    pos1 = jnp.sum(jnp.where(is1, slot, 0), axis=0)
    pos = jnp.concatenate([pos0, pos1], axis=1).reshape(nt, 1, 2 * ROW_TILE)
    ti = jnp.arange(n_tiles)
    te = jnp.minimum(jnp.sum((ti[:, None] >= tile_end[None, :]).astype(jnp.int32), axis=1), N_EXPERTS - 1)
    tv = jnp.clip(counts[te] - (ti - tile_start[te]) * tile_rows, 0, tile_rows)
    tv = jnp.where(ti < tile_end[-1], tv, 0)
    last_e = te[jnp.maximum(tile_end[-1] - 1, 0)]
    te = jnp.where(ti < tile_end[-1], te, last_e)
    return pos.astype(jnp.int32), te.astype(jnp.int32), tv.astype(jnp.int32)


def kernel(x_prompt, x_sample, cache_k, cache_v, c_prompt, c_sample, ln_in_g, ln_in_b, w_in, w_out, rel_bias_table, gmlp_ln_g, gmlp_ln_b, gmlp_w_s, gmlp_b_s, out_norm_a, out_norm_b, ada_w, ada_b, ln1_g, ln1_b, ln2_g, ln2_b, ffn_w_gate, ffn_w_up, ffn_w_down, moe_router, moe_w_gate, moe_w_up, moe_w_down):
    nb, seq, d = x_prompt.shape
    ns, dseq, _ = x_sample.shape
    depth = w_in.shape[0]
    tp = nb * seq
    ts = ns * dseq
    t = tp + ts
    assert depth == DEPTH and dseq == CHUNK and seq % ROW_TILE == 0 and ts % ROW_TILE == 0
    assert seq % (Q_GROUP * Q_TILE) == 0 and cache_k.shape[2] == LEFT_CTX
    ntp = tp // ROW_TILE

    n_cond = nb + ns
    c_all = jnp.concatenate([c_prompt, c_sample], axis=0)
    c_pad = jnp.pad(c_all, ((0, (-n_cond) % 8), (0, 0)))
    mods = _ada_call(c_pad, ada_w, ada_b)
    mods = jnp.transpose(mods[:, :n_cond].reshape(depth, n_cond, 6, d), (0, 2, 1, 3))
    cpb = seq // CHUNK
    mods_p = jnp.broadcast_to(mods[:, :, :nb, None, :], (depth, 6, nb, cpb, d)).reshape(depth, 6, nb * cpb, d)
    mods = jnp.concatenate([mods_p, mods[:, :, nb:]], axis=2)

    def row(v):
        return v.reshape(1, -1)

    w_in_bf = w_in.astype(BF16)
    w_out_bf = w_out.astype(BF16)
    cache_kt = jnp.transpose(cache_k, (0, 1, 3, 4, 2))
    cache_vt = jnp.transpose(cache_v, (0, 1, 3, 4, 2))
    k_tail, v_tail, gv_rows = [], [], []
    for l in range(depth):
        wcat, bs = _gating_weights(gmlp_w_s[l], gmlp_b_s[l])
        if l == 0:
            x_in = (x_prompt.reshape(tp, d), x_sample.reshape(ts, d), row(ln_in_g), row(ln_in_b))
        else:
            x_in = (x,)
        res = _inproj_call(l, x_in, mods, w_in_bf, row(gmlp_ln_g[l]), row(gmlp_ln_b[l]), wcat, bs,
                           row(out_norm_b[l]),
                           n_prompt_tiles=ntp, tiles_per_batch=seq // ROW_TILE, n_batch=nb, n_sample_rows=ts)
        if l == 0:
            x, q, k, v, kf, vf, gv, bn = res
        else:
            q, k, v, kf, vf, gv, bn = res
        k_tail.append(kf)
        v_tail.append(vf)
        gv_rows.append(gv)

        table = rel_bias_table[l]
        diag_p = jnp.pad(_bias_diagonals(table, Q_TILE, 3 * Q_TILE), ((0, 0), (1, 0)))
        bias_s = _rel_bias(table, CHUNK, LEFT_CTX + CHUNK)
        na = row(out_norm_a[l])
        routed = l % 2 == 1
        n_tiles = -(-2 * t // MOE_TILE) + N_EXPERTS
        an_p = _attn_prompt_call(q, k, v, diag_p, na, n_batch=nb, q_tiles_per_batch=seq // Q_TILE,
                                 zeros_shape=(n_tiles * MOE_TILE, d) if routed else None)
        if routed:
            an_p, sorted_init = an_p
        an_s = _attn_sample_call(l, q, k, v, cache_kt, cache_vt, bias_s[:, :, :LEFT_CTX], bias_s[:, :, LEFT_CTX:], na,
                                 first_chunk=tp // CHUNK)

        lg1, lb1, lg2, lb2 = row(ln1_g[l]), row(ln1_b[l]), row(ln2_g[l]), row(ln2_b[l])
        i = l // 2
        if not routed:
            x1, hp = _outproj_call(l, False, x, an_p, an_s, bn, w_out_bf, mods, lg1, lb1)
            x = _dense_ffn_call(l, hp, ffn_w_gate[i].astype(BF16), ffn_w_up[i].astype(BF16),
                                ffn_w_down[i].astype(BF16), x1, mods, lg2, lb2)
        else:
            r_pad = jnp.pad(moe_router[i], ((0, 0), (0, LANES - N_EXPERTS)))
            r_hi = r_pad.astype(BF16)
            r_lo = (r_pad - r_hi.astype(F32)).astype(BF16)
            r_cat = jnp.concatenate([r_hi, r_lo], axis=1)
            x1, hp, rt, et = _outproj_call(l, True, x, an_p, an_s, bn, w_out_bf, mods, lg1, lb1, r_cat)
            pos, te, tv = _route_plan(et, n_tiles, MOE_TILE)
            xs = _dispatch_call(pos, hp, sorted_init)
            ys = _ffn_call(te, tv, xs, moe_w_gate[i], moe_w_up[i], moe_w_down[i], MOE_TILE)
            y_p, y_s = _final_moe_call(l, pos, x1, rt, mods, lg2, lb2, ys, n_prompt_tiles=ntp)

    y_prompt = y_p.reshape(nb, seq, d)
    y_sample = y_s.reshape(ns, dseq, d)

    def tails(rows):
        kp = jnp.stack([r[:nb * ROW_TILE].reshape(nb, ROW_TILE, N_HEADS, HEAD_DIM) for r in rows])
        ksn = jnp.stack([r[nb * ROW_TILE:].reshape(ns, dseq, N_HEADS, HEAD_DIM) for r in rows])
        return kp, ksn

    k_prompt_new, k_sample_new = tails(k_tail)
    v_prompt_new, v_sample_new = tails(v_tail)
    gmlp_v_sample_new = jnp.stack([g.reshape(ns, dseq, D_B) for g in gv_rows])
    return (y_prompt, y_sample, k_prompt_new, v_prompt_new, k_sample_new, v_sample_new, gmlp_v_sample_new)
```

```python
import functools

import jax
import jax.numpy as jnp
from jax import lax
from jax.experimental import pallas as pl
from jax.experimental.pallas import tpu as pltpu

CHUNK = 64
LEFT_CHUNKS = 8
LEFT_CTX = LEFT_CHUNKS * CHUNK
N_HEADS = 8
HEAD_DIM = 64
D_A = N_HEADS * HEAD_DIM
N_GROUPS = 8
GROUP_DIM = 64
D_B = N_GROUPS * GROUP_DIM
MLP_CHUNK = 128
REL_CLIP = 128
N_EXPERTS = 8
DEPTH = 2
ALPHA = (2 * DEPTH) ** 0.25
LN_EPS = 1e-5
ATTN_SCALE = HEAD_DIM ** -0.5
LOG2E = 1.4426950408889634
NEG = -1e30
SH1, SC1, G1, SH2, SC2, G2 = range(6)

LANES = 128
ROW_TILE = 512
IN_PART = 512
Q_TILE = 256
Q_GROUP = 4
ZERO_ROWS_MAX = 1024
FFN_SPLIT = 1536
MOE_TILE = 2048
FFN_SUB = 512
FFN_COLS = 256
VMEM_LIMIT = 56 * 1024 * 1024

BF16 = jnp.bfloat16
F32 = jnp.float32


def _cparams(*sem):
    return pltpu.CompilerParams(dimension_semantics=sem, vmem_limit_bytes=VMEM_LIMIT)


def _ln(x, g, b):
    mu = jnp.mean(x, axis=-1, keepdims=True)
    xc = x - mu
    var = jnp.mean(xc * xc, axis=-1, keepdims=True)
    return xc * lax.rsqrt(var + LN_EPS) * g + b


def _rms(x, g):
    return x * lax.rsqrt(jnp.mean(x * x, axis=-1, keepdims=True) + LN_EPS) * g


def _modulate(x, sc_ref, sh_ref):
    parts = []
    for c in range(x.shape[0] // CHUNK):
        xc = x[c * CHUNK:(c + 1) * CHUNK]
        parts.append(xc * (1.0 + sc_ref[c:c + 1, :]) + sh_ref[c:c + 1, :])
    return jnp.concatenate(parts, axis=0)


def _gate_rows(x, g_ref):
    parts = []
    for c in range(x.shape[0] // CHUNK):
        parts.append(x[c * CHUNK:(c + 1) * CHUNK] * (1.0 + g_ref[c:c + 1, :]))
    return jnp.concatenate(parts, axis=0)


def _mod_spec(layer, comp, rows, d):
    return pl.BlockSpec((None, None, rows // CHUNK, d), lambda i, *_: (layer, comp, i, 0))


def _split_bf16(x):
    hi = x.astype(BF16)
    lo = (x - hi.astype(F32)).astype(BF16)
    return hi, lo


def _dot(a, b):
    return jnp.dot(a, b, preferred_element_type=F32)


def _ada_kernel(c_ref, w_ref, b_ref, o_ref):
    c = c_ref[...]
    s = c * jax.nn.sigmoid(c)
    s_hi, s_lo = _split_bf16(s)
    w_hi, w_lo = _split_bf16(w_ref[0])
    o_ref[0] = _dot(s_hi, w_hi) + _dot(s_lo, w_hi) + _dot(s_hi, w_lo) + b_ref[0]


def _ada_call(c_pad, ada_w, ada_b):
    depth, d, n = ada_w.shape
    rows = c_pad.shape[0]
    tn = n // 4
    return pl.pallas_call(
        _ada_kernel,
        grid=(depth, n // tn),
        in_specs=[
            pl.BlockSpec((rows, d), lambda l, j: (0, 0)),
            pl.BlockSpec((1, d, tn), lambda l, j: (l, 0, j)),
            pl.BlockSpec((1, 1, tn), lambda l, j: (l, 0, j)),
        ],
        out_specs=pl.BlockSpec((1, rows, tn), lambda l, j: (l, 0, j)),
        out_shape=jax.ShapeDtypeStruct((depth, rows, n), F32),
        compiler_params=_cparams("arbitrary", "arbitrary"),
        name="ada_mod",
    )(c_pad, ada_w, ada_b.reshape(depth, 1, n))


def _inproj_kernel(first, ntp, *refs):
    if first:
        xp_ref, xs_ref, lng_ref, lnb_ref = refs[:4]
        xn_ref = refs[-8]
        refs = refs[4:-8] + refs[-7:]
    else:
        x_ref = refs[0]
        refs = refs[1:]
    (sc_ref, sh_ref, w_ref, gg_ref, gb_ref, wcat_ref, bs_ref, nb_ref,
     q_ref, k_ref, v_ref, kf_ref, vf_ref, gv_ref, bn_ref) = refs
    lane = lax.broadcasted_iota(jnp.int32, (MLP_CHUNK, LANES), 1)
    lo = lane < GROUP_DIM
    zero = jnp.zeros((MLP_CHUNK, LANES), BF16)

    def part(r0, n_rows):
        rows = slice(r0, r0 + n_rows)
        mod_rows = slice(r0 // CHUNK, (r0 + n_rows) // CHUNK)
        if first:
            x = jnp.where(pl.program_id(0) >= ntp, xs_ref[rows, :], xp_ref[rows, :])
            x = _ln(x, lng_ref[...], lnb_ref[...])
            xn_ref[rows, :] = x
        else:
            x = x_ref[rows, :]
        h = _modulate(x, sc_ref.at[mod_rows, :], sh_ref.at[mod_rows, :]).astype(BF16)

        q = _dot(h, w_ref[:, 0:D_A])
        q_ref[rows, :] = (q * (ATTN_SCALE * LOG2E)).astype(BF16)
        k = _dot(h, w_ref[:, D_A:2 * D_A])
        k_ref[rows, :] = k.astype(BF16)
        kf_ref[rows, :] = k
        v = _dot(h, w_ref[:, 2 * D_A:3 * D_A])
        v_ref[rows, :] = v.astype(BF16)
        vf_ref[rows, :] = v

        u = jax.nn.gelu(_dot(h, w_ref[:, 3 * D_A:3 * D_A + D_B]), approximate=True)
        vb = jax.nn.gelu(_dot(h, w_ref[:, 3 * D_A + D_B:3 * D_A + 2 * D_B]), approximate=True)
        vn = _ln(vb, gg_ref[...], gb_ref[...])
        gv_ref[rows, :] = vn

        vnb = vn.astype(BF16)
        n_chunks = n_rows // MLP_CHUNK
        mixed = []
        for p in range(D_B // LANES):
            stacks = []
            for c in range(n_chunks):
                slab = vnb[c * MLP_CHUNK:(c + 1) * MLP_CHUNK, p * LANES:(p + 1) * LANES]
                stacks.append(jnp.concatenate([jnp.where(lo, slab, zero), jnp.where(lo, zero, slab)], axis=0))
            mixed.append(_dot(wcat_ref[0, p], jnp.concatenate(stacks, axis=1)))
        outs = []
        for c in range(n_chunks):
            c0 = c * MLP_CHUNK
            mix_c = jnp.concatenate([m[:, c * LANES:(c + 1) * LANES] for m in mixed], axis=1) + bs_ref[0]
            outs.append(u[c0:c0 + MLP_CHUNK, :] * mix_c)
        bo = jnp.concatenate(outs, axis=0)
        bn_ref[rows, :] = _rms(bo, nb_ref[...]).astype(BF16)

    for r0 in range(0, ROW_TILE, IN_PART):
        part(r0, IN_PART)


def _inproj_call(layer, x_in, mods, w_bf, gg, gb, wcat, bs, nb, *, n_prompt_tiles, tiles_per_batch, n_batch,
                 n_sample_rows):
    first = len(x_in) == 4
    d = x_in[0].shape[1]
    t = mods.shape[2] * CHUNK
    nt = t // ROW_TILE
    ntp = n_prompt_tiles
    n_tail = n_batch + (nt - ntp)

    def tile(i):
        return (i, 0)

    def const2(i):
        return (0, 0)

    def selmap4(i):
        return (jnp.where(i >= ntp, 1, 0), 0, 0, 0)

    def selmap3(i):
        return (jnp.where(i >= ntp, 1, 0), 0, 0)

    def tailmap(i):
        return (jnp.where(i < ntp, i // tiles_per_batch, n_batch + i - ntp), 0)

    def gvmap(i):
        return (jnp.maximum(i - ntp, 0), 0)

    if first:
        x_specs = [
            pl.BlockSpec((ROW_TILE, d), lambda i: (jnp.minimum(i, ntp - 1), 0)),
            pl.BlockSpec((ROW_TILE, d), lambda i: (jnp.maximum(i - ntp, 0), 0)),
            pl.BlockSpec((1, d), const2),
            pl.BlockSpec((1, d), const2),
        ]
    else:
        x_specs = [pl.BlockSpec((ROW_TILE, d), tile)]
    in_specs = x_specs + [
        _mod_spec(layer, SC1, ROW_TILE, d),
        _mod_spec(layer, SH1, ROW_TILE, d),
        pl.BlockSpec((None,) + w_bf.shape[1:], lambda i: (layer, 0, 0)),
        pl.BlockSpec((1, D_B), const2),
        pl.BlockSpec((1, D_B), const2),
        pl.BlockSpec((1,) + wcat.shape[1:], selmap4),
        pl.BlockSpec((1, MLP_CHUNK, D_B), selmap3),
        pl.BlockSpec((1, D_B), const2),
    ]
    out_specs = [
        pl.BlockSpec((ROW_TILE, D_A), tile),
        pl.BlockSpec((ROW_TILE, D_A), tile),
        pl.BlockSpec((ROW_TILE, D_A), tile),
        pl.BlockSpec((ROW_TILE, D_A), tailmap),
        pl.BlockSpec((ROW_TILE, D_A), tailmap),
        pl.BlockSpec((ROW_TILE, D_B), gvmap),
        pl.BlockSpec((ROW_TILE, D_B), tile),
    ]
    out_shape = [
        jax.ShapeDtypeStruct((t, D_A), BF16),
        jax.ShapeDtypeStruct((t, D_A), BF16),
        jax.ShapeDtypeStruct((t, D_A), BF16),
        jax.ShapeDtypeStruct((n_tail * ROW_TILE, D_A), F32),
        jax.ShapeDtypeStruct((n_tail * ROW_TILE, D_A), F32),
        jax.ShapeDtypeStruct((n_sample_rows, D_B), F32),
        jax.ShapeDtypeStruct((t, D_B), BF16),
    ]
    if first:
        out_specs = [pl.BlockSpec((ROW_TILE, d), tile)] + out_specs
        out_shape = [jax.ShapeDtypeStruct((t, d), F32)] + out_shape
    return pl.pallas_call(
        functools.partial(_inproj_kernel, first, ntp),
        grid=(nt,),
        in_specs=in_specs,
        out_specs=out_specs,
        out_shape=out_shape,
        compiler_params=_cparams("arbitrary"),
        name="in_proj_first" if first else "in_proj",
    )(*x_in, mods, mods, w_bf, gg, gb, wcat, bs, nb)


def _row_max(pieces):
    folded = None
    m = None
    for s in pieces:
        if s.shape[1] % LANES == 0:
            for c0 in range(0, s.shape[1], LANES):
                piece = s[:, c0:c0 + LANES]
                folded = piece if folded is None else jnp.maximum(folded, piece)
        else:
            mj = s.max(axis=1, keepdims=True)
            m = mj if m is None else jnp.maximum(m, mj)
    if folded is not None:
        mj = folded.max(axis=1, keepdims=True)
        m = mj if m is None else jnp.maximum(m, mj)
    return m


def _attend(q_of, r, kv_slabs, bias_of, part_rows, part_ok, na):
    nt_dims = (((1,), (1,)), ((), ()))
    lo = lax.broadcasted_iota(jnp.int32, (1, LANES), 1) < HEAD_DIM
    lo_t = lax.broadcasted_iota(jnp.int32, (LANES, 1), 0) < HEAD_DIM
    bounds = sorted({0, r} | {b for rows in part_rows for b in rows})
    segs = list(zip(bounds[:-1], bounds[1:]))

    def covered(seg, rows):
        return rows[0] <= seg[0] and seg[1] <= rows[1]

    outs = []
    for p in range(D_A // LANES):
        qp = q_of(p)
        slabs = kv_slabs(p)
        accs = []
        for half in range(2):
            h = 2 * p + half
            keep = lo if half == 0 else jnp.logical_not(lo)
            keep_t = lo_t if half == 0 else jnp.logical_not(lo_t)
            qh = jnp.where(keep, qp, jnp.zeros_like(qp))
            ss = []
            for j, (kp, _, transposed) in enumerate(slabs):
                qj = qh[part_rows[j][0]:part_rows[j][1]]
                if transposed:
                    s = _dot(qj, kp)
                else:
                    s = lax.dot_general(qj, kp, nt_dims, preferred_element_type=F32)
                s = s + bias_of(h, j)
                if part_ok is not None and part_ok[j] is not None:
                    s = jnp.where(part_ok[j], s, NEG)
                ss.append(s)
            m_segs = [_row_max([s[a - rows[0]:b - rows[0]] for s, rows in zip(ss, part_rows) if covered((a, b), rows)])
                      for a, b in segs]
            acc_segs = [jnp.zeros((b - a, LANES), F32) for a, b in segs]
            for s, (_, vp, transposed), rows in zip(ss, slabs, part_rows):
                m = jnp.concatenate([m_segs[i] for i, seg in enumerate(segs) if covered(seg, rows)], axis=0)
                e = jnp.exp2(s - m).astype(BF16)
                if transposed:
                    vh = jnp.where(keep_t, vp, jnp.ones_like(vp))
                    c = lax.dot_general(e, vh, nt_dims, preferred_element_type=F32)
                else:
                    c = _dot(e, jnp.where(keep, vp, jnp.ones_like(vp)))
                for i, (a, b) in enumerate(segs):
                    if covered((a, b), rows):
                        acc_segs[i] = acc_segs[i] + c[a - rows[0]:b - rows[0]]
            accs.append(jnp.concatenate(acc_segs, axis=0))
        num = jnp.where(lo, accs[0], accs[1])
        den = pltpu.roll(jnp.where(lo, accs[1], accs[0]), HEAD_DIM, 1)
        outs.append(num / den)
    a = jnp.concatenate(outs, axis=1)
    return _rms(a, na)


def _attn_prompt_kernel(zero_fill, q_ref, *refs):
    n_parts = Q_GROUP + 2
    k_refs = refs[:n_parts]
    v_refs = refs[n_parts:2 * n_parts]
    diag_ref, na_ref, o_ref = refs[2 * n_parts:2 * n_parts + 3]
    rest = refs[2 * n_parts + 3:]
    j = pl.program_id(1)
    if zero_fill:
        z_hbm, bias_ref, zbuf, zsem = rest
    else:
        bias_ref, = rest

    @pl.when((pl.program_id(0) == 0) & (j == 0))
    def _():
        shape = (Q_TILE, 3 * Q_TILE)
        q_chunk = lax.broadcasted_iota(jnp.int32, shape, 0) // CHUNK
        k_chunk = lax.broadcasted_iota(jnp.int32, shape, 1) // CHUNK
        visible = (k_chunk >= q_chunk) & (k_chunk <= q_chunk + LEFT_CHUNKS)
        for h in range(N_HEADS):
            rows = jnp.broadcast_to(diag_ref[h:h + 1, :], (Q_TILE, diag_ref.shape[1]))
            skew = pltpu.roll(rows, 0, 1, stride=1, stride_axis=0)
            bias_ref[h] = jnp.where(visible, skew[:, Q_TILE:], NEG)

    if zero_fill:
        step = pl.program_id(0) * pl.num_programs(1) + j
        zr = zbuf.shape[0]
        zbuf[...] = jnp.zeros(zbuf.shape, zbuf.dtype)
        z_copies = [pltpu.make_async_copy(zbuf, z_hbm.at[pl.ds((step * zero_fill + c) * zr, zr), :], zsem)
                    for c in range(zero_fill)]
        for cp in z_copies:
            cp.start()
    tiles_of = [[t for t in range(Q_GROUP) if t <= b <= t + 2] for b in range(n_parts)]
    part_rows = [(ts[0] * Q_TILE, (ts[-1] + 1) * Q_TILE) for ts in tiles_of]

    def q_of(p):
        return q_ref[:, p * LANES:(p + 1) * LANES]

    def kv_slabs(p):
        sl = slice(p * LANES, (p + 1) * LANES)
        return [(kr[:, sl], vr[:, sl], False) for kr, vr in zip(k_refs, v_refs)]

    def bias_of(h, b):
        return jnp.concatenate([bias_ref[h, :, (b - t) * Q_TILE:(b - t + 1) * Q_TILE] for t in tiles_of[b]], axis=0)

    def run(part_ok):
        out = _attend(q_of, Q_GROUP * Q_TILE, kv_slabs, bias_of, part_rows, part_ok, na_ref[...])
        o_ref[...] = out.astype(BF16)

    pl.when(j >= 1)(functools.partial(run, None))
    pl.when(j < 1)(functools.partial(run, [j >= 1, j >= 1] + [None] * Q_GROUP))
    if zero_fill:
        for cp in z_copies:
            cp.wait()


def _attn_prompt_call(q, k, v, diag, na, *, n_batch, q_tiles_per_batch, zeros_shape=None):
    nq = q_tiles_per_batch // Q_GROUP

    def kmap(part):
        return lambda b, j: (b * q_tiles_per_batch + jnp.maximum(Q_GROUP * j - 2 + part, 0), 0)

    blk = pl.BlockSpec((Q_GROUP * Q_TILE, D_A), lambda b, j: (b * nq + j, 0))
    kv_specs = [pl.BlockSpec((Q_TILE, D_A), kmap(part)) for part in range(Q_GROUP + 2)]
    out_specs = [blk]
    out_shape = [jax.ShapeDtypeStruct((n_batch * nq * Q_GROUP * Q_TILE, D_A), BF16)]
    scratch = [pltpu.VMEM((N_HEADS, Q_TILE, 3 * Q_TILE), F32)]
    copies = 0
    if zeros_shape is not None:
        rows, cols = zeros_shape
        per_step, rem = divmod(rows, n_batch * nq)
        assert rem == 0 and per_step % 8 == 0
        zr = max(r for r in range(8, ZERO_ROWS_MAX + 1, 8) if per_step % r == 0)
        copies = per_step // zr
        out_specs.append(pl.BlockSpec(memory_space=pl.ANY))
        out_shape.append(jax.ShapeDtypeStruct((rows, cols), F32))
        scratch += [pltpu.VMEM((zr, cols), F32), pltpu.SemaphoreType.DMA(())]
    res = pl.pallas_call(
        functools.partial(_attn_prompt_kernel, copies),
        grid=(n_batch, nq),
        in_specs=[blk] + kv_specs + kv_specs + [
            pl.BlockSpec(diag.shape, lambda b, j: (0, 0)),
            pl.BlockSpec((1, D_A), lambda b, j: (0, 0)),
        ],
        out_specs=out_specs,
        out_shape=out_shape,
        scratch_shapes=scratch,
        compiler_params=_cparams("arbitrary", "arbitrary"),
        name="attn_prompt_zeros" if zeros_shape is not None else "attn_prompt",
    )(q, *([k] * (Q_GROUP + 2)), *([v] * (Q_GROUP + 2)), diag, na)
    return res if zeros_shape is not None else res[0]


def _attn_sample_kernel(q_ref, kn_ref, vn_ref, ck_ref, cv_ref, bias_c_ref, bias_n_ref, na_ref, o_ref):
    def kv_slabs(p):
        sl = slice(p * LANES, (p + 1) * LANES)

        def cached(c_ref):
            return c_ref[2 * p:2 * p + 2].reshape(LANES, c_ref.shape[2]).astype(BF16)
        return [(cached(ck_ref), cached(cv_ref), True), (kn_ref[:, sl], vn_ref[:, sl], False)]

    def bias_of(h, b):
        return bias_c_ref[h] if b == 0 else bias_n_ref[h]

    def q_of(p):
        return q_ref[:, p * LANES:(p + 1) * LANES]

    r = q_ref.shape[0]
    o_ref[...] = _attend(q_of, r, kv_slabs, bias_of, [(0, r), (0, r)], None, na_ref[...]).astype(BF16)


def _attn_sample_call(layer, q, k, v, cache_kt, cache_vt, bias_c, bias_n, na, *, first_chunk):
    _, nb, nh, hd, win = cache_kt.shape
    blk = pl.BlockSpec((CHUNK, D_A), lambda b: (first_chunk + b, 0))
    cache = pl.BlockSpec((None, None, nh, hd, win), lambda b: (layer, b, 0, 0, 0))
    return pl.pallas_call(
        _attn_sample_kernel,
        grid=(nb,),
        in_specs=[blk, blk, blk, cache, cache,
                  pl.BlockSpec(bias_c.shape, lambda b: (0, 0, 0)),
                  pl.BlockSpec(bias_n.shape, lambda b: (0, 0, 0)),
                  pl.BlockSpec((1, D_A), lambda b: (0, 0))],
        out_specs=pl.BlockSpec((CHUNK, D_A), lambda b: (b, 0)),
        out_shape=jax.ShapeDtypeStruct((nb * CHUNK, D_A), BF16),
        compiler_params=_cparams("arbitrary"),
        name="attn_sample",
    )(q, k, v, cache_kt, cache_vt, bias_c, bias_n, na)


def _mixer_close(ntp, x_ref, anp_ref, ans_ref, bn_ref, w_ref, g1_ref, sc_ref, sh_ref, lg_ref, lb_ref):
    an = jnp.where(pl.program_id(0) >= ntp, ans_ref[...], anp_ref[...])
    mix = _dot(an, w_ref[0:D_A, :]) + _dot(bn_ref[...], w_ref[D_A:D_A + D_B, :])
    x1 = _ln(ALPHA * x_ref[...] + _gate_rows(mix, g1_ref), lg_ref[...], lb_ref[...])
    return x1, _modulate(x1, sc_ref, sh_ref)


def _mixer_specs(layer, d, ntp, w_shape):
    def tile(i):
        return (i, 0)

    def const2(i):
        return (0, 0)

    return [
        pl.BlockSpec((ROW_TILE, d), tile),
        pl.BlockSpec((ROW_TILE, D_A), lambda i: (jnp.minimum(i, ntp - 1), 0)),
        pl.BlockSpec((ROW_TILE, D_A), lambda i: (jnp.maximum(i - ntp, 0), 0)),
        pl.BlockSpec((ROW_TILE, D_B), tile),
        pl.BlockSpec((None,) + w_shape[1:], lambda i: (layer, 0, 0)),
        _mod_spec(layer, G1, ROW_TILE, d),
        _mod_spec(layer, SC2, ROW_TILE, d),
        _mod_spec(layer, SH2, ROW_TILE, d),
        pl.BlockSpec((1, d), const2),
        pl.BlockSpec((1, d), const2),
    ]


def _outproj_kernel(ntp, x_ref, anp_ref, ans_ref, bn_ref, w_ref, g1_ref, sc_ref, sh_ref, lg_ref, lb_ref,
                    rc_ref, x1_ref, hp_ref, rt_ref, et_ref):
    x1, h2 = _mixer_close(ntp, x_ref, anp_ref, ans_ref, bn_ref, w_ref, g1_ref, sc_ref, sh_ref, lg_ref, lb_ref)
    x1_ref[...] = x1
    hp_ref[...] = h2
    h_hi, h_lo = _split_bf16(h2)
    both = _dot(h_hi, rc_ref[...])
    logits = both[:, :LANES] + both[:, LANES:] + _dot(h_lo, rc_ref[:, :LANES])
    lane = lax.broadcasted_iota(jnp.int32, logits.shape, 1)
    logits = jnp.where(lane < N_EXPERTS, logits, -jnp.inf)
    m1 = logits.max(axis=1, keepdims=True)
    i1 = jnp.where(logits == m1, lane, LANES).min(axis=1, keepdims=True)
    rest_l = jnp.where(lane == i1, -jnp.inf, logits)
    m2 = rest_l.max(axis=1, keepdims=True)
    i2 = jnp.where(rest_l == m2, lane, LANES).min(axis=1, keepdims=True)
    e2 = jnp.exp(m2 - m1)
    w1 = 1.0 / (1.0 + e2)
    w2 = e2 / (1.0 + e2)
    rt = jnp.where(lane == 0, i1.astype(F32),
                   jnp.where(lane == 1, i2.astype(F32),
                             jnp.where(lane == 2, w1, jnp.where(lane == 3, w2, 0.0))))
    rt_ref[...] = rt
    et_ref[...] = jnp.transpose(rt)[0:8, :]


def _outproj_call(layer, x, an_p, an_s, bn, w_bf, mods, lg, lb, r_cat):
    t, d = x.shape
    nt = t // ROW_TILE
    ntp = an_p.shape[0] // ROW_TILE
    tile = pl.BlockSpec((ROW_TILE, d), lambda i: (i, 0))
    return pl.pallas_call(
        functools.partial(_outproj_kernel, ntp),
        grid=(nt,),
        in_specs=_mixer_specs(layer, d, ntp, w_bf.shape) + [pl.BlockSpec(r_cat.shape, lambda i: (0, 0))],
        out_specs=[tile, tile, pl.BlockSpec((ROW_TILE, LANES), lambda i: (i, 0)),
                   pl.BlockSpec((None, 8, ROW_TILE), lambda i: (i, 0, 0))],
        out_shape=[jax.ShapeDtypeStruct((t, d), F32), jax.ShapeDtypeStruct((t, d), F32),
                   jax.ShapeDtypeStruct((t, LANES), F32), jax.ShapeDtypeStruct((nt, 8, ROW_TILE), F32)],
        compiler_params=_cparams("arbitrary"),
        name="out_proj_route",
    )(x, an_p, an_s, bn, w_bf, mods, mods, mods, lg, lb, r_cat)


def _swiglu_chains(n_sub, xb_ref, wg_ref, wu_ref, wd_ref, o_ref):
    wg = wg_ref[0].astype(BF16)
    wu = wu_ref[0].astype(BF16)
    wd = wd_ref[0].astype(BF16)
    for s in range(n_sub):
        rows = slice(s * FFN_SUB, (s + 1) * FFN_SUB)
        xs = xb_ref[rows, :]
        g = _dot(xs, wg)
        u = _dot(xs, wu)
        a = (g * jax.nn.sigmoid(g) * u).astype(BF16)
        o_ref[rows, :] += _dot(a, wd)


def _ffn_kernel(te_ref, tv_ref, x_ref, wg_ref, wu_ref, wd_ref, o_ref, xb_ref):
    del te_ref
    i = pl.program_id(0)
    j = pl.program_id(1)
    valid = tv_ref[i]
    nsub = (valid + FFN_SUB - 1) // FFN_SUB

    @pl.when(j == 0)
    def _():
        xb_ref[...] = x_ref[...].astype(BF16)
        o_ref[...] = jnp.zeros(o_ref.shape, F32)

    for n_sub in range(1, x_ref.shape[0] // FFN_SUB + 1):
        pl.when(nsub == n_sub)(functools.partial(_swiglu_chains, n_sub, xb_ref, wg_ref, wu_ref, wd_ref, o_ref))


def _ffn_call(tile_expert, tile_valid, xs, wg, wu, wd, tile_rows):
    p, d = xs.shape
    f = wg.shape[2]
    nt = p // tile_rows
    nc = f // FFN_COLS

    def active_col(i, j, tv):
        return jnp.where(tv[i] > 0, j, nc - 1)

    in_specs = [
        pl.BlockSpec((tile_rows, d), lambda i, j, te, tv: (i, 0)),
        pl.BlockSpec((1, d, FFN_COLS), lambda i, j, te, tv: (te[i], 0, active_col(i, j, tv))),
        pl.BlockSpec((1, d, FFN_COLS), lambda i, j, te, tv: (te[i], 0, active_col(i, j, tv))),
        pl.BlockSpec((1, FFN_COLS, d), lambda i, j, te, tv: (te[i], active_col(i, j, tv), 0)),
    ]
    return pl.pallas_call(
        _ffn_kernel,
        grid_spec=pltpu.PrefetchScalarGridSpec(
            num_scalar_prefetch=2,
            grid=(nt, nc),
            in_specs=in_specs,
            out_specs=pl.BlockSpec((tile_rows, d), lambda i, j, te, tv: (i, 0)),
            scratch_shapes=[pltpu.VMEM((tile_rows, d), BF16)],
        ),
        out_shape=jax.ShapeDtypeStruct((p, d), F32),
        compiler_params=_cparams("arbitrary", "arbitrary"),
        name="swiglu_grouped",
    )(tile_expert, tile_valid, xs, wg, wu, wd)


def _dense_layer_kernel(ntp, x_ref, anp_ref, ans_ref, bn_ref, w_ref, g1_ref, sc_ref, sh_ref, lg1_ref, lb1_ref,
                        wg_ref, wu_ref, wd_ref, g2_ref, lg2_ref, lb2_ref, o_ref):
    x1, h2 = _mixer_close(ntp, x_ref, anp_ref, ans_ref, bn_ref, w_ref, g1_ref, sc_ref, sh_ref, lg1_ref, lb1_ref)
    h = h2.astype(BF16)
    f = wg_ref.shape[1]
    acc = None
    for c0 in range(0, f, FFN_SPLIT):
        c1 = min(c0 + FFN_SPLIT, f)
        g = _dot(h, wg_ref[:, c0:c1])
        u = _dot(h, wu_ref[:, c0:c1])
        a = (g * jax.nn.sigmoid(g) * u).astype(BF16)
        part = _dot(a, wd_ref[c0:c1, :])
        acc = part if acc is None else acc + part
    o_ref[...] = _ln(ALPHA * x1 + _gate_rows(acc, g2_ref), lg2_ref[...], lb2_ref[...])


def _dense_layer_call(layer, x, an_p, an_s, bn, w_bf, mods, lg1, lb1, wg_bf, wu_bf, wd_bf, lg2, lb2):
    t, d = x.shape
    f = wg_bf.shape[1]
    ntp = an_p.shape[0] // ROW_TILE
    tile = pl.BlockSpec((ROW_TILE, d), lambda i: (i, 0))
    vec = pl.BlockSpec((1, d), lambda i: (0, 0))

    def resident(shape):
        return pl.BlockSpec(shape, lambda i: (0, 0), pipeline_mode=pl.Buffered(1))

    return pl.pallas_call(
        functools.partial(_dense_layer_kernel, ntp),
        grid=(t // ROW_TILE,),
        in_specs=_mixer_specs(layer, d, ntp, w_bf.shape) + [
            resident((d, f)), resident((d, f)), resident((f, d)), _mod_spec(layer, G2, ROW_TILE, d), vec, vec],
        out_specs=tile,
        out_shape=jax.ShapeDtypeStruct((t, d), F32),
        compiler_params=_cparams("arbitrary"),
        name="dense_layer_close",
    )(x, an_p, an_s, bn, w_bf, mods, mods, mods, lg1, lb1, wg_bf, wu_bf, wd_bf, mods, lg2, lb2)


def _dispatch_kernel(pos_ref, src_ref, init_ref, dst_ref, sem):
    del init_ref

    def issue(r, carry):
        row = src_ref.at[pl.ds(r, 1), :]
        pltpu.make_async_copy(row, dst_ref.at[pl.ds(pos_ref[0, r], 1), :], sem).start()
        pltpu.make_async_copy(row, dst_ref.at[pl.ds(pos_ref[0, ROW_TILE + r], 1), :], sem).start()
        return carry
    lax.fori_loop(0, ROW_TILE, issue, 0, unroll=8)
    for _ in range(2):
        pltpu.make_async_copy(src_ref, dst_ref.at[pl.ds(0, ROW_TILE), :], sem).wait()


def _dispatch_call(pos_tiles, hp, zeros):
    t, d = hp.shape
    n_rows_sorted = zeros.shape[0]
    nt = t // ROW_TILE
    return pl.pallas_call(
        _dispatch_kernel,
        grid=(nt,),
        in_specs=[
            pl.BlockSpec((None, 1, 2 * ROW_TILE), lambda i: (i, 0, 0), memory_space=pltpu.SMEM),
            pl.BlockSpec((ROW_TILE, d), lambda i: (i, 0)),
            pl.BlockSpec(memory_space=pl.ANY),
        ],
        out_specs=pl.BlockSpec(memory_space=pl.ANY),
        out_shape=jax.ShapeDtypeStruct((n_rows_sorted, d), hp.dtype),
        scratch_shapes=[pltpu.SemaphoreType.DMA(())],
        input_output_aliases={2: 0},
        compiler_params=_cparams("arbitrary"),
        name="dispatch_rows",
    )(pos_tiles, hp, zeros)


def _final_moe_kernel(ntp, pos_ref, x_ref, rt_ref, g2_ref, lg_ref, lb_ref, ys_ref, op_ref, os_ref, ybuf, sem):
    def issue(r, carry):
        pltpu.make_async_copy(ys_ref.at[pl.ds(pos_ref[0, r], 1), :], ybuf.at[0, pl.ds(r, 1), :], sem).start()
        pltpu.make_async_copy(ys_ref.at[pl.ds(pos_ref[0, ROW_TILE + r], 1), :], ybuf.at[1, pl.ds(r, 1), :],
                              sem).start()
        return carry
    lax.fori_loop(0, ROW_TILE, issue, 0, unroll=8)
    for k in range(2):
        pltpu.make_async_copy(ys_ref.at[pl.ds(0, ROW_TILE), :], ybuf.at[k], sem).wait()
    rt = rt_ref[...]
    f = rt[:, 2:3] * ybuf[0] + rt[:, 3:4] * ybuf[1]
    y = _ln(ALPHA * x_ref[...] + _gate_rows(f, g2_ref), lg_ref[...], lb_ref[...])
    i = pl.program_id(0)

    @pl.when(i < ntp)
    def _():
        op_ref[...] = y

    @pl.when(i >= ntp)
    def _():
        os_ref[...] = y


def _final_moe_call(layer, pos_tiles, x1, rt, mods, lg, lb, ys, *, n_prompt_tiles):
    t, d = x1.shape
    ntp = n_prompt_tiles
    tile = pl.BlockSpec((ROW_TILE, d), lambda i: (i, 0))
    vec = pl.BlockSpec((1, d), lambda i: (0, 0))
    return pl.pallas_call(
        functools.partial(_final_moe_kernel, ntp),
        grid=(t // ROW_TILE,),
        in_specs=[
            pl.BlockSpec((None, 1, 2 * ROW_TILE), lambda i: (i, 0, 0), memory_space=pltpu.SMEM),
            tile,
            pl.BlockSpec((ROW_TILE, LANES), lambda i: (i, 0)),
            _mod_spec(layer, G2, ROW_TILE, d),
            vec, vec,
            pl.BlockSpec(memory_space=pl.ANY),
        ],
        out_specs=[pl.BlockSpec((ROW_TILE, d), lambda i: (jnp.minimum(i, ntp - 1), 0)),
                   pl.BlockSpec((ROW_TILE, d), lambda i: (jnp.maximum(i - ntp, 0), 0))],
        out_shape=[jax.ShapeDtypeStruct((ntp * ROW_TILE, d), F32),
                   jax.ShapeDtypeStruct((t - ntp * ROW_TILE, d), F32)],
        scratch_shapes=[pltpu.VMEM((2, ROW_TILE, d), F32), pltpu.SemaphoreType.DMA(())],
        compiler_params=_cparams("arbitrary"),
        name="final_moe",
    )(pos_tiles, x1, rt, mods, lg, lb, ys)


def _bias_diagonals(table, n_q, n_k):
    k = jnp.arange(n_q + n_k - 1)
    return table[:, jnp.clip(LEFT_CTX + (n_q - 1) - k, -REL_CLIP, REL_CLIP) + REL_CLIP].astype(F32) * LOG2E


def _rel_bias(table, n_q, n_k):
    diag = _bias_diagonals(table, n_q, n_k)
    n_diag = diag.shape[1]
    padded = jnp.concatenate([diag, jnp.zeros((diag.shape[0], 1), F32)], axis=1)
    skew = jnp.tile(padded, (1, n_q))[:, :n_q * n_diag].reshape(-1, n_q, n_diag)
    return skew[:, :, n_q - 1:n_q - 1 + n_k]


def _gating_weights(w_s, b_s):
    n = MLP_CHUNK
    tril = jnp.tril(jnp.ones((n, n), bool))
    wm = jnp.where(tril[None], w_s, 0.0)
    h = CHUNK
    top = wm[:, :h, :h]
    z = jnp.zeros_like(top)
    wm_s = jnp.concatenate([jnp.concatenate([top, z], 2), jnp.concatenate([z, top], 2)], 1)
    both = jnp.stack([wm, wm_s])
    wcat = jnp.concatenate([both[:, 0::2], both[:, 1::2]], axis=-1)
    bias_p = jnp.repeat(jnp.transpose(b_s), GROUP_DIM, axis=1)
    bias_s = jnp.concatenate([bias_p[:h], bias_p[:h]], 0)
    return wcat.astype(BF16), jnp.stack([bias_p, bias_s]).astype(F32)


def _route_plan(et, n_tiles, tile_rows):
    nt = et.shape[0]
    e0 = et[:, 0, :].astype(jnp.int32)
    e1 = et[:, 1, :].astype(jnp.int32)
    experts = jnp.arange(N_EXPERTS, dtype=jnp.int32)[:, None, None]
    is0 = e0[None] == experts
    is1 = e1[None] == experts
    hit = (is0 | is1).astype(jnp.int32)
    in_tile = jnp.cumsum(hit, axis=2)
    tile_tot = in_tile[:, :, -1]
    before_tile = jnp.cumsum(tile_tot, axis=1) - tile_tot
    before = before_tile[:, :, None] + in_tile - hit
    counts = jnp.sum(tile_tot, axis=1)
    tiles_e = (counts + tile_rows - 1) // tile_rows
    tile_end = jnp.cumsum(tiles_e)
    tile_start = tile_end - tiles_e
    slot = before + (tile_start * tile_rows)[:, None, None]
    pos0 = jnp.sum(jnp.where(is0, slot, 0), axis=0)
    pos1 = jnp.sum(jnp.where(is1, slot, 0), axis=0)
    pos = jnp.concatenate([pos0, pos1], axis=1).reshape(nt, 1, 2 * ROW_TILE)
    ti = jnp.arange(n_tiles)
    te = jnp.minimum(jnp.sum((ti[:, None] >= tile_end[None, :]).astype(jnp.int32), axis=1), N_EXPERTS - 1)
    tv = jnp.clip(counts[te] - (ti - tile_start[te]) * tile_rows, 0, tile_rows)
    tv = jnp.where(ti < tile_end[-1], tv, 0)
    last_e = te[jnp.maximum(tile_end[-1] - 1, 0)]
    te = jnp.where(ti < tile_end[-1], te, last_e)
    return pos.astype(jnp.int32), te.astype(jnp.int32), tv.astype(jnp.int32)


def kernel(x_prompt, x_sample, cache_k, cache_v, c_prompt, c_sample, ln_in_g, ln_in_b, w_in, w_out, rel_bias_table, gmlp_ln_g, gmlp_ln_b, gmlp_w_s, gmlp_b_s, out_norm_a, out_norm_b, ada_w, ada_b, ln1_g, ln1_b, ln2_g, ln2_b, ffn_w_gate, ffn_w_up, ffn_w_down, moe_router, moe_w_gate, moe_w_up, moe_w_down):
    nb, seq, d = x_prompt.shape
    ns, dseq, _ = x_sample.shape
    depth = w_in.shape[0]
    tp = nb * seq
    ts = ns * dseq
    t = tp + ts
    assert depth == DEPTH and dseq == CHUNK and seq % ROW_TILE == 0 and ts % ROW_TILE == 0
    assert seq % (Q_GROUP * Q_TILE) == 0 and cache_k.shape[2] == LEFT_CTX
    ntp = tp // ROW_TILE

    n_cond = nb + ns
    c_all = jnp.concatenate([c_prompt, c_sample], axis=0)
    c_pad = jnp.pad(c_all, ((0, (-n_cond) % 8), (0, 0)))
    mods = _ada_call(c_pad, ada_w, ada_b)
    mods = jnp.transpose(mods[:, :n_cond].reshape(depth, n_cond, 6, d), (0, 2, 1, 3))
    cpb = seq // CHUNK
    mods_p = jnp.broadcast_to(mods[:, :, :nb, None, :], (depth, 6, nb, cpb, d)).reshape(depth, 6, nb * cpb, d)
    mods = jnp.concatenate([mods_p, mods[:, :, nb:]], axis=2)

    def row(v):
        return v.reshape(1, -1)

    w_in_bf = w_in.astype(BF16)
    w_out_bf = w_out.astype(BF16)
    cache_kt = jnp.transpose(cache_k, (0, 1, 3, 4, 2))
    cache_vt = jnp.transpose(cache_v, (0, 1, 3, 4, 2))
    k_tail, v_tail, gv_rows = [], [], []
    for l in range(depth):
        wcat, bs = _gating_weights(gmlp_w_s[l], gmlp_b_s[l])
        if l == 0:
            x_in = (x_prompt.reshape(tp, d), x_sample.reshape(ts, d), row(ln_in_g), row(ln_in_b))
        else:
            x_in = (x,)
        res = _inproj_call(l, x_in, mods, w_in_bf, row(gmlp_ln_g[l]), row(gmlp_ln_b[l]), wcat, bs,
                           row(out_norm_b[l]),
                           n_prompt_tiles=ntp, tiles_per_batch=seq // ROW_TILE, n_batch=nb, n_sample_rows=ts)
        if l == 0:
            x, q, k, v, kf, vf, gv, bn = res
        else:
            q, k, v, kf, vf, gv, bn = res
        k_tail.append(kf)
        v_tail.append(vf)
        gv_rows.append(gv)

        table = rel_bias_table[l]
        diag_p = jnp.pad(_bias_diagonals(table, Q_TILE, 3 * Q_TILE), ((0, 0), (1, 0)))
        bias_s = _rel_bias(table, CHUNK, LEFT_CTX + CHUNK)
        na = row(out_norm_a[l])
        routed = l % 2 == 1
        n_tiles = -(-2 * t // MOE_TILE) + N_EXPERTS
        an_p = _attn_prompt_call(q, k, v, diag_p, na, n_batch=nb, q_tiles_per_batch=seq // Q_TILE,
                                 zeros_shape=(n_tiles * MOE_TILE, d) if routed else None)
        if routed:
            an_p, sorted_init = an_p
        an_s = _attn_sample_call(l, q, k, v, cache_kt, cache_vt, bias_s[:, :, :LEFT_CTX], bias_s[:, :, LEFT_CTX:], na,
                                 first_chunk=tp // CHUNK)

        lg1, lb1, lg2, lb2 = row(ln1_g[l]), row(ln1_b[l]), row(ln2_g[l]), row(ln2_b[l])
        i = l // 2
        if not routed:
            x = _dense_layer_call(l, x, an_p, an_s, bn, w_out_bf, mods, lg1, lb1, ffn_w_gate[i].astype(BF16),
                                  ffn_w_up[i].astype(BF16), ffn_w_down[i].astype(BF16), lg2, lb2)
        else:
            r_pad = jnp.pad(moe_router[i], ((0, 0), (0, LANES - N_EXPERTS)))
            r_hi = r_pad.astype(BF16)
            r_lo = (r_pad - r_hi.astype(F32)).astype(BF16)
            r_cat = jnp.concatenate([r_hi, r_lo], axis=1)
            x1, hp, rt, et = _outproj_call(l, x, an_p, an_s, bn, w_out_bf, mods, lg1, lb1, r_cat)
            pos, te, tv = _route_plan(et, n_tiles, MOE_TILE)
            xs = _dispatch_call(pos, hp, sorted_init)
            ys = _ffn_call(te, tv, xs, moe_w_gate[i], moe_w_up[i], moe_w_down[i], MOE_TILE)
            y_p, y_s = _final_moe_call(l, pos, x1, rt, mods, lg2, lb2, ys, n_prompt_tiles=ntp)

    y_prompt = y_p.reshape(nb, seq, d)
    y_sample = y_s.reshape(ns, dseq, d)

    def tails(rows):
        kp = jnp.stack([r[:nb * ROW_TILE].reshape(nb, ROW_TILE, N_HEADS, HEAD_DIM) for r in rows])
        ksn = jnp.stack([r[nb * ROW_TILE:].reshape(ns, dseq, N_HEADS, HEAD_DIM) for r in rows])
        return kp, ksn

    k_prompt_new, k_sample_new = tails(k_tail)
    v_prompt_new, v_sample_new = tails(v_tail)
    gmlp_v_sample_new = jnp.stack([g.reshape(ns, dseq, D_B) for g in gv_rows])
    return (y_prompt, y_sample, k_prompt_new, v_prompt_new, k_sample_new, v_sample_new, gmlp_v_sample_new)
```

```python
import functools

import jax
import jax.numpy as jnp
from jax import lax
from jax.experimental import pallas as pl
from jax.experimental.pallas import tpu as pltpu

CHUNK = 64
LEFT_CHUNKS = 8
LEFT_CTX = LEFT_CHUNKS * CHUNK
N_HEADS = 8
HEAD_DIM = 64
D_A = N_HEADS * HEAD_DIM
N_GROUPS = 8
GROUP_DIM = 64
D_B = N_GROUPS * GROUP_DIM
MLP_CHUNK = 128
REL_CLIP = 128
N_EXPERTS = 8
DEPTH = 2
ALPHA = (2 * DEPTH) ** 0.25
LN_EPS = 1e-5
ATTN_SCALE = HEAD_DIM ** -0.5
LOG2E = 1.4426950408889634
NEG = -1e30
SH1, SC1, G1, SH2, SC2, G2 = range(6)

LANES = 128
ROW_TILE = 512
IN_PART = 512
Q_TILE = 256
Q_GROUP = 4
ZERO_ROWS_MAX = 1024
FFN_SPLIT = 1536
MOE_TILE = 2048
FFN_SUB = 512
FFN_COLS = 256
VMEM_LIMIT = 56 * 1024 * 1024

BF16 = jnp.bfloat16
F32 = jnp.float32


def _cparams(*sem):
    return pltpu.CompilerParams(dimension_semantics=sem, vmem_limit_bytes=VMEM_LIMIT)


def _ln(x, g, b):
    mu = jnp.mean(x, axis=-1, keepdims=True)
    xc = x - mu
    var = jnp.mean(xc * xc, axis=-1, keepdims=True)
    return xc * lax.rsqrt(var + LN_EPS) * g + b


def _rms(x, g):
    return x * lax.rsqrt(jnp.mean(x * x, axis=-1, keepdims=True) + LN_EPS) * g


def _modulate(x, sc_ref, sh_ref):
    parts = []
    for c in range(x.shape[0] // CHUNK):
        xc = x[c * CHUNK:(c + 1) * CHUNK]
        parts.append(xc * (1.0 + sc_ref[c:c + 1, :]) + sh_ref[c:c + 1, :])
    return jnp.concatenate(parts, axis=0)


def _gate_rows(x, g_ref):
    parts = []
    for c in range(x.shape[0] // CHUNK):
        parts.append(x[c * CHUNK:(c + 1) * CHUNK] * (1.0 + g_ref[c:c + 1, :]))
    return jnp.concatenate(parts, axis=0)


def _mod_spec(layer, comp, rows, d):
    return pl.BlockSpec((None, None, rows // CHUNK, d), lambda i, *_: (layer, comp, i, 0))


def _split_bf16(x):
    hi = x.astype(BF16)
    lo = (x - hi.astype(F32)).astype(BF16)
    return hi, lo


def _dot(a, b):
    return jnp.dot(a, b, preferred_element_type=F32)


def _ada_kernel(c_ref, w_ref, b_ref, o_ref):
    c = c_ref[...]
    s = c * jax.nn.sigmoid(c)
    s_hi, s_lo = _split_bf16(s)
    w_hi, w_lo = _split_bf16(w_ref[0])
    o_ref[0] = _dot(s_hi, w_hi) + _dot(s_lo, w_hi) + _dot(s_hi, w_lo) + b_ref[0]


def _ada_call(c_pad, ada_w, ada_b):
    depth, d, n = ada_w.shape
    rows = c_pad.shape[0]
    tn = n // 4
    return pl.pallas_call(
        _ada_kernel,
        grid=(depth, n // tn),
        in_specs=[
            pl.BlockSpec((rows, d), lambda l, j: (0, 0)),
            pl.BlockSpec((1, d, tn), lambda l, j: (l, 0, j)),
            pl.BlockSpec((1, 1, tn), lambda l, j: (l, 0, j)),
        ],
        out_specs=pl.BlockSpec((1, rows, tn), lambda l, j: (l, 0, j)),
        out_shape=jax.ShapeDtypeStruct((depth, rows, n), F32),
        compiler_params=_cparams("arbitrary", "arbitrary"),
        name="ada_mod",
    )(c_pad, ada_w, ada_b.reshape(depth, 1, n))


def _inproj_kernel(first, ntp, *refs):
    if first:
        xp_ref, xs_ref, lng_ref, lnb_ref = refs[:4]
        xn_ref = refs[-8]
        refs = refs[4:-8] + refs[-7:]
    else:
        x_ref = refs[0]
        refs = refs[1:]
    (sc_ref, sh_ref, w_ref, gg_ref, gb_ref, wcat_ref, bs_ref, nb_ref,
     q_ref, k_ref, v_ref, kf_ref, vf_ref, gv_ref, bn_ref) = refs
    lane = lax.broadcasted_iota(jnp.int32, (MLP_CHUNK, LANES), 1)
    lo = lane < GROUP_DIM
    zero = jnp.zeros((MLP_CHUNK, LANES), BF16)

    def part(r0, n_rows):
        rows = slice(r0, r0 + n_rows)
        mod_rows = slice(r0 // CHUNK, (r0 + n_rows) // CHUNK)
        if first:
            x = jnp.where(pl.program_id(0) >= ntp, xs_ref[rows, :], xp_ref[rows, :])
            x = _ln(x, lng_ref[...], lnb_ref[...])
            xn_ref[rows, :] = x
        else:
            x = x_ref[rows, :]
        h = _modulate(x, sc_ref.at[mod_rows, :], sh_ref.at[mod_rows, :]).astype(BF16)

        q = _dot(h, w_ref[:, 0:D_A])
        q_ref[rows, :] = (q * (ATTN_SCALE * LOG2E)).astype(BF16)
        k = _dot(h, w_ref[:, D_A:2 * D_A])
        k_ref[rows, :] = k.astype(BF16)
        kf_ref[rows, :] = k
        v = _dot(h, w_ref[:, 2 * D_A:3 * D_A])
        v_ref[rows, :] = v.astype(BF16)
        vf_ref[rows, :] = v

        u = jax.nn.gelu(_dot(h, w_ref[:, 3 * D_A:3 * D_A + D_B]), approximate=True)
        vb = jax.nn.gelu(_dot(h, w_ref[:, 3 * D_A + D_B:3 * D_A + 2 * D_B]), approximate=True)
        vn = _ln(vb, gg_ref[...], gb_ref[...])
        gv_ref[rows, :] = vn

        vnb = vn.astype(BF16)
        n_chunks = n_rows // MLP_CHUNK
        mixed = []
        for p in range(D_B // LANES):
            stacks = []
            for c in range(n_chunks):
                slab = vnb[c * MLP_CHUNK:(c + 1) * MLP_CHUNK, p * LANES:(p + 1) * LANES]
                stacks.append(jnp.concatenate([jnp.where(lo, slab, zero), jnp.where(lo, zero, slab)], axis=0))
            mixed.append(_dot(wcat_ref[0, p], jnp.concatenate(stacks, axis=1)))
        outs = []
        for c in range(n_chunks):
            c0 = c * MLP_CHUNK
            mix_c = jnp.concatenate([m[:, c * LANES:(c + 1) * LANES] for m in mixed], axis=1) + bs_ref[0]
            outs.append(u[c0:c0 + MLP_CHUNK, :] * mix_c)
        bo = jnp.concatenate(outs, axis=0)
        bn_ref[rows, :] = _rms(bo, nb_ref[...]).astype(BF16)

    for r0 in range(0, ROW_TILE, IN_PART):
        part(r0, IN_PART)


def _inproj_call(layer, x_in, mods, w_bf, gg, gb, wcat, bs, nb, *, n_prompt_tiles, tiles_per_batch, n_batch,
                 n_sample_rows):
    first = len(x_in) == 4
    d = x_in[0].shape[1]
    t = mods.shape[2] * CHUNK
    nt = t // ROW_TILE
    ntp = n_prompt_tiles
    n_tail = n_batch + (nt - ntp)

    def tile(i):
        return (i, 0)

    def const2(i):
        return (0, 0)

    def selmap4(i):
        return (jnp.where(i >= ntp, 1, 0), 0, 0, 0)

    def selmap3(i):
        return (jnp.where(i >= ntp, 1, 0), 0, 0)

    def tailmap(i):
        return (jnp.where(i < ntp, i // tiles_per_batch, n_batch + i - ntp), 0)

    def gvmap(i):
        return (jnp.maximum(i - ntp, 0), 0)

    if first:
        x_specs = [
            pl.BlockSpec((ROW_TILE, d), lambda i: (jnp.minimum(i, ntp - 1), 0)),
            pl.BlockSpec((ROW_TILE, d), lambda i: (jnp.maximum(i - ntp, 0), 0)),
            pl.BlockSpec((1, d), const2),
            pl.BlockSpec((1, d), const2),
        ]
    else:
        x_specs = [pl.BlockSpec((ROW_TILE, d), tile)]
    in_specs = x_specs + [
        _mod_spec(layer, SC1, ROW_TILE, d),
        _mod_spec(layer, SH1, ROW_TILE, d),
        pl.BlockSpec((None,) + w_bf.shape[1:], lambda i: (layer, 0, 0)),
        pl.BlockSpec((1, D_B), const2),
        pl.BlockSpec((1, D_B), const2),
        pl.BlockSpec((1,) + wcat.shape[1:], selmap4),
        pl.BlockSpec((1, MLP_CHUNK, D_B), selmap3),
        pl.BlockSpec((1, D_B), const2),
    ]
    out_specs = [
        pl.BlockSpec((ROW_TILE, D_A), tile),
        pl.BlockSpec((ROW_TILE, D_A), tile),
        pl.BlockSpec((ROW_TILE, D_A), tile),
        pl.BlockSpec((ROW_TILE, D_A), tailmap),
        pl.BlockSpec((ROW_TILE, D_A), tailmap),
        pl.BlockSpec((ROW_TILE, D_B), gvmap),
        pl.BlockSpec((ROW_TILE, D_B), tile),
    ]
    out_shape = [
        jax.ShapeDtypeStruct((t, D_A), BF16),
        jax.ShapeDtypeStruct((t, D_A), BF16),
        jax.ShapeDtypeStruct((t, D_A), BF16),
        jax.ShapeDtypeStruct((n_tail * ROW_TILE, D_A), F32),
        jax.ShapeDtypeStruct((n_tail * ROW_TILE, D_A), F32),
        jax.ShapeDtypeStruct((n_sample_rows, D_B), F32),
        jax.ShapeDtypeStruct((t, D_B), BF16),
    ]
    if first:
        out_specs = [pl.BlockSpec((ROW_TILE, d), tile)] + out_specs
        out_shape = [jax.ShapeDtypeStruct((t, d), F32)] + out_shape
    return pl.pallas_call(
        functools.partial(_inproj_kernel, first, ntp),
        grid=(nt,),
        in_specs=in_specs,
        out_specs=out_specs,
        out_shape=out_shape,
        compiler_params=_cparams("arbitrary"),
        name="in_proj_first" if first else "in_proj",
    )(*x_in, mods, mods, w_bf, gg, gb, wcat, bs, nb)


def _row_max(pieces):
    folded = None
    m = None
    for s in pieces:
        if s.shape[1] % LANES == 0:
            for c0 in range(0, s.shape[1], LANES):
                piece = s[:, c0:c0 + LANES]
                folded = piece if folded is None else jnp.maximum(folded, piece)
        else:
            mj = s.max(axis=1, keepdims=True)
            m = mj if m is None else jnp.maximum(m, mj)
    if folded is not None:
        mj = folded.max(axis=1, keepdims=True)
        m = mj if m is None else jnp.maximum(m, mj)
    return m


def _attend(q_of, r, kv_slabs, bias_of, part_rows, part_ok, na):
    nt_dims = (((1,), (1,)), ((), ()))
    lo = lax.broadcasted_iota(jnp.int32, (1, LANES), 1) < HEAD_DIM
    lo_t = lax.broadcasted_iota(jnp.int32, (LANES, 1), 0) < HEAD_DIM
    bounds = sorted({0, r} | {b for rows in part_rows for b in rows})
    segs = list(zip(bounds[:-1], bounds[1:]))

    def covered(seg, rows):
        return rows[0] <= seg[0] and seg[1] <= rows[1]

    outs = []
    for p in range(D_A // LANES):
        qp = q_of(p)
        slabs = kv_slabs(p)
        accs = []
        for half in range(2):
            h = 2 * p + half
            keep = lo if half == 0 else jnp.logical_not(lo)
            keep_t = lo_t if half == 0 else jnp.logical_not(lo_t)
            qh = jnp.where(keep, qp, jnp.zeros_like(qp))
            ss = []
            for j, (kp, _, transposed) in enumerate(slabs):
                qj = qh[part_rows[j][0]:part_rows[j][1]]
                if transposed:
                    s = _dot(qj, kp)
                else:
                    s = lax.dot_general(qj, kp, nt_dims, preferred_element_type=F32)
                s = s + bias_of(h, j)
                if part_ok is not None and part_ok[j] is not None:
                    s = jnp.where(part_ok[j], s, NEG)
                ss.append(s)
            m_segs = [_row_max([s[a - rows[0]:b - rows[0]] for s, rows in zip(ss, part_rows) if covered((a, b), rows)])
                      for a, b in segs]
            acc_segs = [jnp.zeros((b - a, LANES), F32) for a, b in segs]
            for s, (_, vp, transposed), rows in zip(ss, slabs, part_rows):
                m = jnp.concatenate([m_segs[i] for i, seg in enumerate(segs) if covered(seg, rows)], axis=0)
                e = jnp.exp2(s - m).astype(BF16)
                if transposed:
                    vh = jnp.where(keep_t, vp, jnp.ones_like(vp))
                    c = lax.dot_general(e, vh, nt_dims, preferred_element_type=F32)
                else:
                    c = _dot(e, jnp.where(keep, vp, jnp.ones_like(vp)))
                for i, (a, b) in enumerate(segs):
                    if covered((a, b), rows):
                        acc_segs[i] = acc_segs[i] + c[a - rows[0]:b - rows[0]]
            accs.append(jnp.concatenate(acc_segs, axis=0))
        num = jnp.where(lo, accs[0], accs[1])
        den = pltpu.roll(jnp.where(lo, accs[1], accs[0]), HEAD_DIM, 1)
        outs.append(num / den)
    a = jnp.concatenate(outs, axis=1)
    return _rms(a, na)


def _attn_prompt_kernel(zero_fill, q_ref, *refs):
    n_parts = Q_GROUP + 2
    k_refs = refs[:n_parts]
    v_refs = refs[n_parts:2 * n_parts]
    diag_ref, na_ref, o_ref = refs[2 * n_parts:2 * n_parts + 3]
    rest = refs[2 * n_parts + 3:]
    j = pl.program_id(1)
    if zero_fill:
        z_hbm, bias_ref, zbuf, zsem = rest
    else:
        bias_ref, = rest

    @pl.when((pl.program_id(0) == 0) & (j == 0))
    def _():
        shape = (Q_TILE, 3 * Q_TILE)
        q_chunk = lax.broadcasted_iota(jnp.int32, shape, 0) // CHUNK
        k_chunk = lax.broadcasted_iota(jnp.int32, shape, 1) // CHUNK
        visible = (k_chunk >= q_chunk) & (k_chunk <= q_chunk + LEFT_CHUNKS)
        for h in range(N_HEADS):
            rows = jnp.broadcast_to(diag_ref[h:h + 1, :], (Q_TILE, diag_ref.shape[1]))
            skew = pltpu.roll(rows, 0, 1, stride=1, stride_axis=0)
            bias_ref[h] = jnp.where(visible, skew[:, Q_TILE:], NEG)

    if zero_fill:
        step = pl.program_id(0) * pl.num_programs(1) + j
        zr = zbuf.shape[0]
        zbuf[...] = jnp.zeros(zbuf.shape, zbuf.dtype)
        z_copies = [pltpu.make_async_copy(zbuf, z_hbm.at[pl.ds((step * zero_fill + c) * zr, zr), :], zsem)
                    for c in range(zero_fill)]
        for cp in z_copies:
            cp.start()
    tiles_of = [[t for t in range(Q_GROUP) if t <= b <= t + 2] for b in range(n_parts)]
    part_rows = [(ts[0] * Q_TILE, (ts[-1] + 1) * Q_TILE) for ts in tiles_of]

    def q_of(p):
        return q_ref[:, p * LANES:(p + 1) * LANES]

    def kv_slabs(p):
        sl = slice(p * LANES, (p + 1) * LANES)
        return [(kr[:, sl], vr[:, sl], False) for kr, vr in zip(k_refs, v_refs)]

    def bias_of(h, b):
        return jnp.concatenate([bias_ref[h, :, (b - t) * Q_TILE:(b - t + 1) * Q_TILE] for t in tiles_of[b]], axis=0)

    def run(part_ok):
        out = _attend(q_of, Q_GROUP * Q_TILE, kv_slabs, bias_of, part_rows, part_ok, na_ref[...])
        o_ref[...] = out.astype(BF16)

    pl.when(j >= 1)(functools.partial(run, None))
    pl.when(j < 1)(functools.partial(run, [j >= 1, j >= 1] + [None] * Q_GROUP))
    if zero_fill:
        for cp in z_copies:
            cp.wait()


def _attn_prompt_call(q, k, v, diag, na, *, n_batch, q_tiles_per_batch, zeros_shape=None):
    nq = q_tiles_per_batch // Q_GROUP

    def kmap(part):
        return lambda b, j: (b * q_tiles_per_batch + jnp.maximum(Q_GROUP * j - 2 + part, 0), 0)

    blk = pl.BlockSpec((Q_GROUP * Q_TILE, D_A), lambda b, j: (b * nq + j, 0))
    kv_specs = [pl.BlockSpec((Q_TILE, D_A), kmap(part)) for part in range(Q_GROUP + 2)]
    out_specs = [blk]
    out_shape = [jax.ShapeDtypeStruct((n_batch * nq * Q_GROUP * Q_TILE, D_A), BF16)]
    scratch = [pltpu.VMEM((N_HEADS, Q_TILE, 3 * Q_TILE), F32)]
    copies = 0
    if zeros_shape is not None:
        rows, cols = zeros_shape
        per_step, rem = divmod(rows, n_batch * nq)
        assert rem == 0 and per_step % 8 == 0
        zr = max(r for r in range(8, ZERO_ROWS_MAX + 1, 8) if per_step % r == 0)
        copies = per_step // zr
        out_specs.append(pl.BlockSpec(memory_space=pl.ANY))
        out_shape.append(jax.ShapeDtypeStruct((rows, cols), F32))
        scratch += [pltpu.VMEM((zr, cols), F32), pltpu.SemaphoreType.DMA(())]
    res = pl.pallas_call(
        functools.partial(_attn_prompt_kernel, copies),
        grid=(n_batch, nq),
        in_specs=[blk] + kv_specs + kv_specs + [
            pl.BlockSpec(diag.shape, lambda b, j: (0, 0)),
            pl.BlockSpec((1, D_A), lambda b, j: (0, 0)),
        ],
        out_specs=out_specs,
        out_shape=out_shape,
        scratch_shapes=scratch,
        compiler_params=_cparams("arbitrary", "arbitrary"),
        name="attn_prompt_zeros" if zeros_shape is not None else "attn_prompt",
    )(q, *([k] * (Q_GROUP + 2)), *([v] * (Q_GROUP + 2)), diag, na)
    return res if zeros_shape is not None else res[0]


def _attn_sample_kernel(q_ref, kn_ref, vn_ref, ck_ref, cv_ref, bias_c_ref, bias_n_ref, na_ref, o_ref):
    def kv_slabs(p):
        sl = slice(p * LANES, (p + 1) * LANES)

        def cached(c_ref):
            return c_ref[2 * p:2 * p + 2].reshape(LANES, c_ref.shape[2]).astype(BF16)
        return [(cached(ck_ref), cached(cv_ref), True), (kn_ref[:, sl], vn_ref[:, sl], False)]

    def bias_of(h, b):
        return bias_c_ref[h] if b == 0 else bias_n_ref[h]

    def q_of(p):
        return q_ref[:, p * LANES:(p + 1) * LANES]

    r = q_ref.shape[0]
    o_ref[...] = _attend(q_of, r, kv_slabs, bias_of, [(0, r), (0, r)], None, na_ref[...]).astype(BF16)


def _attn_sample_call(layer, q, k, v, cache_kt, cache_vt, bias_c, bias_n, na, *, first_chunk):
    _, nb, nh, hd, win = cache_kt.shape
    blk = pl.BlockSpec((CHUNK, D_A), lambda b: (first_chunk + b, 0))
    cache = pl.BlockSpec((None, None, nh, hd, win), lambda b: (layer, b, 0, 0, 0))
    return pl.pallas_call(
        _attn_sample_kernel,
        grid=(nb,),
        in_specs=[blk, blk, blk, cache, cache,
                  pl.BlockSpec(bias_c.shape, lambda b: (0, 0, 0)),
                  pl.BlockSpec(bias_n.shape, lambda b: (0, 0, 0)),
                  pl.BlockSpec((1, D_A), lambda b: (0, 0))],
        out_specs=pl.BlockSpec((CHUNK, D_A), lambda b: (b, 0)),
        out_shape=jax.ShapeDtypeStruct((nb * CHUNK, D_A), BF16),
        compiler_params=_cparams("arbitrary"),
        name="attn_sample",
    )(q, k, v, cache_kt, cache_vt, bias_c, bias_n, na)


def _mixer_close(ntp, x_ref, anp_ref, ans_ref, bn_ref, w_ref, g1_ref, sc_ref, sh_ref, lg_ref, lb_ref):
    an = jnp.where(pl.program_id(0) >= ntp, ans_ref[...], anp_ref[...])
    mix = _dot(an, w_ref[0:D_A, :]) + _dot(bn_ref[...], w_ref[D_A:D_A + D_B, :])
    x1 = _ln(ALPHA * x_ref[...] + _gate_rows(mix, g1_ref), lg_ref[...], lb_ref[...])
    return x1, _modulate(x1, sc_ref, sh_ref)


def _mixer_specs(layer, d, ntp, w_shape):
    def tile(i):
        return (i, 0)

    def const2(i):
        return (0, 0)

    return [
        pl.BlockSpec((ROW_TILE, d), tile),
        pl.BlockSpec((ROW_TILE, D_A), lambda i: (jnp.minimum(i, ntp - 1), 0)),
        pl.BlockSpec((ROW_TILE, D_A), lambda i: (jnp.maximum(i - ntp, 0), 0)),
        pl.BlockSpec((ROW_TILE, D_B), tile),
        pl.BlockSpec((None,) + w_shape[1:], lambda i: (layer, 0, 0)),
        _mod_spec(layer, G1, ROW_TILE, d),
        _mod_spec(layer, SC2, ROW_TILE, d),
        _mod_spec(layer, SH2, ROW_TILE, d),
        pl.BlockSpec((1, d), const2),
        pl.BlockSpec((1, d), const2),
    ]


def _outproj_kernel(ntp, x_ref, anp_ref, ans_ref, bn_ref, w_ref, g1_ref, sc_ref, sh_ref, lg_ref, lb_ref,
                    rc_ref, x1_ref, rt_ref, et_ref):
    x1, h2 = _mixer_close(ntp, x_ref, anp_ref, ans_ref, bn_ref, w_ref, g1_ref, sc_ref, sh_ref, lg_ref, lb_ref)
    x1_ref[...] = x1
    h_hi, h_lo = _split_bf16(h2)
    both = _dot(h_hi, rc_ref[...])
    logits = both[:, :LANES] + both[:, LANES:] + _dot(h_lo, rc_ref[:, :LANES])
    lane = lax.broadcasted_iota(jnp.int32, logits.shape, 1)
    logits = jnp.where(lane < N_EXPERTS, logits, -jnp.inf)
    m1 = logits.max(axis=1, keepdims=True)
    i1 = jnp.where(logits == m1, lane, LANES).min(axis=1, keepdims=True)
    rest_l = jnp.where(lane == i1, -jnp.inf, logits)
    m2 = rest_l.max(axis=1, keepdims=True)
    i2 = jnp.where(rest_l == m2, lane, LANES).min(axis=1, keepdims=True)
    e2 = jnp.exp(m2 - m1)
    w1 = 1.0 / (1.0 + e2)
    w2 = e2 / (1.0 + e2)
    rt = jnp.where(lane == 0, i1.astype(F32),
                   jnp.where(lane == 1, i2.astype(F32),
                             jnp.where(lane == 2, w1, jnp.where(lane == 3, w2, 0.0))))
    rt_ref[...] = rt
    et_ref[...] = jnp.transpose(rt)[0:8, :]


def _outproj_call(layer, x, an_p, an_s, bn, w_bf, mods, lg, lb, r_cat):
    t, d = x.shape
    nt = t // ROW_TILE
    ntp = an_p.shape[0] // ROW_TILE
    tile = pl.BlockSpec((ROW_TILE, d), lambda i: (i, 0))
    return pl.pallas_call(
        functools.partial(_outproj_kernel, ntp),
        grid=(nt,),
        in_specs=_mixer_specs(layer, d, ntp, w_bf.shape) + [pl.BlockSpec(r_cat.shape, lambda i: (0, 0))],
        out_specs=[tile, pl.BlockSpec((ROW_TILE, LANES), lambda i: (i, 0)),
                   pl.BlockSpec((None, 8, ROW_TILE), lambda i: (i, 0, 0))],
        out_shape=[jax.ShapeDtypeStruct((t, d), F32),
                   jax.ShapeDtypeStruct((t, LANES), F32), jax.ShapeDtypeStruct((nt, 8, ROW_TILE), F32)],
        compiler_params=_cparams("arbitrary"),
        name="out_proj_route",
    )(x, an_p, an_s, bn, w_bf, mods, mods, mods, lg, lb, r_cat)


def _swiglu_chains(n_sub, xb_ref, wg_ref, wu_ref, wd_ref, o_ref):
    wg = wg_ref[0].astype(BF16)
    wu = wu_ref[0].astype(BF16)
    wd = wd_ref[0].astype(BF16)
    for s in range(n_sub):
        rows = slice(s * FFN_SUB, (s + 1) * FFN_SUB)
        xs = xb_ref[rows, :]
        g = _dot(xs, wg)
        u = _dot(xs, wu)
        a = (g * jax.nn.sigmoid(g) * u).astype(BF16)
        o_ref[rows, :] += _dot(a, wd)


def _ffn_kernel(te_ref, tv_ref, x_ref, wg_ref, wu_ref, wd_ref, o_ref, xb_ref):
    del te_ref
    i = pl.program_id(0)
    j = pl.program_id(1)
    valid = tv_ref[i]
    nsub = (valid + FFN_SUB - 1) // FFN_SUB

    @pl.when(j == 0)
    def _():
        xb_ref[...] = x_ref[...].astype(BF16)
        o_ref[...] = jnp.zeros(o_ref.shape, F32)

    for n_sub in range(1, x_ref.shape[0] // FFN_SUB + 1):
        pl.when(nsub == n_sub)(functools.partial(_swiglu_chains, n_sub, xb_ref, wg_ref, wu_ref, wd_ref, o_ref))


def _ffn_call(tile_expert, tile_valid, xs, wg, wu, wd, tile_rows):
    p, d = xs.shape
    f = wg.shape[2]
    nt = p // tile_rows
    nc = f // FFN_COLS

    def active_col(i, j, tv):
        return jnp.where(tv[i] > 0, j, nc - 1)

    in_specs = [
        pl.BlockSpec((tile_rows, d), lambda i, j, te, tv: (i, 0)),
        pl.BlockSpec((1, d, FFN_COLS), lambda i, j, te, tv: (te[i], 0, active_col(i, j, tv))),
        pl.BlockSpec((1, d, FFN_COLS), lambda i, j, te, tv: (te[i], 0, active_col(i, j, tv))),
        pl.BlockSpec((1, FFN_COLS, d), lambda i, j, te, tv: (te[i], active_col(i, j, tv), 0)),
    ]
    return pl.pallas_call(
        _ffn_kernel,
        grid_spec=pltpu.PrefetchScalarGridSpec(
            num_scalar_prefetch=2,
            grid=(nt, nc),
            in_specs=in_specs,
            out_specs=pl.BlockSpec((tile_rows, d), lambda i, j, te, tv: (i, 0)),
            scratch_shapes=[pltpu.VMEM((tile_rows, d), BF16)],
        ),
        out_shape=jax.ShapeDtypeStruct((p, d), F32),
        compiler_params=_cparams("arbitrary", "arbitrary"),
        name="swiglu_grouped",
    )(tile_expert, tile_valid, xs, wg, wu, wd)


def _dense_layer_kernel(ntp, x_ref, anp_ref, ans_ref, bn_ref, w_ref, g1_ref, sc_ref, sh_ref, lg1_ref, lb1_ref,
                        wg_ref, wu_ref, wd_ref, g2_ref, lg2_ref, lb2_ref, o_ref):
    x1, h2 = _mixer_close(ntp, x_ref, anp_ref, ans_ref, bn_ref, w_ref, g1_ref, sc_ref, sh_ref, lg1_ref, lb1_ref)
    h = h2.astype(BF16)
    f = wg_ref.shape[1]
    acc = None
    for c0 in range(0, f, FFN_SPLIT):
        c1 = min(c0 + FFN_SPLIT, f)
        g = _dot(h, wg_ref[:, c0:c1])
        u = _dot(h, wu_ref[:, c0:c1])
        a = (g * jax.nn.sigmoid(g) * u).astype(BF16)
        part = _dot(a, wd_ref[c0:c1, :])
        acc = part if acc is None else acc + part
    o_ref[...] = _ln(ALPHA * x1 + _gate_rows(acc, g2_ref), lg2_ref[...], lb2_ref[...])


def _dense_layer_call(layer, x, an_p, an_s, bn, w_bf, mods, lg1, lb1, wg_bf, wu_bf, wd_bf, lg2, lb2):
    t, d = x.shape
    f = wg_bf.shape[1]
    ntp = an_p.shape[0] // ROW_TILE
    tile = pl.BlockSpec((ROW_TILE, d), lambda i: (i, 0))
    vec = pl.BlockSpec((1, d), lambda i: (0, 0))

    def resident(shape):
        return pl.BlockSpec(shape, lambda i: (0, 0), pipeline_mode=pl.Buffered(1))

    return pl.pallas_call(
        functools.partial(_dense_layer_kernel, ntp),
        grid=(t // ROW_TILE,),
        in_specs=_mixer_specs(layer, d, ntp, w_bf.shape) + [
            resident((d, f)), resident((d, f)), resident((f, d)), _mod_spec(layer, G2, ROW_TILE, d), vec, vec],
        out_specs=tile,
        out_shape=jax.ShapeDtypeStruct((t, d), F32),
        compiler_params=_cparams("arbitrary"),
        name="dense_layer_close",
    )(x, an_p, an_s, bn, w_bf, mods, mods, mods, lg1, lb1, wg_bf, wu_bf, wd_bf, mods, lg2, lb2)


def _dispatch_kernel(pos_ref, x1_ref, sc_ref, sh_ref, init_ref, dst_ref, hbuf, sem):
    del init_ref
    hbuf[...] = _modulate(x1_ref[...], sc_ref, sh_ref)

    def issue(r, carry):
        row = hbuf.at[pl.ds(r, 1), :]
        pltpu.make_async_copy(row, dst_ref.at[pl.ds(pos_ref[0, r], 1), :], sem).start()
        pltpu.make_async_copy(row, dst_ref.at[pl.ds(pos_ref[0, ROW_TILE + r], 1), :], sem).start()
        return carry
    lax.fori_loop(0, ROW_TILE, issue, 0, unroll=8)
    for _ in range(2):
        pltpu.make_async_copy(hbuf, dst_ref.at[pl.ds(0, ROW_TILE), :], sem).wait()


def _dispatch_call(layer, pos_tiles, x1, mods, zeros):
    t, d = x1.shape
    n_rows_sorted = zeros.shape[0]
    nt = t // ROW_TILE
    return pl.pallas_call(
        _dispatch_kernel,
        grid=(nt,),
        in_specs=[
            pl.BlockSpec((None, 1, 2 * ROW_TILE), lambda i: (i, 0, 0), memory_space=pltpu.SMEM),
            pl.BlockSpec((ROW_TILE, d), lambda i: (i, 0)),
            _mod_spec(layer, SC2, ROW_TILE, d),
            _mod_spec(layer, SH2, ROW_TILE, d),
            pl.BlockSpec(memory_space=pl.ANY),
        ],
        out_specs=pl.BlockSpec(memory_space=pl.ANY),
        out_shape=jax.ShapeDtypeStruct((n_rows_sorted, d), F32),
        scratch_shapes=[pltpu.VMEM((ROW_TILE, d), F32), pltpu.SemaphoreType.DMA(())],
        input_output_aliases={4: 0},
        compiler_params=_cparams("arbitrary"),
        name="dispatch_rows",
    )(pos_tiles, x1, mods, mods, zeros)


def _final_moe_kernel(ntp, pos_ref, x_ref, rt_ref, g2_ref, lg_ref, lb_ref, ys_ref, op_ref, os_ref, ybuf, sem):
    def issue(r, carry):
        pltpu.make_async_copy(ys_ref.at[pl.ds(pos_ref[0, r], 1), :], ybuf.at[0, pl.ds(r, 1), :], sem).start()
        pltpu.make_async_copy(ys_ref.at[pl.ds(pos_ref[0, ROW_TILE + r], 1), :], ybuf.at[1, pl.ds(r, 1), :],
                              sem).start()
        return carry
    lax.fori_loop(0, ROW_TILE, issue, 0, unroll=8)
    for k in range(2):
        pltpu.make_async_copy(ys_ref.at[pl.ds(0, ROW_TILE), :], ybuf.at[k], sem).wait()
    rt = rt_ref[...]
    f = rt[:, 2:3] * ybuf[0] + rt[:, 3:4] * ybuf[1]
    y = _ln(ALPHA * x_ref[...] + _gate_rows(f, g2_ref), lg_ref[...], lb_ref[...])
    i = pl.program_id(0)

    @pl.when(i < ntp)
    def _():
        op_ref[...] = y

    @pl.when(i >= ntp)
    def _():
        os_ref[...] = y


def _final_moe_call(layer, pos_tiles, x1, rt, mods, lg, lb, ys, *, n_prompt_tiles):
    t, d = x1.shape
    ntp = n_prompt_tiles
    tile = pl.BlockSpec((ROW_TILE, d), lambda i: (i, 0))
    vec = pl.BlockSpec((1, d), lambda i: (0, 0))
    return pl.pallas_call(
        functools.partial(_final_moe_kernel, ntp),
        grid=(t // ROW_TILE,),
        in_specs=[
            pl.BlockSpec((None, 1, 2 * ROW_TILE), lambda i: (i, 0, 0), memory_space=pltpu.SMEM),
            tile,
            pl.BlockSpec((ROW_TILE, LANES), lambda i: (i, 0)),
            _mod_spec(layer, G2, ROW_TILE, d),
            vec, vec,
            pl.BlockSpec(memory_space=pl.ANY),
        ],
        out_specs=[pl.BlockSpec((ROW_TILE, d), lambda i: (jnp.minimum(i, ntp - 1), 0)),
                   pl.BlockSpec((ROW_TILE, d), lambda i: (jnp.maximum(i - ntp, 0), 0))],
        out_shape=[jax.ShapeDtypeStruct((ntp * ROW_TILE, d), F32),
                   jax.ShapeDtypeStruct((t - ntp * ROW_TILE, d), F32)],
        scratch_shapes=[pltpu.VMEM((2, ROW_TILE, d), F32), pltpu.SemaphoreType.DMA(())],
        compiler_params=_cparams("arbitrary"),
        name="final_moe",
    )(pos_tiles, x1, rt, mods, lg, lb, ys)


def _bias_diagonals(table, n_q, n_k):
    k = jnp.arange(n_q + n_k - 1)
    return table[:, jnp.clip(LEFT_CTX + (n_q - 1) - k, -REL_CLIP, REL_CLIP) + REL_CLIP].astype(F32) * LOG2E


def _rel_bias(table, n_q, n_k):
    diag = _bias_diagonals(table, n_q, n_k)
    n_diag = diag.shape[1]
    padded = jnp.concatenate([diag, jnp.zeros((diag.shape[0], 1), F32)], axis=1)
    skew = jnp.tile(padded, (1, n_q))[:, :n_q * n_diag].reshape(-1, n_q, n_diag)
    return skew[:, :, n_q - 1:n_q - 1 + n_k]


def _gating_weights(w_s, b_s):
    n = MLP_CHUNK
    tril = jnp.tril(jnp.ones((n, n), bool))
    wm = jnp.where(tril[None], w_s, 0.0)
    h = CHUNK
    top = wm[:, :h, :h]
    z = jnp.zeros_like(top)
    wm_s = jnp.concatenate([jnp.concatenate([top, z], 2), jnp.concatenate([z, top], 2)], 1)
    both = jnp.stack([wm, wm_s])
    wcat = jnp.concatenate([both[:, 0::2], both[:, 1::2]], axis=-1)
    bias_p = jnp.repeat(jnp.transpose(b_s), GROUP_DIM, axis=1)
    bias_s = jnp.concatenate([bias_p[:h], bias_p[:h]], 0)
    return wcat.astype(BF16), jnp.stack([bias_p, bias_s]).astype(F32)


def _route_plan(et, n_tiles, tile_rows):
    nt = et.shape[0]
    e0 = et[:, 0, :].astype(jnp.int32)
    e1 = et[:, 1, :].astype(jnp.int32)
    experts = jnp.arange(N_EXPERTS, dtype=jnp.int32)[:, None, None]
    is0 = e0[None] == experts
    is1 = e1[None] == experts
    hit = (is0 | is1).astype(jnp.int32)
    in_tile = jnp.cumsum(hit, axis=2)
    tile_tot = in_tile[:, :, -1]
    before_tile = jnp.cumsum(tile_tot, axis=1) - tile_tot
    before = before_tile[:, :, None] + in_tile - hit
    counts = jnp.sum(tile_tot, axis=1)
    tiles_e = (counts + tile_rows - 1) // tile_rows
    tile_end = jnp.cumsum(tiles_e)
    tile_start = tile_end - tiles_e
    slot = before + (tile_start * tile_rows)[:, None, None]
    pos0 = jnp.sum(jnp.where(is0, slot, 0), axis=0)
    pos1 = jnp.sum(jnp.where(is1, slot, 0), axis=0)
    pos = jnp.concatenate([pos0, pos1], axis=1).reshape(nt, 1, 2 * ROW_TILE)
    ti = jnp.arange(n_tiles)
    te = jnp.minimum(jnp.sum((ti[:, None] >= tile_end[None, :]).astype(jnp.int32), axis=1), N_EXPERTS - 1)
    tv = jnp.clip(counts[te] - (ti - tile_start[te]) * tile_rows, 0, tile_rows)
    tv = jnp.where(ti < tile_end[-1], tv, 0)
    last_e = te[jnp.maximum(tile_end[-1] - 1, 0)]
    te = jnp.where(ti < tile_end[-1], te, last_e)
    return pos.astype(jnp.int32), te.astype(jnp.int32), tv.astype(jnp.int32)


def kernel(x_prompt, x_sample, cache_k, cache_v, c_prompt, c_sample, ln_in_g, ln_in_b, w_in, w_out, rel_bias_table, gmlp_ln_g, gmlp_ln_b, gmlp_w_s, gmlp_b_s, out_norm_a, out_norm_b, ada_w, ada_b, ln1_g, ln1_b, ln2_g, ln2_b, ffn_w_gate, ffn_w_up, ffn_w_down, moe_router, moe_w_gate, moe_w_up, moe_w_down):
    nb, seq, d = x_prompt.shape
    ns, dseq, _ = x_sample.shape
    depth = w_in.shape[0]
    tp = nb * seq
    ts = ns * dseq
    t = tp + ts
    assert depth == DEPTH and dseq == CHUNK and seq % ROW_TILE == 0 and ts % ROW_TILE == 0
    assert seq % (Q_GROUP * Q_TILE) == 0 and cache_k.shape[2] == LEFT_CTX
    ntp = tp // ROW_TILE

    n_cond = nb + ns
    c_all = jnp.concatenate([c_prompt, c_sample], axis=0)
    c_pad = jnp.pad(c_all, ((0, (-n_cond) % 8), (0, 0)))
    mods = _ada_call(c_pad, ada_w, ada_b)
    mods = jnp.transpose(mods[:, :n_cond].reshape(depth, n_cond, 6, d), (0, 2, 1, 3))
    cpb = seq // CHUNK
    mods_p = jnp.broadcast_to(mods[:, :, :nb, None, :], (depth, 6, nb, cpb, d)).reshape(depth, 6, nb * cpb, d)
    mods = jnp.concatenate([mods_p, mods[:, :, nb:]], axis=2)

    def row(v):
        return v.reshape(1, -1)

    w_in_bf = w_in.astype(BF16)
    w_out_bf = w_out.astype(BF16)
    cache_kt = jnp.transpose(cache_k, (0, 1, 3, 4, 2))
    cache_vt = jnp.transpose(cache_v, (0, 1, 3, 4, 2))
    k_tail, v_tail, gv_rows = [], [], []
    for l in range(depth):
        wcat, bs = _gating_weights(gmlp_w_s[l], gmlp_b_s[l])
        if l == 0:
            x_in = (x_prompt.reshape(tp, d), x_sample.reshape(ts, d), row(ln_in_g), row(ln_in_b))
        else:
            x_in = (x,)
        res = _inproj_call(l, x_in, mods, w_in_bf, row(gmlp_ln_g[l]), row(gmlp_ln_b[l]), wcat, bs,
                           row(out_norm_b[l]),
                           n_prompt_tiles=ntp, tiles_per_batch=seq // ROW_TILE, n_batch=nb, n_sample_rows=ts)
        if l == 0:
            x, q, k, v, kf, vf, gv, bn = res
        else:
            q, k, v, kf, vf, gv, bn = res
        k_tail.append(kf)
        v_tail.append(vf)
        gv_rows.append(gv)

        table = rel_bias_table[l]
        diag_p = jnp.pad(_bias_diagonals(table, Q_TILE, 3 * Q_TILE), ((0, 0), (1, 0)))
        bias_s = _rel_bias(table, CHUNK, LEFT_CTX + CHUNK)
        na = row(out_norm_a[l])
        routed = l % 2 == 1
        n_tiles = -(-2 * t // MOE_TILE) + N_EXPERTS
        an_p = _attn_prompt_call(q, k, v, diag_p, na, n_batch=nb, q_tiles_per_batch=seq // Q_TILE,
                                 zeros_shape=(n_tiles * MOE_TILE, d) if routed else None)
        if routed:
            an_p, sorted_init = an_p
        an_s = _attn_sample_call(l, q, k, v, cache_kt, cache_vt, bias_s[:, :, :LEFT_CTX], bias_s[:, :, LEFT_CTX:], na,
                                 first_chunk=tp // CHUNK)

        lg1, lb1, lg2, lb2 = row(ln1_g[l]), row(ln1_b[l]), row(ln2_g[l]), row(ln2_b[l])
        i = l // 2
        if not routed:
            x = _dense_layer_call(l, x, an_p, an_s, bn, w_out_bf, mods, lg1, lb1, ffn_w_gate[i].astype(BF16),
                                  ffn_w_up[i].astype(BF16), ffn_w_down[i].astype(BF16), lg2, lb2)
        else:
            r_pad = jnp.pad(moe_router[i], ((0, 0), (0, LANES - N_EXPERTS)))
            r_hi = r_pad.astype(BF16)
            r_lo = (r_pad - r_hi.astype(F32)).astype(BF16)
            r_cat = jnp.concatenate([r_hi, r_lo], axis=1)
            x1, rt, et = _outproj_call(l, x, an_p, an_s, bn, w_out_bf, mods, lg1, lb1, r_cat)
            pos, te, tv = _route_plan(et, n_tiles, MOE_TILE)
            xs = _dispatch_call(l, pos, x1, mods, sorted_init)
            ys = _ffn_call(te, tv, xs, moe_w_gate[i], moe_w_up[i], moe_w_down[i], MOE_TILE)
            y_p, y_s = _final_moe_call(l, pos, x1, rt, mods, lg2, lb2, ys, n_prompt_tiles=ntp)

    y_prompt = y_p.reshape(nb, seq, d)
    y_sample = y_s.reshape(ns, dseq, d)

    def tails(rows):
        kp = jnp.stack([r[:nb * ROW_TILE].reshape(nb, ROW_TILE, N_HEADS, HEAD_DIM) for r in rows])
        ksn = jnp.stack([r[nb * ROW_TILE:].reshape(ns, dseq, N_HEADS, HEAD_DIM) for r in rows])
        return kp, ksn

    k_prompt_new, k_sample_new = tails(k_tail)
    v_prompt_new, v_sample_new = tails(v_tail)
    gmlp_v_sample_new = jnp.stack([g.reshape(ns, dseq, D_B) for g in gv_rows])
    return (y_prompt, y_sample, k_prompt_new, v_prompt_new, k_sample_new, v_sample_new, gmlp_v_sample_new)
```

```python
import functools

import jax
import jax.numpy as jnp
from jax import lax
from jax.experimental import pallas as pl
from jax.experimental.pallas import tpu as pltpu

CHUNK = 64
LEFT_CHUNKS = 8
LEFT_CTX = LEFT_CHUNKS * CHUNK
N_HEADS = 8
HEAD_DIM = 64
D_A = N_HEADS * HEAD_DIM
N_GROUPS = 8
GROUP_DIM = 64
D_B = N_GROUPS * GROUP_DIM
MLP_CHUNK = 128
REL_CLIP = 128
N_EXPERTS = 8
DEPTH = 2
ALPHA = (2 * DEPTH) ** 0.25
LN_EPS = 1e-5
ATTN_SCALE = HEAD_DIM ** -0.5
LOG2E = 1.4426950408889634
NEG = -1e30
SH1, SC1, G1, SH2, SC2, G2 = range(6)

LANES = 128
ROW_TILE = 512
IN_PART = 512
Q_TILE = 256
Q_GROUP = 4
Q_PHASE_PAIRS = 2
S_GROUP = 8
ZERO_ROWS_MAX = 1024
FFN_SPLIT = 1536
MOE_TILE = 2048
FFN_SUB = 512
FFN_COLS = 256
VMEM_LIMIT = 56 * 1024 * 1024

BF16 = jnp.bfloat16
F32 = jnp.float32


def _cparams(*sem):
    return pltpu.CompilerParams(dimension_semantics=sem, vmem_limit_bytes=VMEM_LIMIT)


def _ln(x, g, b):
    mu = jnp.mean(x, axis=-1, keepdims=True)
    xc = x - mu
    var = jnp.mean(xc * xc, axis=-1, keepdims=True)
    return xc * lax.rsqrt(var + LN_EPS) * g + b


def _rms(x, g):
    return x * lax.rsqrt(jnp.mean(x * x, axis=-1, keepdims=True) + LN_EPS) * g


def _modulate(x, sc_ref, sh_ref):
    parts = []
    for c in range(x.shape[0] // CHUNK):
        xc = x[c * CHUNK:(c + 1) * CHUNK]
        parts.append(xc * (1.0 + sc_ref[c:c + 1, :]) + sh_ref[c:c + 1, :])
    return jnp.concatenate(parts, axis=0)


def _gate_rows(x, g_ref):
    parts = []
    for c in range(x.shape[0] // CHUNK):
        parts.append(x[c * CHUNK:(c + 1) * CHUNK] * (1.0 + g_ref[c:c + 1, :]))
    return jnp.concatenate(parts, axis=0)


def _mod_spec(layer, comp, rows, d):
    return pl.BlockSpec((None, None, rows // CHUNK, d), lambda i, *_: (layer, comp, i, 0))


def _split_bf16(x):
    hi = x.astype(BF16)
    lo = (x - hi.astype(F32)).astype(BF16)
    return hi, lo


def _dot(a, b):
    return jnp.dot(a, b, preferred_element_type=F32)


def _ada_kernel(c_ref, w_ref, b_ref, o_ref):
    c = c_ref[...]
    s = c * jax.nn.sigmoid(c)
    s_hi, s_lo = _split_bf16(s)
    w_hi, w_lo = _split_bf16(w_ref[0])
    o_ref[0] = _dot(s_hi, w_hi) + _dot(s_lo, w_hi) + _dot(s_hi, w_lo) + b_ref[0]


def _ada_call(c_pad, ada_w, ada_b):
    depth, d, n = ada_w.shape
    rows = c_pad.shape[0]
    tn = n // 4
    return pl.pallas_call(
        _ada_kernel,
        grid=(depth, n // tn),
        in_specs=[
            pl.BlockSpec((rows, d), lambda l, j: (0, 0)),
            pl.BlockSpec((1, d, tn), lambda l, j: (l, 0, j)),
            pl.BlockSpec((1, 1, tn), lambda l, j: (l, 0, j)),
        ],
        out_specs=pl.BlockSpec((1, rows, tn), lambda l, j: (l, 0, j)),
        out_shape=jax.ShapeDtypeStruct((depth, rows, n), F32),
        compiler_params=_cparams("arbitrary", "arbitrary"),
        name="ada_mod",
    )(c_pad, ada_w, ada_b.reshape(depth, 1, n))


def _inproj_kernel(first, ntp, *refs):
    if first:
        xp_ref, xs_ref, lng_ref, lnb_ref = refs[:4]
        xn_ref = refs[-8]
        refs = refs[4:-8] + refs[-7:]
    else:
        x_ref = refs[0]
        refs = refs[1:]
    (sc_ref, sh_ref, w_ref, gg_ref, gb_ref, wcat_ref, bs_ref, nb_ref,
     q_ref, k_ref, v_ref, kf_ref, vf_ref, gv_ref, bn_ref) = refs
    lane = lax.broadcasted_iota(jnp.int32, (MLP_CHUNK, LANES), 1)
    lo = lane < GROUP_DIM
    zero = jnp.zeros((MLP_CHUNK, LANES), BF16)

    def part(r0, n_rows):
        rows = slice(r0, r0 + n_rows)
        mod_rows = slice(r0 // CHUNK, (r0 + n_rows) // CHUNK)
        if first:
            x = jnp.where(pl.program_id(0) >= ntp, xs_ref[rows, :], xp_ref[rows, :])
            x = _ln(x, lng_ref[...], lnb_ref[...])
            xn_ref[rows, :] = x
        else:
            x = x_ref[rows, :]
        h = _modulate(x, sc_ref.at[mod_rows, :], sh_ref.at[mod_rows, :]).astype(BF16)

        q = _dot(h, w_ref[:, 0:D_A])
        q_ref[rows, :] = (q * (ATTN_SCALE * LOG2E)).astype(BF16)
        k = _dot(h, w_ref[:, D_A:2 * D_A])
        k_ref[rows, :] = k.astype(BF16)
        kf_ref[rows, :] = k
        v = _dot(h, w_ref[:, 2 * D_A:3 * D_A])
        v_ref[rows, :] = v.astype(BF16)
        vf_ref[rows, :] = v

        u = jax.nn.gelu(_dot(h, w_ref[:, 3 * D_A:3 * D_A + D_B]), approximate=True)
        vb = jax.nn.gelu(_dot(h, w_ref[:, 3 * D_A + D_B:3 * D_A + 2 * D_B]), approximate=True)
        vn = _ln(vb, gg_ref[...], gb_ref[...])
        gv_ref[rows, :] = vn

        vnb = vn.astype(BF16)
        n_chunks = n_rows // MLP_CHUNK
        mixed = []
        for p in range(D_B // LANES):
            stacks = []
            for c in range(n_chunks):
                slab = vnb[c * MLP_CHUNK:(c + 1) * MLP_CHUNK, p * LANES:(p + 1) * LANES]
                stacks.append(jnp.concatenate([jnp.where(lo, slab, zero), jnp.where(lo, zero, slab)], axis=0))
            mixed.append(_dot(wcat_ref[0, p], jnp.concatenate(stacks, axis=1)))
        outs = []
        for c in range(n_chunks):
            c0 = c * MLP_CHUNK
            mix_c = jnp.concatenate([m[:, c * LANES:(c + 1) * LANES] for m in mixed], axis=1) + bs_ref[0]
            outs.append(u[c0:c0 + MLP_CHUNK, :] * mix_c)
        bo = jnp.concatenate(outs, axis=0)
        bn_ref[rows, :] = _rms(bo, nb_ref[...]).astype(BF16)

    for r0 in range(0, ROW_TILE, IN_PART):
        part(r0, IN_PART)


def _inproj_call(layer, x_in, mods, w_bf, gg, gb, wcat, bs, nb, *, n_prompt_tiles, tiles_per_batch, n_batch,
                 n_sample_rows):
    first = len(x_in) == 4
    d = x_in[0].shape[1]
    t = mods.shape[2] * CHUNK
    nt = t // ROW_TILE
    ntp = n_prompt_tiles
    n_tail = n_batch + (nt - ntp)

    def tile(i):
        return (i, 0)

    def const2(i):
        return (0, 0)

    def selmap4(i):
        return (jnp.where(i >= ntp, 1, 0), 0, 0, 0)

    def selmap3(i):
        return (jnp.where(i >= ntp, 1, 0), 0, 0)

    def tailmap(i):
        return (jnp.where(i < ntp, i // tiles_per_batch, n_batch + i - ntp), 0)

    def gvmap(i):
        return (jnp.maximum(i - ntp, 0), 0)

    if first:
        x_specs = [
            pl.BlockSpec((ROW_TILE, d), lambda i: (jnp.minimum(i, ntp - 1), 0)),
            pl.BlockSpec((ROW_TILE, d), lambda i: (jnp.maximum(i - ntp, 0), 0)),
            pl.BlockSpec((1, d), const2),
            pl.BlockSpec((1, d), const2),
        ]
    else:
        x_specs = [pl.BlockSpec((ROW_TILE, d), tile)]
    in_specs = x_specs + [
        _mod_spec(layer, SC1, ROW_TILE, d),
        _mod_spec(layer, SH1, ROW_TILE, d),
        pl.BlockSpec((None,) + w_bf.shape[1:], lambda i: (layer, 0, 0)),
        pl.BlockSpec((1, D_B), const2),
        pl.BlockSpec((1, D_B), const2),
        pl.BlockSpec((1,) + wcat.shape[1:], selmap4),
        pl.BlockSpec((1, MLP_CHUNK, D_B), selmap3),
        pl.BlockSpec((1, D_B), const2),
    ]
    out_specs = [
        pl.BlockSpec((ROW_TILE, D_A), tile),
        pl.BlockSpec((ROW_TILE, D_A), tile),
        pl.BlockSpec((ROW_TILE, D_A), tile),
        pl.BlockSpec((ROW_TILE, D_A), tailmap),
        pl.BlockSpec((ROW_TILE, D_A), tailmap),
        pl.BlockSpec((ROW_TILE, D_B), gvmap),
        pl.BlockSpec((ROW_TILE, D_B), tile),
    ]
    out_shape = [
        jax.ShapeDtypeStruct((t, D_A), BF16),
        jax.ShapeDtypeStruct((t, D_A), BF16),
        jax.ShapeDtypeStruct((t, D_A), BF16),
        jax.ShapeDtypeStruct((n_tail * ROW_TILE, D_A), F32),
        jax.ShapeDtypeStruct((n_tail * ROW_TILE, D_A), F32),
        jax.ShapeDtypeStruct((n_sample_rows, D_B), F32),
        jax.ShapeDtypeStruct((t, D_B), BF16),
    ]
    if first:
        out_specs = [pl.BlockSpec((ROW_TILE, d), tile)] + out_specs
        out_shape = [jax.ShapeDtypeStruct((t, d), F32)] + out_shape
    return pl.pallas_call(
        functools.partial(_inproj_kernel, first, ntp),
        grid=(nt,),
        in_specs=in_specs,
        out_specs=out_specs,
        out_shape=out_shape,
        compiler_params=_cparams("arbitrary"),
        name="in_proj_first" if first else "in_proj",
    )(*x_in, mods, mods, w_bf, gg, gb, wcat, bs, nb)


def _row_max(pieces):
    folded = None
    m = None
    for s in pieces:
        if s.shape[1] % LANES == 0:
            for c0 in range(0, s.shape[1], LANES):
                piece = s[:, c0:c0 + LANES]
                folded = piece if folded is None else jnp.maximum(folded, piece)
        else:
            mj = s.max(axis=1, keepdims=True)
            m = mj if m is None else jnp.maximum(m, mj)
    if folded is not None:
        mj = folded.max(axis=1, keepdims=True)
        m = mj if m is None else jnp.maximum(m, mj)
    return m


def _attend(q_of, r, kv_slabs, bias_of, part_rows, part_ok, na, pairs_per_phase):
    nt_dims = (((1,), (1,)), ((), ()))
    lo = lax.broadcasted_iota(jnp.int32, (1, LANES), 1) < HEAD_DIM
    lo_t = lax.broadcasted_iota(jnp.int32, (LANES, 1), 0) < HEAD_DIM
    bounds = sorted({0, r} | {b for rows in part_rows for b in rows})
    segs = list(zip(bounds[:-1], bounds[1:]))

    def covered(seg, rows):
        return rows[0] <= seg[0] and seg[1] <= rows[1]

    def scores(p, half, slabs):
        h = 2 * p + half
        keep = lo if half == 0 else jnp.logical_not(lo)
        qp = q_of(p)
        qh = jnp.where(keep, qp, jnp.zeros_like(qp))
        ss = []
        for j, (kp, _, transposed) in enumerate(slabs):
            qj = qh[part_rows[j][0]:part_rows[j][1]]
            if transposed:
                s = _dot(qj, kp)
            else:
                s = lax.dot_general(qj, kp, nt_dims, preferred_element_type=F32)
            s = s + bias_of(h, j)
            if part_ok is not None and part_ok[j] is not None:
                s = jnp.where(part_ok[j], s, NEG)
            ss.append(s)
        return ss

    def row_maxes(ss):
        return [_row_max([s[a - rows[0]:b - rows[0]] for s, rows in zip(ss, part_rows) if covered((a, b), rows)])
                for a, b in segs]

    def weighted_values(half, ss, m_segs, slabs):
        keep = lo if half == 0 else jnp.logical_not(lo)
        keep_t = lo_t if half == 0 else jnp.logical_not(lo_t)
        acc_segs = [jnp.zeros((b - a, LANES), F32) for a, b in segs]
        for s, (_, vp, transposed), rows in zip(ss, slabs, part_rows):
            m = jnp.concatenate([m_segs[i] for i, seg in enumerate(segs) if covered(seg, rows)], axis=0)
            e = jnp.exp2(s - m).astype(BF16)
            if transposed:
                vh = jnp.where(keep_t, vp, jnp.ones_like(vp))
                c = lax.dot_general(e, vh, nt_dims, preferred_element_type=F32)
            else:
                c = _dot(e, jnp.where(keep, vp, jnp.ones_like(vp)))
            for i, (a, b) in enumerate(segs):
                if covered((a, b), rows):
                    acc_segs[i] = acc_segs[i] + c[a - rows[0]:b - rows[0]]
        return jnp.concatenate(acc_segs, axis=0)

    def normalised(accs):
        num = jnp.where(lo, accs[0], accs[1])
        den = pltpu.roll(jnp.where(lo, accs[1], accs[0]), HEAD_DIM, 1)
        return num / den

    n_pairs = D_A // LANES
    outs = []
    for p0 in range(0, n_pairs, pairs_per_phase):
        pairs = range(p0, min(p0 + pairs_per_phase, n_pairs))
        slabs = {p: kv_slabs(p) for p in pairs}
        ss = {(p, half): scores(p, half, slabs[p]) for p in pairs for half in range(2)}
        ms = {key: row_maxes(val) for key, val in ss.items()}
        accs = {key: weighted_values(key[1], ss[key], ms[key], slabs[key[0]]) for key in ss}
        outs += [normalised([accs[(p, 0)], accs[(p, 1)]]) for p in pairs]
    a = jnp.concatenate(outs, axis=1)
    return _rms(a, na)


def _attn_prompt_kernel(zero_fill, q_ref, *refs):
    n_parts = Q_GROUP + 2
    k_refs = refs[:n_parts]
    v_refs = refs[n_parts:2 * n_parts]
    diag_ref, na_ref, o_ref = refs[2 * n_parts:2 * n_parts + 3]
    rest = refs[2 * n_parts + 3:]
    j = pl.program_id(1)
    if zero_fill:
        z_hbm, bias_ref, zbuf, zsem = rest
    else:
        bias_ref, = rest

    @pl.when((pl.program_id(0) == 0) & (j == 0))
    def _():
        shape = (Q_TILE, 3 * Q_TILE)
        q_chunk = lax.broadcasted_iota(jnp.int32, shape, 0) // CHUNK
        k_chunk = lax.broadcasted_iota(jnp.int32, shape, 1) // CHUNK
        visible = (k_chunk >= q_chunk) & (k_chunk <= q_chunk + LEFT_CHUNKS)
        for h in range(N_HEADS):
            rows = jnp.broadcast_to(diag_ref[h:h + 1, :], (Q_TILE, diag_ref.shape[1]))
            skew = pltpu.roll(rows, 0, 1, stride=1, stride_axis=0)
            bias_ref[h] = jnp.where(visible, skew[:, Q_TILE:], NEG)

    if zero_fill:
        step = pl.program_id(0) * pl.num_programs(1) + j
        zr = zbuf.shape[0]
        zbuf[...] = jnp.zeros(zbuf.shape, zbuf.dtype)
        z_copies = [pltpu.make_async_copy(zbuf, z_hbm.at[pl.ds((step * zero_fill + c) * zr, zr), :], zsem)
                    for c in range(zero_fill)]
        for cp in z_copies:
            cp.start()
    tiles_of = [[t for t in range(Q_GROUP) if t <= b <= t + 2] for b in range(n_parts)]
    part_rows = [(ts[0] * Q_TILE, (ts[-1] + 1) * Q_TILE) for ts in tiles_of]

    def q_of(p):
        return q_ref[:, p * LANES:(p + 1) * LANES]

    def kv_slabs(p):
        sl = slice(p * LANES, (p + 1) * LANES)
        return [(kr[:, sl], vr[:, sl], False) for kr, vr in zip(k_refs, v_refs)]

    def bias_of(h, b):
        return jnp.concatenate([bias_ref[h, :, (b - t) * Q_TILE:(b - t + 1) * Q_TILE] for t in tiles_of[b]], axis=0)

    def run(part_ok):
        out = _attend(q_of, Q_GROUP * Q_TILE, kv_slabs, bias_of, part_rows, part_ok, na_ref[...], Q_PHASE_PAIRS)
        o_ref[...] = out.astype(BF16)

    pl.when(j >= 1)(functools.partial(run, None))
    pl.when(j < 1)(functools.partial(run, [j >= 1, j >= 1] + [None] * Q_GROUP))
    if zero_fill:
        for cp in z_copies:
            cp.wait()


def _attn_prompt_call(q, k, v, diag, na, *, n_batch, q_tiles_per_batch, zeros_shape=None):
    nq = q_tiles_per_batch // Q_GROUP

    def kmap(part):
        return lambda b, j: (b * q_tiles_per_batch + jnp.maximum(Q_GROUP * j - 2 + part, 0), 0)

    blk = pl.BlockSpec((Q_GROUP * Q_TILE, D_A), lambda b, j: (b * nq + j, 0))
    kv_specs = [pl.BlockSpec((Q_TILE, D_A), kmap(part)) for part in range(Q_GROUP + 2)]
    out_specs = [blk]
    out_shape = [jax.ShapeDtypeStruct((n_batch * nq * Q_GROUP * Q_TILE, D_A), BF16)]
    scratch = [pltpu.VMEM((N_HEADS, Q_TILE, 3 * Q_TILE), F32)]
    copies = 0
    if zeros_shape is not None:
        rows, cols = zeros_shape
        per_step, rem = divmod(rows, n_batch * nq)
        assert rem == 0 and per_step % 8 == 0
        zr = max(r for r in range(8, ZERO_ROWS_MAX + 1, 8) if per_step % r == 0)
        copies = per_step // zr
        out_specs.append(pl.BlockSpec(memory_space=pl.ANY))
        out_shape.append(jax.ShapeDtypeStruct((rows, cols), F32))
        scratch += [pltpu.VMEM((zr, cols), F32), pltpu.SemaphoreType.DMA(())]
    res = pl.pallas_call(
        functools.partial(_attn_prompt_kernel, copies),
        grid=(n_batch, nq),
        in_specs=[blk] + kv_specs + kv_specs + [
            pl.BlockSpec(diag.shape, lambda b, j: (0, 0)),
            pl.BlockSpec((1, D_A), lambda b, j: (0, 0)),
        ],
        out_specs=out_specs,
        out_shape=out_shape,
        scratch_shapes=scratch,
        compiler_params=_cparams("arbitrary", "arbitrary"),
        name="attn_prompt_zeros" if zeros_shape is not None else "attn_prompt",
    )(q, *([k] * (Q_GROUP + 2)), *([v] * (Q_GROUP + 2)), diag, na)
    return res if zeros_shape is not None else res[0]


def _attn_sample_kernel(q_ref, kn_ref, vn_ref, ck_ref, cv_ref, bias_c_ref, bias_n_ref, na_ref, o_ref):
    def bias_of(h, b):
        return bias_c_ref[h] if b == 0 else bias_n_ref[h]

    for e in range(S_GROUP):
        rows = slice(e * CHUNK, (e + 1) * CHUNK)

        def kv_slabs(p, e=e, rows=rows):
            sl = slice(p * LANES, (p + 1) * LANES)

            def cached(c_ref):
                return c_ref[e, 2 * p:2 * p + 2].reshape(LANES, c_ref.shape[3]).astype(BF16)
            return [(cached(ck_ref), cached(cv_ref), True), (kn_ref[rows, sl], vn_ref[rows, sl], False)]

        def q_of(p, rows=rows):
            return q_ref[rows, p * LANES:(p + 1) * LANES]

        out = _attend(q_of, CHUNK, kv_slabs, bias_of, [(0, CHUNK), (0, CHUNK)], None, na_ref[...], D_A // LANES)
        o_ref[rows, :] = out.astype(BF16)


def _attn_sample_call(layer, q, k, v, cache_kt, cache_vt, bias_c, bias_n, na, *, first_chunk):
    _, nb, nh, hd, win = cache_kt.shape
    assert nb % S_GROUP == 0 and first_chunk % S_GROUP == 0
    rows = S_GROUP * CHUNK
    blk = pl.BlockSpec((rows, D_A), lambda b: (first_chunk // S_GROUP + b, 0))
    cache = pl.BlockSpec((None, S_GROUP, nh, hd, win), lambda b: (layer, b, 0, 0, 0))
    return pl.pallas_call(
        _attn_sample_kernel,
        grid=(nb // S_GROUP,),
        in_specs=[blk, blk, blk, cache, cache,
                  pl.BlockSpec(bias_c.shape, lambda b: (0, 0, 0)),
                  pl.BlockSpec(bias_n.shape, lambda b: (0, 0, 0)),
                  pl.BlockSpec((1, D_A), lambda b: (0, 0))],
        out_specs=pl.BlockSpec((rows, D_A), lambda b: (b, 0)),
        out_shape=jax.ShapeDtypeStruct((nb * CHUNK, D_A), BF16),
        compiler_params=_cparams("arbitrary"),
        name="attn_sample",
    )(q, k, v, cache_kt, cache_vt, bias_c, bias_n, na)


def _mixer_close(ntp, x_ref, anp_ref, ans_ref, bn_ref, w_ref, g1_ref, sc_ref, sh_ref, lg_ref, lb_ref):
    an = jnp.where(pl.program_id(0) >= ntp, ans_ref[...], anp_ref[...])
    mix = _dot(an, w_ref[0:D_A, :]) + _dot(bn_ref[...], w_ref[D_A:D_A + D_B, :])
    x1 = _ln(ALPHA * x_ref[...] + _gate_rows(mix, g1_ref), lg_ref[...], lb_ref[...])
    return x1, _modulate(x1, sc_ref, sh_ref)


def _mixer_specs(layer, d, ntp, w_shape):
    def tile(i):
        return (i, 0)

    def const2(i):
        return (0, 0)

    return [
        pl.BlockSpec((ROW_TILE, d), tile),
        pl.BlockSpec((ROW_TILE, D_A), lambda i: (jnp.minimum(i, ntp - 1), 0)),
        pl.BlockSpec((ROW_TILE, D_A), lambda i: (jnp.maximum(i - ntp, 0), 0)),
        pl.BlockSpec((ROW_TILE, D_B), tile),
        pl.BlockSpec((None,) + w_shape[1:], lambda i: (layer, 0, 0)),
        _mod_spec(layer, G1, ROW_TILE, d),
        _mod_spec(layer, SC2, ROW_TILE, d),
        _mod_spec(layer, SH2, ROW_TILE, d),
        pl.BlockSpec((1, d), const2),
        pl.BlockSpec((1, d), const2),
    ]


def _outproj_kernel(ntp, x_ref, anp_ref, ans_ref, bn_ref, w_ref, g1_ref, sc_ref, sh_ref, lg_ref, lb_ref,
                    rc_ref, x1_ref, hp_ref, rt_ref, et_ref):
    x1, h2 = _mixer_close(ntp, x_ref, anp_ref, ans_ref, bn_ref, w_ref, g1_ref, sc_ref, sh_ref, lg_ref, lb_ref)
    x1_ref[...] = x1
    hp_ref[...] = h2
    h_hi, h_lo = _split_bf16(h2)
    both = _dot(h_hi, rc_ref[...])
    logits = both[:, :LANES] + both[:, LANES:] + _dot(h_lo, rc_ref[:, :LANES])
    lane = lax.broadcasted_iota(jnp.int32, logits.shape, 1)
    logits = jnp.where(lane < N_EXPERTS, logits, -jnp.inf)
    m1 = logits.max(axis=1, keepdims=True)
    i1 = jnp.where(logits == m1, lane, LANES).min(axis=1, keepdims=True)
    rest_l = jnp.where(lane == i1, -jnp.inf, logits)
    m2 = rest_l.max(axis=1, keepdims=True)
    i2 = jnp.where(rest_l == m2, lane, LANES).min(axis=1, keepdims=True)
    e2 = jnp.exp(m2 - m1)
    w1 = 1.0 / (1.0 + e2)
    w2 = e2 / (1.0 + e2)
    rt = jnp.where(lane == 0, i1.astype(F32),
                   jnp.where(lane == 1, i2.astype(F32),
                             jnp.where(lane == 2, w1, jnp.where(lane == 3, w2, 0.0))))
    rt_ref[...] = rt
    et_ref[...] = jnp.transpose(rt)[0:8, :]


def _outproj_call(layer, x, an_p, an_s, bn, w_bf, mods, lg, lb, r_cat):
    t, d = x.shape
    nt = t // ROW_TILE
    ntp = an_p.shape[0] // ROW_TILE
    tile = pl.BlockSpec((ROW_TILE, d), lambda i: (i, 0))
    return pl.pallas_call(
        functools.partial(_outproj_kernel, ntp),
        grid=(nt,),
        in_specs=_mixer_specs(layer, d, ntp, w_bf.shape) + [pl.BlockSpec(r_cat.shape, lambda i: (0, 0))],
        out_specs=[tile, tile, pl.BlockSpec((ROW_TILE, LANES), lambda i: (i, 0)),
                   pl.BlockSpec((None, 8, ROW_TILE), lambda i: (i, 0, 0))],
        out_shape=[jax.ShapeDtypeStruct((t, d), F32), jax.ShapeDtypeStruct((t, d), F32),
                   jax.ShapeDtypeStruct((t, LANES), F32), jax.ShapeDtypeStruct((nt, 8, ROW_TILE), F32)],
        compiler_params=_cparams("arbitrary"),
        name="out_proj_route",
    )(x, an_p, an_s, bn, w_bf, mods, mods, mods, lg, lb, r_cat)


def _swiglu_chains(n_sub, xb_ref, wg_ref, wu_ref, wd_ref, o_ref):
    wg = wg_ref[0].astype(BF16)
    wu = wu_ref[0].astype(BF16)
    wd = wd_ref[0].astype(BF16)
    for s in range(n_sub):
        rows = slice(s * FFN_SUB, (s + 1) * FFN_SUB)
        xs = xb_ref[rows, :]
        g = _dot(xs, wg)
        u = _dot(xs, wu)
        a = (g * jax.nn.sigmoid(g) * u).astype(BF16)
        o_ref[rows, :] += _dot(a, wd)


def _ffn_kernel(te_ref, tv_ref, x_ref, wg_ref, wu_ref, wd_ref, o_ref, xb_ref):
    del te_ref
    i = pl.program_id(0)
    j = pl.program_id(1)
    valid = tv_ref[i]
    nsub = (valid + FFN_SUB - 1) // FFN_SUB

    @pl.when(j == 0)
    def _():
        xb_ref[...] = x_ref[...].astype(BF16)
        o_ref[...] = jnp.zeros(o_ref.shape, F32)

    for n_sub in range(1, x_ref.shape[0] // FFN_SUB + 1):
        pl.when(nsub == n_sub)(functools.partial(_swiglu_chains, n_sub, xb_ref, wg_ref, wu_ref, wd_ref, o_ref))


def _ffn_call(tile_expert, tile_valid, xs, wg, wu, wd, tile_rows):
    p, d = xs.shape
    f = wg.shape[2]
    nt = p // tile_rows
    nc = f // FFN_COLS

    def active_col(i, j, tv):
        return jnp.where(tv[i] > 0, j, nc - 1)

    in_specs = [
        pl.BlockSpec((tile_rows, d), lambda i, j, te, tv: (i, 0)),
        pl.BlockSpec((1, d, FFN_COLS), lambda i, j, te, tv: (te[i], 0, active_col(i, j, tv))),
        pl.BlockSpec((1, d, FFN_COLS), lambda i, j, te, tv: (te[i], 0, active_col(i, j, tv))),
        pl.BlockSpec((1, FFN_COLS, d), lambda i, j, te, tv: (te[i], active_col(i, j, tv), 0)),
    ]
    return pl.pallas_call(
        _ffn_kernel,
        grid_spec=pltpu.PrefetchScalarGridSpec(
            num_scalar_prefetch=2,
            grid=(nt, nc),
            in_specs=in_specs,
            out_specs=pl.BlockSpec((tile_rows, d), lambda i, j, te, tv: (i, 0)),
            scratch_shapes=[pltpu.VMEM((tile_rows, d), BF16)],
        ),
        out_shape=jax.ShapeDtypeStruct((p, d), F32),
        compiler_params=_cparams("arbitrary", "arbitrary"),
        name="swiglu_grouped",
    )(tile_expert, tile_valid, xs, wg, wu, wd)


def _dense_layer_kernel(ntp, x_ref, anp_ref, ans_ref, bn_ref, w_ref, g1_ref, sc_ref, sh_ref, lg1_ref, lb1_ref,
                        wg_ref, wu_ref, wd_ref, g2_ref, lg2_ref, lb2_ref, o_ref):
    x1, h2 = _mixer_close(ntp, x_ref, anp_ref, ans_ref, bn_ref, w_ref, g1_ref, sc_ref, sh_ref, lg1_ref, lb1_ref)
    h = h2.astype(BF16)
    f = wg_ref.shape[1]
    acc = None
    for c0 in range(0, f, FFN_SPLIT):
        c1 = min(c0 + FFN_SPLIT, f)
        g = _dot(h, wg_ref[:, c0:c1])
        u = _dot(h, wu_ref[:, c0:c1])
        a = (g * jax.nn.sigmoid(g) * u).astype(BF16)
        part = _dot(a, wd_ref[c0:c1, :])
        acc = part if acc is None else acc + part
    o_ref[...] = _ln(ALPHA * x1 + _gate_rows(acc, g2_ref), lg2_ref[...], lb2_ref[...])


def _dense_layer_call(layer, x, an_p, an_s, bn, w_bf, mods, lg1, lb1, wg_bf, wu_bf, wd_bf, lg2, lb2):
    t, d = x.shape
    f = wg_bf.shape[1]
    ntp = an_p.shape[0] // ROW_TILE
    tile = pl.BlockSpec((ROW_TILE, d), lambda i: (i, 0))
    vec = pl.BlockSpec((1, d), lambda i: (0, 0))

    def resident(shape):
        return pl.BlockSpec(shape, lambda i: (0, 0), pipeline_mode=pl.Buffered(1))

    return pl.pallas_call(
        functools.partial(_dense_layer_kernel, ntp),
        grid=(t // ROW_TILE,),
        in_specs=_mixer_specs(layer, d, ntp, w_bf.shape) + [
            resident((d, f)), resident((d, f)), resident((f, d)), _mod_spec(layer, G2, ROW_TILE, d), vec, vec],
        out_specs=tile,
        out_shape=jax.ShapeDtypeStruct((t, d), F32),
        compiler_params=_cparams("arbitrary"),
        name="dense_layer_close",
    )(x, an_p, an_s, bn, w_bf, mods, mods, mods, lg1, lb1, wg_bf, wu_bf, wd_bf, mods, lg2, lb2)


def _dispatch_kernel(pos_ref, src_ref, init_ref, dst_ref, sem):
    del init_ref

    def issue(r, carry):
        row = src_ref.at[pl.ds(r, 1), :]
        pltpu.make_async_copy(row, dst_ref.at[pl.ds(pos_ref[0, r], 1), :], sem).start()
        pltpu.make_async_copy(row, dst_ref.at[pl.ds(pos_ref[0, ROW_TILE + r], 1), :], sem).start()
        return carry
    lax.fori_loop(0, ROW_TILE, issue, 0, unroll=8)
    for _ in range(2):
        pltpu.make_async_copy(src_ref, dst_ref.at[pl.ds(0, ROW_TILE), :], sem).wait()


def _dispatch_call(pos_tiles, hp, zeros):
    t, d = hp.shape
    n_rows_sorted = zeros.shape[0]
    nt = t // ROW_TILE
    return pl.pallas_call(
        _dispatch_kernel,
        grid=(nt,),
        in_specs=[
            pl.BlockSpec((None, 1, 2 * ROW_TILE), lambda i: (i, 0, 0), memory_space=pltpu.SMEM),
            pl.BlockSpec((ROW_TILE, d), lambda i: (i, 0)),
            pl.BlockSpec(memory_space=pl.ANY),
        ],
        out_specs=pl.BlockSpec(memory_space=pl.ANY),
        out_shape=jax.ShapeDtypeStruct((n_rows_sorted, d), hp.dtype),
        scratch_shapes=[pltpu.SemaphoreType.DMA(())],
        input_output_aliases={2: 0},
        compiler_params=_cparams("arbitrary"),
        name="dispatch_rows",
    )(pos_tiles, hp, zeros)


def _final_moe_kernel(ntp, pos_ref, x_ref, rt_ref, g2_ref, lg_ref, lb_ref, ys_ref, op_ref, os_ref, ybuf, sem):
    def issue(r, carry):
        pltpu.make_async_copy(ys_ref.at[pl.ds(pos_ref[0, r], 1), :], ybuf.at[0, pl.ds(r, 1), :], sem).start()
        pltpu.make_async_copy(ys_ref.at[pl.ds(pos_ref[0, ROW_TILE + r], 1), :], ybuf.at[1, pl.ds(r, 1), :],
                              sem).start()
        return carry
    lax.fori_loop(0, ROW_TILE, issue, 0, unroll=8)
    for k in range(2):
        pltpu.make_async_copy(ys_ref.at[pl.ds(0, ROW_TILE), :], ybuf.at[k], sem).wait()
    rt = rt_ref[...]
    f = rt[:, 2:3] * ybuf[0] + rt[:, 3:4] * ybuf[1]
    y = _ln(ALPHA * x_ref[...] + _gate_rows(f, g2_ref), lg_ref[...], lb_ref[...])
    i = pl.program_id(0)

    @pl.when(i < ntp)
    def _():
        op_ref[...] = y

    @pl.when(i >= ntp)
    def _():
        os_ref[...] = y


def _final_moe_call(layer, pos_tiles, x1, rt, mods, lg, lb, ys, *, n_prompt_tiles):
    t, d = x1.shape
    ntp = n_prompt_tiles
    tile = pl.BlockSpec((ROW_TILE, d), lambda i: (i, 0))
    vec = pl.BlockSpec((1, d), lambda i: (0, 0))
    return pl.pallas_call(
        functools.partial(_final_moe_kernel, ntp),
        grid=(t // ROW_TILE,),
        in_specs=[
            pl.BlockSpec((None, 1, 2 * ROW_TILE), lambda i: (i, 0, 0), memory_space=pltpu.SMEM),
            tile,
            pl.BlockSpec((ROW_TILE, LANES), lambda i: (i, 0)),
            _mod_spec(layer, G2, ROW_TILE, d),
            vec, vec,
            pl.BlockSpec(memory_space=pl.ANY),
        ],
        out_specs=[pl.BlockSpec((ROW_TILE, d), lambda i: (jnp.minimum(i, ntp - 1), 0)),
                   pl.BlockSpec((ROW_TILE, d), lambda i: (jnp.maximum(i - ntp, 0), 0))],
        out_shape=[jax.ShapeDtypeStruct((ntp * ROW_TILE, d), F32),
                   jax.ShapeDtypeStruct((t - ntp * ROW_TILE, d), F32)],
        scratch_shapes=[pltpu.VMEM((2, ROW_TILE, d), F32), pltpu.SemaphoreType.DMA(())],
        compiler_params=_cparams("arbitrary"),
        name="final_moe",
    )(pos_tiles, x1, rt, mods, lg, lb, ys)


def _bias_diagonals(table, n_q, n_k):
    k = jnp.arange(n_q + n_k - 1)
    return table[:, jnp.clip(LEFT_CTX + (n_q - 1) - k, -REL_CLIP, REL_CLIP) + REL_CLIP].astype(F32) * LOG2E


def _rel_bias(table, n_q, n_k):
    diag = _bias_diagonals(table, n_q, n_k)
    n_diag = diag.shape[1]
    padded = jnp.concatenate([diag, jnp.zeros((diag.shape[0], 1), F32)], axis=1)
    skew = jnp.tile(padded, (1, n_q))[:, :n_q * n_diag].reshape(-1, n_q, n_diag)
    return skew[:, :, n_q - 1:n_q - 1 + n_k]


def _gating_weights(w_s, b_s):
    n = MLP_CHUNK
    tril = jnp.tril(jnp.ones((n, n), bool))
    wm = jnp.where(tril[None], w_s, 0.0)
    h = CHUNK
    top = wm[:, :h, :h]
    z = jnp.zeros_like(top)
    wm_s = jnp.concatenate([jnp.concatenate([top, z], 2), jnp.concatenate([z, top], 2)], 1)
    both = jnp.stack([wm, wm_s])
    wcat = jnp.concatenate([both[:, 0::2], both[:, 1::2]], axis=-1)
    bias_p = jnp.repeat(jnp.transpose(b_s), GROUP_DIM, axis=1)
    bias_s = jnp.concatenate([bias_p[:h], bias_p[:h]], 0)
    return wcat.astype(BF16), jnp.stack([bias_p, bias_s]).astype(F32)


def _route_plan(et, n_tiles, tile_rows):
    nt = et.shape[0]
    e0 = et[:, 0, :].astype(jnp.int32)
    e1 = et[:, 1, :].astype(jnp.int32)
    experts = jnp.arange(N_EXPERTS, dtype=jnp.int32)[:, None, None]
    is0 = e0[None] == experts
    is1 = e1[None] == experts
    hit = (is0 | is1).astype(jnp.int32)
    in_tile = jnp.cumsum(hit, axis=2)
    tile_tot = in_tile[:, :, -1]
    before_tile = jnp.cumsum(tile_tot, axis=1) - tile_tot
    before = before_tile[:, :, None] + in_tile - hit
    counts = jnp.sum(tile_tot, axis=1)
    tiles_e = (counts + tile_rows - 1) // tile_rows
    tile_end = jnp.cumsum(tiles_e)
    tile_start = tile_end - tiles_e
    slot = before + (tile_start * tile_rows)[:, None, None]
    pos0 = jnp.sum(jnp.where(is0, slot, 0), axis=0)
    pos1 = jnp.sum(jnp.where(is1, slot, 0), axis=0)
    pos = jnp.concatenate([pos0, pos1], axis=1).reshape(nt, 1, 2 * ROW_TILE)
    ti = jnp.arange(n_tiles)
    te = jnp.minimum(jnp.sum((ti[:, None] >= tile_end[None, :]).astype(jnp.int32), axis=1), N_EXPERTS - 1)
    tv = jnp.clip(counts[te] - (ti - tile_start[te]) * tile_rows, 0, tile_rows)
    tv = jnp.where(ti < tile_end[-1], tv, 0)
    last_e = te[jnp.maximum(tile_end[-1] - 1, 0)]
    te = jnp.where(ti < tile_end[-1], te, last_e)
    return pos.astype(jnp.int32), te.astype(jnp.int32), tv.astype(jnp.int32)


def kernel(x_prompt, x_sample, cache_k, cache_v, c_prompt, c_sample, ln_in_g, ln_in_b, w_in, w_out, rel_bias_table, gmlp_ln_g, gmlp_ln_b, gmlp_w_s, gmlp_b_s, out_norm_a, out_norm_b, ada_w, ada_b, ln1_g, ln1_b, ln2_g, ln2_b, ffn_w_gate, ffn_w_up, ffn_w_down, moe_router, moe_w_gate, moe_w_up, moe_w_down):
    nb, seq, d = x_prompt.shape
    ns, dseq, _ = x_sample.shape
    depth = w_in.shape[0]
    tp = nb * seq
    ts = ns * dseq
    t = tp + ts
    assert depth == DEPTH and dseq == CHUNK and seq % ROW_TILE == 0 and ts % ROW_TILE == 0
    assert seq % (Q_GROUP * Q_TILE) == 0 and cache_k.shape[2] == LEFT_CTX
    ntp = tp // ROW_TILE

    n_cond = nb + ns
    c_all = jnp.concatenate([c_prompt, c_sample], axis=0)
    c_pad = jnp.pad(c_all, ((0, (-n_cond) % 8), (0, 0)))
    mods = _ada_call(c_pad, ada_w, ada_b)
    mods = jnp.transpose(mods[:, :n_cond].reshape(depth, n_cond, 6, d), (0, 2, 1, 3))
    cpb = seq // CHUNK
    mods_p = jnp.broadcast_to(mods[:, :, :nb, None, :], (depth, 6, nb, cpb, d)).reshape(depth, 6, nb * cpb, d)
    mods = jnp.concatenate([mods_p, mods[:, :, nb:]], axis=2)

    def row(v):
        return v.reshape(1, -1)

    w_in_bf = w_in.astype(BF16)
    w_out_bf = w_out.astype(BF16)
    cache_kt = jnp.transpose(cache_k, (0, 1, 3, 4, 2))
    cache_vt = jnp.transpose(cache_v, (0, 1, 3, 4, 2))
    k_tail, v_tail, gv_rows = [], [], []
    for l in range(depth):
        wcat, bs = _gating_weights(gmlp_w_s[l], gmlp_b_s[l])
        if l == 0:
            x_in = (x_prompt.reshape(tp, d), x_sample.reshape(ts, d), row(ln_in_g), row(ln_in_b))
        else:
            x_in = (x,)
        res = _inproj_call(l, x_in, mods, w_in_bf, row(gmlp_ln_g[l]), row(gmlp_ln_b[l]), wcat, bs,
                           row(out_norm_b[l]),
                           n_prompt_tiles=ntp, tiles_per_batch=seq // ROW_TILE, n_batch=nb, n_sample_rows=ts)
        if l == 0:
            x, q, k, v, kf, vf, gv, bn = res
        else:
            q, k, v, kf, vf, gv, bn = res
        k_tail.append(kf)
        v_tail.append(vf)
        gv_rows.append(gv)

        table = rel_bias_table[l]
        diag_p = jnp.pad(_bias_diagonals(table, Q_TILE, 3 * Q_TILE), ((0, 0), (1, 0)))
        bias_s = _rel_bias(table, CHUNK, LEFT_CTX + CHUNK)
        na = row(out_norm_a[l])
        routed = l % 2 == 1
        n_tiles = -(-2 * t // MOE_TILE) + N_EXPERTS
        an_p = _attn_prompt_call(q, k, v, diag_p, na, n_batch=nb, q_tiles_per_batch=seq // Q_TILE,
                                 zeros_shape=(n_tiles * MOE_TILE, d) if routed else None)
        if routed:
            an_p, sorted_init = an_p
        an_s = _attn_sample_call(l, q, k, v, cache_kt, cache_vt, bias_s[:, :, :LEFT_CTX], bias_s[:, :, LEFT_CTX:], na,
                                 first_chunk=tp // CHUNK)

        lg1, lb1, lg2, lb2 = row(ln1_g[l]), row(ln1_b[l]), row(ln2_g[l]), row(ln2_b[l])
        i = l // 2
        if not routed:
            x = _dense_layer_call(l, x, an_p, an_s, bn, w_out_bf, mods, lg1, lb1, ffn_w_gate[i].astype(BF16),
                                  ffn_w_up[i].astype(BF16), ffn_w_down[i].astype(BF16), lg2, lb2)
        else:
            r_pad = jnp.pad(moe_router[i], ((0, 0), (0, LANES - N_EXPERTS)))
            r_hi = r_pad.astype(BF16)
            r_lo = (r_pad - r_hi.astype(F32)).astype(BF16)
            r_cat = jnp.concatenate([r_hi, r_lo], axis=1)
            x1, hp, rt, et = _outproj_call(l, x, an_p, an_s, bn, w_out_bf, mods, lg1, lb1, r_cat)
            pos, te, tv = _route_plan(et, n_tiles, MOE_TILE)
            xs = _dispatch_call(pos, hp, sorted_init)
            ys = _ffn_call(te, tv, xs, moe_w_gate[i], moe_w_up[i], moe_w_down[i], MOE_TILE)
            y_p, y_s = _final_moe_call(l, pos, x1, rt, mods, lg2, lb2, ys, n_prompt_tiles=ntp)

    y_prompt = y_p.reshape(nb, seq, d)
    y_sample = y_s.reshape(ns, dseq, d)

    def tails(rows):
        kp = jnp.stack([r[:nb * ROW_TILE].reshape(nb, ROW_TILE, N_HEADS, HEAD_DIM) for r in rows])
        ksn = jnp.stack([r[nb * ROW_TILE:].reshape(ns, dseq, N_HEADS, HEAD_DIM) for r in rows])
        return kp, ksn

    k_prompt_new, k_sample_new = tails(k_tail)
    v_prompt_new, v_sample_new = tails(v_tail)
    gmlp_v_sample_new = jnp.stack([g.reshape(ns, dseq, D_B) for g in gv_rows])
    return (y_prompt, y_sample, k_prompt_new, v_prompt_new, k_sample_new, v_sample_new, gmlp_v_sample_new)
```
